```python
import jax, jax.numpy as jnp
from jax import lax
import numpy as np

D_MODEL = 1024
BATCH = 8
SEQ = 4096
DEPTH = 2

CHUNK = 64
Q_BLOCK = 128
HEAD_DIM = 64
NORM_EPS = 1e-6
NEG_INF = -1e30
MLA_HEADS = 6
MLA_NOPE = 64
MLA_ROPE = 32
MLA_V = 64
Q_LORA = 256
KV_LORA = 128
ROPE_THETA = 10000.0
SB_HEADS = 5
FOX_HEADS = 5
N_BRANCH = 3
PEER_HEADS = 8
N_KEYS = 128
N_EXPERTS = N_KEYS * N_KEYS
D_KEY = 256
HALF_KEY = D_KEY // 2
PEER_TOPK = 16
PEER_BLOCK = 128

SB_W = SB_HEADS * HEAD_DIM
FOX_W = FOX_HEADS * HEAD_DIM
MLA_OUT = MLA_HEADS * MLA_V
MLA_QK = MLA_NOPE + MLA_ROPE
IN_SPLITS = (Q_LORA, KV_LORA, MLA_ROPE, SB_W, SB_W, SB_W, FOX_W, FOX_W, FOX_W, FOX_HEADS, N_BRANCH * D_MODEL)
D_IN = sum(IN_SPLITS)
IN_OFFSETS = tuple(int(o) for o in np.cumsum(IN_SPLITS)[:-1])

kernel_name = 'hybrid_mla_stickbreak_fox_peer'


def rmsnorm(x, g):
    xf = x.astype(jnp.float32)
    y = xf * lax.rsqrt(jnp.mean(xf * xf, axis=-1, keepdims=True) + NORM_EPS)
    return (y * g.astype(jnp.float32)).astype(x.dtype)


def rope_angles(pos):
    inv = ROPE_THETA ** (-jnp.arange(0, MLA_ROPE, 2, dtype=jnp.float32) / MLA_ROPE)
    ang = pos.astype(jnp.float32)[..., None] * inv
    return jnp.cos(ang), jnp.sin(ang)


def apply_rope(x, cos, sin):
    x1, x2 = jnp.split(x.astype(jnp.float32), 2, axis=-1)
    return jnp.concatenate([x1 * cos - x2 * sin, x1 * sin + x2 * cos], axis=-1).astype(x.dtype)


def to_blocks(a):
    b, s = a.shape[:2]
    a = a.reshape((b, s // Q_BLOCK, Q_BLOCK) + a.shape[2:])
    return jnp.moveaxis(a, 1, 0)


def from_blocks(a):
    a = jnp.moveaxis(a, 0, 1)
    return a.reshape((a.shape[0], a.shape[1] * a.shape[2]) + a.shape[3:])


def mla_branch(c_q, c_kv, k_rope_raw, pos, g_q, w_uq, g_kv, w_ukv):
    b, s, _ = c_q.shape
    q = (rmsnorm(c_q, g_q) @ w_uq).reshape(b, s, MLA_HEADS, MLA_QK)
    q_nope, q_rope = q[..., :MLA_NOPE], q[..., MLA_NOPE:]
    kv = (rmsnorm(c_kv, g_kv) @ w_ukv).reshape(b, s, MLA_HEADS, MLA_NOPE + MLA_V)
    k_nope, v = kv[..., :MLA_NOPE], kv[..., MLA_NOPE:]
    cos, sin = rope_angles(pos)
    q_rope = apply_rope(q_rope, cos[:, :, None], sin[:, :, None])
    k_rope = apply_rope(k_rope_raw, cos, sin)
    scale = MLA_QK ** -0.5
    key_chunk = jnp.arange(s) // CHUNK

    def block(args):
        qn, qr, i = args
        q_chunk = (i * Q_BLOCK + jnp.arange(Q_BLOCK)) // CHUNK
        logits = (jnp.einsum('bqhd,bkhd->bhqk', qn, k_nope)
                  + jnp.einsum('bqhr,bkr->bhqk', qr, k_rope)).astype(jnp.float32) * scale
        allowed = key_chunk[None, :] <= q_chunk[:, None]
        p = jax.nn.softmax(jnp.where(allowed, logits, NEG_INF), axis=-1)
        return jnp.einsum('bhqk,bkhd->bqhd', p.astype(v.dtype), v)

    o = lax.map(block, (to_blocks(q_nope), to_blocks(q_rope), jnp.arange(s // Q_BLOCK)))
    return from_blocks(o).reshape(b, s, MLA_OUT)


def stick_breaking_branch(q, k, v):
    b, s = q.shape[:2]
    scale = HEAD_DIM ** -0.5
    key_idx = jnp.arange(s)

    def block(args):
        qb, i = args
        q_idx = i * Q_BLOCK + jnp.arange(Q_BLOCK)
        z = jnp.einsum('bqhd,bkhd->bhqk', qb, k).astype(jnp.float32) * scale
        strict = key_idx[None, :] < q_idx[:, None]
        log_1m = jnp.where(strict, jax.nn.log_sigmoid(-z), 0.0)
        csum = jnp.cumsum(log_1m, axis=-1)
        log_a = jax.nn.log_sigmoid(z) + (csum[..., -1:] - csum)
        a = jnp.where(strict, jnp.exp(log_a), 0.0)
        return jnp.einsum('bhqk,bkhd->bqhd', a.astype(v.dtype), v)

    o = lax.map(block, (to_blocks(q), jnp.arange(s // Q_BLOCK)))
    return from_blocks(o).reshape(b, s, SB_W)


def forgetting_branch(q, k, v, f_raw, b_f):
    b, s = q.shape[:2]
    scale = HEAD_DIM ** -0.5
    key_idx = jnp.arange(s)
    log_f = jax.nn.log_sigmoid(f_raw.astype(jnp.float32) + b_f.astype(jnp.float32))
    cum_f = jnp.cumsum(log_f, axis=1)
    cum_k = jnp.transpose(cum_f, (0, 2, 1))

    def block(args):
        qb, fq, i = args
        q_idx = i * Q_BLOCK + jnp.arange(Q_BLOCK)
        logits = jnp.einsum('bqhd,bkhd->bhqk', qb, k).astype(jnp.float32) * scale
        decay = jnp.transpose(fq, (0, 2, 1))[..., :, None] - cum_k[:, :, None, :]
        causal = key_idx[None, :] <= q_idx[:, None]
        p = jax.nn.softmax(jnp.where(causal, logits + decay, NEG_INF), axis=-1)
        return jnp.einsum('bhqk,bkhd->bqhd', p.astype(v.dtype), v)

    o = lax.map(block, (to_blocks(q), to_blocks(cum_f), jnp.arange(s // Q_BLOCK)))
    return from_blocks(o).reshape(b, s, FOX_W)


def hybrid_mixer(h, pos, w_in, g_q, w_uq, g_kv, w_ukv, b_f, w_o_mla, w_o_sb, w_o_fox, b_gate, w_out):
    b, s, _ = h.shape
    z = h @ w_in
    (c_q, c_kv, k_r, q_sb, k_sb, v_sb, q_fx, k_fx, v_fx, f_raw, gate_raw) = jnp.split(z, IN_OFFSETS, axis=-1)
    sb_h = lambda a: a.reshape(b, s, SB_HEADS, HEAD_DIM)
    fx_h = lambda a: a.reshape(b, s, FOX_HEADS, HEAD_DIM)
    y_a = mla_branch(c_q, c_kv, k_r, pos, g_q, w_uq, g_kv, w_ukv) @ w_o_mla
    y_b = stick_breaking_branch(sb_h(q_sb), sb_h(k_sb), sb_h(v_sb)) @ w_o_sb
    y_c = forgetting_branch(fx_h(q_fx), fx_h(k_fx), fx_h(v_fx), f_raw, b_f) @ w_o_fox
    gates = jax.nn.sigmoid((gate_raw + b_gate).astype(jnp.float32)).astype(h.dtype)
    gates = gates.reshape(b, s, N_BRANCH, D_MODEL)
    y = gates[:, :, 0] * y_a + gates[:, :, 1] * y_b + gates[:, :, 2] * y_c
    return y @ w_out


def peer_ffn(h, w_q, sub_keys, u, v):
    b, s, d = h.shape
    t = b * s

    def block(xt):
        q = (xt @ w_q).reshape(PEER_BLOCK, PEER_HEADS, D_KEY)
        s1 = jnp.einsum('thc,nc->thn', q[..., :HALF_KEY], sub_keys[0])
        s2 = jnp.einsum('thc,nc->thn', q[..., HALF_KEY:], sub_keys[1])
        v1, i1 = lax.top_k(s1, PEER_TOPK)
        v2, i2 = lax.top_k(s2, PEER_TOPK)
        cand_s = (v1[..., :, None] + v2[..., None, :]).reshape(PEER_BLOCK, PEER_HEADS, PEER_TOPK * PEER_TOPK)
        cand_i = (i1[..., :, None] * N_KEYS + i2[..., None, :]).reshape(PEER_BLOCK, PEER_HEADS, PEER_TOPK * PEER_TOPK)
        top_s, sel = lax.top_k(cand_s, PEER_TOPK)
        e_idx = jnp.take_along_axis(cand_i, sel, axis=-1)
        g = jax.nn.softmax(top_s.astype(jnp.float32), axis=-1).astype(xt.dtype)
        u_e = jnp.take(u, e_idx, axis=0)
        v_e = jnp.take(v, e_idx, axis=0)
        act = jax.nn.gelu(jnp.einsum('thkd,td->thk', u_e, xt), approximate=False)
        return jnp.einsum('thk,thkd->td', g * act, v_e)

    out = lax.map(block, h.reshape(t // PEER_BLOCK, PEER_BLOCK, d))
    return out.reshape(b, s, d)


def setup_inputs(seed: int = 0) -> dict:
    key = jax.random.key(seed)
    ks = jax.random.split(key, 24)
    f32 = jnp.float32

    def w(k, shape, fan_in):
        return jax.random.normal(k, shape, f32) * (fan_in ** -0.5)

    def gain(k, shape):
        return 1.0 + 0.05 * jax.random.normal(k, shape, f32)

    L = DEPTH
    x = jax.random.normal(ks[0], (BATCH, SEQ, D_MODEL), f32)
    offs = jax.random.randint(ks[1], (BATCH, 1), 0, 64, dtype=jnp.int32) * CHUNK
    positions = (offs + jnp.arange(SEQ, dtype=jnp.int32)[None, :]).astype(jnp.int32)
    return {
        'x': x,
        'positions': positions,
        'norm1_g': gain(ks[2], (L, D_MODEL)),
        'w_in': w(ks[3], (L, D_MODEL, D_IN), D_MODEL),
        'mla_q_norm_g': gain(ks[4], (L, Q_LORA)),
        'w_uq': w(ks[5], (L, Q_LORA, MLA_HEADS * MLA_QK), Q_LORA),
        'mla_kv_norm_g': gain(ks[6], (L, KV_LORA)),
        'w_ukv': w(ks[7], (L, KV_LORA, MLA_HEADS * (MLA_NOPE + MLA_V)), KV_LORA),
        'fox_b_f': 3.0 + 1.0 * jax.random.normal(ks[8], (L, FOX_HEADS), f32),
        'w_o_mla': w(ks[9], (L, MLA_OUT, D_MODEL), MLA_OUT),
        'w_o_sb': w(ks[10], (L, SB_W, D_MODEL), SB_W),
        'w_o_fox': w(ks[11], (L, FOX_W, D_MODEL), FOX_W),
        'b_gate': 0.1 * jax.random.normal(ks[12], (L, N_BRANCH * D_MODEL), f32),
        'w_out': w(ks[13], (L, D_MODEL, D_MODEL), D_MODEL),
        'norm2_g': gain(ks[14], (L, D_MODEL)),
        'peer_w_q': w(ks[15], (L, D_MODEL, PEER_HEADS * D_KEY), D_MODEL),
        'peer_sub_keys': w(ks[16], (L, 2, N_KEYS, HALF_KEY), HALF_KEY),
        'peer_u': w(ks[17], (L, N_EXPERTS, D_MODEL), D_MODEL),
        'peer_v': w(ks[18], (L, N_EXPERTS, D_MODEL), PEER_HEADS),
        'final_norm_g': gain(ks[19], (D_MODEL,)),
    }


def reference(x, positions, norm1_g, w_in, mla_q_norm_g, w_uq, mla_kv_norm_g, w_ukv, fox_b_f,
              w_o_mla, w_o_sb, w_o_fox, b_gate, w_out, norm2_g, peer_w_q, peer_sub_keys,
              peer_u, peer_v, final_norm_g):
    for l in range(DEPTH):
        h = rmsnorm(x, norm1_g[l])
        x = x + hybrid_mixer(h, positions, w_in[l], mla_q_norm_g[l], w_uq[l], mla_kv_norm_g[l], w_ukv[l],
                             fox_b_f[l], w_o_mla[l], w_o_sb[l], w_o_fox[l], b_gate[l], w_out[l])
        h = rmsnorm(x, norm2_g[l])
        x = x + peer_ffn(h, peer_w_q[l], peer_sub_keys[l], peer_u[l], peer_v[l])
    return rmsnorm(x, final_norm_g)
```

```python
import functools

import numpy as np
import jax
import jax.numpy as jnp
from jax import lax
from jax.experimental import pallas as pl
from jax.experimental.pallas import tpu as pltpu

F32 = jnp.float32
BF16 = jnp.bfloat16

D_MODEL = 1024
CHUNK = 64
HEAD_DIM = 64
NORM_EPS = 1e-6
NEG_INF = -1e30
MLA_HEADS = 6
MLA_NOPE = 64
MLA_ROPE = 32
MLA_V = 64
Q_LORA = 256
KV_LORA = 128
ROPE_THETA = 10000.0
SB_HEADS = 5
FOX_HEADS = 5
N_BRANCH = 3
PEER_HEADS = 8
N_KEYS = 128
N_EXPERTS = N_KEYS * N_KEYS
D_KEY = 256
HALF_KEY = D_KEY // 2
PEER_TOPK = 16
MLA_QK = MLA_NOPE + MLA_ROPE
SB_W = SB_HEADS * HEAD_DIM
FOX_W = FOX_HEADS * HEAD_DIM
MLA_OUT = MLA_HEADS * MLA_V
IN_SPLITS = (Q_LORA, KV_LORA, MLA_ROPE, SB_W, SB_W, SB_W, FOX_W, FOX_W, FOX_W, FOX_HEADS, N_BRANCH * D_MODEL)
IN_OFFSETS = tuple(int(o) for o in np.cumsum((0,) + IN_SPLITS))

LANE = 128
HEAD_PAD_W = 384
MLA_IN_W = Q_LORA + KV_LORA + 2 * LANE
QKV_W = 3 * HEAD_PAD_W
IN_W = MLA_IN_W + 2 * QKV_W + LANE
VMEM_LIMIT = 48 * 1024 * 1024

_NT = (((1,), (1,)), ((), ()))


def _cparams(*sem):
    return pltpu.CompilerParams(dimension_semantics=sem, vmem_limit_bytes=VMEM_LIMIT)


def _rms(x, g):
    return x * lax.rsqrt(jnp.mean(x * x, axis=-1, keepdims=True) + NORM_EPS) * g


def _dot(a, b):
    return jnp.dot(a, b, preferred_element_type=F32)


def _dot_nt(a, b):
    return lax.dot_general(a, b, _NT, preferred_element_type=F32)


def _full(shape):
    return pl.BlockSpec(shape, lambda *_: (0,) * len(shape))


def _rope_kernel(pos_ref, inv_ref, cos_ref, sin_ref):
    ang = pos_ref[...].astype(F32) * inv_ref[...]
    cos_ref[...] = jnp.cos(ang)
    sin_ref[...] = jnp.sin(ang)


def _rope_tables(positions, tm):
    t = positions.size
    inv = ROPE_THETA ** (-jnp.arange(0, MLA_ROPE, 2, dtype=F32) / MLA_ROPE)
    inv_row = jnp.zeros((1, LANE), F32).at[0, MLA_NOPE:MLA_NOPE + MLA_ROPE].set(jnp.concatenate([inv, inv]))
    return pl.pallas_call(
        _rope_kernel,
        grid=(t // tm,),
        in_specs=[pl.BlockSpec((tm, 1), lambda i: (i, 0)), _full((1, LANE))],
        out_specs=[pl.BlockSpec((tm, LANE), lambda i: (i, 0))] * 2,
        out_shape=[jax.ShapeDtypeStruct((t, LANE), F32)] * 2,
        compiler_params=_cparams("parallel"),
        name="rope_tables",
    )(positions.reshape(t, 1), inv_row)


def _inproj_kernel(x_ref, g_ref, w_ref, mla_ref, sb_ref, fx_ref, f_ref):
    hn = _rms(x_ref[...], g_ref[...]).astype(BF16)
    o = 0
    for ref, width in ((mla_ref, MLA_IN_W), (sb_ref, QKV_W), (fx_ref, QKV_W), (f_ref, LANE)):
        ref[...] = _dot(hn, w_ref[:, o:o + width]).astype(ref.dtype)
        o += width


def _inproj(x2, g, w_cat, tm):
    t = x2.shape[0]
    row = lambda w: pl.BlockSpec((tm, w), lambda i: (i, 0))
    return pl.pallas_call(
        _inproj_kernel,
        grid=(t // tm,),
        in_specs=[row(D_MODEL), _full((1, D_MODEL)), _full((D_MODEL, IN_W))],
        out_specs=[row(MLA_IN_W), row(QKV_W), row(QKV_W), row(LANE)],
        out_shape=[jax.ShapeDtypeStruct((t, MLA_IN_W), F32), jax.ShapeDtypeStruct((t, QKV_W), BF16),
                   jax.ShapeDtypeStruct((t, QKV_W), BF16), jax.ShapeDtypeStruct((t, LANE), F32)],
        compiler_params=_cparams("parallel"),
        name="inproj",
    )(x2, g, w_cat)


def _mla_prep_kernel(in_ref, cos_ref, sin_ref, gq_ref, wqa_ref, wqb_ref, gkv_ref, wk_ref, wv_ref,
                     q_ref, k_ref, v_ref):
    cos = cos_ref[...]
    sin = sin_ref[...]
    qn = _rms(in_ref[:, :Q_LORA], gq_ref[...]).astype(BF16)
    kn = _rms(in_ref[:, Q_LORA:Q_LORA + KV_LORA], gkv_ref[...]).astype(BF16)
    o = Q_LORA + KV_LORA
    k_rot = in_ref[:, o:o + LANE] * cos + in_ref[:, o + LANE:o + 2 * LANE] * sin
    scale = MLA_QK ** -0.5
    for h in range(MLA_HEADS):
        sl = slice(h * LANE, (h + 1) * LANE)
        qa = _dot(qn, wqa_ref[:, sl])
        qb = _dot(qn, wqb_ref[:, sl])
        q_ref[:, sl] = ((qa * cos + qb * sin) * scale).astype(BF16)
        k_ref[:, sl] = (_dot(kn, wk_ref[:, sl]) + k_rot).astype(BF16)
    v_ref[...] = _dot(kn, wv_ref[...]).astype(BF16)


def _mla_prep(mla_in, cos, sin, gq, wqa, wqb, gkv, wk, wv, tm):
    t = mla_in.shape[0]
    row = lambda w: pl.BlockSpec((tm, w), lambda i: (i, 0))
    hw = MLA_HEADS * LANE
    return pl.pallas_call(
        _mla_prep_kernel,
        grid=(t // tm,),
        in_specs=[row(MLA_IN_W), row(LANE), row(LANE), _full((1, Q_LORA)), _full((Q_LORA, hw)),
                  _full((Q_LORA, hw)), _full((1, KV_LORA)), _full((KV_LORA, hw)), _full((KV_LORA, MLA_OUT))],
        out_specs=[row(hw), row(hw), row(MLA_OUT)],
        out_shape=[jax.ShapeDtypeStruct((t, hw), BF16), jax.ShapeDtypeStruct((t, hw), BF16),
                   jax.ShapeDtypeStruct((t, MLA_OUT), BF16)],
        compiler_params=_cparams("parallel"),
        name="mla_prep",
    )(mla_in, cos, sin, gq, wqa, wqb, gkv, wk, wv)


def _fox_prep_kernel(f_ref, bf_ref, cum_ref, cumt_ref, *, blk):
    s = f_ref.shape[1]
    row = lax.broadcasted_iota(jnp.int32, (blk, blk), 0)
    col = lax.broadcasted_iota(jnp.int32, (blk, blk), 1)
    tri = jnp.where(col <= row, 1.0, 0.0).astype(BF16)
    carry = jnp.zeros((1, LANE), F32)
    for b in range(s // blk):
        f = f_ref[0, b * blk:(b + 1) * blk, :] + bf_ref[...]
        lf = jnp.minimum(f, 0.0) - jnp.log1p(jnp.exp(-jnp.abs(f)))
        hi = lf.astype(BF16)
        r1 = lf - hi.astype(F32)
        mid = r1.astype(BF16)
        lo = (r1 - mid.astype(F32)).astype(BF16)
        c = _dot(tri, hi) + _dot(tri, mid) + _dot(tri, lo) + carry
        cum_ref[0, b * blk:(b + 1) * blk, :] = c
        cumt_ref[0, :, b * blk:(b + 1) * blk] = c.T[:8, :]
        carry = c[blk - 1:blk, :]


def _fox_prep(f3, bf_row, blk):
    b, s, _ = f3.shape
    return pl.pallas_call(
        functools.partial(_fox_prep_kernel, blk=blk),
        grid=(b,),
        in_specs=[pl.BlockSpec((1, s, LANE), lambda i: (i, 0, 0)), _full((1, LANE))],
        out_specs=[pl.BlockSpec((1, s, LANE), lambda i: (i, 0, 0)), pl.BlockSpec((1, 8, s), lambda i: (i, 0, 0))],
        out_shape=[jax.ShapeDtypeStruct((b, s, LANE), F32), jax.ShapeDtypeStruct((b, 8, s), F32)],
        compiler_params=_cparams("parallel"),
        name="fox_prep",
    )(f3, bf_row)


def _softmax_step(q, k, v, carry, bias, mask):
    m, l, acc = carry
    s = _dot_nt(q, k)
    if bias is not None:
        s = s + bias
    if mask is not None:
        s = jnp.where(mask, s, NEG_INF)
    m_new = jnp.maximum(m, jnp.max(s, axis=-1, keepdims=True))
    alpha = jnp.exp(m - m_new)
    p = jnp.exp(s - m_new)
    l = alpha * l + jnp.sum(p, axis=-1, keepdims=True)
    acc = alpha * acc + _dot(p.astype(BF16), v)
    return m_new, l, acc


def _softmax_init(tq):
    return (jnp.full((tq, 1), NEG_INF, F32), jnp.zeros((tq, 1), F32), jnp.zeros((tq, LANE), F32))


def _half_mask(tq, half):
    lane = lax.broadcasted_iota(jnp.int32, (tq, LANE), 1)
    return (lane < HEAD_DIM) if half == 0 else (lane >= HEAD_DIM)


def _mla_attn_kernel(q_ref, k_ref, v_ref, o_ref, *, tq):
    i = pl.program_id(1)
    r = lax.broadcasted_iota(jnp.int32, (tq, tq), 0)
    c = lax.broadcasted_iota(jnp.int32, (tq, tq), 1)
    diag_mask = (c // CHUNK) <= (r // CHUNK)
    lo_half = _half_mask(tq, 0)
    for hb in range(MLA_HEADS // 2):
        vsl = slice(hb * LANE, (hb + 1) * LANE)
        outs = []
        for half in range(2):
            sl = slice((2 * hb + half) * LANE, (2 * hb + half + 1) * LANE)
            q = q_ref[0, :, sl]

            def step(j, carry, mask, sl=sl, vsl=vsl, q=q):
                rows = pl.ds(pl.multiple_of(j * tq, tq), tq)
                return _softmax_step(q, k_ref[0, rows, sl], v_ref[0, rows, vsl], carry, None, mask)

            carry = lax.fori_loop(0, i, functools.partial(step, mask=None), _softmax_init(tq))
            _, l, acc = step(i, carry, diag_mask)
            outs.append(acc / l)
        o_ref[0, :, vsl] = jnp.where(lo_half, outs[0], outs[1]).astype(BF16)


def _fox_attn_kernel(q_ref, k_ref, v_ref, cum_ref, cumt_ref, o_ref, *, tq):
    i = pl.program_id(1)
    r = lax.broadcasted_iota(jnp.int32, (tq, tq), 0)
    c = lax.broadcasted_iota(jnp.int32, (tq, tq), 1)
    diag_mask = c <= r
    lo_half = _half_mask(tq, 0)
    for hb in range(HEAD_PAD_W // LANE):
        sl = slice(hb * LANE, (hb + 1) * LANE)
        qf = q_ref[0, :, sl].astype(F32)
        outs = []
        for half in range(2):
            h = 2 * hb + half
            if h >= FOX_HEADS:
                outs.append(jnp.zeros((tq, LANE), F32))
                continue
            q = jnp.where(_half_mask(tq, half), qf, 0.0).astype(BF16)
            cq = cum_ref[0, :, h:h + 1]

            def step(j, carry, mask, sl=sl, q=q, cq=cq, h=h):
                start = pl.multiple_of(j * tq, tq)
                bias = cq - cumt_ref[0, h:h + 1, pl.ds(start, tq)]
                rows = pl.ds(start, tq)
                return _softmax_step(q, k_ref[0, rows, sl], v_ref[0, rows, sl], carry, bias, mask)

            carry = lax.fori_loop(0, i, functools.partial(step, mask=None), _softmax_init(tq))
            _, l, acc = step(i, carry, diag_mask)
            outs.append(acc / l)
        o_ref[0, :, sl] = jnp.where(lo_half, outs[0], outs[1]).astype(BF16)


def _sb_attn_kernel(q_ref, k_ref, v_ref, o_ref, *, tq):
    i = pl.program_id(1)
    r = lax.broadcasted_iota(jnp.int32, (tq, tq), 0)
    c = lax.broadcasted_iota(jnp.int32, (tq, tq), 1)
    strict = c < r
    suffix = jnp.where(r > c, 1.0, 0.0).astype(BF16)
    lo_half = _half_mask(tq, 0)
    for hb in range(HEAD_PAD_W // LANE):
        sl = slice(hb * LANE, (hb + 1) * LANE)
        qf = q_ref[0, :, sl].astype(F32)
        outs = []
        for half in range(2):
            h = 2 * hb + half
            if h >= SB_HEADS:
                outs.append(jnp.zeros((tq, LANE), F32))
                continue
            q = jnp.where(_half_mask(tq, half), qf, 0.0).astype(BF16)

            def step(j, carry, mask, sl=sl, q=q):
                rest, acc = carry
                rows = pl.ds(pl.multiple_of(j * tq, tq), tq)
                z = _dot_nt(q, k_ref[0, rows, sl])
                sp = jnp.log1p(jnp.exp(-jnp.abs(z)))
                log_b = jnp.minimum(z, 0.0) - sp
                log_1m = log_b - z
                if mask is not None:
                    log_1m = jnp.where(mask, log_1m, 0.0)
                log_a = log_b + _dot(log_1m.astype(BF16), suffix) + rest
                a = jnp.exp(log_a)
                if mask is not None:
                    a = jnp.where(mask, a, 0.0)
                acc = acc + _dot(a.astype(BF16), v_ref[0, rows, sl])
                return rest + jnp.sum(log_1m, axis=-1, keepdims=True), acc

            carry = step(i, (jnp.zeros((tq, 1), F32), jnp.zeros((tq, LANE), F32)), strict)
            _, acc = lax.fori_loop(0, i, lambda n, cr: step(i - 1 - n, cr, None), carry)
            outs.append(acc)
        o_ref[0, :, sl] = jnp.where(lo_half, outs[0], outs[1]).astype(BF16)


def _attention(kernel, q, k, v, cols, extra, extra_specs, qw, vw, tq, name):
    b, s, _ = q.shape
    blk = lambda w, n=0: pl.BlockSpec((1, tq, w), lambda bi, i: (bi, i, n))
    seq = lambda w, n: pl.BlockSpec((1, s, w), lambda bi, i: (bi, 0, n))
    return pl.pallas_call(
        functools.partial(kernel, tq=tq),
        grid=(b, s // tq),
        in_specs=[blk(qw, cols[0]), seq(qw, cols[1]), seq(vw, cols[2])] + extra_specs,
        out_specs=blk(vw),
        out_shape=jax.ShapeDtypeStruct((b, s, vw), BF16),
        compiler_params=_cparams("parallel", "parallel"),
        name=name,
    )(q, k, v, *extra)


def _merge_kernel(x_ref, g_ref, wg_ref, bg_ref, oa_ref, ob_ref, oc_ref, wa_ref, wb_ref, wc_ref, wo_ref, out_ref):
    x = x_ref[...]
    hn = _rms(x, g_ref[...]).astype(BF16)
    y = None
    for n, (o_ref, w_ref) in enumerate(((oa_ref, wa_ref), (ob_ref, wb_ref), (oc_ref, wc_ref))):
        sl = slice(n * D_MODEL, (n + 1) * D_MODEL)
        gate = jax.nn.sigmoid(_dot(hn, wg_ref[:, sl]) + bg_ref[:, sl])
        term = gate * _dot(o_ref[...], w_ref[...])
        y = term if y is None else y + term
    out_ref[...] = x + _dot(y.astype(BF16), wo_ref[...])


def _merge(x2, g, wg, bg, oa, ob, oc, wa, wb, wc, wo, tm):
    t = x2.shape[0]
    row = lambda w: pl.BlockSpec((tm, w), lambda i: (i, 0))
    return pl.pallas_call(
        _merge_kernel,
        grid=(t // tm,),
        in_specs=[row(D_MODEL), _full((1, D_MODEL)), _full((D_MODEL, N_BRANCH * D_MODEL)),
                  _full((1, N_BRANCH * D_MODEL)), row(HEAD_PAD_W), row(HEAD_PAD_W), row(HEAD_PAD_W),
                  _full((HEAD_PAD_W, D_MODEL)), _full((HEAD_PAD_W, D_MODEL)), _full((HEAD_PAD_W, D_MODEL)),
                  _full((D_MODEL, D_MODEL))],
        out_specs=row(D_MODEL),
        out_shape=jax.ShapeDtypeStruct((t, D_MODEL), F32),
        compiler_params=_cparams("parallel"),
        name="merge",
    )(x2, g, wg, bg, oa, ob, oc, wa, wb, wc, wo)


def _kth_largest_rows(x, k):
    for _ in range(k - 1):
        x = jnp.where(x >= jnp.max(x, axis=0, keepdims=True), -jnp.inf, x)
    return jnp.max(x, axis=0, keepdims=True)


def _peer_route_kernel(x_ref, g_ref, wq_ref, keys_ref, h_ref, s1_ref, s2_ref, tau_ref, c1_ref, m2_ref,
                       top_ref, cand_ref):
    hn = _rms(x_ref[...], g_ref[...]).astype(BF16)
    h_ref[...] = hn
    for h in range(PEER_HEADS):
        tops = []
        for side, s_ref in ((0, s1_ref), (1, s2_ref)):
            o = h * D_KEY + side * HALF_KEY
            q = _dot(hn, wq_ref[:, o:o + HALF_KEY]).astype(BF16)
            s = _dot_nt(keys_ref[side], q)
            s_ref[h] = s
            for r in range(PEER_TOPK):
                m = jnp.max(s, axis=0, keepdims=True)
                top_ref[side, r:r + 1, :] = m
                s = jnp.where(s >= m, -jnp.inf, s)
        for a in range(PEER_TOPK):
            cand_ref[a * PEER_TOPK:(a + 1) * PEER_TOPK, :] = top_ref[0, a:a + 1, :] + top_ref[1]
        cand = cand_ref[...]
        tau = _kth_largest_rows(cand, PEER_TOPK)
        m1 = top_ref[0, 0:1, :]
        m2 = top_ref[1, 0:1, :]
        zsum = jnp.sum(jnp.where(cand >= tau, jnp.exp(cand - (m1 + m2)), 0.0), axis=0, keepdims=True)
        tau_ref[h:h + 1, :] = tau
        c1_ref[h:h + 1, :] = m1 + jnp.log(zsum)
        m2_ref[h:h + 1, :] = m2


def _peer_route(x2, g, wq, keys, tm):
    t = x2.shape[0]
    row = pl.BlockSpec((tm, D_MODEL), lambda i: (i, 0))
    sc = pl.BlockSpec((PEER_HEADS, N_KEYS, tm), lambda i: (0, 0, i))
    small = pl.BlockSpec((PEER_HEADS, tm), lambda i: (0, i))
    return pl.pallas_call(
        _peer_route_kernel,
        grid=(t // tm,),
        in_specs=[row, _full((1, D_MODEL)), _full((D_MODEL, PEER_HEADS * D_KEY)), _full((2, N_KEYS, HALF_KEY))],
        out_specs=[row, sc, sc, small, small, small],
        out_shape=[jax.ShapeDtypeStruct((t, D_MODEL), BF16),
                   jax.ShapeDtypeStruct((PEER_HEADS, N_KEYS, t), F32),
                   jax.ShapeDtypeStruct((PEER_HEADS, N_KEYS, t), F32),
                   jax.ShapeDtypeStruct((PEER_HEADS, t), F32),
                   jax.ShapeDtypeStruct((PEER_HEADS, t), F32),
                   jax.ShapeDtypeStruct((PEER_HEADS, t), F32)],
        scratch_shapes=[pltpu.VMEM((2, PEER_TOPK, tm), F32), pltpu.VMEM((PEER_TOPK * PEER_TOPK, tm), F32)],
        compiler_params=_cparams("parallel"),
        name="peer_route",
    )(x2, g, wq, keys)


def _peer_dense_kernel(x_ref, h_ref, s1_ref, s2_ref, tau_ref, c1_ref, m2_ref, u_ref, vt_ref, gf_ref, out_ref,
                       thr_ref, e1_ref, e2_ref, p_ref, acc_ref, *, te, final_norm):
    c = pl.program_id(1)
    rows_per_step = te // N_KEYS

    @pl.when(c == 0)
    def _():
        acc_ref[...] = jnp.zeros_like(acc_ref)
        for h in range(PEER_HEADS):
            s1 = s1_ref[h]
            thr_ref[h] = tau_ref[h:h + 1, :] - s1
            e1_ref[h] = jnp.exp(s1 - c1_ref[h:h + 1, :])
            e2_ref[h] = jnp.exp(s2_ref[h] - m2_ref[h:h + 1, :])

    pre = _dot_nt(u_ref[...], h_ref[...])
    act = 0.5 * pre * (1.0 + lax.erf(pre * (0.5 ** 0.5)))
    for j in range(rows_per_step):
        i1 = c * rows_per_step + j
        w = None
        for h in range(PEER_HEADS):
            thr = thr_ref[h, pl.ds(i1, 1), :]
            e1 = e1_ref[h, pl.ds(i1, 1), :]
            term = jnp.where(s2_ref[h] >= thr, e2_ref[h], 0.0) * e1
            w = term if w is None else w + term
        p_ref[j * N_KEYS:(j + 1) * N_KEYS, :] = (w * act[j * N_KEYS:(j + 1) * N_KEYS, :]).astype(BF16)
    acc_ref[...] += _dot(vt_ref[...], p_ref[...])

    @pl.when(c == pl.num_programs(1) - 1)
    def _():
        y = x_ref[...] + acc_ref[...].T
        if final_norm:
            y = _rms(y, gf_ref[...])
        out_ref[...] = y


def _peer_dense(x2, hn, s1, s2, tau, c1, m2, u, vt, gf, tm, te, final_norm):
    t = x2.shape[0]
    row = pl.BlockSpec((tm, D_MODEL), lambda i, c: (i, 0))
    sc = pl.BlockSpec((PEER_HEADS, N_KEYS, tm), lambda i, c: (0, 0, i))
    small = pl.BlockSpec((PEER_HEADS, tm), lambda i, c: (0, i))
    big = pltpu.VMEM((PEER_HEADS, N_KEYS, tm), F32)
    return pl.pallas_call(
        functools.partial(_peer_dense_kernel, te=te, final_norm=final_norm),
        grid=(t // tm, N_EXPERTS // te),
        in_specs=[row, row, sc, sc, small, small, small,
                  pl.BlockSpec((te, D_MODEL), lambda i, c: (c, 0)),
                  pl.BlockSpec((D_MODEL, te), lambda i, c: (0, c)),
                  pl.BlockSpec((1, D_MODEL), lambda i, c: (0, 0))],
        out_specs=row,
        out_shape=jax.ShapeDtypeStruct((t, D_MODEL), F32),
        scratch_shapes=[big, big, big, pltpu.VMEM((te, tm), BF16), pltpu.VMEM((D_MODEL, tm), F32)],
        compiler_params=_cparams("parallel", "arbitrary"),
        name="peer_dense",
    )(x2, hn, s1, s2, tau, c1, m2, u, vt, gf)


def _pad_cols(w, width):
    return jnp.pad(w, ((0, 0), (0, width - w.shape[1])))


def _rope_partner(w):
    half = MLA_ROPE // 2
    return jnp.concatenate([-w[:, half:], w[:, :half]], axis=1)


def _rope_lanes(w):
    return jnp.pad(w, ((0, 0), (MLA_NOPE, LANE - MLA_NOPE - MLA_ROPE)))


def _layer_weights(w_in, w_uq, w_ukv, w_o_sb, w_o_fox):
    o = IN_OFFSETS
    seg = lambda n: w_in[:, o[n]:o[n + 1]]
    scale = HEAD_DIM ** -0.5
    w_kr = seg(2)
    w_cat = jnp.concatenate(
        [seg(0), seg(1), _rope_lanes(w_kr), _rope_lanes(_rope_partner(w_kr)),
         _pad_cols(seg(3) * scale, HEAD_PAD_W), _pad_cols(seg(4), HEAD_PAD_W), _pad_cols(seg(5), HEAD_PAD_W),
         _pad_cols(seg(6) * scale, HEAD_PAD_W), _pad_cols(seg(7), HEAD_PAD_W), _pad_cols(seg(8), HEAD_PAD_W),
         _pad_cols(seg(9), LANE)], axis=1).astype(BF16)
    w_gate = seg(10).astype(BF16)
    uq = w_uq.reshape(Q_LORA, MLA_HEADS, MLA_QK)
    nope, rope = uq[..., :MLA_NOPE], uq[..., MLA_NOPE:]
    zpad = jnp.zeros((Q_LORA, MLA_HEADS, LANE - MLA_QK), F32)
    wqa = jnp.concatenate([nope, rope, zpad], axis=-1).reshape(Q_LORA, MLA_HEADS * LANE).astype(BF16)
    partner = jnp.concatenate([-rope[..., MLA_ROPE // 2:], rope[..., :MLA_ROPE // 2]], axis=-1)
    wqb = jnp.concatenate([jnp.zeros_like(nope), partner, zpad], axis=-1).reshape(Q_LORA, MLA_HEADS * LANE).astype(BF16)
    ukv = w_ukv.reshape(KV_LORA, MLA_HEADS, MLA_NOPE + MLA_V)
    wk = jnp.pad(ukv[..., :MLA_NOPE], ((0, 0), (0, 0), (0, LANE - MLA_NOPE))).reshape(KV_LORA, MLA_HEADS * LANE).astype(BF16)
    wv = ukv[..., MLA_NOPE:].reshape(KV_LORA, MLA_OUT).astype(BF16)
    pad_rows = lambda w: jnp.pad(w, ((0, HEAD_PAD_W - w.shape[0]), (0, 0))).astype(BF16)
    return w_cat, w_gate, wqa, wqb, wk, wv, pad_rows(w_o_sb), pad_rows(w_o_fox)


def _tiles(b, s):
    t = b * s
    tm = min(512, t)
    tq = min(256, s)
    return t, tm, tq


def kernel(x, positions, norm1_g, w_in, mla_q_norm_g, w_uq, mla_kv_norm_g, w_ukv, fox_b_f, w_o_mla, w_o_sb,
           w_o_fox, b_gate, w_out, norm2_g, peer_w_q, peer_sub_keys, peer_u, peer_v, final_norm_g):
    b, s, d = x.shape
    assert d == D_MODEL and MLA_OUT == HEAD_PAD_W
    depth = w_in.shape[0]
    t, tm, tq = _tiles(b, s)
    assert t % tm == 0 and s % tq == 0 and tq % CHUNK == 0
    te = 512
    x2 = x.reshape(t, d)
    cos, sin = _rope_tables(positions, tm)
    gf = final_norm_g.reshape(1, d)
    for l in range(depth):
        w_cat, w_gate, wqa, wqb, wk, wv, wo_sb, wo_fox = _layer_weights(w_in[l], w_uq[l], w_ukv[l], w_o_sb[l], w_o_fox[l])
        g1 = norm1_g[l].reshape(1, d)
        mla_in, sb, fx, f = _inproj(x2, g1, w_cat, tm)
        q_a, k_a, v_a = _mla_prep(mla_in, cos, sin, mla_q_norm_g[l].reshape(1, Q_LORA), wqa, wqb,
                                  mla_kv_norm_g[l].reshape(1, KV_LORA), wk, wv, tm)
        bf_row = jnp.zeros((1, LANE), F32).at[0, :FOX_HEADS].set(fox_b_f[l])
        cum, cumt = _fox_prep(f.reshape(b, s, LANE), bf_row, tq)
        hw = MLA_HEADS * LANE
        o_a = _attention(_mla_attn_kernel, q_a.reshape(b, s, hw), k_a.reshape(b, s, hw), v_a.reshape(b, s, MLA_OUT),
                         (0, 0, 0), [], [], hw, MLA_OUT, tq, "mla_attn")
        sb3 = sb.reshape(b, s, QKV_W)
        fx3 = fx.reshape(b, s, QKV_W)
        o_b = _attention(_sb_attn_kernel, sb3, sb3, sb3, (0, 1, 2), [], [], HEAD_PAD_W, HEAD_PAD_W, tq, "sb_attn")
        fox_specs = [pl.BlockSpec((1, tq, LANE), lambda bi, i: (bi, i, 0)),
                     pl.BlockSpec((1, 8, s), lambda bi, i: (bi, 0, 0))]
        o_c = _attention(_fox_attn_kernel, fx3, fx3, fx3, (0, 1, 2), [cum, cumt], fox_specs,
                         HEAD_PAD_W, HEAD_PAD_W, tq, "fox_attn")
        x2 = _merge(x2, g1, w_gate, b_gate[l].reshape(1, -1), o_a.reshape(t, MLA_OUT), o_b.reshape(t, HEAD_PAD_W),
                    o_c.reshape(t, HEAD_PAD_W), w_o_mla[l].astype(BF16), wo_sb, wo_fox, w_out[l].astype(BF16), tm)
        hn, s1, s2, tau, c1, m2 = _peer_route(x2, norm2_g[l].reshape(1, d), peer_w_q[l].astype(BF16),
                                              peer_sub_keys[l].astype(BF16), tm)
        x2 = _peer_dense(x2, hn, s1, s2, tau, c1, m2, peer_u[l].astype(BF16), peer_v[l].T.astype(BF16), gf, tm, te,
                         final_norm=(l == depth - 1))
    return x2.reshape(b, s, d)
```

```python
import functools

import numpy as np
import jax
import jax.numpy as jnp
from jax import lax
from jax.experimental import pallas as pl
from jax.experimental.pallas import tpu as pltpu

F32 = jnp.float32
BF16 = jnp.bfloat16

D_MODEL = 1024
CHUNK = 64
HEAD_DIM = 64
NORM_EPS = 1e-6
NEG_INF = -1e30
MLA_HEADS = 6
MLA_NOPE = 64
MLA_ROPE = 32
MLA_V = 64
Q_LORA = 256
KV_LORA = 128
ROPE_THETA = 10000.0
SB_HEADS = 5
FOX_HEADS = 5
N_BRANCH = 3
PEER_HEADS = 8
N_KEYS = 128
N_EXPERTS = N_KEYS * N_KEYS
D_KEY = 256
HALF_KEY = D_KEY // 2
PEER_TOPK = 16
MLA_QK = MLA_NOPE + MLA_ROPE
SB_W = SB_HEADS * HEAD_DIM
FOX_W = FOX_HEADS * HEAD_DIM
MLA_OUT = MLA_HEADS * MLA_V
IN_SPLITS = (Q_LORA, KV_LORA, MLA_ROPE, SB_W, SB_W, SB_W, FOX_W, FOX_W, FOX_W, FOX_HEADS, N_BRANCH * D_MODEL)
IN_OFFSETS = tuple(int(o) for o in np.cumsum((0,) + IN_SPLITS))

LANE = 128
HEAD_PAD_W = 384
MLA_IN_W = Q_LORA + KV_LORA + 2 * LANE
QKV_W = 3 * HEAD_PAD_W
IN_W = MLA_IN_W + 2 * QKV_W + LANE
VMEM_LIMIT = 48 * 1024 * 1024

_NT = (((1,), (1,)), ((), ()))


def _cparams(*sem):
    return pltpu.CompilerParams(dimension_semantics=sem, vmem_limit_bytes=VMEM_LIMIT)


def _rms(x, g):
    return x * lax.rsqrt(jnp.mean(x * x, axis=-1, keepdims=True) + NORM_EPS) * g


def _dot(a, b):
    return jnp.dot(a, b, preferred_element_type=F32)


def _dot_nt(a, b):
    return lax.dot_general(a, b, _NT, preferred_element_type=F32)


def _full(shape):
    return pl.BlockSpec(shape, lambda *_: (0,) * len(shape))


def _rope_kernel(pos_ref, inv_ref, cos_ref, sin_ref):
    ang = pos_ref[...].astype(F32) * inv_ref[...]
    cos_ref[...] = jnp.cos(ang)
    sin_ref[...] = jnp.sin(ang)


def _rope_tables(positions, tm):
    t = positions.size
    inv = ROPE_THETA ** (-jnp.arange(0, MLA_ROPE, 2, dtype=F32) / MLA_ROPE)
    inv_row = jnp.zeros((1, LANE), F32).at[0, MLA_NOPE:MLA_NOPE + MLA_ROPE].set(jnp.concatenate([inv, inv]))
    return pl.pallas_call(
        _rope_kernel,
        grid=(t // tm,),
        in_specs=[pl.BlockSpec((tm, 1), lambda i: (i, 0)), _full((1, LANE))],
        out_specs=[pl.BlockSpec((tm, LANE), lambda i: (i, 0))] * 2,
        out_shape=[jax.ShapeDtypeStruct((t, LANE), F32)] * 2,
        compiler_params=_cparams("parallel"),
        name="rope_tables",
    )(positions.reshape(t, 1), inv_row)


def _inproj_kernel(x_ref, g_ref, w_ref, mla_ref, sb_ref, fx_ref, f_ref):
    hn = _rms(x_ref[...], g_ref[...]).astype(BF16)
    o = 0
    for ref, width in ((mla_ref, MLA_IN_W), (sb_ref, QKV_W), (fx_ref, QKV_W), (f_ref, LANE)):
        ref[...] = _dot(hn, w_ref[:, o:o + width]).astype(ref.dtype)
        o += width


def _inproj(x2, g, w_cat, tm):
    t = x2.shape[0]
    row = lambda w: pl.BlockSpec((tm, w), lambda i: (i, 0))
    return pl.pallas_call(
        _inproj_kernel,
        grid=(t // tm,),
        in_specs=[row(D_MODEL), _full((1, D_MODEL)), _full((D_MODEL, IN_W))],
        out_specs=[row(MLA_IN_W), row(QKV_W), row(QKV_W), row(LANE)],
        out_shape=[jax.ShapeDtypeStruct((t, MLA_IN_W), F32), jax.ShapeDtypeStruct((t, QKV_W), BF16),
                   jax.ShapeDtypeStruct((t, QKV_W), BF16), jax.ShapeDtypeStruct((t, LANE), F32)],
        compiler_params=_cparams("parallel"),
        name="inproj",
    )(x2, g, w_cat)


def _mla_prep_kernel(in_ref, cos_ref, sin_ref, gq_ref, wqa_ref, wqb_ref, gkv_ref, wk_ref, wv_ref,
                     q_ref, k_ref, v_ref):
    cos = cos_ref[...]
    sin = sin_ref[...]
    qn = _rms(in_ref[:, :Q_LORA], gq_ref[...]).astype(BF16)
    kn = _rms(in_ref[:, Q_LORA:Q_LORA + KV_LORA], gkv_ref[...]).astype(BF16)
    o = Q_LORA + KV_LORA
    k_rot = in_ref[:, o:o + LANE] * cos + in_ref[:, o + LANE:o + 2 * LANE] * sin
    scale = MLA_QK ** -0.5
    for h in range(MLA_HEADS):
        sl = slice(h * LANE, (h + 1) * LANE)
        qa = _dot(qn, wqa_ref[:, sl])
        qb = _dot(qn, wqb_ref[:, sl])
        q_ref[:, sl] = ((qa * cos + qb * sin) * scale).astype(BF16)
        k_ref[:, sl] = (_dot(kn, wk_ref[:, sl]) + k_rot).astype(BF16)
    v_ref[...] = _dot(kn, wv_ref[...]).astype(BF16)


def _mla_prep(mla_in, cos, sin, gq, wqa, wqb, gkv, wk, wv, tm):
    t = mla_in.shape[0]
    row = lambda w: pl.BlockSpec((tm, w), lambda i: (i, 0))
    hw = MLA_HEADS * LANE
    return pl.pallas_call(
        _mla_prep_kernel,
        grid=(t // tm,),
        in_specs=[row(MLA_IN_W), row(LANE), row(LANE), _full((1, Q_LORA)), _full((Q_LORA, hw)),
                  _full((Q_LORA, hw)), _full((1, KV_LORA)), _full((KV_LORA, hw)), _full((KV_LORA, MLA_OUT))],
        out_specs=[row(hw), row(hw), row(MLA_OUT)],
        out_shape=[jax.ShapeDtypeStruct((t, hw), BF16), jax.ShapeDtypeStruct((t, hw), BF16),
                   jax.ShapeDtypeStruct((t, MLA_OUT), BF16)],
        compiler_params=_cparams("parallel"),
        name="mla_prep",
    )(mla_in, cos, sin, gq, wqa, wqb, gkv, wk, wv)


def _fox_prep_kernel(f_ref, bf_ref, cum_ref, cumt_ref, *, blk):
    s = f_ref.shape[1]
    row = lax.broadcasted_iota(jnp.int32, (blk, blk), 0)
    col = lax.broadcasted_iota(jnp.int32, (blk, blk), 1)
    tri = jnp.where(col <= row, 1.0, 0.0).astype(BF16)
    carry = jnp.zeros((1, LANE), F32)
    for b in range(s // blk):
        f = f_ref[0, b * blk:(b + 1) * blk, :] + bf_ref[...]
        lf = jnp.minimum(f, 0.0) - jnp.log1p(jnp.exp(-jnp.abs(f)))
        hi = lf.astype(BF16)
        r1 = lf - hi.astype(F32)
        mid = r1.astype(BF16)
        lo = (r1 - mid.astype(F32)).astype(BF16)
        c = _dot(tri, hi) + _dot(tri, mid) + _dot(tri, lo) + carry
        cum_ref[0, b * blk:(b + 1) * blk, :] = c
        cumt_ref[0, :, b * blk:(b + 1) * blk] = c.T[:8, :]
        carry = c[blk - 1:blk, :]


def _fox_prep(f3, bf_row, blk):
    b, s, _ = f3.shape
    return pl.pallas_call(
        functools.partial(_fox_prep_kernel, blk=blk),
        grid=(b,),
        in_specs=[pl.BlockSpec((1, s, LANE), lambda i: (i, 0, 0)), _full((1, LANE))],
        out_specs=[pl.BlockSpec((1, s, LANE), lambda i: (i, 0, 0)), pl.BlockSpec((1, 8, s), lambda i: (i, 0, 0))],
        out_shape=[jax.ShapeDtypeStruct((b, s, LANE), F32), jax.ShapeDtypeStruct((b, 8, s), F32)],
        compiler_params=_cparams("parallel"),
        name="fox_prep",
    )(f3, bf_row)


def _softmax_update(h, s, v, m_ref, l_ref, acc_ref):
    m_old = m_ref[h]
    m_new = jnp.maximum(m_old, jnp.max(s, axis=-1, keepdims=True))
    alpha = jnp.exp(m_old - m_new)
    p = jnp.exp(s - m_new)
    part = p[:, :LANE]
    for n in range(1, p.shape[1] // LANE):
        part = part + p[:, n * LANE:(n + 1) * LANE]
    l_ref[h] = alpha * l_ref[h] + part
    m_ref[h] = m_new
    acc_ref[h] = alpha * acc_ref[h] + _dot(p.astype(BF16), v)


def _softmax_reset(m_ref, l_ref, acc_ref):
    m_ref[...] = jnp.full(m_ref.shape, NEG_INF, F32)
    l_ref[...] = jnp.zeros(l_ref.shape, F32)
    acc_ref[...] = jnp.zeros(acc_ref.shape, F32)


def _half_mask(tq, half):
    lane = lax.broadcasted_iota(jnp.int32, (tq, LANE), 1)
    return (lane < HEAD_DIM) if half == 0 else (lane >= HEAD_DIM)


def _store_head_pairs(o_ref, heads, value_of):
    tq = o_ref.shape[1]
    lo_half = _half_mask(tq, 0)
    for hb in range(o_ref.shape[2] // LANE):
        lo = value_of(2 * hb)
        hi = value_of(2 * hb + 1) if 2 * hb + 1 < heads else jnp.zeros_like(lo)
        o_ref[0, :, hb * LANE:(hb + 1) * LANE] = jnp.where(lo_half, lo, hi).astype(o_ref.dtype)


def _masked_queries(q_ref, qm_ref, heads):
    tq = q_ref.shape[1]
    for h in range(heads):
        qf = q_ref[0, :, (h // 2) * LANE:(h // 2 + 1) * LANE].astype(F32)
        qm_ref[h] = jnp.where(_half_mask(tq, h % 2), qf, 0.0).astype(BF16)


def _block_iotas(tq):
    return lax.broadcasted_iota(jnp.int32, (tq, tq), 0), lax.broadcasted_iota(jnp.int32, (tq, tq), 1)


def _mla_attn_kernel(q_ref, k_ref, v_ref, o_ref, m_ref, l_ref, acc_ref, *, tq):
    i = pl.program_id(1)
    _softmax_reset(m_ref, l_ref, acc_ref)

    def block(j, diagonal):
        rows = pl.ds(pl.multiple_of(j * tq, tq), tq)
        for h in range(MLA_HEADS):
            sl = slice(h * LANE, (h + 1) * LANE)
            s = _dot_nt(q_ref[0, :, sl], k_ref[0, rows, sl])
            if diagonal:
                r, c = _block_iotas(tq)
                s = jnp.where((c // CHUNK) <= (r // CHUNK), s, NEG_INF)
            _softmax_update(h, s, v_ref[0, rows, (h // 2) * LANE:(h // 2 + 1) * LANE], m_ref, l_ref, acc_ref)

    def body(j, carry):
        block(j, False)
        return carry

    lax.fori_loop(0, i, body, 0)
    block(i, True)
    _store_head_pairs(o_ref, MLA_HEADS, lambda h: acc_ref[h] / jnp.sum(l_ref[h], axis=-1, keepdims=True))


def _fox_attn_kernel(q_ref, k_ref, v_ref, cum_ref, cumt_ref, o_ref, qm_ref, m_ref, l_ref, acc_ref, *, tq):
    i = pl.program_id(1)
    _softmax_reset(m_ref, l_ref, acc_ref)
    _masked_queries(q_ref, qm_ref, FOX_HEADS)

    def block(j, diagonal):
        start = pl.multiple_of(j * tq, tq)
        rows = pl.ds(start, tq)
        for h in range(FOX_HEADS):
            sl = slice((h // 2) * LANE, (h // 2 + 1) * LANE)
            s = _dot_nt(qm_ref[h], k_ref[0, rows, sl]) + (cum_ref[0, :, h:h + 1] - cumt_ref[0, h:h + 1, rows])
            if diagonal:
                r, c = _block_iotas(tq)
                s = jnp.where(c <= r, s, NEG_INF)
            _softmax_update(h, s, v_ref[0, rows, sl], m_ref, l_ref, acc_ref)

    def body(j, carry):
        block(j, False)
        return carry

    lax.fori_loop(0, i, body, 0)
    block(i, True)
    _store_head_pairs(o_ref, FOX_HEADS, lambda h: acc_ref[h] / jnp.sum(l_ref[h], axis=-1, keepdims=True))


def _sb_attn_kernel(q_ref, k_ref, v_ref, o_ref, qm_ref, suffix_ref, rest_ref, acc_ref, *, tq):
    i = pl.program_id(1)
    r, c = _block_iotas(tq)
    suffix_ref[...] = jnp.where(r > c, 1.0, 0.0).astype(BF16)
    rest_ref[...] = jnp.zeros(rest_ref.shape, F32)
    acc_ref[...] = jnp.zeros(acc_ref.shape, F32)
    _masked_queries(q_ref, qm_ref, SB_HEADS)

    def block(j, diagonal):
        rows = pl.ds(pl.multiple_of(j * tq, tq), tq)
        for h in range(SB_HEADS):
            sl = slice((h // 2) * LANE, (h // 2 + 1) * LANE)
            z = _dot_nt(qm_ref[h], k_ref[0, rows, sl])
            sp = jnp.log1p(jnp.exp(-jnp.abs(z)))
            log_b = jnp.minimum(z, 0.0) - sp
            log_1m = log_b - z
            if diagonal:
                strict = _block_iotas(tq)[1] < _block_iotas(tq)[0]
                log_1m = jnp.where(strict, log_1m, 0.0)
            a = jnp.exp(log_b + _dot(log_1m.astype(BF16), suffix_ref[...]) + rest_ref[h])
            if diagonal:
                a = jnp.where(strict, a, 0.0)
            acc_ref[h] += _dot(a.astype(BF16), v_ref[0, rows, sl])
            rest_ref[h] += jnp.sum(log_1m, axis=-1, keepdims=True)

    block(i, True)

    def body(n, carry):
        block(i - 1 - n, False)
        return carry

    lax.fori_loop(0, i, body, 0)
    _store_head_pairs(o_ref, SB_HEADS, lambda h: acc_ref[h])


def _attention(kernel, q, k, v, cols, extra, extra_specs, scratch, qw, vw, tq, name):
    b, s, _ = q.shape
    blk = lambda w, n=0: pl.BlockSpec((1, tq, w), lambda bi, i: (bi, i, n))
    seq = lambda w, n: pl.BlockSpec((1, s, w), lambda bi, i: (bi, 0, n))
    return pl.pallas_call(
        functools.partial(kernel, tq=tq),
        grid=(b, s // tq),
        in_specs=[blk(qw, cols[0]), seq(qw, cols[1]), seq(vw, cols[2])] + extra_specs,
        out_specs=blk(vw),
        out_shape=jax.ShapeDtypeStruct((b, s, vw), BF16),
        scratch_shapes=scratch,
        compiler_params=_cparams("parallel", "parallel"),
        name=name,
    )(q, k, v, *extra)


def _head_state(heads, tq):
    col = pltpu.VMEM((heads, tq, 1), F32)
    return col, pltpu.VMEM((heads, tq, LANE), F32)


def _merge_kernel(x_ref, g_ref, wg_ref, bg_ref, oa_ref, ob_ref, oc_ref, wa_ref, wb_ref, wc_ref, wo_ref, out_ref):
    x = x_ref[...]
    hn = _rms(x, g_ref[...]).astype(BF16)
    y = None
    for n, (o_ref, w_ref) in enumerate(((oa_ref, wa_ref), (ob_ref, wb_ref), (oc_ref, wc_ref))):
        sl = slice(n * D_MODEL, (n + 1) * D_MODEL)
        gate = jax.nn.sigmoid(_dot(hn, wg_ref[:, sl]) + bg_ref[:, sl])
        term = gate * _dot(o_ref[...], w_ref[...])
        y = term if y is None else y + term
    out_ref[...] = x + _dot(y.astype(BF16), wo_ref[...])


def _merge(x2, g, wg, bg, oa, ob, oc, wa, wb, wc, wo, tm):
    t = x2.shape[0]
    row = lambda w: pl.BlockSpec((tm, w), lambda i: (i, 0))
    return pl.pallas_call(
        _merge_kernel,
        grid=(t // tm,),
        in_specs=[row(D_MODEL), _full((1, D_MODEL)), _full((D_MODEL, N_BRANCH * D_MODEL)),
                  _full((1, N_BRANCH * D_MODEL)), row(HEAD_PAD_W), row(HEAD_PAD_W), row(HEAD_PAD_W),
                  _full((HEAD_PAD_W, D_MODEL)), _full((HEAD_PAD_W, D_MODEL)), _full((HEAD_PAD_W, D_MODEL)),
                  _full((D_MODEL, D_MODEL))],
        out_specs=row(D_MODEL),
        out_shape=jax.ShapeDtypeStruct((t, D_MODEL), F32),
        compiler_params=_cparams("parallel"),
        name="merge",
    )(x2, g, wg, bg, oa, ob, oc, wa, wb, wc, wo)


def _kth_largest_rows(x, k):
    for _ in range(k - 1):
        x = jnp.where(x >= jnp.max(x, axis=0, keepdims=True), -jnp.inf, x)
    return jnp.max(x, axis=0, keepdims=True)


def _peer_route_kernel(x_ref, g_ref, wq_ref, keys_ref, h_ref, s1_ref, s2_ref, tau_ref, c1_ref, m2_ref,
                       top_ref, cand_ref):
    hn = _rms(x_ref[...], g_ref[...]).astype(BF16)
    h_ref[...] = hn
    for h in range(PEER_HEADS):
        for side, s_ref in ((0, s1_ref), (1, s2_ref)):
            o = h * D_KEY + side * HALF_KEY
            q = _dot(hn, wq_ref[:, o:o + HALF_KEY]).astype(BF16)
            s_ref[h] = _dot_nt(keys_ref[side], q)

    def lane_tile(lt, carry):
        lanes = pl.ds(pl.multiple_of(lt * LANE, LANE), LANE)
        for h in range(PEER_HEADS):
            for side, s_ref in ((0, s1_ref), (1, s2_ref)):
                s = s_ref[h, :, lanes]
                for r in range(PEER_TOPK):
                    m = jnp.max(s, axis=0, keepdims=True)
                    top_ref[side, r:r + 1, :] = m
                    s = jnp.where(s >= m, -jnp.inf, s)
            cand_ref[0:PEER_TOPK, :] = top_ref[0, 0:1, :] + top_ref[1]
            for a in range(1, PEER_TOPK):
                cand_ref[8 + 8 * a:16 + 8 * a, :] = top_ref[0, a:a + 1, :] + top_ref[1, 0:8, :]
            cand = cand_ref[...]
            tau = _kth_largest_rows(cand, PEER_TOPK)
            m1 = top_ref[0, 0:1, :]
            m2 = top_ref[1, 0:1, :]
            zsum = jnp.sum(jnp.where(cand >= tau, jnp.exp(cand - (m1 + m2)), 0.0), axis=0, keepdims=True)
            tau_ref[h:h + 1, lanes] = tau
            c1_ref[h:h + 1, lanes] = m1 + jnp.log(zsum)
            m2_ref[h:h + 1, lanes] = m2
        return carry

    lax.fori_loop(0, x_ref.shape[0] // LANE, lane_tile, 0)


def _peer_route(x2, g, wq, keys, tm):
    t = x2.shape[0]
    row = pl.BlockSpec((tm, D_MODEL), lambda i: (i, 0))
    sc = pl.BlockSpec((PEER_HEADS, N_KEYS, tm), lambda i: (0, 0, i))
    small = pl.BlockSpec((PEER_HEADS, tm), lambda i: (0, i))
    return pl.pallas_call(
        _peer_route_kernel,
        grid=(t // tm,),
        in_specs=[row, _full((1, D_MODEL)), _full((D_MODEL, PEER_HEADS * D_KEY)), _full((2, N_KEYS, HALF_KEY))],
        out_specs=[row, sc, sc, small, small, small],
        out_shape=[jax.ShapeDtypeStruct((t, D_MODEL), BF16),
                   jax.ShapeDtypeStruct((PEER_HEADS, N_KEYS, t), F32),
                   jax.ShapeDtypeStruct((PEER_HEADS, N_KEYS, t), F32),
                   jax.ShapeDtypeStruct((PEER_HEADS, t), F32),
                   jax.ShapeDtypeStruct((PEER_HEADS, t), F32),
                   jax.ShapeDtypeStruct((PEER_HEADS, t), F32)],
        scratch_shapes=[pltpu.VMEM((2, PEER_TOPK, LANE), F32), pltpu.VMEM((8 + 8 * PEER_TOPK, LANE), F32)],
        compiler_params=_cparams("parallel"),
        name="peer_route",
    )(x2, g, wq, keys)


def _peer_dense_kernel(x_ref, h_ref, s1_ref, s2_ref, tau_ref, c1_ref, m2_ref, u_ref, vt_ref, gf_ref, out_ref,
                       thr_ref, e1_ref, e2_ref, act_ref, p_ref, acc_ref, *, te, final_norm):
    c = pl.program_id(1)
    rows_per_step = te // N_KEYS
    assert rows_per_step == 8
    part = 32

    @pl.when(c == 0)
    def _():
        acc_ref[...] = jnp.zeros_like(acc_ref)
        for h in range(PEER_HEADS):
            s1 = s1_ref[h]
            thr_ref[h] = tau_ref[h:h + 1, :] - s1
            e1_ref[h] = jnp.exp(s1 - c1_ref[h:h + 1, :])
            e2_ref[h] = jnp.exp(s2_ref[h] - m2_ref[h:h + 1, :])

    pre = _dot_nt(u_ref[...], h_ref[...])
    act_ref[...] = 0.5 * pre * (1.0 + lax.erf(pre * (0.5 ** 0.5)))

    first_keys = pl.ds(pl.multiple_of(c * rows_per_step, rows_per_step), rows_per_step)

    for lt in range(x_ref.shape[0] // LANE):
        lanes = slice(lt * LANE, (lt + 1) * LANE)
        for rp in range(N_KEYS // part):
            w = [None] * rows_per_step
            for h in range(PEER_HEADS):
                s2 = s2_ref[h, rp * part:(rp + 1) * part, lanes]
                e2 = e2_ref[h, rp * part:(rp + 1) * part, lanes]
                thr = thr_ref[h, first_keys, lanes]
                e1 = e1_ref[h, first_keys, lanes]
                for j in range(rows_per_step):
                    term = jnp.where(s2 >= thr[j:j + 1, :], e2, 0.0) * e1[j:j + 1, :]
                    w[j] = term if w[j] is None else w[j] + term
            for j in range(rows_per_step):
                rows = slice(j * N_KEYS + rp * part, j * N_KEYS + (rp + 1) * part)
                p_ref[rows, lanes] = (w[j] * act_ref[rows, lanes]).astype(BF16)
    acc_ref[...] += _dot(vt_ref[...], p_ref[...])

    @pl.when(c == pl.num_programs(1) - 1)
    def _():
        y = x_ref[...] + acc_ref[...].T
        if final_norm:
            y = _rms(y, gf_ref[...])
        out_ref[...] = y


def _peer_dense(x2, hn, s1, s2, tau, c1, m2, u, vt, gf, tm, te, final_norm):
    t = x2.shape[0]
    row = pl.BlockSpec((tm, D_MODEL), lambda i, c: (i, 0))
    sc = pl.BlockSpec((PEER_HEADS, N_KEYS, tm), lambda i, c: (0, 0, i))
    small = pl.BlockSpec((PEER_HEADS, tm), lambda i, c: (0, i))
    big = pltpu.VMEM((PEER_HEADS, N_KEYS, tm), F32)
    return pl.pallas_call(
        functools.partial(_peer_dense_kernel, te=te, final_norm=final_norm),
        grid=(t // tm, N_EXPERTS // te),
        in_specs=[row, row, sc, sc, small, small, small,
                  pl.BlockSpec((te, D_MODEL), lambda i, c: (c, 0)),
                  pl.BlockSpec((D_MODEL, te), lambda i, c: (0, c)),
                  pl.BlockSpec((1, D_MODEL), lambda i, c: (0, 0))],
        out_specs=row,
        out_shape=jax.ShapeDtypeStruct((t, D_MODEL), F32),
        scratch_shapes=[big, big, big, pltpu.VMEM((te, tm), F32), pltpu.VMEM((te, tm), BF16),
                        pltpu.VMEM((D_MODEL, tm), F32)],
        compiler_params=_cparams("parallel", "arbitrary"),
        name="peer_dense",
    )(x2, hn, s1, s2, tau, c1, m2, u, vt, gf)


def _pad_cols(w, width):
    return jnp.pad(w, ((0, 0), (0, width - w.shape[1])))


def _rope_partner(w):
    half = MLA_ROPE // 2
    return jnp.concatenate([-w[:, half:], w[:, :half]], axis=1)


def _rope_lanes(w):
    return jnp.pad(w, ((0, 0), (MLA_NOPE, LANE - MLA_NOPE - MLA_ROPE)))


def _layer_weights(w_in, w_uq, w_ukv, w_o_sb, w_o_fox):
    o = IN_OFFSETS
    seg = lambda n: w_in[:, o[n]:o[n + 1]]
    scale = HEAD_DIM ** -0.5
    w_kr = seg(2)
    w_cat = jnp.concatenate(
        [seg(0), seg(1), _rope_lanes(w_kr), _rope_lanes(_rope_partner(w_kr)),
         _pad_cols(seg(3) * scale, HEAD_PAD_W), _pad_cols(seg(4), HEAD_PAD_W), _pad_cols(seg(5), HEAD_PAD_W),
         _pad_cols(seg(6) * scale, HEAD_PAD_W), _pad_cols(seg(7), HEAD_PAD_W), _pad_cols(seg(8), HEAD_PAD_W),
         _pad_cols(seg(9), LANE)], axis=1).astype(BF16)
    w_gate = seg(10).astype(BF16)
    uq = w_uq.reshape(Q_LORA, MLA_HEADS, MLA_QK)
    nope, rope = uq[..., :MLA_NOPE], uq[..., MLA_NOPE:]
    zpad = jnp.zeros((Q_LORA, MLA_HEADS, LANE - MLA_QK), F32)
    wqa = jnp.concatenate([nope, rope, zpad], axis=-1).reshape(Q_LORA, MLA_HEADS * LANE).astype(BF16)
    partner = jnp.concatenate([-rope[..., MLA_ROPE // 2:], rope[..., :MLA_ROPE // 2]], axis=-1)
    wqb = jnp.concatenate([jnp.zeros_like(nope), partner, zpad], axis=-1).reshape(Q_LORA, MLA_HEADS * LANE).astype(BF16)
    ukv = w_ukv.reshape(KV_LORA, MLA_HEADS, MLA_NOPE + MLA_V)
    wk = jnp.pad(ukv[..., :MLA_NOPE], ((0, 0), (0, 0), (0, LANE - MLA_NOPE))).reshape(KV_LORA, MLA_HEADS * LANE).astype(BF16)
    wv = ukv[..., MLA_NOPE:].reshape(KV_LORA, MLA_OUT).astype(BF16)
    pad_rows = lambda w: jnp.pad(w, ((0, HEAD_PAD_W - w.shape[0]), (0, 0))).astype(BF16)
    return w_cat, w_gate, wqa, wqb, wk, wv, pad_rows(w_o_sb), pad_rows(w_o_fox)


def _tiles(b, s):
    t = b * s
    tm = min(512, t)
    tq = min(512, s)
    return t, tm, tq


def kernel(x, positions, norm1_g, w_in, mla_q_norm_g, w_uq, mla_kv_norm_g, w_ukv, fox_b_f, w_o_mla, w_o_sb,
           w_o_fox, b_gate, w_out, norm2_g, peer_w_q, peer_sub_keys, peer_u, peer_v, final_norm_g):
    b, s, d = x.shape
    assert d == D_MODEL and MLA_OUT == HEAD_PAD_W
    depth = w_in.shape[0]
    t, tm, tq = _tiles(b, s)
    assert t % tm == 0 and s % tq == 0 and tq % CHUNK == 0
    te = 8 * N_KEYS
    x2 = x.reshape(t, d)
    cos, sin = _rope_tables(positions, tm)
    gf = final_norm_g.reshape(1, d)
    for l in range(depth):
        w_cat, w_gate, wqa, wqb, wk, wv, wo_sb, wo_fox = _layer_weights(w_in[l], w_uq[l], w_ukv[l], w_o_sb[l], w_o_fox[l])
        g1 = norm1_g[l].reshape(1, d)
        mla_in, sb, fx, f = _inproj(x2, g1, w_cat, tm)
        q_a, k_a, v_a = _mla_prep(mla_in, cos, sin, mla_q_norm_g[l].reshape(1, Q_LORA), wqa, wqb,
                                  mla_kv_norm_g[l].reshape(1, KV_LORA), wk, wv, tm)
        bf_row = jnp.zeros((1, LANE), F32).at[0, :FOX_HEADS].set(fox_b_f[l])
        cum, cumt = _fox_prep(f.reshape(b, s, LANE), bf_row, min(256, s))
        hw = MLA_HEADS * LANE
        col_a, acc_a = _head_state(MLA_HEADS, tq)
        o_a = _attention(_mla_attn_kernel, q_a.reshape(b, s, hw), k_a.reshape(b, s, hw), v_a.reshape(b, s, MLA_OUT),
                         (0, 0, 0), [], [], [col_a, acc_a, acc_a], hw, MLA_OUT, tq, "mla_attn")
        sb3 = sb.reshape(b, s, QKV_W)
        fx3 = fx.reshape(b, s, QKV_W)
        col_b, acc_b = _head_state(SB_HEADS, tq)
        qm = pltpu.VMEM((SB_HEADS, tq, LANE), BF16)
        o_b = _attention(_sb_attn_kernel, sb3, sb3, sb3, (0, 1, 2), [], [],
                         [qm, pltpu.VMEM((tq, tq), BF16), col_b, acc_b], HEAD_PAD_W, HEAD_PAD_W, tq, "sb_attn")
        fox_specs = [pl.BlockSpec((1, tq, LANE), lambda bi, i: (bi, i, 0)),
                     pl.BlockSpec((1, 8, s), lambda bi, i: (bi, 0, 0))]
        o_c = _attention(_fox_attn_kernel, fx3, fx3, fx3, (0, 1, 2), [cum, cumt], fox_specs,
                         [qm, col_b, acc_b, acc_b], HEAD_PAD_W, HEAD_PAD_W, tq, "fox_attn")
        x2 = _merge(x2, g1, w_gate, b_gate[l].reshape(1, -1), o_a.reshape(t, MLA_OUT), o_b.reshape(t, HEAD_PAD_W),
                    o_c.reshape(t, HEAD_PAD_W), w_o_mla[l].astype(BF16), wo_sb, wo_fox, w_out[l].astype(BF16), tm)
        hn, s1, s2, tau, c1, m2 = _peer_route(x2, norm2_g[l].reshape(1, d), peer_w_q[l].astype(BF16),
                                              peer_sub_keys[l].astype(BF16), tm)
        x2 = _peer_dense(x2, hn, s1, s2, tau, c1, m2, peer_u[l].astype(BF16), peer_v[l].T.astype(BF16), gf, tm, te,
                         final_norm=(l == depth - 1))
    return x2.reshape(b, s, d)
```

```python
import functools

import numpy as np
import jax
import jax.numpy as jnp
from jax import lax
from jax.experimental import pallas as pl
from jax.experimental.pallas import tpu as pltpu

F32 = jnp.float32
BF16 = jnp.bfloat16

D_MODEL = 1024
CHUNK = 64
HEAD_DIM = 64
NORM_EPS = 1e-6
NEG_INF = -1e30
MLA_HEADS = 6
MLA_NOPE = 64
MLA_ROPE = 32
MLA_V = 64
Q_LORA = 256
KV_LORA = 128
ROPE_THETA = 10000.0
SB_HEADS = 5
FOX_HEADS = 5
N_BRANCH = 3
PEER_HEADS = 8
N_KEYS = 128
N_EXPERTS = N_KEYS * N_KEYS
D_KEY = 256
HALF_KEY = D_KEY // 2
PEER_TOPK = 16
MLA_QK = MLA_NOPE + MLA_ROPE
SB_W = SB_HEADS * HEAD_DIM
FOX_W = FOX_HEADS * HEAD_DIM
MLA_OUT = MLA_HEADS * MLA_V
IN_SPLITS = (Q_LORA, KV_LORA, MLA_ROPE, SB_W, SB_W, SB_W, FOX_W, FOX_W, FOX_W, FOX_HEADS, N_BRANCH * D_MODEL)
IN_OFFSETS = tuple(int(o) for o in np.cumsum((0,) + IN_SPLITS))

LANE = 128
HEAD_PAD_W = 384
MLA_IN_W = Q_LORA + KV_LORA + 2 * LANE
QKV_W = 3 * HEAD_PAD_W
IN_W = MLA_IN_W + 2 * QKV_W + LANE
VMEM_LIMIT = 48 * 1024 * 1024

_NT = (((1,), (1,)), ((), ()))


def _cparams(*sem):
    return pltpu.CompilerParams(dimension_semantics=sem, vmem_limit_bytes=VMEM_LIMIT)


def _rms(x, g):
    return x * lax.rsqrt(jnp.mean(x * x, axis=-1, keepdims=True) + NORM_EPS) * g


def _dot(a, b):
    return jnp.dot(a, b, preferred_element_type=F32)


def _dot_nt(a, b):
    return lax.dot_general(a, b, _NT, preferred_element_type=F32)


def _full(shape):
    return pl.BlockSpec(shape, lambda *_: (0,) * len(shape))


def _rope_kernel(pos_ref, inv_ref, cos_ref, sin_ref):
    ang = pos_ref[...].astype(F32) * inv_ref[...]
    cos_ref[...] = jnp.cos(ang)
    sin_ref[...] = jnp.sin(ang)


def _rope_tables(positions, tm):
    t = positions.size
    inv = ROPE_THETA ** (-jnp.arange(0, MLA_ROPE, 2, dtype=F32) / MLA_ROPE)
    inv_row = jnp.zeros((1, LANE), F32).at[0, MLA_NOPE:MLA_NOPE + MLA_ROPE].set(jnp.concatenate([inv, inv]))
    return pl.pallas_call(
        _rope_kernel,
        grid=(t // tm,),
        in_specs=[pl.BlockSpec((tm, 1), lambda i: (i, 0)), _full((1, LANE))],
        out_specs=[pl.BlockSpec((tm, LANE), lambda i: (i, 0))] * 2,
        out_shape=[jax.ShapeDtypeStruct((t, LANE), F32)] * 2,
        compiler_params=_cparams("parallel"),
        name="rope_tables",
    )(positions.reshape(t, 1), inv_row)


def _inproj_kernel(x_ref, g_ref, w_ref, mla_ref, sb_ref, fx_ref, f_ref):
    hn = _rms(x_ref[...], g_ref[...]).astype(BF16)
    o = 0
    for ref, width in ((mla_ref, MLA_IN_W), (sb_ref, QKV_W), (fx_ref, QKV_W), (f_ref, LANE)):
        ref[...] = _dot(hn, w_ref[:, o:o + width]).astype(ref.dtype)
        o += width


def _inproj(x2, g, w_cat, tm):
    t = x2.shape[0]
    row = lambda w: pl.BlockSpec((tm, w), lambda i: (i, 0))
    return pl.pallas_call(
        _inproj_kernel,
        grid=(t // tm,),
        in_specs=[row(D_MODEL), _full((1, D_MODEL)), _full((D_MODEL, IN_W))],
        out_specs=[row(MLA_IN_W), row(QKV_W), row(QKV_W), row(LANE)],
        out_shape=[jax.ShapeDtypeStruct((t, MLA_IN_W), F32), jax.ShapeDtypeStruct((t, QKV_W), BF16),
                   jax.ShapeDtypeStruct((t, QKV_W), BF16), jax.ShapeDtypeStruct((t, LANE), F32)],
        compiler_params=_cparams("parallel"),
        name="inproj",
    )(x2, g, w_cat)


def _mla_prep_kernel(in_ref, cos_ref, sin_ref, gq_ref, wqa_ref, wqb_ref, gkv_ref, wk_ref, wv_ref,
                     q_ref, k_ref, v_ref):
    cos = cos_ref[...]
    sin = sin_ref[...]
    qn = _rms(in_ref[:, :Q_LORA], gq_ref[...]).astype(BF16)
    kn = _rms(in_ref[:, Q_LORA:Q_LORA + KV_LORA], gkv_ref[...]).astype(BF16)
    o = Q_LORA + KV_LORA
    k_rot = in_ref[:, o:o + LANE] * cos + in_ref[:, o + LANE:o + 2 * LANE] * sin
    scale = MLA_QK ** -0.5
    for h in range(MLA_HEADS):
        sl = slice(h * LANE, (h + 1) * LANE)
        qa = _dot(qn, wqa_ref[:, sl])
        qb = _dot(qn, wqb_ref[:, sl])
        q_ref[:, sl] = ((qa * cos + qb * sin) * scale).astype(BF16)
        k_ref[:, sl] = (_dot(kn, wk_ref[:, sl]) + k_rot).astype(BF16)
    v_ref[...] = _dot(kn, wv_ref[...]).astype(BF16)


def _mla_prep(mla_in, cos, sin, gq, wqa, wqb, gkv, wk, wv, tm):
    t = mla_in.shape[0]
    row = lambda w: pl.BlockSpec((tm, w), lambda i: (i, 0))
    hw = MLA_HEADS * LANE
    return pl.pallas_call(
        _mla_prep_kernel,
        grid=(t // tm,),
        in_specs=[row(MLA_IN_W), row(LANE), row(LANE), _full((1, Q_LORA)), _full((Q_LORA, hw)),
                  _full((Q_LORA, hw)), _full((1, KV_LORA)), _full((KV_LORA, hw)), _full((KV_LORA, MLA_OUT))],
        out_specs=[row(hw), row(hw), row(MLA_OUT)],
        out_shape=[jax.ShapeDtypeStruct((t, hw), BF16), jax.ShapeDtypeStruct((t, hw), BF16),
                   jax.ShapeDtypeStruct((t, MLA_OUT), BF16)],
        compiler_params=_cparams("parallel"),
        name="mla_prep",
    )(mla_in, cos, sin, gq, wqa, wqb, gkv, wk, wv)


def _fox_prep_kernel(f_ref, bf_ref, cum_ref, cumt_ref, *, blk):
    s = f_ref.shape[1]
    row = lax.broadcasted_iota(jnp.int32, (blk, blk), 0)
    col = lax.broadcasted_iota(jnp.int32, (blk, blk), 1)
    tri = jnp.where(col <= row, 1.0, 0.0).astype(BF16)
    carry = jnp.zeros((1, LANE), F32)
    for b in range(s // blk):
        f = f_ref[0, b * blk:(b + 1) * blk, :] + bf_ref[...]
        lf = jnp.minimum(f, 0.0) - jnp.log1p(jnp.exp(-jnp.abs(f)))
        hi = lf.astype(BF16)
        r1 = lf - hi.astype(F32)
        mid = r1.astype(BF16)
        lo = (r1 - mid.astype(F32)).astype(BF16)
        c = _dot(tri, hi) + _dot(tri, mid) + _dot(tri, lo) + carry
        cum_ref[0, b * blk:(b + 1) * blk, :] = c
        cumt_ref[0, :, b * blk:(b + 1) * blk] = c.T[:8, :]
        carry = c[blk - 1:blk, :]


def _fox_prep(f3, bf_row, blk):
    b, s, _ = f3.shape
    return pl.pallas_call(
        functools.partial(_fox_prep_kernel, blk=blk),
        grid=(b,),
        in_specs=[pl.BlockSpec((1, s, LANE), lambda i: (i, 0, 0)), _full((1, LANE))],
        out_specs=[pl.BlockSpec((1, s, LANE), lambda i: (i, 0, 0)), pl.BlockSpec((1, 8, s), lambda i: (i, 0, 0))],
        out_shape=[jax.ShapeDtypeStruct((b, s, LANE), F32), jax.ShapeDtypeStruct((b, 8, s), F32)],
        compiler_params=_cparams("parallel"),
        name="fox_prep",
    )(f3, bf_row)


def _softmax_update(h, s, v, m_ref, l_ref, acc_ref):
    m_old = m_ref[h]
    m_new = jnp.maximum(m_old, jnp.max(s, axis=-1, keepdims=True))
    alpha = jnp.exp(m_old - m_new)
    p = jnp.exp(s - m_new)
    part = p[:, :LANE]
    for n in range(1, p.shape[1] // LANE):
        part = part + p[:, n * LANE:(n + 1) * LANE]
    l_ref[h] = alpha * l_ref[h] + part
    m_ref[h] = m_new
    acc_ref[h] = alpha * acc_ref[h] + _dot(p.astype(BF16), v)


def _softmax_reset(m_ref, l_ref, acc_ref):
    m_ref[...] = jnp.full(m_ref.shape, NEG_INF, F32)
    l_ref[...] = jnp.zeros(l_ref.shape, F32)
    acc_ref[...] = jnp.zeros(acc_ref.shape, F32)


def _half_mask(tq, half):
    lane = lax.broadcasted_iota(jnp.int32, (tq, LANE), 1)
    return (lane < HEAD_DIM) if half == 0 else (lane >= HEAD_DIM)


def _store_head_pairs(o_ref, heads, value_of):
    tq = o_ref.shape[1]
    lo_half = _half_mask(tq, 0)
    for hb in range(o_ref.shape[2] // LANE):
        lo = value_of(2 * hb)
        hi = value_of(2 * hb + 1) if 2 * hb + 1 < heads else jnp.zeros_like(lo)
        o_ref[0, :, hb * LANE:(hb + 1) * LANE] = jnp.where(lo_half, lo, hi).astype(o_ref.dtype)


def _masked_queries(q_ref, qm_ref, heads):
    tq = q_ref.shape[1]
    for h in range(heads):
        qf = q_ref[0, :, (h // 2) * LANE:(h // 2 + 1) * LANE].astype(F32)
        qm_ref[h] = jnp.where(_half_mask(tq, h % 2), qf, 0.0).astype(BF16)


def _block_iotas(tq):
    return lax.broadcasted_iota(jnp.int32, (tq, tq), 0), lax.broadcasted_iota(jnp.int32, (tq, tq), 1)


def _mla_attn_kernel(q_ref, k_ref, v_ref, o_ref, m_ref, l_ref, acc_ref, *, tq):
    i = pl.program_id(1)
    _softmax_reset(m_ref, l_ref, acc_ref)

    def block(j, diagonal):
        rows = pl.ds(pl.multiple_of(j * tq, tq), tq)
        def scores(h):
            sl = slice(h * LANE, (h + 1) * LANE)
            return _dot_nt(q_ref[0, :, sl], k_ref[0, rows, sl])

        s_next = scores(0)
        for h in range(MLA_HEADS):
            s = s_next
            if h + 1 < MLA_HEADS:
                s_next = scores(h + 1)
            if diagonal:
                r, c = _block_iotas(tq)
                s = jnp.where((c // CHUNK) <= (r // CHUNK), s, NEG_INF)
            _softmax_update(h, s, v_ref[0, rows, (h // 2) * LANE:(h // 2 + 1) * LANE], m_ref, l_ref, acc_ref)

    def body(j, carry):
        block(j, False)
        return carry

    lax.fori_loop(0, i, body, 0)
    block(i, True)
    _store_head_pairs(o_ref, MLA_HEADS, lambda h: acc_ref[h] / jnp.sum(l_ref[h], axis=-1, keepdims=True))


def _fox_attn_kernel(q_ref, k_ref, v_ref, cum_ref, cumt_ref, o_ref, qm_ref, m_ref, l_ref, acc_ref, *, tq):
    i = pl.program_id(1)
    _softmax_reset(m_ref, l_ref, acc_ref)
    _masked_queries(q_ref, qm_ref, FOX_HEADS)

    def block(j, diagonal):
        start = pl.multiple_of(j * tq, tq)
        rows = pl.ds(start, tq)
        def scores(h):
            return _dot_nt(qm_ref[h], k_ref[0, rows, (h // 2) * LANE:(h // 2 + 1) * LANE])

        s_next = scores(0)
        for h in range(FOX_HEADS):
            sl = slice((h // 2) * LANE, (h // 2 + 1) * LANE)
            s = s_next
            if h + 1 < FOX_HEADS:
                s_next = scores(h + 1)
            s = s + (cum_ref[0, :, h:h + 1] - cumt_ref[0, h:h + 1, rows])
            if diagonal:
                r, c = _block_iotas(tq)
                s = jnp.where(c <= r, s, NEG_INF)
            _softmax_update(h, s, v_ref[0, rows, sl], m_ref, l_ref, acc_ref)

    def body(j, carry):
        block(j, False)
        return carry

    lax.fori_loop(0, i, body, 0)
    block(i, True)
    _store_head_pairs(o_ref, FOX_HEADS, lambda h: acc_ref[h] / jnp.sum(l_ref[h], axis=-1, keepdims=True))


def _sb_attn_kernel(q_ref, k_ref, v_ref, o_ref, qm_ref, suffix_ref, rest_ref, acc_ref, *, tq):
    i = pl.program_id(1)
    r, c = _block_iotas(tq)
    suffix_ref[...] = jnp.where(r > c, 1.0, 0.0).astype(BF16)
    rest_ref[...] = jnp.zeros(rest_ref.shape, F32)
    acc_ref[...] = jnp.zeros(acc_ref.shape, F32)
    _masked_queries(q_ref, qm_ref, SB_HEADS)

    def block(j, diagonal):
        rows = pl.ds(pl.multiple_of(j * tq, tq), tq)
        def scores(h):
            return _dot_nt(qm_ref[h], k_ref[0, rows, (h // 2) * LANE:(h // 2 + 1) * LANE])

        z_next = scores(0)
        for h in range(SB_HEADS):
            sl = slice((h // 2) * LANE, (h // 2 + 1) * LANE)
            z = z_next
            if h + 1 < SB_HEADS:
                z_next = scores(h + 1)
            sp = jnp.log(1.0 + jnp.exp(-jnp.abs(z)))
            log_b = jnp.minimum(z, 0.0) - sp
            log_1m = log_b - z
            if diagonal:
                strict = _block_iotas(tq)[1] < _block_iotas(tq)[0]
                log_1m = jnp.where(strict, log_1m, 0.0)
            a = jnp.exp(log_b + _dot(log_1m.astype(BF16), suffix_ref[...]) + rest_ref[h])
            if diagonal:
                a = jnp.where(strict, a, 0.0)
            acc_ref[h] += _dot(a.astype(BF16), v_ref[0, rows, sl])
            rest_ref[h] += jnp.sum(log_1m, axis=-1, keepdims=True)

    block(i, True)

    def body(n, carry):
        block(i - 1 - n, False)
        return carry

    lax.fori_loop(0, i, body, 0)
    _store_head_pairs(o_ref, SB_HEADS, lambda h: acc_ref[h])


def _attention(kernel, q, k, v, cols, extra, extra_specs, scratch, qw, vw, tq, name):
    b, s, _ = q.shape
    blk = lambda w, n=0: pl.BlockSpec((1, tq, w), lambda bi, i: (bi, i, n))
    seq = lambda w, n: pl.BlockSpec((1, s, w), lambda bi, i: (bi, 0, n))
    return pl.pallas_call(
        functools.partial(kernel, tq=tq),
        grid=(b, s // tq),
        in_specs=[blk(qw, cols[0]), seq(qw, cols[1]), seq(vw, cols[2])] + extra_specs,
        out_specs=blk(vw),
        out_shape=jax.ShapeDtypeStruct((b, s, vw), BF16),
        scratch_shapes=scratch,
        compiler_params=_cparams("parallel", "parallel"),
        name=name,
    )(q, k, v, *extra)


def _head_state(heads, tq):
    col = pltpu.VMEM((heads, tq, 1), F32)
    return col, pltpu.VMEM((heads, tq, LANE), F32)


def _merge_kernel(x_ref, g_ref, wg_ref, bg_ref, oa_ref, ob_ref, oc_ref, wa_ref, wb_ref, wc_ref, wo_ref, out_ref):
    x = x_ref[...]
    hn = _rms(x, g_ref[...]).astype(BF16)
    y = None
    for n, (o_ref, w_ref) in enumerate(((oa_ref, wa_ref), (ob_ref, wb_ref), (oc_ref, wc_ref))):
        sl = slice(n * D_MODEL, (n + 1) * D_MODEL)
        gate = jax.nn.sigmoid(_dot(hn, wg_ref[:, sl]) + bg_ref[:, sl])
        term = gate * _dot(o_ref[...], w_ref[...])
        y = term if y is None else y + term
    out_ref[...] = x + _dot(y.astype(BF16), wo_ref[...])


def _merge(x2, g, wg, bg, oa, ob, oc, wa, wb, wc, wo, tm):
    t = x2.shape[0]
    row = lambda w: pl.BlockSpec((tm, w), lambda i: (i, 0))
    return pl.pallas_call(
        _merge_kernel,
        grid=(t // tm,),
        in_specs=[row(D_MODEL), _full((1, D_MODEL)), _full((D_MODEL, N_BRANCH * D_MODEL)),
                  _full((1, N_BRANCH * D_MODEL)), row(HEAD_PAD_W), row(HEAD_PAD_W), row(HEAD_PAD_W),
                  _full((HEAD_PAD_W, D_MODEL)), _full((HEAD_PAD_W, D_MODEL)), _full((HEAD_PAD_W, D_MODEL)),
                  _full((D_MODEL, D_MODEL))],
        out_specs=row(D_MODEL),
        out_shape=jax.ShapeDtypeStruct((t, D_MODEL), F32),
        compiler_params=_cparams("parallel"),
        name="merge",
    )(x2, g, wg, bg, oa, ob, oc, wa, wb, wc, wo)


def _kth_largest_rows(x, k):
    for _ in range(k - 1):
        x = jnp.where(x >= jnp.max(x, axis=0, keepdims=True), -jnp.inf, x)
    return jnp.max(x, axis=0, keepdims=True)


def _peer_route_kernel(x_ref, g_ref, wq_ref, keys_ref, h_ref, s1_ref, s2_ref, tau_ref, c1_ref, m2_ref,
                       q_ref, top_ref, cand_ref):
    hn = _rms(x_ref[...], g_ref[...]).astype(BF16)
    h_ref[...] = hn
    wide = 2 * D_KEY
    for n in range(PEER_HEADS * D_KEY // wide):
        q_ref[:, n * wide:(n + 1) * wide] = _dot(hn, wq_ref[:, n * wide:(n + 1) * wide]).astype(BF16)
    for h in range(PEER_HEADS):
        for side, s_ref in ((0, s1_ref), (1, s2_ref)):
            o = h * D_KEY + side * HALF_KEY
            s_ref[h] = _dot_nt(keys_ref[side], q_ref[:, o:o + HALF_KEY])

    def lane_tile(lt, carry):
        lanes = pl.ds(pl.multiple_of(lt * LANE, LANE), LANE)
        for h in range(PEER_HEADS):
            for side, s_ref in ((0, s1_ref), (1, s2_ref)):
                s = s_ref[h, :, lanes]
                for r in range(PEER_TOPK):
                    m = jnp.max(s, axis=0, keepdims=True)
                    top_ref[side, r:r + 1, :] = m
                    s = jnp.where(s >= m, -jnp.inf, s)
            cand_ref[0:PEER_TOPK, :] = top_ref[0, 0:1, :] + top_ref[1]
            for a in range(1, PEER_TOPK):
                cand_ref[8 + 8 * a:16 + 8 * a, :] = top_ref[0, a:a + 1, :] + top_ref[1, 0:8, :]
            cand = cand_ref[...]
            tau = _kth_largest_rows(cand, PEER_TOPK)
            m1 = top_ref[0, 0:1, :]
            m2 = top_ref[1, 0:1, :]
            zsum = jnp.sum(jnp.where(cand >= tau, jnp.exp(cand - (m1 + m2)), 0.0), axis=0, keepdims=True)
            tau_ref[h:h + 1, lanes] = tau
            c1_ref[h:h + 1, lanes] = m1 + jnp.log(zsum)
            m2_ref[h:h + 1, lanes] = m2
        return carry

    lax.fori_loop(0, x_ref.shape[0] // LANE, lane_tile, 0)


def _peer_route(x2, g, wq, keys, tm):
    t = x2.shape[0]
    row = pl.BlockSpec((tm, D_MODEL), lambda i: (i, 0))
    sc = pl.BlockSpec((PEER_HEADS, N_KEYS, tm), lambda i: (0, 0, i))
    small = pl.BlockSpec((PEER_HEADS, tm), lambda i: (0, i))
    return pl.pallas_call(
        _peer_route_kernel,
        grid=(t // tm,),
        in_specs=[row, _full((1, D_MODEL)), _full((D_MODEL, PEER_HEADS * D_KEY)), _full((2, N_KEYS, HALF_KEY))],
        out_specs=[row, sc, sc, small, small, small],
        out_shape=[jax.ShapeDtypeStruct((t, D_MODEL), BF16),
                   jax.ShapeDtypeStruct((PEER_HEADS, N_KEYS, t), F32),
                   jax.ShapeDtypeStruct((PEER_HEADS, N_KEYS, t), F32),
                   jax.ShapeDtypeStruct((PEER_HEADS, t), F32),
                   jax.ShapeDtypeStruct((PEER_HEADS, t), F32),
                   jax.ShapeDtypeStruct((PEER_HEADS, t), F32)],
        scratch_shapes=[pltpu.VMEM((tm, PEER_HEADS * D_KEY), BF16), pltpu.VMEM((2, PEER_TOPK, LANE), F32),
                        pltpu.VMEM((8 + 8 * PEER_TOPK, LANE), F32)],
        compiler_params=_cparams("parallel"),
        name="peer_route",
    )(x2, g, wq, keys)


def _peer_dense_step(c, parity, h_ref, s2_ref, u_ref, vt_ref, thr_ref, e1_ref, e2_ref, pre_ref, p_ref, acc_ref,
                     *, te, n_chunks):
    tm = h_ref.shape[0]
    rows_per_step = te // N_KEYS
    assert 2 * rows_per_step == 8
    part = 32
    score_buf, gate_buf, fold_buf = parity, 1 - parity, parity
    key_tile = pl.ds(pl.multiple_of((jnp.clip(c - 1, 0, n_chunks - 1) // 2) * 8, 8), 8)
    row0 = rows_per_step * (1 - parity)

    def score(half):
        rows = slice(half * te // 2, (half + 1) * te // 2)
        pre_ref[score_buf, rows, :] = _dot_nt(u_ref[rows, :], h_ref[...])

    def fold(quarter):
        rows = slice(quarter * D_MODEL // 4, (quarter + 1) * D_MODEL // 4)
        acc_ref[rows, :] += _dot(vt_ref[rows, :], p_ref[fold_buf])

    def gate(lt):
        lanes = slice(lt * LANE, (lt + 1) * LANE)
        for rp in range(N_KEYS // part):
            w = [None] * rows_per_step
            for h in range(PEER_HEADS):
                s2 = s2_ref[h, rp * part:(rp + 1) * part, lanes]
                e2 = e2_ref[h, rp * part:(rp + 1) * part, lanes]
                thr = thr_ref[h, key_tile, lanes]
                e1 = e1_ref[h, key_tile, lanes]
                for j in range(rows_per_step):
                    term = jnp.where(s2 >= thr[row0 + j:row0 + j + 1, :], e2, 0.0) * e1[row0 + j:row0 + j + 1, :]
                    w[j] = term if w[j] is None else w[j] + term
            for j in range(rows_per_step):
                rows = slice(j * N_KEYS + rp * part, j * N_KEYS + (rp + 1) * part)
                pre = pre_ref[gate_buf, rows, lanes]
                p = w[j] * (pre * (1.0 + lax.erf(pre * (0.5 ** 0.5))))
                p_ref[gate_buf, rows, lanes] = p.astype(BF16)

    assert tm // LANE == 4
    gate(0), score(0), fold(0), gate(1), fold(1), gate(2), score(1), fold(2), gate(3), fold(3)


def _peer_dense_kernel(x_ref, h_ref, s1_ref, s2_ref, tau_ref, c1_ref, m2_ref, u_ref, vt_ref, gf_ref, out_ref,
                       thr_ref, e1_ref, e2_ref, pre_ref, p_ref, acc_ref, *, te, n_chunks, final_norm):
    c = pl.program_id(1)

    @pl.when(c == 0)
    def _():
        acc_ref[...] = jnp.zeros_like(acc_ref)
        pre_ref[...] = jnp.zeros_like(pre_ref)
        p_ref[...] = jnp.zeros_like(p_ref)
        for h in range(PEER_HEADS):
            s1 = s1_ref[h]
            thr_ref[h] = tau_ref[h:h + 1, :] - s1
            e1_ref[h] = 0.5 * jnp.exp(s1 - c1_ref[h:h + 1, :])
            e2_ref[h] = jnp.exp(s2_ref[h] - m2_ref[h:h + 1, :])

    for parity in range(2):
        @pl.when(c % 2 == parity)
        def _(parity=parity):
            _peer_dense_step(c, parity, h_ref, s2_ref, u_ref, vt_ref, thr_ref, e1_ref, e2_ref, pre_ref, p_ref,
                             acc_ref, te=te, n_chunks=n_chunks)

    @pl.when(c == 0)
    def _():
        p_ref[1] = jnp.zeros(p_ref.shape[1:], BF16)

    @pl.when(c == n_chunks + 1)
    def _():
        y = x_ref[...] + acc_ref[...].T
        if final_norm:
            y = _rms(y, gf_ref[...])
        out_ref[...] = y


def _peer_dense(x2, hn, s1, s2, tau, c1, m2, u, vt, gf, tm, te, final_norm):
    t = x2.shape[0]
    n_chunks = N_EXPERTS // te
    row = pl.BlockSpec((tm, D_MODEL), lambda i, c: (i, 0))
    sc = pl.BlockSpec((PEER_HEADS, N_KEYS, tm), lambda i, c: (0, 0, i))
    small = pl.BlockSpec((PEER_HEADS, tm), lambda i, c: (0, i))
    big = pltpu.VMEM((PEER_HEADS, N_KEYS, tm), F32)
    return pl.pallas_call(
        functools.partial(_peer_dense_kernel, te=te, n_chunks=n_chunks, final_norm=final_norm),
        grid=(t // tm, n_chunks + 2),
        in_specs=[row, row, sc, sc, small, small, small,
                  pl.BlockSpec((te, D_MODEL), lambda i, c: (jnp.minimum(c, n_chunks - 1), 0)),
                  pl.BlockSpec((D_MODEL, te), lambda i, c: (0, jnp.clip(c - 2, 0, n_chunks - 1))),
                  pl.BlockSpec((1, D_MODEL), lambda i, c: (0, 0))],
        out_specs=row,
        out_shape=jax.ShapeDtypeStruct((t, D_MODEL), F32),
        scratch_shapes=[big, big, big, pltpu.VMEM((2, te, tm), F32), pltpu.VMEM((2, te, tm), BF16),
                        pltpu.VMEM((D_MODEL, tm), F32)],
        compiler_params=_cparams("parallel", "arbitrary"),
        name="peer_dense",
    )(x2, hn, s1, s2, tau, c1, m2, u, vt, gf)


def _pad_cols(w, width):
    return jnp.pad(w, ((0, 0), (0, width - w.shape[1])))


def _rope_partner(w):
    half = MLA_ROPE // 2
    return jnp.concatenate([-w[:, half:], w[:, :half]], axis=1)


def _rope_lanes(w):
    return jnp.pad(w, ((0, 0), (MLA_NOPE, LANE - MLA_NOPE - MLA_ROPE)))


def _layer_weights(w_in, w_uq, w_ukv, w_o_sb, w_o_fox):
    o = IN_OFFSETS
    seg = lambda n: w_in[:, o[n]:o[n + 1]]
    scale = HEAD_DIM ** -0.5
    w_kr = seg(2)
    w_cat = jnp.concatenate(
        [seg(0), seg(1), _rope_lanes(w_kr), _rope_lanes(_rope_partner(w_kr)),
         _pad_cols(seg(3) * scale, HEAD_PAD_W), _pad_cols(seg(4), HEAD_PAD_W), _pad_cols(seg(5), HEAD_PAD_W),
         _pad_cols(seg(6) * scale, HEAD_PAD_W), _pad_cols(seg(7), HEAD_PAD_W), _pad_cols(seg(8), HEAD_PAD_W),
         _pad_cols(seg(9), LANE)], axis=1).astype(BF16)
    w_gate = seg(10).astype(BF16)
    uq = w_uq.reshape(Q_LORA, MLA_HEADS, MLA_QK)
    nope, rope = uq[..., :MLA_NOPE], uq[..., MLA_NOPE:]
    zpad = jnp.zeros((Q_LORA, MLA_HEADS, LANE - MLA_QK), F32)
    wqa = jnp.concatenate([nope, rope, zpad], axis=-1).reshape(Q_LORA, MLA_HEADS * LANE).astype(BF16)
    partner = jnp.concatenate([-rope[..., MLA_ROPE // 2:], rope[..., :MLA_ROPE // 2]], axis=-1)
    wqb = jnp.concatenate([jnp.zeros_like(nope), partner, zpad], axis=-1).reshape(Q_LORA, MLA_HEADS * LANE).astype(BF16)
    ukv = w_ukv.reshape(KV_LORA, MLA_HEADS, MLA_NOPE + MLA_V)
    wk = jnp.pad(ukv[..., :MLA_NOPE], ((0, 0), (0, 0), (0, LANE - MLA_NOPE))).reshape(KV_LORA, MLA_HEADS * LANE).astype(BF16)
    wv = ukv[..., MLA_NOPE:].reshape(KV_LORA, MLA_OUT).astype(BF16)
    pad_rows = lambda w: jnp.pad(w, ((0, HEAD_PAD_W - w.shape[0]), (0, 0))).astype(BF16)
    return w_cat, w_gate, wqa, wqb, wk, wv, pad_rows(w_o_sb), pad_rows(w_o_fox)


def _tiles(b, s):
    t = b * s
    tm = min(512, t)
    tq = min(512, s)
    return t, tm, tq


def kernel(x, positions, norm1_g, w_in, mla_q_norm_g, w_uq, mla_kv_norm_g, w_ukv, fox_b_f, w_o_mla, w_o_sb,
           w_o_fox, b_gate, w_out, norm2_g, peer_w_q, peer_sub_keys, peer_u, peer_v, final_norm_g):
    b, s, d = x.shape
    assert d == D_MODEL and MLA_OUT == HEAD_PAD_W
    depth = w_in.shape[0]
    t, tm, tq = _tiles(b, s)
    assert t % tm == 0 and s % tq == 0 and tq % CHUNK == 0
    te = 4 * N_KEYS
    x2 = x.reshape(t, d)
    cos, sin = _rope_tables(positions, tm)
    gf = final_norm_g.reshape(1, d)
    for l in range(depth):
        w_cat, w_gate, wqa, wqb, wk, wv, wo_sb, wo_fox = _layer_weights(w_in[l], w_uq[l], w_ukv[l], w_o_sb[l], w_o_fox[l])
        g1 = norm1_g[l].reshape(1, d)
        mla_in, sb, fx, f = _inproj(x2, g1, w_cat, tm)
        q_a, k_a, v_a = _mla_prep(mla_in, cos, sin, mla_q_norm_g[l].reshape(1, Q_LORA), wqa, wqb,
                                  mla_kv_norm_g[l].reshape(1, KV_LORA), wk, wv, tm)
        bf_row = jnp.zeros((1, LANE), F32).at[0, :FOX_HEADS].set(fox_b_f[l])
        cum, cumt = _fox_prep(f.reshape(b, s, LANE), bf_row, min(256, s))
        hw = MLA_HEADS * LANE
        col_a, acc_a = _head_state(MLA_HEADS, tq)
        o_a = _attention(_mla_attn_kernel, q_a.reshape(b, s, hw), k_a.reshape(b, s, hw), v_a.reshape(b, s, MLA_OUT),
                         (0, 0, 0), [], [], [col_a, acc_a, acc_a], hw, MLA_OUT, tq, "mla_attn")
        sb3 = sb.reshape(b, s, QKV_W)
        fx3 = fx.reshape(b, s, QKV_W)
        col_b, acc_b = _head_state(SB_HEADS, tq)
        qm = pltpu.VMEM((SB_HEADS, tq, LANE), BF16)
        o_b = _attention(_sb_attn_kernel, sb3, sb3, sb3, (0, 1, 2), [], [],
                         [qm, pltpu.VMEM((tq, tq), BF16), col_b, acc_b], HEAD_PAD_W, HEAD_PAD_W, tq, "sb_attn")
        fox_specs = [pl.BlockSpec((1, tq, LANE), lambda bi, i: (bi, i, 0)),
                     pl.BlockSpec((1, 8, s), lambda bi, i: (bi, 0, 0))]
        o_c = _attention(_fox_attn_kernel, fx3, fx3, fx3, (0, 1, 2), [cum, cumt], fox_specs,
                         [qm, col_b, acc_b, acc_b], HEAD_PAD_W, HEAD_PAD_W, tq, "fox_attn")
        x2 = _merge(x2, g1, w_gate, b_gate[l].reshape(1, -1), o_a.reshape(t, MLA_OUT), o_b.reshape(t, HEAD_PAD_W),
                    o_c.reshape(t, HEAD_PAD_W), w_o_mla[l].astype(BF16), wo_sb, wo_fox, w_out[l].astype(BF16), tm)
        hn, s1, s2, tau, c1, m2 = _peer_route(x2, norm2_g[l].reshape(1, d), peer_w_q[l].astype(BF16),
                                              peer_sub_keys[l].astype(BF16), tm)
        x2 = _peer_dense(x2, hn, s1, s2, tau, c1, m2, peer_u[l].astype(BF16), peer_v[l].T.astype(BF16), gf, tm, te,
                         final_norm=(l == depth - 1))
    return x2.reshape(b, s, d)
```

```python
import functools

import numpy as np
import jax
import jax.numpy as jnp
from jax import lax
from jax.experimental import pallas as pl
from jax.experimental.pallas import tpu as pltpu

F32 = jnp.float32
BF16 = jnp.bfloat16

D_MODEL = 1024
CHUNK = 64
HEAD_DIM = 64
NORM_EPS = 1e-6
NEG_INF = -1e30
MLA_HEADS = 6
MLA_NOPE = 64
MLA_ROPE = 32
MLA_V = 64
Q_LORA = 256
KV_LORA = 128
ROPE_THETA = 10000.0
SB_HEADS = 5
FOX_HEADS = 5
N_BRANCH = 3
PEER_HEADS = 8
N_KEYS = 128
N_EXPERTS = N_KEYS * N_KEYS
D_KEY = 256
HALF_KEY = D_KEY // 2
PEER_TOPK = 16
MLA_QK = MLA_NOPE + MLA_ROPE
SB_W = SB_HEADS * HEAD_DIM
FOX_W = FOX_HEADS * HEAD_DIM
MLA_OUT = MLA_HEADS * MLA_V
IN_SPLITS = (Q_LORA, KV_LORA, MLA_ROPE, SB_W, SB_W, SB_W, FOX_W, FOX_W, FOX_W, FOX_HEADS, N_BRANCH * D_MODEL)
IN_OFFSETS = tuple(int(o) for o in np.cumsum((0,) + IN_SPLITS))

LANE = 128
HEAD_PAD_W = 384
MLA_IN_W = Q_LORA + KV_LORA + 2 * LANE
QKV_W = 3 * HEAD_PAD_W
IN_W = MLA_IN_W + 2 * QKV_W + LANE
VMEM_LIMIT = 48 * 1024 * 1024

LOG2E = 1.4426950408889634
_NT = (((1,), (1,)), ((), ()))


def _cparams(*sem):
    return pltpu.CompilerParams(dimension_semantics=sem, vmem_limit_bytes=VMEM_LIMIT)


def _rms(x, g):
    return x * lax.rsqrt(jnp.mean(x * x, axis=-1, keepdims=True) + NORM_EPS) * g


def _dot(a, b):
    return jnp.dot(a, b, preferred_element_type=F32)


def _dot_nt(a, b):
    return lax.dot_general(a, b, _NT, preferred_element_type=F32)


def _full(shape):
    return pl.BlockSpec(shape, lambda *_: (0,) * len(shape))


def _rope_kernel(pos_ref, inv_ref, cos_ref, sin_ref):
    ang = pos_ref[...].astype(F32) * inv_ref[...]
    cos_ref[...] = jnp.cos(ang)
    sin_ref[...] = jnp.sin(ang)


def _rope_tables(positions, tm):
    t = positions.size
    inv = ROPE_THETA ** (-jnp.arange(0, MLA_ROPE, 2, dtype=F32) / MLA_ROPE)
    inv_row = jnp.zeros((1, LANE), F32).at[0, MLA_NOPE:MLA_NOPE + MLA_ROPE].set(jnp.concatenate([inv, inv]))
    return pl.pallas_call(
        _rope_kernel,
        grid=(t // tm,),
        in_specs=[pl.BlockSpec((tm, 1), lambda i: (i, 0)), _full((1, LANE))],
        out_specs=[pl.BlockSpec((tm, LANE), lambda i: (i, 0))] * 2,
        out_shape=[jax.ShapeDtypeStruct((t, LANE), F32)] * 2,
        compiler_params=_cparams("parallel"),
        name="rope_tables",
    )(positions.reshape(t, 1), inv_row)


def _inproj_kernel(x_ref, g_ref, w_ref, mla_ref, sb_ref, fx_ref, f_ref):
    hn = _rms(x_ref[...], g_ref[...]).astype(BF16)
    o = 0
    for ref, width in ((mla_ref, MLA_IN_W), (sb_ref, QKV_W), (fx_ref, QKV_W), (f_ref, LANE)):
        ref[...] = _dot(hn, w_ref[:, o:o + width]).astype(ref.dtype)
        o += width


def _inproj(x2, g, w_cat, tm):
    t = x2.shape[0]
    row = lambda w: pl.BlockSpec((tm, w), lambda i: (i, 0))
    return pl.pallas_call(
        _inproj_kernel,
        grid=(t // tm,),
        in_specs=[row(D_MODEL), _full((1, D_MODEL)), _full((D_MODEL, IN_W))],
        out_specs=[row(MLA_IN_W), row(QKV_W), row(QKV_W), row(LANE)],
        out_shape=[jax.ShapeDtypeStruct((t, MLA_IN_W), F32), jax.ShapeDtypeStruct((t, QKV_W), BF16),
                   jax.ShapeDtypeStruct((t, QKV_W), BF16), jax.ShapeDtypeStruct((t, LANE), F32)],
        compiler_params=_cparams("parallel"),
        name="inproj",
    )(x2, g, w_cat)


def _mla_prep_kernel(in_ref, cos_ref, sin_ref, gq_ref, wqa_ref, wqb_ref, gkv_ref, wk_ref, wv_ref,
                     q_ref, k_ref, v_ref):
    cos = cos_ref[...]
    sin = sin_ref[...]
    qn = _rms(in_ref[:, :Q_LORA], gq_ref[...]).astype(BF16)
    kn = _rms(in_ref[:, Q_LORA:Q_LORA + KV_LORA], gkv_ref[...]).astype(BF16)
    o = Q_LORA + KV_LORA
    k_rot = in_ref[:, o:o + LANE] * cos + in_ref[:, o + LANE:o + 2 * LANE] * sin
    scale = MLA_QK ** -0.5 * LOG2E
    for h in range(MLA_HEADS):
        sl = slice(h * LANE, (h + 1) * LANE)
        qa = _dot(qn, wqa_ref[:, sl])
        qb = _dot(qn, wqb_ref[:, sl])
        q_ref[:, sl] = ((qa * cos + qb * sin) * scale).astype(BF16)
        k_ref[:, sl] = (_dot(kn, wk_ref[:, sl]) + k_rot).astype(BF16)
    v_ref[...] = _dot(kn, wv_ref[...]).astype(BF16)


def _mla_prep(mla_in, cos, sin, gq, wqa, wqb, gkv, wk, wv, tm):
    t = mla_in.shape[0]
    row = lambda w: pl.BlockSpec((tm, w), lambda i: (i, 0))
    hw = MLA_HEADS * LANE
    return pl.pallas_call(
        _mla_prep_kernel,
        grid=(t // tm,),
        in_specs=[row(MLA_IN_W), row(LANE), row(LANE), _full((1, Q_LORA)), _full((Q_LORA, hw)),
                  _full((Q_LORA, hw)), _full((1, KV_LORA)), _full((KV_LORA, hw)), _full((KV_LORA, MLA_OUT))],
        out_specs=[row(hw), row(hw), row(MLA_OUT)],
        out_shape=[jax.ShapeDtypeStruct((t, hw), BF16), jax.ShapeDtypeStruct((t, hw), BF16),
                   jax.ShapeDtypeStruct((t, MLA_OUT), BF16)],
        compiler_params=_cparams("parallel"),
        name="mla_prep",
    )(mla_in, cos, sin, gq, wqa, wqb, gkv, wk, wv)


def _fox_prep_kernel(f_ref, bf_ref, cum_ref, cumt_ref, *, blk):
    s = f_ref.shape[1]
    row = lax.broadcasted_iota(jnp.int32, (blk, blk), 0)
    col = lax.broadcasted_iota(jnp.int32, (blk, blk), 1)
    tri = jnp.where(col <= row, 1.0, 0.0).astype(BF16)
    carry = jnp.zeros((1, LANE), F32)
    for b in range(s // blk):
        f = f_ref[0, b * blk:(b + 1) * blk, :] + bf_ref[...]
        lf = (jnp.minimum(f, 0.0) - jnp.log1p(jnp.exp(-jnp.abs(f)))) * LOG2E
        hi = lf.astype(BF16)
        r1 = lf - hi.astype(F32)
        mid = r1.astype(BF16)
        lo = (r1 - mid.astype(F32)).astype(BF16)
        c = _dot(tri, hi) + _dot(tri, mid) + _dot(tri, lo) + carry
        cum_ref[0, b * blk:(b + 1) * blk, :] = c
        cumt_ref[0, :, b * blk:(b + 1) * blk] = c.T[:8, :]
        carry = c[blk - 1:blk, :]


def _fox_prep(f3, bf_row, blk):
    b, s, _ = f3.shape
    return pl.pallas_call(
        functools.partial(_fox_prep_kernel, blk=blk),
        grid=(b,),
        in_specs=[pl.BlockSpec((1, s, LANE), lambda i: (i, 0, 0)), _full((1, LANE))],
        out_specs=[pl.BlockSpec((1, s, LANE), lambda i: (i, 0, 0)), pl.BlockSpec((1, 8, s), lambda i: (i, 0, 0))],
        out_shape=[jax.ShapeDtypeStruct((b, s, LANE), F32), jax.ShapeDtypeStruct((b, 8, s), F32)],
        compiler_params=_cparams("parallel"),
        name="fox_prep",
    )(f3, bf_row)


def _softmax_update(h, s, v, m_ref, l_ref, acc_ref):
    m_old = m_ref[h]
    m_new = jnp.maximum(m_old, jnp.max(s, axis=-1, keepdims=True))
    alpha = jnp.exp2(m_old - m_new)
    p = [jnp.exp2(s[:, n * LANE:(n + 1) * LANE] - m_new) for n in range(s.shape[1] // LANE)]
    part = p[0]
    for p_n in p[1:]:
        part = part + p_n
    l_ref[h] = alpha * l_ref[h] + part
    m_ref[h] = m_new
    acc_ref[h] = alpha * acc_ref[h] + _dot(jnp.concatenate(p, axis=1).astype(BF16), v)


def _softmax_reset(m_ref, l_ref, acc_ref):
    m_ref[...] = jnp.full(m_ref.shape, NEG_INF, F32)
    l_ref[...] = jnp.zeros(l_ref.shape, F32)
    acc_ref[...] = jnp.zeros(acc_ref.shape, F32)


def _half_mask(tq, half):
    lane = lax.broadcasted_iota(jnp.int32, (tq, LANE), 1)
    return (lane < HEAD_DIM) if half == 0 else (lane >= HEAD_DIM)


def _store_head_pairs(o_ref, heads, value_of):
    tq = o_ref.shape[1]
    lo_half = _half_mask(tq, 0)
    for hb in range(o_ref.shape[2] // LANE):
        lo = value_of(2 * hb)
        hi = value_of(2 * hb + 1) if 2 * hb + 1 < heads else jnp.zeros_like(lo)
        o_ref[0, :, hb * LANE:(hb + 1) * LANE] = jnp.where(lo_half, lo, hi).astype(o_ref.dtype)


def _masked_queries(q_ref, qm_ref, heads):
    tq = q_ref.shape[1]
    for h in range(heads):
        qf = q_ref[0, :, (h // 2) * LANE:(h // 2 + 1) * LANE].astype(F32)
        qm_ref[h] = jnp.where(_half_mask(tq, h % 2), qf, 0.0).astype(BF16)


def _block_iotas(tq):
    return lax.broadcasted_iota(jnp.int32, (tq, tq), 0), lax.broadcasted_iota(jnp.int32, (tq, tq), 1)


def _mla_attn_kernel(q_ref, k_ref, v_ref, o_ref, m_ref, l_ref, acc_ref, *, tq):
    i = pl.program_id(1)
    _softmax_reset(m_ref, l_ref, acc_ref)

    def block(j, diagonal):
        rows = pl.ds(pl.multiple_of(j * tq, tq), tq)
        def scores(h):
            sl = slice(h * LANE, (h + 1) * LANE)
            return _dot_nt(q_ref[0, :, sl], k_ref[0, rows, sl])

        s_next = scores(0)
        for h in range(MLA_HEADS):
            s = s_next
            if h + 1 < MLA_HEADS:
                s_next = scores(h + 1)
            if diagonal:
                r, c = _block_iotas(tq)
                s = jnp.where((c // CHUNK) <= (r // CHUNK), s, NEG_INF)
            _softmax_update(h, s, v_ref[0, rows, (h // 2) * LANE:(h // 2 + 1) * LANE], m_ref, l_ref, acc_ref)

    def body(j, carry):
        block(j, False)
        return carry

    lax.fori_loop(0, i, body, 0)
    block(i, True)
    _store_head_pairs(o_ref, MLA_HEADS, lambda h: acc_ref[h] / jnp.sum(l_ref[h], axis=-1, keepdims=True))


def _fox_attn_kernel(q_ref, k_ref, v_ref, cum_ref, cumt_ref, o_ref, qm_ref, cq_ref, m_ref, l_ref, acc_ref, *, tq):
    i = pl.program_id(1)
    _softmax_reset(m_ref, l_ref, acc_ref)
    _masked_queries(q_ref, qm_ref, FOX_HEADS)
    for h in range(FOX_HEADS):
        cq_ref[h] = jnp.broadcast_to(cum_ref[0, :, h:h + 1], (tq, LANE))

    def block(j, diagonal):
        start = pl.multiple_of(j * tq, tq)
        rows = pl.ds(start, tq)
        def scores(h):
            return _dot_nt(qm_ref[h], k_ref[0, rows, (h // 2) * LANE:(h // 2 + 1) * LANE])

        s_next = scores(0)
        for h in range(FOX_HEADS):
            sl = slice((h // 2) * LANE, (h // 2 + 1) * LANE)
            s = s_next
            if h + 1 < FOX_HEADS:
                s_next = scores(h + 1)
            ck = cumt_ref[0, h:h + 1, rows]
            s = jnp.concatenate([s[:, n * LANE:(n + 1) * LANE] + (cq_ref[h] - ck[:, n * LANE:(n + 1) * LANE])
                                 for n in range(tq // LANE)], axis=1)
            if diagonal:
                r, c = _block_iotas(tq)
                s = jnp.where(c <= r, s, NEG_INF)
            _softmax_update(h, s, v_ref[0, rows, sl], m_ref, l_ref, acc_ref)

    def body(j, carry):
        block(j, False)
        return carry

    lax.fori_loop(0, i, body, 0)
    block(i, True)
    _store_head_pairs(o_ref, FOX_HEADS, lambda h: acc_ref[h] / jnp.sum(l_ref[h], axis=-1, keepdims=True))


def _sb_attn_kernel(q_ref, k_ref, v_ref, o_ref, qm_ref, suffix_ref, rest_ref, acc_ref, *, tq):
    i = pl.program_id(1)
    r, c = _block_iotas(tq)
    suffix_ref[...] = jnp.where(r > c, 1.0, 0.0).astype(BF16)
    rest_ref[...] = jnp.zeros(rest_ref.shape, F32)
    acc_ref[...] = jnp.zeros(acc_ref.shape, F32)
    _masked_queries(q_ref, qm_ref, SB_HEADS)

    def block(j, diagonal):
        rows = pl.ds(pl.multiple_of(j * tq, tq), tq)
        def scores(h):
            return _dot_nt(qm_ref[h], k_ref[0, rows, (h // 2) * LANE:(h // 2 + 1) * LANE])

        z_next = scores(0)
        for h in range(SB_HEADS):
            sl = slice((h // 2) * LANE, (h // 2 + 1) * LANE)
            z = z_next
            if h + 1 < SB_HEADS:
                z_next = scores(h + 1)
            log_b = jnp.minimum(z, 0.0) - jnp.log(1.0 + jnp.exp2(-jnp.abs(z))) * LOG2E
            log_1m = log_b - z
            if diagonal:
                strict = _block_iotas(tq)[1] < _block_iotas(tq)[0]
                log_1m = jnp.where(strict, log_1m, 0.0)
            log_a = log_b + _dot(log_1m.astype(BF16), suffix_ref[...])
            rest = rest_ref[h]
            a = jnp.concatenate([jnp.exp2(log_a[:, n * LANE:(n + 1) * LANE] + rest) for n in range(tq // LANE)], axis=1)
            if diagonal:
                a = jnp.where(strict, a, 0.0)
            acc_ref[h] += _dot(a.astype(BF16), v_ref[0, rows, sl])
            rest_ref[h] = rest + jnp.sum(log_1m, axis=-1, keepdims=True)

    block(i, True)

    def body(n, carry):
        block(i - 1 - n, False)
        return carry

    lax.fori_loop(0, i, body, 0)
    _store_head_pairs(o_ref, SB_HEADS, lambda h: acc_ref[h])


def _attention(kernel, q, k, v, cols, extra, extra_specs, scratch, qw, vw, tq, name):
    b, s, _ = q.shape
    blk = lambda w, n=0: pl.BlockSpec((1, tq, w), lambda bi, i: (bi, i, n))
    seq = lambda w, n: pl.BlockSpec((1, s, w), lambda bi, i: (bi, 0, n))
    return pl.pallas_call(
        functools.partial(kernel, tq=tq),
        grid=(b, s // tq),
        in_specs=[blk(qw, cols[0]), seq(qw, cols[1]), seq(vw, cols[2])] + extra_specs,
        out_specs=blk(vw),
        out_shape=jax.ShapeDtypeStruct((b, s, vw), BF16),
        scratch_shapes=scratch,
        compiler_params=_cparams("parallel", "parallel"),
        name=name,
    )(q, k, v, *extra)


def _head_state(heads, tq):
    return pltpu.VMEM((heads, tq, LANE), F32)


def _merge_kernel(x_ref, g_ref, wg_ref, bg_ref, oa_ref, ob_ref, oc_ref, wa_ref, wb_ref, wc_ref, wo_ref, out_ref):
    x = x_ref[...]
    hn = _rms(x, g_ref[...]).astype(BF16)
    y = None
    for n, (o_ref, w_ref) in enumerate(((oa_ref, wa_ref), (ob_ref, wb_ref), (oc_ref, wc_ref))):
        sl = slice(n * D_MODEL, (n + 1) * D_MODEL)
        gate = jax.nn.sigmoid(_dot(hn, wg_ref[:, sl]) + bg_ref[:, sl])
        term = gate * _dot(o_ref[...], w_ref[...])
        y = term if y is None else y + term
    out_ref[...] = x + _dot(y.astype(BF16), wo_ref[...])


def _merge(x2, g, wg, bg, oa, ob, oc, wa, wb, wc, wo, tm):
    t = x2.shape[0]
    row = lambda w: pl.BlockSpec((tm, w), lambda i: (i, 0))
    return pl.pallas_call(
        _merge_kernel,
        grid=(t // tm,),
        in_specs=[row(D_MODEL), _full((1, D_MODEL)), _full((D_MODEL, N_BRANCH * D_MODEL)),
                  _full((1, N_BRANCH * D_MODEL)), row(HEAD_PAD_W), row(HEAD_PAD_W), row(HEAD_PAD_W),
                  _full((HEAD_PAD_W, D_MODEL)), _full((HEAD_PAD_W, D_MODEL)), _full((HEAD_PAD_W, D_MODEL)),
                  _full((D_MODEL, D_MODEL))],
        out_specs=row(D_MODEL),
        out_shape=jax.ShapeDtypeStruct((t, D_MODEL), F32),
        compiler_params=_cparams("parallel"),
        name="merge",
    )(x2, g, wg, bg, oa, ob, oc, wa, wb, wc, wo)


def _kth_largest_rows(x, k):
    for _ in range(k - 1):
        x = jnp.where(x >= jnp.max(x, axis=0, keepdims=True), -jnp.inf, x)
    return jnp.max(x, axis=0, keepdims=True)


def _peer_route_kernel(x_ref, g_ref, wq_ref, keys_ref, h_ref, s1_ref, s2_ref, tau_ref, c1_ref, m2_ref,
                       q_ref, top_ref, cand_ref):
    hn = _rms(x_ref[...], g_ref[...]).astype(BF16)
    h_ref[...] = hn
    wide = 2 * D_KEY
    for n in range(PEER_HEADS * D_KEY // wide):
        q_ref[:, n * wide:(n + 1) * wide] = _dot(hn, wq_ref[:, n * wide:(n + 1) * wide]).astype(BF16)
    for h in range(PEER_HEADS):
        for side, s_ref in ((0, s1_ref), (1, s2_ref)):
            o = h * D_KEY + side * HALF_KEY
            s_ref[h] = _dot_nt(keys_ref[side], q_ref[:, o:o + HALF_KEY])

    def lane_tile(lt, carry):
        lanes = pl.ds(pl.multiple_of(lt * LANE, LANE), LANE)
        for h in range(PEER_HEADS):
            for side, s_ref in ((0, s1_ref), (1, s2_ref)):
                s = s_ref[h, :, lanes]
                for r in range(PEER_TOPK):
                    m = jnp.max(s, axis=0, keepdims=True)
                    top_ref[side, r:r + 1, :] = m
                    s = jnp.where(s >= m, -jnp.inf, s)
            cand_ref[0:PEER_TOPK, :] = top_ref[0, 0:1, :] + top_ref[1]
            for a in range(1, PEER_TOPK):
                cand_ref[8 + 8 * a:16 + 8 * a, :] = top_ref[0, a:a + 1, :] + top_ref[1, 0:8, :]
            cand = cand_ref[...]
            tau = _kth_largest_rows(cand, PEER_TOPK)
            m1 = top_ref[0, 0:1, :]
            m2 = top_ref[1, 0:1, :]
            zsum = jnp.sum(jnp.where(cand >= tau, jnp.exp(cand - (m1 + m2)), 0.0), axis=0, keepdims=True)
            tau_ref[h:h + 1, lanes] = tau
            c1_ref[h:h + 1, lanes] = m1 + jnp.log(zsum)
            m2_ref[h:h + 1, lanes] = m2
        return carry

    lax.fori_loop(0, x_ref.shape[0] // LANE, lane_tile, 0)


def _peer_route(x2, g, wq, keys, tm):
    t = x2.shape[0]
    row = pl.BlockSpec((tm, D_MODEL), lambda i: (i, 0))
    sc = pl.BlockSpec((PEER_HEADS, N_KEYS, tm), lambda i: (0, 0, i))
    small = pl.BlockSpec((PEER_HEADS, tm), lambda i: (0, i))
    return pl.pallas_call(
        _peer_route_kernel,
        grid=(t // tm,),
        in_specs=[row, _full((1, D_MODEL)), _full((D_MODEL, PEER_HEADS * D_KEY)), _full((2, N_KEYS, HALF_KEY))],
        out_specs=[row, sc, sc, small, small, small],
        out_shape=[jax.ShapeDtypeStruct((t, D_MODEL), BF16),
                   jax.ShapeDtypeStruct((PEER_HEADS, N_KEYS, t), F32),
                   jax.ShapeDtypeStruct((PEER_HEADS, N_KEYS, t), F32),
                   jax.ShapeDtypeStruct((PEER_HEADS, t), F32),
                   jax.ShapeDtypeStruct((PEER_HEADS, t), F32),
                   jax.ShapeDtypeStruct((PEER_HEADS, t), F32)],
        scratch_shapes=[pltpu.VMEM((tm, PEER_HEADS * D_KEY), BF16), pltpu.VMEM((2, PEER_TOPK, LANE), F32),
                        pltpu.VMEM((8 + 8 * PEER_TOPK, LANE), F32)],
        compiler_params=_cparams("parallel"),
        name="peer_route",
    )(x2, g, wq, keys)


def _peer_dense_step(c, parity, h_ref, s2_ref, u_ref, vt_ref, thr_ref, e1_ref, e2_ref, pre_ref, p_ref, acc_ref,
                     *, te, n_chunks):
    tm = h_ref.shape[0]
    rows_per_step = te // N_KEYS
    assert 2 * rows_per_step == 8
    part = 32
    score_buf, gate_buf, fold_buf = parity, 1 - parity, parity
    key_tile = pl.ds(pl.multiple_of((jnp.clip(c - 1, 0, n_chunks - 1) // 2) * 8, 8), 8)
    row0 = rows_per_step * (1 - parity)

    def score(half):
        rows = slice(half * te // 2, (half + 1) * te // 2)
        pre_ref[score_buf, rows, :] = _dot_nt(u_ref[rows, :], h_ref[...])

    def fold(quarter):
        rows = slice(quarter * D_MODEL // 4, (quarter + 1) * D_MODEL // 4)
        acc_ref[rows, :] += _dot(vt_ref[rows, :], p_ref[fold_buf])

    def gate(lt):
        lanes = slice(lt * LANE, (lt + 1) * LANE)
        for rp in range(N_KEYS // part):
            w = [None] * rows_per_step
            for h in range(PEER_HEADS):
                s2 = s2_ref[h, rp * part:(rp + 1) * part, lanes]
                e2 = e2_ref[h, rp * part:(rp + 1) * part, lanes]
                thr = thr_ref[h, key_tile, lanes]
                e1 = e1_ref[h, key_tile, lanes]
                for j in range(rows_per_step):
                    term = jnp.where(s2 >= thr[row0 + j:row0 + j + 1, :], e2, 0.0) * e1[row0 + j:row0 + j + 1, :]
                    w[j] = term if w[j] is None else w[j] + term
            for j in range(rows_per_step):
                rows = slice(j * N_KEYS + rp * part, j * N_KEYS + (rp + 1) * part)
                pre = pre_ref[gate_buf, rows, lanes]
                p = w[j] * (pre * (1.0 + lax.erf(pre * (0.5 ** 0.5))))
                p_ref[gate_buf, rows, lanes] = p.astype(BF16)

    assert tm // LANE == 4
    gate(0), score(0), fold(0), gate(1), fold(1), gate(2), score(1), fold(2), gate(3), fold(3)


def _peer_dense_kernel(x_ref, h_ref, s1_ref, s2_ref, tau_ref, c1_ref, m2_ref, u_ref, vt_ref, gf_ref, out_ref,
                       thr_ref, e1_ref, e2_ref, pre_ref, p_ref, acc_ref, *, te, n_chunks, final_norm):
    c = pl.program_id(1)

    @pl.when(c == 0)
    def _():
        acc_ref[...] = jnp.zeros_like(acc_ref)
        pre_ref[...] = jnp.zeros_like(pre_ref)
        p_ref[...] = jnp.zeros_like(p_ref)
        for h in range(PEER_HEADS):
            s1 = s1_ref[h]
            thr_ref[h] = tau_ref[h:h + 1, :] - s1
            e1_ref[h] = 0.5 * jnp.exp(s1 - c1_ref[h:h + 1, :])
            e2_ref[h] = jnp.exp(s2_ref[h] - m2_ref[h:h + 1, :])

    for parity in range(2):
        @pl.when(c % 2 == parity)
        def _(parity=parity):
            _peer_dense_step(c, parity, h_ref, s2_ref, u_ref, vt_ref, thr_ref, e1_ref, e2_ref, pre_ref, p_ref,
                             acc_ref, te=te, n_chunks=n_chunks)

    @pl.when(c == 0)
    def _():
        p_ref[1] = jnp.zeros(p_ref.shape[1:], BF16)

    @pl.when(c == n_chunks + 1)
    def _():
        y = x_ref[...] + acc_ref[...].T
        if final_norm:
            y = _rms(y, gf_ref[...])
        out_ref[...] = y


def _peer_dense(x2, hn, s1, s2, tau, c1, m2, u, vt, gf, tm, te, final_norm):
    t = x2.shape[0]
    n_chunks = N_EXPERTS // te
    row = pl.BlockSpec((tm, D_MODEL), lambda i, c: (i, 0))
    sc = pl.BlockSpec((PEER_HEADS, N_KEYS, tm), lambda i, c: (0, 0, i))
    small = pl.BlockSpec((PEER_HEADS, tm), lambda i, c: (0, i))
    big = pltpu.VMEM((PEER_HEADS, N_KEYS, tm), F32)
    return pl.pallas_call(
        functools.partial(_peer_dense_kernel, te=te, n_chunks=n_chunks, final_norm=final_norm),
        grid=(t // tm, n_chunks + 2),
        in_specs=[row, row, sc, sc, small, small, small,
                  pl.BlockSpec((te, D_MODEL), lambda i, c: (jnp.minimum(c, n_chunks - 1), 0)),
                  pl.BlockSpec((D_MODEL, te), lambda i, c: (0, jnp.clip(c - 2, 0, n_chunks - 1))),
                  pl.BlockSpec((1, D_MODEL), lambda i, c: (0, 0))],
        out_specs=row,
        out_shape=jax.ShapeDtypeStruct((t, D_MODEL), F32),
        scratch_shapes=[big, big, big, pltpu.VMEM((2, te, tm), F32), pltpu.VMEM((2, te, tm), BF16),
                        pltpu.VMEM((D_MODEL, tm), F32)],
        compiler_params=_cparams("parallel", "arbitrary"),
        name="peer_dense",
    )(x2, hn, s1, s2, tau, c1, m2, u, vt, gf)


def _pad_cols(w, width):
    return jnp.pad(w, ((0, 0), (0, width - w.shape[1])))


def _rope_partner(w):
    half = MLA_ROPE // 2
    return jnp.concatenate([-w[:, half:], w[:, :half]], axis=1)


def _rope_lanes(w):
    return jnp.pad(w, ((0, 0), (MLA_NOPE, LANE - MLA_NOPE - MLA_ROPE)))


def _layer_weights(w_in, w_uq, w_ukv, w_o_sb, w_o_fox):
    o = IN_OFFSETS
    seg = lambda n: w_in[:, o[n]:o[n + 1]]
    scale = HEAD_DIM ** -0.5 * LOG2E
    w_kr = seg(2)
    w_cat = jnp.concatenate(
        [seg(0), seg(1), _rope_lanes(w_kr), _rope_lanes(_rope_partner(w_kr)),
         _pad_cols(seg(3) * scale, HEAD_PAD_W), _pad_cols(seg(4), HEAD_PAD_W), _pad_cols(seg(5), HEAD_PAD_W),
         _pad_cols(seg(6) * scale, HEAD_PAD_W), _pad_cols(seg(7), HEAD_PAD_W), _pad_cols(seg(8), HEAD_PAD_W),
         _pad_cols(seg(9), LANE)], axis=1).astype(BF16)
    w_gate = seg(10).astype(BF16)
    uq = w_uq.reshape(Q_LORA, MLA_HEADS, MLA_QK)
    nope, rope = uq[..., :MLA_NOPE], uq[..., MLA_NOPE:]
    zpad = jnp.zeros((Q_LORA, MLA_HEADS, LANE - MLA_QK), F32)
    wqa = jnp.concatenate([nope, rope, zpad], axis=-1).reshape(Q_LORA, MLA_HEADS * LANE).astype(BF16)
    partner = jnp.concatenate([-rope[..., MLA_ROPE // 2:], rope[..., :MLA_ROPE // 2]], axis=-1)
    wqb = jnp.concatenate([jnp.zeros_like(nope), partner, zpad], axis=-1).reshape(Q_LORA, MLA_HEADS * LANE).astype(BF16)
    ukv = w_ukv.reshape(KV_LORA, MLA_HEADS, MLA_NOPE + MLA_V)
    wk = jnp.pad(ukv[..., :MLA_NOPE], ((0, 0), (0, 0), (0, LANE - MLA_NOPE))).reshape(KV_LORA, MLA_HEADS * LANE).astype(BF16)
    wv = ukv[..., MLA_NOPE:].reshape(KV_LORA, MLA_OUT).astype(BF16)
    pad_rows = lambda w: jnp.pad(w, ((0, HEAD_PAD_W - w.shape[0]), (0, 0))).astype(BF16)
    return w_cat, w_gate, wqa, wqb, wk, wv, pad_rows(w_o_sb), pad_rows(w_o_fox)


def _tiles(b, s):
    t = b * s
    tm = min(512, t)
    tq = min(512, s)
    return t, tm, tq


def kernel(x, positions, norm1_g, w_in, mla_q_norm_g, w_uq, mla_kv_norm_g, w_ukv, fox_b_f, w_o_mla, w_o_sb,
           w_o_fox, b_gate, w_out, norm2_g, peer_w_q, peer_sub_keys, peer_u, peer_v, final_norm_g):
    b, s, d = x.shape
    assert d == D_MODEL and MLA_OUT == HEAD_PAD_W
    depth = w_in.shape[0]
    t, tm, tq = _tiles(b, s)
    assert t % tm == 0 and s % tq == 0 and tq % CHUNK == 0
    te = 4 * N_KEYS
    x2 = x.reshape(t, d)
    cos, sin = _rope_tables(positions, tm)
    gf = final_norm_g.reshape(1, d)
    for l in range(depth):
        w_cat, w_gate, wqa, wqb, wk, wv, wo_sb, wo_fox = _layer_weights(w_in[l], w_uq[l], w_ukv[l], w_o_sb[l], w_o_fox[l])
        g1 = norm1_g[l].reshape(1, d)
        mla_in, sb, fx, f = _inproj(x2, g1, w_cat, tm)
        q_a, k_a, v_a = _mla_prep(mla_in, cos, sin, mla_q_norm_g[l].reshape(1, Q_LORA), wqa, wqb,
                                  mla_kv_norm_g[l].reshape(1, KV_LORA), wk, wv, tm)
        bf_row = jnp.zeros((1, LANE), F32).at[0, :FOX_HEADS].set(fox_b_f[l])
        cum, cumt = _fox_prep(f.reshape(b, s, LANE), bf_row, min(256, s))
        hw = MLA_HEADS * LANE
        acc_a = _head_state(MLA_HEADS, tq)
        o_a = _attention(_mla_attn_kernel, q_a.reshape(b, s, hw), k_a.reshape(b, s, hw), v_a.reshape(b, s, MLA_OUT),
                         (0, 0, 0), [], [], [acc_a, acc_a, acc_a], hw, MLA_OUT, tq, "mla_attn")
        sb3 = sb.reshape(b, s, QKV_W)
        fx3 = fx.reshape(b, s, QKV_W)
        acc_b = _head_state(SB_HEADS, tq)
        qm = pltpu.VMEM((SB_HEADS, tq, LANE), BF16)
        o_b = _attention(_sb_attn_kernel, sb3, sb3, sb3, (0, 1, 2), [], [],
                         [qm, pltpu.VMEM((tq, tq), BF16), acc_b, acc_b], HEAD_PAD_W, HEAD_PAD_W, tq, "sb_attn")
        fox_specs = [pl.BlockSpec((1, tq, LANE), lambda bi, i: (bi, i, 0)),
                     pl.BlockSpec((1, 8, s), lambda bi, i: (bi, 0, 0))]
        o_c = _attention(_fox_attn_kernel, fx3, fx3, fx3, (0, 1, 2), [cum, cumt], fox_specs,
                         [qm, acc_b, acc_b, acc_b, acc_b], HEAD_PAD_W, HEAD_PAD_W, tq, "fox_attn")
        x2 = _merge(x2, g1, w_gate, b_gate[l].reshape(1, -1), o_a.reshape(t, MLA_OUT), o_b.reshape(t, HEAD_PAD_W),
                    o_c.reshape(t, HEAD_PAD_W), w_o_mla[l].astype(BF16), wo_sb, wo_fox, w_out[l].astype(BF16), tm)
        hn, s1, s2, tau, c1, m2 = _peer_route(x2, norm2_g[l].reshape(1, d), peer_w_q[l].astype(BF16),
                                              peer_sub_keys[l].astype(BF16), tm)
        x2 = _peer_dense(x2, hn, s1, s2, tau, c1, m2, peer_u[l].astype(BF16), peer_v[l].T.astype(BF16), gf, tm, te,
                         final_norm=(l == depth - 1))
    return x2.reshape(b, s, d)
```

```python
import functools

import numpy as np
import jax
import jax.numpy as jnp
from jax import lax
from jax.experimental import pallas as pl
from jax.experimental.pallas import tpu as pltpu

F32 = jnp.float32
BF16 = jnp.bfloat16

D_MODEL = 1024
CHUNK = 64
HEAD_DIM = 64
NORM_EPS = 1e-6
NEG_INF = -1e30
MLA_HEADS = 6
MLA_NOPE = 64
MLA_ROPE = 32
MLA_V = 64
Q_LORA = 256
KV_LORA = 128
ROPE_THETA = 10000.0
SB_HEADS = 5
FOX_HEADS = 5
N_BRANCH = 3
PEER_HEADS = 8
N_KEYS = 128
N_EXPERTS = N_KEYS * N_KEYS
D_KEY = 256
HALF_KEY = D_KEY // 2
PEER_TOPK = 16
MLA_QK = MLA_NOPE + MLA_ROPE
SB_W = SB_HEADS * HEAD_DIM
FOX_W = FOX_HEADS * HEAD_DIM
MLA_OUT = MLA_HEADS * MLA_V
IN_SPLITS = (Q_LORA, KV_LORA, MLA_ROPE, SB_W, SB_W, SB_W, FOX_W, FOX_W, FOX_W, FOX_HEADS, N_BRANCH * D_MODEL)
IN_OFFSETS = tuple(int(o) for o in np.cumsum((0,) + IN_SPLITS))

LANE = 128
HEAD_PAD_W = 384
MLA_IN_W = Q_LORA + KV_LORA + 2 * LANE
QKV_W = 3 * HEAD_PAD_W
IN_W = MLA_IN_W + 2 * QKV_W + LANE
VMEM_LIMIT = 48 * 1024 * 1024

LOG2E = 1.4426950408889634
_NT = (((1,), (1,)), ((), ()))


def _cparams(*sem):
    return pltpu.CompilerParams(dimension_semantics=sem, vmem_limit_bytes=VMEM_LIMIT)


def _rms(x, g):
    return x * lax.rsqrt(jnp.mean(x * x, axis=-1, keepdims=True) + NORM_EPS) * g


def _dot(a, b):
    return jnp.dot(a, b, preferred_element_type=F32)


def _dot_nt(a, b):
    return lax.dot_general(a, b, _NT, preferred_element_type=F32)


def _full(shape):
    return pl.BlockSpec(shape, lambda *_: (0,) * len(shape))


def _rope_kernel(pos_ref, inv_ref, cos_ref, sin_ref):
    ang = pos_ref[...].astype(F32) * inv_ref[...]
    cos_ref[...] = jnp.cos(ang)
    sin_ref[...] = jnp.sin(ang)


def _rope_tables(positions, tm):
    t = positions.size
    inv = ROPE_THETA ** (-jnp.arange(0, MLA_ROPE, 2, dtype=F32) / MLA_ROPE)
    inv_row = jnp.zeros((1, LANE), F32).at[0, MLA_NOPE:MLA_NOPE + MLA_ROPE].set(jnp.concatenate([inv, inv]))
    return pl.pallas_call(
        _rope_kernel,
        grid=(t // tm,),
        in_specs=[pl.BlockSpec((tm, 1), lambda i: (i, 0)), _full((1, LANE))],
        out_specs=[pl.BlockSpec((tm, LANE), lambda i: (i, 0))] * 2,
        out_shape=[jax.ShapeDtypeStruct((t, LANE), F32)] * 2,
        compiler_params=_cparams("parallel"),
        name="rope_tables",
    )(positions.reshape(t, 1), inv_row)


def _inproj_kernel(x_ref, g_ref, w_ref, mla_ref, sb_ref, fx_ref, f_ref):
    hn = _rms(x_ref[...], g_ref[...]).astype(BF16)
    o = 0
    for ref, width in ((mla_ref, MLA_IN_W), (sb_ref, QKV_W), (fx_ref, QKV_W), (f_ref, LANE)):
        ref[...] = _dot(hn, w_ref[:, o:o + width]).astype(ref.dtype)
        o += width


def _inproj(x2, g, w_cat, tm):
    t = x2.shape[0]
    row = lambda w: pl.BlockSpec((tm, w), lambda i: (i, 0))
    return pl.pallas_call(
        _inproj_kernel,
        grid=(t // tm,),
        in_specs=[row(D_MODEL), _full((1, D_MODEL)), _full((D_MODEL, IN_W))],
        out_specs=[row(MLA_IN_W), row(QKV_W), row(QKV_W), row(LANE)],
        out_shape=[jax.ShapeDtypeStruct((t, MLA_IN_W), F32), jax.ShapeDtypeStruct((t, QKV_W), BF16),
                   jax.ShapeDtypeStruct((t, QKV_W), BF16), jax.ShapeDtypeStruct((t, LANE), F32)],
        compiler_params=_cparams("parallel"),
        name="inproj",
    )(x2, g, w_cat)


def _mla_prep_kernel(in_ref, cos_ref, sin_ref, gq_ref, wqa_ref, wqb_ref, gkv_ref, wk_ref, wv_ref,
                     q_ref, k_ref, v_ref):
    cos = cos_ref[...]
    sin = sin_ref[...]
    qn = _rms(in_ref[:, :Q_LORA], gq_ref[...]).astype(BF16)
    kn = _rms(in_ref[:, Q_LORA:Q_LORA + KV_LORA], gkv_ref[...]).astype(BF16)
    o = Q_LORA + KV_LORA
    k_rot = in_ref[:, o:o + LANE] * cos + in_ref[:, o + LANE:o + 2 * LANE] * sin
    scale = MLA_QK ** -0.5 * LOG2E
    for h in range(MLA_HEADS):
        sl = slice(h * LANE, (h + 1) * LANE)
        qa = _dot(qn, wqa_ref[:, sl])
        qb = _dot(qn, wqb_ref[:, sl])
        q_ref[:, sl] = ((qa * cos + qb * sin) * scale).astype(BF16)
        k_ref[:, sl] = (_dot(kn, wk_ref[:, sl]) + k_rot).astype(BF16)
    v_ref[...] = _dot(kn, wv_ref[...]).astype(BF16)


def _mla_prep(mla_in, cos, sin, gq, wqa, wqb, gkv, wk, wv, tm):
    t = mla_in.shape[0]
    row = lambda w: pl.BlockSpec((tm, w), lambda i: (i, 0))
    hw = MLA_HEADS * LANE
    return pl.pallas_call(
        _mla_prep_kernel,
        grid=(t // tm,),
        in_specs=[row(MLA_IN_W), row(LANE), row(LANE), _full((1, Q_LORA)), _full((Q_LORA, hw)),
                  _full((Q_LORA, hw)), _full((1, KV_LORA)), _full((KV_LORA, hw)), _full((KV_LORA, MLA_OUT))],
        out_specs=[row(hw), row(hw), row(MLA_OUT)],
        out_shape=[jax.ShapeDtypeStruct((t, hw), BF16), jax.ShapeDtypeStruct((t, hw), BF16),
                   jax.ShapeDtypeStruct((t, MLA_OUT), BF16)],
        compiler_params=_cparams("parallel"),
        name="mla_prep",
    )(mla_in, cos, sin, gq, wqa, wqb, gkv, wk, wv)


def _fox_prep_kernel(f_ref, bf_ref, cum_ref, cumt_ref, *, blk):
    s = f_ref.shape[1]
    row = lax.broadcasted_iota(jnp.int32, (blk, blk), 0)
    col = lax.broadcasted_iota(jnp.int32, (blk, blk), 1)
    tri = jnp.where(col <= row, 1.0, 0.0).astype(BF16)
    carry = jnp.zeros((1, LANE), F32)
    for b in range(s // blk):
        f = f_ref[0, b * blk:(b + 1) * blk, :] + bf_ref[...]
        lf = (jnp.minimum(f, 0.0) - jnp.log1p(jnp.exp(-jnp.abs(f)))) * LOG2E
        hi = lf.astype(BF16)
        r1 = lf - hi.astype(F32)
        mid = r1.astype(BF16)
        lo = (r1 - mid.astype(F32)).astype(BF16)
        c = _dot(tri, hi) + _dot(tri, mid) + _dot(tri, lo) + carry
        cum_ref[0, b * blk:(b + 1) * blk, :] = c
        cumt_ref[0, :, b * blk:(b + 1) * blk] = c.T[:8, :]
        carry = c[blk - 1:blk, :]


def _fox_prep(f3, bf_row, blk):
    b, s, _ = f3.shape
    return pl.pallas_call(
        functools.partial(_fox_prep_kernel, blk=blk),
        grid=(b,),
        in_specs=[pl.BlockSpec((1, s, LANE), lambda i: (i, 0, 0)), _full((1, LANE))],
        out_specs=[pl.BlockSpec((1, s, LANE), lambda i: (i, 0, 0)), pl.BlockSpec((1, 8, s), lambda i: (i, 0, 0))],
        out_shape=[jax.ShapeDtypeStruct((b, s, LANE), F32), jax.ShapeDtypeStruct((b, 8, s), F32)],
        compiler_params=_cparams("parallel"),
        name="fox_prep",
    )(f3, bf_row)


def _softmax_update(h, s, v, m_ref, l_ref, acc_ref):
    m_old = m_ref[h]
    m_new = jnp.maximum(m_old, jnp.max(s, axis=-1, keepdims=True))
    alpha = jnp.exp2(m_old - m_new)
    p = [jnp.exp2(s[:, n * LANE:(n + 1) * LANE] - m_new) for n in range(s.shape[1] // LANE)]
    part = p[0]
    for p_n in p[1:]:
        part = part + p_n
    l_ref[h] = alpha * l_ref[h] + part
    m_ref[h] = m_new
    acc_ref[h] = alpha * acc_ref[h] + _dot(jnp.concatenate(p, axis=1).astype(BF16), v)


def _softmax_reset(m_ref, l_ref, acc_ref):
    m_ref[...] = jnp.full(m_ref.shape, NEG_INF, F32)
    l_ref[...] = jnp.zeros(l_ref.shape, F32)
    acc_ref[...] = jnp.zeros(acc_ref.shape, F32)


def _half_mask(tq, half):
    lane = lax.broadcasted_iota(jnp.int32, (tq, LANE), 1)
    return (lane < HEAD_DIM) if half == 0 else (lane >= HEAD_DIM)


def _store_head_pairs(o_ref, heads, value_of):
    tq = o_ref.shape[1]
    lo_half = _half_mask(tq, 0)
    for hb in range(o_ref.shape[2] // LANE):
        lo = value_of(2 * hb)
        hi = value_of(2 * hb + 1) if 2 * hb + 1 < heads else jnp.zeros_like(lo)
        o_ref[0, :, hb * LANE:(hb + 1) * LANE] = jnp.where(lo_half, lo, hi).astype(o_ref.dtype)


def _masked_queries(q_ref, qm_ref, heads):
    tq = q_ref.shape[1]
    for h in range(heads):
        qf = q_ref[0, :, (h // 2) * LANE:(h // 2 + 1) * LANE].astype(F32)
        qm_ref[h] = jnp.where(_half_mask(tq, h % 2), qf, 0.0).astype(BF16)


def _block_iotas(tq):
    return lax.broadcasted_iota(jnp.int32, (tq, tq), 0), lax.broadcasted_iota(jnp.int32, (tq, tq), 1)


def _mla_attn_kernel(q_ref, k_ref, v_ref, o_ref, m_ref, l_ref, acc_ref, *, tq):
    i = pl.program_id(1)
    _softmax_reset(m_ref, l_ref, acc_ref)

    def block(j, diagonal):
        rows = pl.ds(pl.multiple_of(j * tq, tq), tq)
        def scores(h):
            sl = slice(h * LANE, (h + 1) * LANE)
            return _dot_nt(q_ref[0, :, sl], k_ref[0, rows, sl])

        s_next = scores(0)
        for h in range(MLA_HEADS):
            s = s_next
            if h + 1 < MLA_HEADS:
                s_next = scores(h + 1)
            if diagonal:
                r, c = _block_iotas(tq)
                s = jnp.where((c // CHUNK) <= (r // CHUNK), s, NEG_INF)
            _softmax_update(h, s, v_ref[0, rows, (h // 2) * LANE:(h // 2 + 1) * LANE], m_ref, l_ref, acc_ref)

    def body(j, carry):
        block(j, False)
        return carry

    lax.fori_loop(0, i, body, 0)
    block(i, True)
    _store_head_pairs(o_ref, MLA_HEADS, lambda h: acc_ref[h] / jnp.sum(l_ref[h], axis=-1, keepdims=True))


def _fox_attn_kernel(q_ref, k_ref, v_ref, cum_ref, cumt_ref, o_ref, qm_ref, cq_ref, m_ref, l_ref, acc_ref, *, tq):
    i = pl.program_id(1)
    _softmax_reset(m_ref, l_ref, acc_ref)
    _masked_queries(q_ref, qm_ref, FOX_HEADS)
    for h in range(FOX_HEADS):
        cq_ref[h] = jnp.broadcast_to(cum_ref[0, :, h:h + 1], (tq, LANE))

    def block(j, diagonal):
        start = pl.multiple_of(j * tq, tq)
        rows = pl.ds(start, tq)
        def scores(h):
            return _dot_nt(qm_ref[h], k_ref[0, rows, (h // 2) * LANE:(h // 2 + 1) * LANE])

        s_next = scores(0)
        for h in range(FOX_HEADS):
            sl = slice((h // 2) * LANE, (h // 2 + 1) * LANE)
            s = s_next
            if h + 1 < FOX_HEADS:
                s_next = scores(h + 1)
            ck = cumt_ref[0, h:h + 1, rows]
            s = jnp.concatenate([s[:, n * LANE:(n + 1) * LANE] + (cq_ref[h] - ck[:, n * LANE:(n + 1) * LANE])
                                 for n in range(tq // LANE)], axis=1)
            if diagonal:
                r, c = _block_iotas(tq)
                s = jnp.where(c <= r, s, NEG_INF)
            _softmax_update(h, s, v_ref[0, rows, sl], m_ref, l_ref, acc_ref)

    def body(j, carry):
        block(j, False)
        return carry

    lax.fori_loop(0, i, body, 0)
    block(i, True)
    _store_head_pairs(o_ref, FOX_HEADS, lambda h: acc_ref[h] / jnp.sum(l_ref[h], axis=-1, keepdims=True))


def _sb_attn_kernel(q_ref, k_ref, v_ref, o_ref, qm_ref, suffix_ref, rest_ref, acc_ref, *, tq):
    i = pl.program_id(1)
    r, c = _block_iotas(tq)
    suffix_ref[...] = jnp.where(r > c, 1.0, 0.0).astype(BF16)
    rest_ref[...] = jnp.zeros(rest_ref.shape, F32)
    acc_ref[...] = jnp.zeros(acc_ref.shape, F32)
    _masked_queries(q_ref, qm_ref, SB_HEADS)

    def block(j, diagonal):
        rows = pl.ds(pl.multiple_of(j * tq, tq), tq)
        def scores(h):
            return _dot_nt(qm_ref[h], k_ref[0, rows, (h // 2) * LANE:(h // 2 + 1) * LANE])

        z_next = scores(0)
        for h in range(SB_HEADS):
            sl = slice((h // 2) * LANE, (h // 2 + 1) * LANE)
            z = z_next
            if h + 1 < SB_HEADS:
                z_next = scores(h + 1)
            log_b = jnp.minimum(z, 0.0) - jnp.log(1.0 + jnp.exp2(-jnp.abs(z))) * LOG2E
            log_1m = log_b - z
            if diagonal:
                strict = _block_iotas(tq)[1] < _block_iotas(tq)[0]
                log_1m = jnp.where(strict, log_1m, 0.0)
            log_a = log_b + _dot(log_1m.astype(BF16), suffix_ref[...])
            rest = rest_ref[h]
            a = jnp.concatenate([jnp.exp2(log_a[:, n * LANE:(n + 1) * LANE] + rest) for n in range(tq // LANE)], axis=1)
            if diagonal:
                a = jnp.where(strict, a, 0.0)
            acc_ref[h] += _dot(a.astype(BF16), v_ref[0, rows, sl])
            rest_ref[h] = rest + jnp.sum(log_1m, axis=-1, keepdims=True)

    block(i, True)

    def body(n, carry):
        block(i - 1 - n, False)
        return carry

    lax.fori_loop(0, i, body, 0)
    _store_head_pairs(o_ref, SB_HEADS, lambda h: acc_ref[h])


def _attention(kernel, q, k, v, cols, extra, extra_specs, scratch, qw, vw, tq, name):
    b, s, _ = q.shape
    blk = lambda w, n=0: pl.BlockSpec((1, tq, w), lambda bi, i: (bi, i, n))
    seq = lambda w, n: pl.BlockSpec((1, s, w), lambda bi, i: (bi, 0, n))
    return pl.pallas_call(
        functools.partial(kernel, tq=tq),
        grid=(b, s // tq),
        in_specs=[blk(qw, cols[0]), seq(qw, cols[1]), seq(vw, cols[2])] + extra_specs,
        out_specs=blk(vw),
        out_shape=jax.ShapeDtypeStruct((b, s, vw), BF16),
        scratch_shapes=scratch,
        compiler_params=_cparams("parallel", "parallel"),
        name=name,
    )(q, k, v, *extra)


def _head_state(heads, tq):
    return pltpu.VMEM((heads, tq, LANE), F32)


def _merge_kernel(x_ref, g_ref, wg_ref, bg_ref, oa_ref, ob_ref, oc_ref, wa_ref, wb_ref, wc_ref, wo_ref, out_ref):
    x = x_ref[...]
    hn = _rms(x, g_ref[...]).astype(BF16)
    y = None
    for n, (o_ref, w_ref) in enumerate(((oa_ref, wa_ref), (ob_ref, wb_ref), (oc_ref, wc_ref))):
        sl = slice(n * D_MODEL, (n + 1) * D_MODEL)
        gate = jax.nn.sigmoid(_dot(hn, wg_ref[:, sl]) + bg_ref[:, sl])
        term = gate * _dot(o_ref[...], w_ref[...])
        y = term if y is None else y + term
    out_ref[...] = x + _dot(y.astype(BF16), wo_ref[...])


def _merge(x2, g, wg, bg, oa, ob, oc, wa, wb, wc, wo, tm):
    t = x2.shape[0]
    row = lambda w: pl.BlockSpec((tm, w), lambda i: (i, 0))
    return pl.pallas_call(
        _merge_kernel,
        grid=(t // tm,),
        in_specs=[row(D_MODEL), _full((1, D_MODEL)), _full((D_MODEL, N_BRANCH * D_MODEL)),
                  _full((1, N_BRANCH * D_MODEL)), row(HEAD_PAD_W), row(HEAD_PAD_W), row(HEAD_PAD_W),
                  _full((HEAD_PAD_W, D_MODEL)), _full((HEAD_PAD_W, D_MODEL)), _full((HEAD_PAD_W, D_MODEL)),
                  _full((D_MODEL, D_MODEL))],
        out_specs=row(D_MODEL),
        out_shape=jax.ShapeDtypeStruct((t, D_MODEL), F32),
        compiler_params=_cparams("parallel"),
        name="merge",
    )(x2, g, wg, bg, oa, ob, oc, wa, wb, wc, wo)


def _kth_largest_rows(x, k):
    for _ in range(k - 1):
        x = jnp.where(x >= jnp.max(x, axis=0, keepdims=True), -jnp.inf, x)
    return jnp.max(x, axis=0, keepdims=True)


def _peer_route_kernel(x_ref, g_ref, wq_ref, keys_ref, h_ref, rank2_ref, e2_ref, cnt_ref, e1_ref,
                       s1_ref, s2_ref, q_ref, top_ref, cand_ref):
    hn = _rms(x_ref[...], g_ref[...]).astype(BF16)
    h_ref[...] = hn
    wide = 2 * D_KEY
    for n in range(PEER_HEADS * D_KEY // wide):
        q_ref[:, n * wide:(n + 1) * wide] = _dot(hn, wq_ref[:, n * wide:(n + 1) * wide]).astype(BF16)
    for h in range(PEER_HEADS):
        for side, s_ref in ((0, s1_ref), (1, s2_ref)):
            o = h * D_KEY + side * HALF_KEY
            s_ref[h] = _dot_nt(keys_ref[side], q_ref[:, o:o + HALF_KEY])

    def lane_tile(lt, carry):
        lanes = pl.ds(pl.multiple_of(lt * LANE, LANE), LANE)
        for h in range(PEER_HEADS):
            rank2 = jnp.full((N_KEYS, LANE), float(PEER_TOPK), F32)
            for side, s_ref in ((0, s1_ref), (1, s2_ref)):
                s = s_ref[h, :, lanes]
                for r in range(PEER_TOPK):
                    m = jnp.max(s, axis=0, keepdims=True)
                    top_ref[side, r:r + 1, :] = m
                    hit = s >= m
                    if side == 1:
                        rank2 = jnp.where(hit, float(r), rank2)
                    s = jnp.where(hit, -jnp.inf, s)
            cand_ref[0:PEER_TOPK, :] = top_ref[0, 0:1, :] + top_ref[1]
            for a in range(1, PEER_TOPK):
                cand_ref[8 + 8 * a:16 + 8 * a, :] = top_ref[0, a:a + 1, :] + top_ref[1, 0:8, :]
            cand = cand_ref[...]
            tau = _kth_largest_rows(cand, PEER_TOPK)
            m1 = top_ref[0, 0:1, :]
            m2 = top_ref[1, 0:1, :]
            zsum = jnp.sum(jnp.where(cand >= tau, jnp.exp(cand - (m1 + m2)), 0.0), axis=0, keepdims=True)
            s1 = s1_ref[h, :, lanes]
            thr = tau - s1
            cnt = jnp.zeros((N_KEYS, LANE), F32)
            for b in range(PEER_TOPK):
                cnt = cnt + jnp.where(top_ref[1, b:b + 1, :] >= thr, 1.0, 0.0)
            rank2_ref[h, :, lanes] = rank2.astype(BF16)
            e2_ref[h, :, lanes] = jnp.exp(s2_ref[h, :, lanes] - m2).astype(BF16)
            cnt_ref[h, :, lanes] = cnt
            e1_ref[h, :, lanes] = 0.5 * jnp.exp(s1 - m1) / zsum
        return carry

    lax.fori_loop(0, x_ref.shape[0] // LANE, lane_tile, 0)


def _peer_route(x2, g, wq, keys, tm):
    t = x2.shape[0]
    row = pl.BlockSpec((tm, D_MODEL), lambda i: (i, 0))
    sc = pl.BlockSpec((PEER_HEADS, N_KEYS, tm), lambda i: (0, 0, i))
    return pl.pallas_call(
        _peer_route_kernel,
        grid=(t // tm,),
        in_specs=[row, _full((1, D_MODEL)), _full((D_MODEL, PEER_HEADS * D_KEY)), _full((2, N_KEYS, HALF_KEY))],
        out_specs=[row, sc, sc, sc, sc],
        out_shape=[jax.ShapeDtypeStruct((t, D_MODEL), BF16),
                   jax.ShapeDtypeStruct((PEER_HEADS, N_KEYS, t), BF16),
                   jax.ShapeDtypeStruct((PEER_HEADS, N_KEYS, t), BF16),
                   jax.ShapeDtypeStruct((PEER_HEADS, N_KEYS, t), F32),
                   jax.ShapeDtypeStruct((PEER_HEADS, N_KEYS, t), F32)],
        scratch_shapes=[pltpu.VMEM((PEER_HEADS, N_KEYS, tm), F32), pltpu.VMEM((PEER_HEADS, N_KEYS, tm), F32),
                        pltpu.VMEM((tm, PEER_HEADS * D_KEY), BF16), pltpu.VMEM((2, PEER_TOPK, LANE), F32),
                        pltpu.VMEM((8 + 8 * PEER_TOPK, LANE), F32)],
        compiler_params=_cparams("parallel"),
        name="peer_route",
    )(x2, g, wq, keys)


def _peer_dense_step(c, parity, h_ref, rank2_ref, e2_ref, cnt_ref, e1_ref, u_ref, vt_ref, pre_ref, p_ref, acc_ref,
                     *, te, n_chunks):
    tm = h_ref.shape[0]
    rows_per_step = te // N_KEYS
    assert 2 * rows_per_step == 8
    part = 64
    gate_w = 2 * LANE
    score_buf, gate_buf, fold_buf = parity, 1 - parity, parity
    key_tile = pl.ds(pl.multiple_of((jnp.clip(c - 1, 0, n_chunks - 1) // 2) * 8, 8), 8)
    row0 = rows_per_step * (1 - parity)

    def score(piece):
        rows = slice((piece // 2) * te // 2, (piece // 2 + 1) * te // 2)
        toks = slice((piece % 2) * gate_w, (piece % 2 + 1) * gate_w)
        pre_ref[score_buf, rows, toks] = _dot_nt(u_ref[rows, :], h_ref[toks, :])

    def fold(piece):
        rows = slice((piece // 2) * D_MODEL // 4, (piece // 2 + 1) * D_MODEL // 4)
        toks = slice((piece % 2) * gate_w, (piece % 2 + 1) * gate_w)
        acc_ref[rows, toks] += _dot(vt_ref[rows, :], p_ref[fold_buf, :, toks])

    def gate_heads(lt, rp, w, heads):
        lanes = slice(lt * gate_w, (lt + 1) * gate_w)
        for h in heads:
            rank2 = rank2_ref[h, rp * part:(rp + 1) * part, lanes]
            e2 = e2_ref[h, rp * part:(rp + 1) * part, lanes]
            cnt = cnt_ref[h, key_tile, lanes]
            e1 = e1_ref[h, key_tile, lanes]
            for j in range(rows_per_step):
                cnt_j = jnp.broadcast_to(cnt[row0 + j:row0 + j + 1, :], (part, gate_w)).astype(BF16)
                e1_j = jnp.broadcast_to(e1[row0 + j:row0 + j + 1, :], (part, gate_w)).astype(BF16)
                term = jnp.where(rank2 < cnt_j, e2, jnp.zeros_like(e2)) * e1_j
                w[j] = term if w[j] is None else w[j] + term

    def gate_store(lt, rp, w):
        lanes = slice(lt * gate_w, (lt + 1) * gate_w)
        for j in range(rows_per_step):
            rows = slice(j * N_KEYS + rp * part, j * N_KEYS + (rp + 1) * part)
            pre = pre_ref[gate_buf, rows, lanes]
            act = pre * (1.0 + lax.erf(pre * (0.5 ** 0.5)))
            p_ref[gate_buf, rows, lanes] = w[j] * act.astype(BF16)

    assert tm // gate_w == 2 and N_KEYS // part == 2
    mxu = [lambda n=n: score(n) for n in range(4)] + [lambda n=n: fold(n) for n in range(8)]
    order = [0, 4, 5, 1, 6, 7, 2, 8, 9, 3, 10, 11]
    pieces = iter([mxu[n] for n in order])
    for lt in range(2):
        for rp in range(2):
            w = [None] * rows_per_step
            for h in range(PEER_HEADS):
                gate_heads(lt, rp, w, (h,))
                if h % 3 == 1:
                    next(pieces)()
            gate_store(lt, rp, w)


def _peer_dense_kernel(x_ref, h_ref, rank2_ref, e2_ref, cnt_ref, e1_ref, u_ref, vt_ref, gf_ref, out_ref,
                       pre_ref, p_ref, acc_ref, *, te, n_chunks, final_norm):
    c = pl.program_id(1)

    @pl.when(c == 0)
    def _():
        acc_ref[...] = jnp.zeros_like(acc_ref)
        pre_ref[...] = jnp.zeros_like(pre_ref)
        p_ref[...] = jnp.zeros_like(p_ref)

    for parity in range(2):
        @pl.when(c % 2 == parity)
        def _(parity=parity):
            _peer_dense_step(c, parity, h_ref, rank2_ref, e2_ref, cnt_ref, e1_ref, u_ref, vt_ref, pre_ref, p_ref,
                             acc_ref, te=te, n_chunks=n_chunks)

    @pl.when(c == 0)
    def _():
        p_ref[1] = jnp.zeros(p_ref.shape[1:], BF16)

    @pl.when(c == n_chunks + 1)
    def _():
        y = x_ref[...] + acc_ref[...].T
        if final_norm:
            y = _rms(y, gf_ref[...])
        out_ref[...] = y


def _peer_dense(x2, hn, rank2, e2, cnt, e1, u, vt, gf, tm, te, final_norm):
    t = x2.shape[0]
    n_chunks = N_EXPERTS // te
    row = pl.BlockSpec((tm, D_MODEL), lambda i, c: (i, 0))
    sc = pl.BlockSpec((PEER_HEADS, N_KEYS, tm), lambda i, c: (0, 0, i))
    return pl.pallas_call(
        functools.partial(_peer_dense_kernel, te=te, n_chunks=n_chunks, final_norm=final_norm),
        grid=(t // tm, n_chunks + 2),
        in_specs=[row, row, sc, sc, sc, sc,
                  pl.BlockSpec((te, D_MODEL), lambda i, c: (jnp.minimum(c, n_chunks - 1), 0)),
                  pl.BlockSpec((D_MODEL, te), lambda i, c: (0, jnp.clip(c - 2, 0, n_chunks - 1))),
                  pl.BlockSpec((1, D_MODEL), lambda i, c: (0, 0))],
        out_specs=row,
        out_shape=jax.ShapeDtypeStruct((t, D_MODEL), F32),
        scratch_shapes=[pltpu.VMEM((2, te, tm), F32), pltpu.VMEM((2, te, tm), BF16),
                        pltpu.VMEM((D_MODEL, tm), F32)],
        compiler_params=_cparams("parallel", "arbitrary"),
        name="peer_dense",
    )(x2, hn, rank2, e2, cnt, e1, u, vt, gf)


def _pad_cols(w, width):
    return jnp.pad(w, ((0, 0), (0, width - w.shape[1])))


def _rope_partner(w):
    half = MLA_ROPE // 2
    return jnp.concatenate([-w[:, half:], w[:, :half]], axis=1)


def _rope_lanes(w):
    return jnp.pad(w, ((0, 0), (MLA_NOPE, LANE - MLA_NOPE - MLA_ROPE)))


def _layer_weights(w_in, w_uq, w_ukv, w_o_sb, w_o_fox):
    o = IN_OFFSETS
    seg = lambda n: w_in[:, o[n]:o[n + 1]]
    scale = HEAD_DIM ** -0.5 * LOG2E
    w_kr = seg(2)
    w_cat = jnp.concatenate(
        [seg(0), seg(1), _rope_lanes(w_kr), _rope_lanes(_rope_partner(w_kr)),
         _pad_cols(seg(3) * scale, HEAD_PAD_W), _pad_cols(seg(4), HEAD_PAD_W), _pad_cols(seg(5), HEAD_PAD_W),
         _pad_cols(seg(6) * scale, HEAD_PAD_W), _pad_cols(seg(7), HEAD_PAD_W), _pad_cols(seg(8), HEAD_PAD_W),
         _pad_cols(seg(9), LANE)], axis=1).astype(BF16)
    w_gate = seg(10).astype(BF16)
    uq = w_uq.reshape(Q_LORA, MLA_HEADS, MLA_QK)
    nope, rope = uq[..., :MLA_NOPE], uq[..., MLA_NOPE:]
    zpad = jnp.zeros((Q_LORA, MLA_HEADS, LANE - MLA_QK), F32)
    wqa = jnp.concatenate([nope, rope, zpad], axis=-1).reshape(Q_LORA, MLA_HEADS * LANE).astype(BF16)
    partner = jnp.concatenate([-rope[..., MLA_ROPE // 2:], rope[..., :MLA_ROPE // 2]], axis=-1)
    wqb = jnp.concatenate([jnp.zeros_like(nope), partner, zpad], axis=-1).reshape(Q_LORA, MLA_HEADS * LANE).astype(BF16)
    ukv = w_ukv.reshape(KV_LORA, MLA_HEADS, MLA_NOPE + MLA_V)
    wk = jnp.pad(ukv[..., :MLA_NOPE], ((0, 0), (0, 0), (0, LANE - MLA_NOPE))).reshape(KV_LORA, MLA_HEADS * LANE).astype(BF16)
    wv = ukv[..., MLA_NOPE:].reshape(KV_LORA, MLA_OUT).astype(BF16)
    pad_rows = lambda w: jnp.pad(w, ((0, HEAD_PAD_W - w.shape[0]), (0, 0))).astype(BF16)
    return w_cat, w_gate, wqa, wqb, wk, wv, pad_rows(w_o_sb), pad_rows(w_o_fox)


def _tiles(b, s):
    t = b * s
    tm = min(512, t)
    tq = min(512, s)
    return t, tm, tq


def kernel(x, positions, norm1_g, w_in, mla_q_norm_g, w_uq, mla_kv_norm_g, w_ukv, fox_b_f, w_o_mla, w_o_sb,
           w_o_fox, b_gate, w_out, norm2_g, peer_w_q, peer_sub_keys, peer_u, peer_v, final_norm_g):
    b, s, d = x.shape
    assert d == D_MODEL and MLA_OUT == HEAD_PAD_W
    depth = w_in.shape[0]
    t, tm, tq = _tiles(b, s)
    assert t % tm == 0 and s % tq == 0 and tq % CHUNK == 0
    te = 4 * N_KEYS
    x2 = x.reshape(t, d)
    cos, sin = _rope_tables(positions, tm)
    gf = final_norm_g.reshape(1, d)
    for l in range(depth):
        w_cat, w_gate, wqa, wqb, wk, wv, wo_sb, wo_fox = _layer_weights(w_in[l], w_uq[l], w_ukv[l], w_o_sb[l], w_o_fox[l])
        g1 = norm1_g[l].reshape(1, d)
        mla_in, sb, fx, f = _inproj(x2, g1, w_cat, tm)
        q_a, k_a, v_a = _mla_prep(mla_in, cos, sin, mla_q_norm_g[l].reshape(1, Q_LORA), wqa, wqb,
                                  mla_kv_norm_g[l].reshape(1, KV_LORA), wk, wv, tm)
        bf_row = jnp.zeros((1, LANE), F32).at[0, :FOX_HEADS].set(fox_b_f[l])
        cum, cumt = _fox_prep(f.reshape(b, s, LANE), bf_row, min(256, s))
        hw = MLA_HEADS * LANE
        acc_a = _head_state(MLA_HEADS, tq)
        o_a = _attention(_mla_attn_kernel, q_a.reshape(b, s, hw), k_a.reshape(b, s, hw), v_a.reshape(b, s, MLA_OUT),
                         (0, 0, 0), [], [], [acc_a, acc_a, acc_a], hw, MLA_OUT, tq, "mla_attn")
        sb3 = sb.reshape(b, s, QKV_W)
        fx3 = fx.reshape(b, s, QKV_W)
        acc_b = _head_state(SB_HEADS, tq)
        qm = pltpu.VMEM((SB_HEADS, tq, LANE), BF16)
        o_b = _attention(_sb_attn_kernel, sb3, sb3, sb3, (0, 1, 2), [], [],
                         [qm, pltpu.VMEM((tq, tq), BF16), acc_b, acc_b], HEAD_PAD_W, HEAD_PAD_W, tq, "sb_attn")
        fox_specs = [pl.BlockSpec((1, tq, LANE), lambda bi, i: (bi, i, 0)),
                     pl.BlockSpec((1, 8, s), lambda bi, i: (bi, 0, 0))]
        o_c = _attention(_fox_attn_kernel, fx3, fx3, fx3, (0, 1, 2), [cum, cumt], fox_specs,
                         [qm, acc_b, acc_b, acc_b, acc_b], HEAD_PAD_W, HEAD_PAD_W, tq, "fox_attn")
        x2 = _merge(x2, g1, w_gate, b_gate[l].reshape(1, -1), o_a.reshape(t, MLA_OUT), o_b.reshape(t, HEAD_PAD_W),
                    o_c.reshape(t, HEAD_PAD_W), w_o_mla[l].astype(BF16), wo_sb, wo_fox, w_out[l].astype(BF16), tm)
        hn, rank2, e2, cnt, e1 = _peer_route(x2, norm2_g[l].reshape(1, d), peer_w_q[l].astype(BF16),
                                             peer_sub_keys[l].astype(BF16), tm)
        x2 = _peer_dense(x2, hn, rank2, e2, cnt, e1, peer_u[l].astype(BF16), peer_v[l].T.astype(BF16), gf, tm, te,
                         final_norm=(l == depth - 1))
    return x2.reshape(b, s, d)
```

```python
import functools

import numpy as np
import jax
import jax.numpy as jnp
from jax import lax
from jax.experimental import pallas as pl
from jax.experimental.pallas import tpu as pltpu

F32 = jnp.float32
BF16 = jnp.bfloat16

D_MODEL = 1024
CHUNK = 64
HEAD_DIM = 64
NORM_EPS = 1e-6
NEG_INF = -1e30
MLA_HEADS = 6
MLA_NOPE = 64
MLA_ROPE = 32
MLA_V = 64
Q_LORA = 256
KV_LORA = 128
ROPE_THETA = 10000.0
SB_HEADS = 5
FOX_HEADS = 5
N_BRANCH = 3
PEER_HEADS = 8
N_KEYS = 128
N_EXPERTS = N_KEYS * N_KEYS
D_KEY = 256
HALF_KEY = D_KEY // 2
PEER_TOPK = 16
MLA_QK = MLA_NOPE + MLA_ROPE
SB_W = SB_HEADS * HEAD_DIM
FOX_W = FOX_HEADS * HEAD_DIM
MLA_OUT = MLA_HEADS * MLA_V
IN_SPLITS = (Q_LORA, KV_LORA, MLA_ROPE, SB_W, SB_W, SB_W, FOX_W, FOX_W, FOX_W, FOX_HEADS, N_BRANCH * D_MODEL)
IN_OFFSETS = tuple(int(o) for o in np.cumsum((0,) + IN_SPLITS))

LANE = 128
HEAD_PAD_W = 384
MLA_IN_W = Q_LORA + KV_LORA + 2 * LANE
QKV_W = 3 * HEAD_PAD_W
IN_W = MLA_IN_W + 2 * QKV_W + LANE
VMEM_LIMIT = 48 * 1024 * 1024

LOG2E = 1.4426950408889634
_NT = (((1,), (1,)), ((), ()))


def _cparams(*sem):
    return pltpu.CompilerParams(dimension_semantics=sem, vmem_limit_bytes=VMEM_LIMIT)


def _rms(x, g):
    return x * lax.rsqrt(jnp.mean(x * x, axis=-1, keepdims=True) + NORM_EPS) * g


def _dot(a, b):
    return jnp.dot(a, b, preferred_element_type=F32)


def _dot_nt(a, b):
    return lax.dot_general(a, b, _NT, preferred_element_type=F32)


def _full(shape):
    return pl.BlockSpec(shape, lambda *_: (0,) * len(shape))


def _rope_kernel(pos_ref, inv_ref, cos_ref, sin_ref):
    ang = pos_ref[...].astype(F32) * inv_ref[...]
    cos_ref[...] = jnp.cos(ang)
    sin_ref[...] = jnp.sin(ang)


def _rope_tables(positions, tm):
    t = positions.size
    inv = ROPE_THETA ** (-jnp.arange(0, MLA_ROPE, 2, dtype=F32) / MLA_ROPE)
    inv_row = jnp.zeros((1, LANE), F32).at[0, MLA_NOPE:MLA_NOPE + MLA_ROPE].set(jnp.concatenate([inv, inv]))
    return pl.pallas_call(
        _rope_kernel,
        grid=(t // tm,),
        in_specs=[pl.BlockSpec((tm, 1), lambda i: (i, 0)), _full((1, LANE))],
        out_specs=[pl.BlockSpec((tm, LANE), lambda i: (i, 0))] * 2,
        out_shape=[jax.ShapeDtypeStruct((t, LANE), F32)] * 2,
        compiler_params=_cparams("parallel"),
        name="rope_tables",
    )(positions.reshape(t, 1), inv_row)


def _inproj_kernel(x_ref, g_ref, w_ref, mla_ref, sb_ref, fx_ref, f_ref):
    hn = _rms(x_ref[...], g_ref[...]).astype(BF16)
    o = 0
    for ref, width in ((mla_ref, MLA_IN_W), (sb_ref, QKV_W), (fx_ref, QKV_W), (f_ref, LANE)):
        ref[...] = _dot(hn, w_ref[:, o:o + width]).astype(ref.dtype)
        o += width


def _inproj(x2, g, w_cat, tm):
    t = x2.shape[0]
    row = lambda w: pl.BlockSpec((tm, w), lambda i: (i, 0))
    return pl.pallas_call(
        _inproj_kernel,
        grid=(t // tm,),
        in_specs=[row(D_MODEL), _full((1, D_MODEL)), _full((D_MODEL, IN_W))],
        out_specs=[row(MLA_IN_W), row(QKV_W), row(QKV_W), row(LANE)],
        out_shape=[jax.ShapeDtypeStruct((t, MLA_IN_W), F32), jax.ShapeDtypeStruct((t, QKV_W), BF16),
                   jax.ShapeDtypeStruct((t, QKV_W), BF16), jax.ShapeDtypeStruct((t, LANE), F32)],
        compiler_params=_cparams("parallel"),
        name="inproj",
    )(x2, g, w_cat)


def _mla_prep_kernel(in_ref, cos_ref, sin_ref, gq_ref, wqa_ref, wqb_ref, gkv_ref, wk_ref, wv_ref,
                     q_ref, k_ref, v_ref):
    cos = cos_ref[...]
    sin = sin_ref[...]
    qn = _rms(in_ref[:, :Q_LORA], gq_ref[...]).astype(BF16)
    kn = _rms(in_ref[:, Q_LORA:Q_LORA + KV_LORA], gkv_ref[...]).astype(BF16)
    o = Q_LORA + KV_LORA
    k_rot = in_ref[:, o:o + LANE] * cos + in_ref[:, o + LANE:o + 2 * LANE] * sin
    scale = MLA_QK ** -0.5 * LOG2E
    for h in range(MLA_HEADS):
        sl = slice(h * LANE, (h + 1) * LANE)
        qa = _dot(qn, wqa_ref[:, sl])
        qb = _dot(qn, wqb_ref[:, sl])
        q_ref[:, sl] = ((qa * cos + qb * sin) * scale).astype(BF16)
        k_ref[:, sl] = (_dot(kn, wk_ref[:, sl]) + k_rot).astype(BF16)
    v_ref[...] = _dot(kn, wv_ref[...]).astype(BF16)


def _mla_prep(mla_in, cos, sin, gq, wqa, wqb, gkv, wk, wv, tm):
    t = mla_in.shape[0]
    row = lambda w: pl.BlockSpec((tm, w), lambda i: (i, 0))
    hw = MLA_HEADS * LANE
    return pl.pallas_call(
        _mla_prep_kernel,
        grid=(t // tm,),
        in_specs=[row(MLA_IN_W), row(LANE), row(LANE), _full((1, Q_LORA)), _full((Q_LORA, hw)),
                  _full((Q_LORA, hw)), _full((1, KV_LORA)), _full((KV_LORA, hw)), _full((KV_LORA, MLA_OUT))],
        out_specs=[row(hw), row(hw), row(MLA_OUT)],
        out_shape=[jax.ShapeDtypeStruct((t, hw), BF16), jax.ShapeDtypeStruct((t, hw), BF16),
                   jax.ShapeDtypeStruct((t, MLA_OUT), BF16)],
        compiler_params=_cparams("parallel"),
        name="mla_prep",
    )(mla_in, cos, sin, gq, wqa, wqb, gkv, wk, wv)


def _fox_prep_kernel(f_ref, bf_ref, cum_ref, cumt_ref, *, blk):
    s = f_ref.shape[1]
    row = lax.broadcasted_iota(jnp.int32, (blk, blk), 0)
    col = lax.broadcasted_iota(jnp.int32, (blk, blk), 1)
    tri = jnp.where(col <= row, 1.0, 0.0).astype(BF16)
    carry = jnp.zeros((1, LANE), F32)
    for b in range(s // blk):
        f = f_ref[0, b * blk:(b + 1) * blk, :] + bf_ref[...]
        lf = (jnp.minimum(f, 0.0) - jnp.log1p(jnp.exp(-jnp.abs(f)))) * LOG2E
        hi = lf.astype(BF16)
        r1 = lf - hi.astype(F32)
        mid = r1.astype(BF16)
        lo = (r1 - mid.astype(F32)).astype(BF16)
        c = _dot(tri, hi) + _dot(tri, mid) + _dot(tri, lo) + carry
        cum_ref[0, b * blk:(b + 1) * blk, :] = c
        cumt_ref[0, :, b * blk:(b + 1) * blk] = c.T[:8, :]
        carry = c[blk - 1:blk, :]


def _fox_prep(f3, bf_row, blk):
    b, s, _ = f3.shape
    return pl.pallas_call(
        functools.partial(_fox_prep_kernel, blk=blk),
        grid=(b,),
        in_specs=[pl.BlockSpec((1, s, LANE), lambda i: (i, 0, 0)), _full((1, LANE))],
        out_specs=[pl.BlockSpec((1, s, LANE), lambda i: (i, 0, 0)), pl.BlockSpec((1, 8, s), lambda i: (i, 0, 0))],
        out_shape=[jax.ShapeDtypeStruct((b, s, LANE), F32), jax.ShapeDtypeStruct((b, 8, s), F32)],
        compiler_params=_cparams("parallel"),
        name="fox_prep",
    )(f3, bf_row)


def _softmax_update(h, s, v, m_ref, l_ref, acc_ref):
    m_old = m_ref[h]
    m_new = jnp.maximum(m_old, jnp.max(s, axis=-1, keepdims=True))
    alpha = jnp.exp2(m_old - m_new)
    p = [jnp.exp2(s[:, n * LANE:(n + 1) * LANE] - m_new) for n in range(s.shape[1] // LANE)]
    part = p[0]
    for p_n in p[1:]:
        part = part + p_n
    l_ref[h] = alpha * l_ref[h] + part
    m_ref[h] = m_new
    acc_ref[h] = alpha * acc_ref[h] + _dot(jnp.concatenate(p, axis=1).astype(BF16), v)


def _softmax_reset(m_ref, l_ref, acc_ref):
    m_ref[...] = jnp.full(m_ref.shape, NEG_INF, F32)
    l_ref[...] = jnp.zeros(l_ref.shape, F32)
    acc_ref[...] = jnp.zeros(acc_ref.shape, F32)


def _half_mask(tq, half):
    lane = lax.broadcasted_iota(jnp.int32, (tq, LANE), 1)
    return (lane < HEAD_DIM) if half == 0 else (lane >= HEAD_DIM)


def _store_head_pairs(o_ref, heads, value_of):
    tq = o_ref.shape[1]
    lo_half = _half_mask(tq, 0)
    for hb in range(o_ref.shape[2] // LANE):
        lo = value_of(2 * hb)
        hi = value_of(2 * hb + 1) if 2 * hb + 1 < heads else jnp.zeros_like(lo)
        o_ref[0, :, hb * LANE:(hb + 1) * LANE] = jnp.where(lo_half, lo, hi).astype(o_ref.dtype)


def _masked_queries(q_ref, qm_ref, heads):
    tq = q_ref.shape[1]
    for h in range(heads):
        qf = q_ref[0, :, (h // 2) * LANE:(h // 2 + 1) * LANE].astype(F32)
        qm_ref[h] = jnp.where(_half_mask(tq, h % 2), qf, 0.0).astype(BF16)


def _block_iotas(tq):
    return lax.broadcasted_iota(jnp.int32, (tq, tq), 0), lax.broadcasted_iota(jnp.int32, (tq, tq), 1)


def _mla_attn_kernel(q_ref, k_ref, v_ref, o_ref, m_ref, l_ref, acc_ref, *, tq):
    i = pl.program_id(1)
    _softmax_reset(m_ref, l_ref, acc_ref)

    def block(j, diagonal):
        rows = pl.ds(pl.multiple_of(j * tq, tq), tq)
        def scores(h):
            sl = slice(h * LANE, (h + 1) * LANE)
            return _dot_nt(q_ref[0, :, sl], k_ref[0, rows, sl])

        s_next = scores(0)
        for h in range(MLA_HEADS):
            s = s_next
            if h + 1 < MLA_HEADS:
                s_next = scores(h + 1)
            if diagonal:
                r, c = _block_iotas(tq)
                s = jnp.where((c // CHUNK) <= (r // CHUNK), s, NEG_INF)
            _softmax_update(h, s, v_ref[0, rows, (h // 2) * LANE:(h // 2 + 1) * LANE], m_ref, l_ref, acc_ref)

    def body(j, carry):
        block(j, False)
        return carry

    lax.fori_loop(0, i, body, 0)
    block(i, True)
    _store_head_pairs(o_ref, MLA_HEADS, lambda h: acc_ref[h] / jnp.sum(l_ref[h], axis=-1, keepdims=True))


def _fox_attn_kernel(q_ref, k_ref, v_ref, cum_ref, cumt_ref, o_ref, qm_ref, cq_ref, m_ref, l_ref, acc_ref, *, tq):
    i = pl.program_id(1)
    _softmax_reset(m_ref, l_ref, acc_ref)
    _masked_queries(q_ref, qm_ref, FOX_HEADS)
    for h in range(FOX_HEADS):
        cq_ref[h] = jnp.broadcast_to(cum_ref[0, :, h:h + 1], (tq, LANE))

    def block(j, diagonal):
        start = pl.multiple_of(j * tq, tq)
        rows = pl.ds(start, tq)
        def scores(h):
            return _dot_nt(qm_ref[h], k_ref[0, rows, (h // 2) * LANE:(h // 2 + 1) * LANE])

        s_next = scores(0)
        for h in range(FOX_HEADS):
            sl = slice((h // 2) * LANE, (h // 2 + 1) * LANE)
            s = s_next
            if h + 1 < FOX_HEADS:
                s_next = scores(h + 1)
            ck = cumt_ref[0, h:h + 1, rows]
            s = jnp.concatenate([s[:, n * LANE:(n + 1) * LANE] + (cq_ref[h] - ck[:, n * LANE:(n + 1) * LANE])
                                 for n in range(tq // LANE)], axis=1)
            if diagonal:
                r, c = _block_iotas(tq)
                s = jnp.where(c <= r, s, NEG_INF)
            _softmax_update(h, s, v_ref[0, rows, sl], m_ref, l_ref, acc_ref)

    def body(j, carry):
        block(j, False)
        return carry

    lax.fori_loop(0, i, body, 0)
    block(i, True)
    _store_head_pairs(o_ref, FOX_HEADS, lambda h: acc_ref[h] / jnp.sum(l_ref[h], axis=-1, keepdims=True))


def _sb_attn_kernel(q_ref, k_ref, v_ref, o_ref, qm_ref, suffix_ref, rest_ref, acc_ref, *, tq):
    i = pl.program_id(1)
    sub = suffix_ref.shape[0]
    r, c = _block_iotas(sub)
    suffix_ref[...] = jnp.where(r > c, 1.0, 0.0).astype(BF16)
    rest_ref[...] = jnp.zeros(rest_ref.shape, F32)
    acc_ref[...] = jnp.zeros(acc_ref.shape, F32)
    _masked_queries(q_ref, qm_ref, SB_HEADS)

    def block(j, diagonal):
        rows = pl.ds(pl.multiple_of(j * tq, tq), tq)
        def scores(h):
            return _dot_nt(qm_ref[h], k_ref[0, rows, (h // 2) * LANE:(h // 2 + 1) * LANE])

        z_next = scores(0)
        for h in range(SB_HEADS):
            sl = slice((h // 2) * LANE, (h // 2 + 1) * LANE)
            z = z_next
            if h + 1 < SB_HEADS:
                z_next = scores(h + 1)
            log_b = jnp.minimum(z, 0.0) - jnp.log(1.0 + jnp.exp2(-jnp.abs(z))) * LOG2E
            log_1m = log_b - z
            if diagonal:
                strict = _block_iotas(tq)[1] < _block_iotas(tq)[0]
                log_1m = jnp.where(strict, log_1m, 0.0)
            later = rest_ref[h]
            a = [None] * (tq // sub)
            for k in reversed(range(tq // sub)):
                cols = slice(k * sub, (k + 1) * sub)
                within = _dot(log_1m[:, cols].astype(BF16), suffix_ref[...])
                a[k] = jnp.concatenate([jnp.exp2(log_b[:, cols][:, n * LANE:(n + 1) * LANE]
                                                 + within[:, n * LANE:(n + 1) * LANE] + later)
                                        for n in range(sub // LANE)], axis=1)
                later = later + jnp.sum(log_1m[:, cols], axis=-1, keepdims=True)
            a = jnp.concatenate(a, axis=1)
            if diagonal:
                a = jnp.where(strict, a, 0.0)
            acc_ref[h] += _dot(a.astype(BF16), v_ref[0, rows, sl])
            rest_ref[h] = later

    block(i, True)

    def body(n, carry):
        block(i - 1 - n, False)
        return carry

    lax.fori_loop(0, i, body, 0)
    _store_head_pairs(o_ref, SB_HEADS, lambda h: acc_ref[h])


def _attention(kernel, q, k, v, cols, extra, extra_specs, scratch, qw, vw, tq, name):
    b, s, _ = q.shape
    blk = lambda w, n=0: pl.BlockSpec((1, tq, w), lambda bi, i: (bi, i, n))
    seq = lambda w, n: pl.BlockSpec((1, s, w), lambda bi, i: (bi, 0, n))
    return pl.pallas_call(
        functools.partial(kernel, tq=tq),
        grid=(b, s // tq),
        in_specs=[blk(qw, cols[0]), seq(qw, cols[1]), seq(vw, cols[2])] + extra_specs,
        out_specs=blk(vw),
        out_shape=jax.ShapeDtypeStruct((b, s, vw), BF16),
        scratch_shapes=scratch,
        compiler_params=_cparams("parallel", "parallel"),
        name=name,
    )(q, k, v, *extra)


def _head_state(heads, tq):
    return pltpu.VMEM((heads, tq, LANE), F32)


def _merge_kernel(x_ref, g_ref, wg_ref, bg_ref, oa_ref, ob_ref, oc_ref, wa_ref, wb_ref, wc_ref, wo_ref, out_ref):
    x = x_ref[...]
    hn = _rms(x, g_ref[...]).astype(BF16)
    y = None
    for n, (o_ref, w_ref) in enumerate(((oa_ref, wa_ref), (ob_ref, wb_ref), (oc_ref, wc_ref))):
        sl = slice(n * D_MODEL, (n + 1) * D_MODEL)
        gate = jax.nn.sigmoid(_dot(hn, wg_ref[:, sl]) + bg_ref[:, sl])
        term = gate * _dot(o_ref[...], w_ref[...])
        y = term if y is None else y + term
    out_ref[...] = x + _dot(y.astype(BF16), wo_ref[...])


def _merge(x2, g, wg, bg, oa, ob, oc, wa, wb, wc, wo, tm):
    t = x2.shape[0]
    row = lambda w: pl.BlockSpec((tm, w), lambda i: (i, 0))
    return pl.pallas_call(
        _merge_kernel,
        grid=(t // tm,),
        in_specs=[row(D_MODEL), _full((1, D_MODEL)), _full((D_MODEL, N_BRANCH * D_MODEL)),
                  _full((1, N_BRANCH * D_MODEL)), row(HEAD_PAD_W), row(HEAD_PAD_W), row(HEAD_PAD_W),
                  _full((HEAD_PAD_W, D_MODEL)), _full((HEAD_PAD_W, D_MODEL)), _full((HEAD_PAD_W, D_MODEL)),
                  _full((D_MODEL, D_MODEL))],
        out_specs=row(D_MODEL),
        out_shape=jax.ShapeDtypeStruct((t, D_MODEL), F32),
        compiler_params=_cparams("parallel"),
        name="merge",
    )(x2, g, wg, bg, oa, ob, oc, wa, wb, wc, wo)


def _kth_largest_rows(x, k):
    for _ in range(k - 1):
        x = jnp.where(x >= jnp.max(x, axis=0, keepdims=True), -jnp.inf, x)
    return jnp.max(x, axis=0, keepdims=True)


def _peer_route_kernel(x_ref, g_ref, wq_ref, keys_ref, h_ref, rank2_ref, e2_ref, cnt_ref, e1_ref,
                       s1_ref, s2_ref, q_ref, top_ref, cand_ref):
    hn = _rms(x_ref[...], g_ref[...]).astype(BF16)
    h_ref[...] = hn
    wide = 2 * D_KEY
    for n in range(PEER_HEADS * D_KEY // wide):
        q_ref[:, n * wide:(n + 1) * wide] = _dot(hn, wq_ref[:, n * wide:(n + 1) * wide]).astype(BF16)
    for h in range(PEER_HEADS):
        for side, s_ref in ((0, s1_ref), (1, s2_ref)):
            o = h * D_KEY + side * HALF_KEY
            s_ref[h] = _dot_nt(keys_ref[side], q_ref[:, o:o + HALF_KEY])

    def lane_tile(lt, carry):
        lanes = pl.ds(pl.multiple_of(lt * LANE, LANE), LANE)
        for h in range(PEER_HEADS):
            rank2 = jnp.full((N_KEYS, LANE), float(PEER_TOPK), F32)
            for side, s_ref in ((0, s1_ref), (1, s2_ref)):
                s = s_ref[h, :, lanes]
                for r in range(PEER_TOPK):
                    m = jnp.max(s, axis=0, keepdims=True)
                    top_ref[side, r:r + 1, :] = m
                    hit = s >= m
                    if side == 1:
                        rank2 = jnp.where(hit, float(r), rank2)
                    s = jnp.where(hit, -jnp.inf, s)
            v2_top8 = top_ref[1, 0:8, :]
            sub = lax.broadcasted_iota(jnp.int32, (8, LANE), 0)
            v2_top4_twice = jnp.where(sub < 4, v2_top8, pltpu.roll(v2_top8, 4, axis=0))
            cand_ref[0:16, :] = top_ref[0, 0:1, :] + top_ref[1]
            for a in (1, 2, 3):
                cand_ref[8 + 8 * a:16 + 8 * a, :] = top_ref[0, a:a + 1, :] + v2_top8
            for n, a in enumerate((4, 6)):
                v1_pair = jnp.where(sub < 4, top_ref[0, a:a + 1, :], top_ref[0, a + 1:a + 2, :])
                cand_ref[40 + 8 * n:48 + 8 * n, :] = v1_pair + v2_top4_twice
            cand_ref[56:64, :] = top_ref[0, 8:16, :] + top_ref[1, 0:1, :]
            cand = cand_ref[...]
            tau = _kth_largest_rows(cand, PEER_TOPK)
            m1 = top_ref[0, 0:1, :]
            m2 = top_ref[1, 0:1, :]
            zsum = jnp.sum(jnp.where(cand >= tau, jnp.exp(cand - (m1 + m2)), 0.0), axis=0, keepdims=True)
            s1 = s1_ref[h, :, lanes]
            thr = tau - s1
            cnt = jnp.zeros((N_KEYS, LANE), F32)
            for b in range(PEER_TOPK):
                cnt = jnp.where(top_ref[1, b:b + 1, :] >= thr, float(b + 1), cnt)
            rank2_ref[h, :, lanes] = rank2.astype(BF16)
            e2_ref[h, :, lanes] = jnp.exp(s2_ref[h, :, lanes] - m2).astype(BF16)
            cnt_ref[h, :, lanes] = cnt
            e1_ref[h, :, lanes] = 0.5 * jnp.exp(s1 - m1) / zsum
        return carry

    lax.fori_loop(0, x_ref.shape[0] // LANE, lane_tile, 0)


def _peer_route(x2, g, wq, keys, tm):
    t = x2.shape[0]
    row = pl.BlockSpec((tm, D_MODEL), lambda i: (i, 0))
    sc = pl.BlockSpec((PEER_HEADS, N_KEYS, tm), lambda i: (0, 0, i))
    return pl.pallas_call(
        _peer_route_kernel,
        grid=(t // tm,),
        in_specs=[row, _full((1, D_MODEL)), _full((D_MODEL, PEER_HEADS * D_KEY)), _full((2, N_KEYS, HALF_KEY))],
        out_specs=[row, sc, sc, sc, sc],
        out_shape=[jax.ShapeDtypeStruct((t, D_MODEL), BF16),
                   jax.ShapeDtypeStruct((PEER_HEADS, N_KEYS, t), BF16),
                   jax.ShapeDtypeStruct((PEER_HEADS, N_KEYS, t), BF16),
                   jax.ShapeDtypeStruct((PEER_HEADS, N_KEYS, t), F32),
                   jax.ShapeDtypeStruct((PEER_HEADS, N_KEYS, t), F32)],
        scratch_shapes=[pltpu.VMEM((PEER_HEADS, N_KEYS, tm), F32), pltpu.VMEM((PEER_HEADS, N_KEYS, tm), F32),
                        pltpu.VMEM((tm, PEER_HEADS * D_KEY), BF16), pltpu.VMEM((2, PEER_TOPK, LANE), F32),
                        pltpu.VMEM((4 * PEER_TOPK, LANE), F32)],
        compiler_params=_cparams("parallel"),
        name="peer_route",
    )(x2, g, wq, keys)


def _peer_dense_step(c, parity, h_ref, rank2_ref, e2_ref, cnt_ref, e1_ref, u_ref, vt_ref, pre_ref, p_ref, acc_ref,
                     *, te, n_chunks):
    tm = h_ref.shape[0]
    rows_per_step = te // N_KEYS
    assert 2 * rows_per_step == 8
    part = 64
    gate_w = 2 * LANE
    score_buf, gate_buf, fold_buf = parity, 1 - parity, parity
    key_tile = pl.ds(pl.multiple_of((jnp.clip(c - 1, 0, n_chunks - 1) // 2) * 8, 8), 8)
    row0 = rows_per_step * (1 - parity)

    def score(piece):
        rows = slice((piece // 2) * te // 2, (piece // 2 + 1) * te // 2)
        toks = slice((piece % 2) * gate_w, (piece % 2 + 1) * gate_w)
        pre_ref[score_buf, rows, toks] = _dot_nt(u_ref[rows, :], h_ref[toks, :])

    def fold(piece):
        rows = slice((piece // 2) * D_MODEL // 4, (piece // 2 + 1) * D_MODEL // 4)
        toks = slice((piece % 2) * gate_w, (piece % 2 + 1) * gate_w)
        acc_ref[rows, toks] += _dot(vt_ref[0, rows, :], p_ref[fold_buf, :, toks])

    def gate_heads(lt, rp, w, heads):
        lanes = slice(lt * gate_w, (lt + 1) * gate_w)
        for h in heads:
            rank2 = rank2_ref[h, rp * part:(rp + 1) * part, lanes]
            e2 = e2_ref[h, rp * part:(rp + 1) * part, lanes]
            cnt = cnt_ref[h, key_tile, lanes]
            e1 = e1_ref[h, key_tile, lanes]
            for j in range(rows_per_step):
                cnt_j = jnp.broadcast_to(cnt[row0 + j:row0 + j + 1, :], (part, gate_w)).astype(BF16)
                e1_j = jnp.broadcast_to(e1[row0 + j:row0 + j + 1, :], (part, gate_w)).astype(BF16)
                term = jnp.where(rank2 < cnt_j, e2, jnp.zeros_like(e2)) * e1_j
                w[j] = term if w[j] is None else w[j] + term

    def gate_store(lt, rp, w):
        lanes = slice(lt * gate_w, (lt + 1) * gate_w)
        for j in range(rows_per_step):
            rows = slice(j * N_KEYS + rp * part, j * N_KEYS + (rp + 1) * part)
            pre = pre_ref[gate_buf, rows, lanes]
            act = pre * (1.0 + lax.erf(pre * (0.5 ** 0.5)))
            p_ref[gate_buf, rows, lanes] = w[j] * act.astype(BF16)

    assert tm // gate_w == 2 and N_KEYS // part == 2
    mxu = [lambda n=n: score(n) for n in range(4)] + [lambda n=n: fold(n) for n in range(8)]
    order = [0, 4, 5, 1, 6, 7, 2, 8, 9, 3, 10, 11]
    pieces = iter([mxu[n] for n in order])
    for lt in range(2):
        for rp in range(2):
            w = [None] * rows_per_step
            for h in range(PEER_HEADS):
                gate_heads(lt, rp, w, (h,))
                if h % 3 == 1:
                    next(pieces)()
            gate_store(lt, rp, w)


def _peer_dense_kernel(x_ref, h_ref, rank2_ref, e2_ref, cnt_ref, e1_ref, u_ref, vt_ref, gf_ref, out_ref,
                       pre_ref, p_ref, acc_ref, *, te, n_chunks, final_norm):
    c = pl.program_id(1)

    @pl.when(c == 0)
    def _():
        acc_ref[...] = jnp.zeros_like(acc_ref)
        pre_ref[...] = jnp.zeros_like(pre_ref)
        p_ref[...] = jnp.zeros_like(p_ref)

    for parity in range(2):
        @pl.when(c % 2 == parity)
        def _(parity=parity):
            _peer_dense_step(c, parity, h_ref, rank2_ref, e2_ref, cnt_ref, e1_ref, u_ref, vt_ref, pre_ref, p_ref,
                             acc_ref, te=te, n_chunks=n_chunks)

    @pl.when(c == 0)
    def _():
        p_ref[1] = jnp.zeros(p_ref.shape[1:], BF16)

    @pl.when(c == n_chunks + 1)
    def _():
        y = x_ref[...] + acc_ref[...].T
        if final_norm:
            y = _rms(y, gf_ref[...])
        out_ref[...] = y


def _peer_dense(x2, hn, rank2, e2, cnt, e1, u, vt, gf, tm, te, final_norm):
    t = x2.shape[0]
    n_chunks = N_EXPERTS // te
    row = pl.BlockSpec((tm, D_MODEL), lambda i, c: (i, 0))
    sc = pl.BlockSpec((PEER_HEADS, N_KEYS, tm), lambda i, c: (0, 0, i))
    return pl.pallas_call(
        functools.partial(_peer_dense_kernel, te=te, n_chunks=n_chunks, final_norm=final_norm),
        grid=(t // tm, n_chunks + 2),
        in_specs=[row, row, sc, sc, sc, sc,
                  pl.BlockSpec((te, D_MODEL), lambda i, c: (jnp.minimum(c, n_chunks - 1), 0)),
                  pl.BlockSpec((1, D_MODEL, te), lambda i, c: (jnp.clip(c - 2, 0, n_chunks - 1), 0, 0)),
                  pl.BlockSpec((1, D_MODEL), lambda i, c: (0, 0))],
        out_specs=row,
        out_shape=jax.ShapeDtypeStruct((t, D_MODEL), F32),
        scratch_shapes=[pltpu.VMEM((2, te, tm), F32), pltpu.VMEM((2, te, tm), BF16),
                        pltpu.VMEM((D_MODEL, tm), F32)],
        compiler_params=_cparams("parallel", "arbitrary"),
        name="peer_dense",
    )(x2, hn, rank2, e2, cnt, e1, u, vt, gf)


def _pad_cols(w, width):
    return jnp.pad(w, ((0, 0), (0, width - w.shape[1])))


def _rope_partner(w):
    half = MLA_ROPE // 2
    return jnp.concatenate([-w[:, half:], w[:, :half]], axis=1)


def _rope_lanes(w):
    return jnp.pad(w, ((0, 0), (MLA_NOPE, LANE - MLA_NOPE - MLA_ROPE)))


def _layer_weights(w_in, w_uq, w_ukv, w_o_sb, w_o_fox):
    o = IN_OFFSETS
    seg = lambda n: w_in[:, o[n]:o[n + 1]]
    scale = HEAD_DIM ** -0.5 * LOG2E
    w_kr = seg(2)
    w_cat = jnp.concatenate(
        [seg(0), seg(1), _rope_lanes(w_kr), _rope_lanes(_rope_partner(w_kr)),
         _pad_cols(seg(3) * scale, HEAD_PAD_W), _pad_cols(seg(4), HEAD_PAD_W), _pad_cols(seg(5), HEAD_PAD_W),
         _pad_cols(seg(6) * scale, HEAD_PAD_W), _pad_cols(seg(7), HEAD_PAD_W), _pad_cols(seg(8), HEAD_PAD_W),
         _pad_cols(seg(9), LANE)], axis=1).astype(BF16)
    w_gate = seg(10).astype(BF16)
    uq = w_uq.reshape(Q_LORA, MLA_HEADS, MLA_QK)
    nope, rope = uq[..., :MLA_NOPE], uq[..., MLA_NOPE:]
    zpad = jnp.zeros((Q_LORA, MLA_HEADS, LANE - MLA_QK), F32)
    wqa = jnp.concatenate([nope, rope, zpad], axis=-1).reshape(Q_LORA, MLA_HEADS * LANE).astype(BF16)
    partner = jnp.concatenate([-rope[..., MLA_ROPE // 2:], rope[..., :MLA_ROPE // 2]], axis=-1)
    wqb = jnp.concatenate([jnp.zeros_like(nope), partner, zpad], axis=-1).reshape(Q_LORA, MLA_HEADS * LANE).astype(BF16)
    ukv = w_ukv.reshape(KV_LORA, MLA_HEADS, MLA_NOPE + MLA_V)
    wk = jnp.pad(ukv[..., :MLA_NOPE], ((0, 0), (0, 0), (0, LANE - MLA_NOPE))).reshape(KV_LORA, MLA_HEADS * LANE).astype(BF16)
    wv = ukv[..., MLA_NOPE:].reshape(KV_LORA, MLA_OUT).astype(BF16)
    pad_rows = lambda w: jnp.pad(w, ((0, HEAD_PAD_W - w.shape[0]), (0, 0))).astype(BF16)
    return w_cat, w_gate, wqa, wqb, wk, wv, pad_rows(w_o_sb), pad_rows(w_o_fox)


def _tiles(b, s):
    t = b * s
    tm = min(512, t)
    tq = min(512, s)
    return t, tm, tq


def kernel(x, positions, norm1_g, w_in, mla_q_norm_g, w_uq, mla_kv_norm_g, w_ukv, fox_b_f, w_o_mla, w_o_sb,
           w_o_fox, b_gate, w_out, norm2_g, peer_w_q, peer_sub_keys, peer_u, peer_v, final_norm_g):
    b, s, d = x.shape
    assert d == D_MODEL and MLA_OUT == HEAD_PAD_W
    depth = w_in.shape[0]
    t, tm, tq = _tiles(b, s)
    assert t % tm == 0 and s % tq == 0 and tq % CHUNK == 0
    te = 4 * N_KEYS
    x2 = x.reshape(t, d)
    cos, sin = _rope_tables(positions, tm)
    gf = final_norm_g.reshape(1, d)
    for l in range(depth):
        w_cat, w_gate, wqa, wqb, wk, wv, wo_sb, wo_fox = _layer_weights(w_in[l], w_uq[l], w_ukv[l], w_o_sb[l], w_o_fox[l])
        g1 = norm1_g[l].reshape(1, d)
        mla_in, sb, fx, f = _inproj(x2, g1, w_cat, tm)
        q_a, k_a, v_a = _mla_prep(mla_in, cos, sin, mla_q_norm_g[l].reshape(1, Q_LORA), wqa, wqb,
                                  mla_kv_norm_g[l].reshape(1, KV_LORA), wk, wv, tm)
        bf_row = jnp.zeros((1, LANE), F32).at[0, :FOX_HEADS].set(fox_b_f[l])
        cum, cumt = _fox_prep(f.reshape(b, s, LANE), bf_row, min(256, s))
        hw = MLA_HEADS * LANE
        acc_a = _head_state(MLA_HEADS, tq)
        o_a = _attention(_mla_attn_kernel, q_a.reshape(b, s, hw), k_a.reshape(b, s, hw), v_a.reshape(b, s, MLA_OUT),
                         (0, 0, 0), [], [], [acc_a, acc_a, acc_a], hw, MLA_OUT, tq, "mla_attn")
        sb3 = sb.reshape(b, s, QKV_W)
        fx3 = fx.reshape(b, s, QKV_W)
        acc_b = _head_state(SB_HEADS, tq)
        qm = pltpu.VMEM((SB_HEADS, tq, LANE), BF16)
        o_b = _attention(_sb_attn_kernel, sb3, sb3, sb3, (0, 1, 2), [], [],
                         [qm, pltpu.VMEM((min(256, tq),) * 2, BF16), acc_b, acc_b], HEAD_PAD_W, HEAD_PAD_W, tq,
                         "sb_attn")
        fox_specs = [pl.BlockSpec((1, tq, LANE), lambda bi, i: (bi, i, 0)),
                     pl.BlockSpec((1, 8, s), lambda bi, i: (bi, 0, 0))]
        o_c = _attention(_fox_attn_kernel, fx3, fx3, fx3, (0, 1, 2), [cum, cumt], fox_specs,
                         [qm, acc_b, acc_b, acc_b, acc_b], HEAD_PAD_W, HEAD_PAD_W, tq, "fox_attn")
        x2 = _merge(x2, g1, w_gate, b_gate[l].reshape(1, -1), o_a.reshape(t, MLA_OUT), o_b.reshape(t, HEAD_PAD_W),
                    o_c.reshape(t, HEAD_PAD_W), w_o_mla[l].astype(BF16), wo_sb, wo_fox, w_out[l].astype(BF16), tm)
        hn, rank2, e2, cnt, e1 = _peer_route(x2, norm2_g[l].reshape(1, d), peer_w_q[l].astype(BF16),
                                             peer_sub_keys[l].astype(BF16), tm)
        vt = peer_v[l].astype(BF16).reshape(N_EXPERTS // te, te, d).transpose(0, 2, 1)
        x2 = _peer_dense(x2, hn, rank2, e2, cnt, e1, peer_u[l].astype(BF16), vt, gf, tm, te,
                         final_norm=(l == depth - 1))
    return x2.reshape(b, s, d)
```

```python
import functools

import numpy as np
import jax
import jax.numpy as jnp
from jax import lax
from jax.experimental import pallas as pl
from jax.experimental.pallas import tpu as pltpu

F32 = jnp.float32
BF16 = jnp.bfloat16

D_MODEL = 1024
CHUNK = 64
HEAD_DIM = 64
NORM_EPS = 1e-6
NEG_INF = -1e30
MLA_HEADS = 6
MLA_NOPE = 64
MLA_ROPE = 32
MLA_V = 64
Q_LORA = 256
KV_LORA = 128
ROPE_THETA = 10000.0
SB_HEADS = 5
FOX_HEADS = 5
N_BRANCH = 3
PEER_HEADS = 8
N_KEYS = 128
N_EXPERTS = N_KEYS * N_KEYS
D_KEY = 256
HALF_KEY = D_KEY // 2
PEER_TOPK = 16
MLA_QK = MLA_NOPE + MLA_ROPE
SB_W = SB_HEADS * HEAD_DIM
FOX_W = FOX_HEADS * HEAD_DIM
MLA_OUT = MLA_HEADS * MLA_V
IN_SPLITS = (Q_LORA, KV_LORA, MLA_ROPE, SB_W, SB_W, SB_W, FOX_W, FOX_W, FOX_W, FOX_HEADS, N_BRANCH * D_MODEL)
IN_OFFSETS = tuple(int(o) for o in np.cumsum((0,) + IN_SPLITS))

LANE = 128
HEAD_PAD_W = 384
MLA_IN_W = Q_LORA + KV_LORA + 2 * LANE
QKV_W = 3 * HEAD_PAD_W
IN_W = MLA_IN_W + 2 * QKV_W + LANE
VMEM_LIMIT = 48 * 1024 * 1024

LOG2E = 1.4426950408889634
_NT = (((1,), (1,)), ((), ()))


def _cparams(*sem):
    return pltpu.CompilerParams(dimension_semantics=sem, vmem_limit_bytes=VMEM_LIMIT)


def _rms(x, g):
    return x * lax.rsqrt(jnp.mean(x * x, axis=-1, keepdims=True) + NORM_EPS) * g


def _dot(a, b):
    return jnp.dot(a, b, preferred_element_type=F32)


def _dot_nt(a, b):
    return lax.dot_general(a, b, _NT, preferred_element_type=F32)


def _full(shape):
    return pl.BlockSpec(shape, lambda *_: (0,) * len(shape))


def _rope_kernel(pos_ref, inv_ref, cos_ref, sin_ref):
    ang = pos_ref[...].astype(F32) * inv_ref[...]
    cos_ref[...] = jnp.cos(ang)
    sin_ref[...] = jnp.sin(ang)


def _rope_tables(positions, tm):
    t = positions.size
    inv = ROPE_THETA ** (-jnp.arange(0, MLA_ROPE, 2, dtype=F32) / MLA_ROPE)
    inv_row = jnp.zeros((1, LANE), F32).at[0, MLA_NOPE:MLA_NOPE + MLA_ROPE].set(jnp.concatenate([inv, inv]))
    return pl.pallas_call(
        _rope_kernel,
        grid=(t // tm,),
        in_specs=[pl.BlockSpec((tm, 1), lambda i: (i, 0)), _full((1, LANE))],
        out_specs=[pl.BlockSpec((tm, LANE), lambda i: (i, 0))] * 2,
        out_shape=[jax.ShapeDtypeStruct((t, LANE), F32)] * 2,
        compiler_params=_cparams("parallel"),
        name="rope_tables",
    )(positions.reshape(t, 1), inv_row)


def _inproj_kernel(x_ref, g_ref, w_ref, mla_ref, sb_ref, fx_ref, f_ref):
    hn = _rms(x_ref[...], g_ref[...]).astype(BF16)
    o = 0
    for ref, width in ((mla_ref, MLA_IN_W), (sb_ref, QKV_W), (fx_ref, QKV_W), (f_ref, LANE)):
        ref[...] = _dot(hn, w_ref[:, o:o + width]).astype(ref.dtype)
        o += width


def _inproj(x2, g, w_cat, tm):
    t = x2.shape[0]
    row = lambda w: pl.BlockSpec((tm, w), lambda i: (i, 0))
    return pl.pallas_call(
        _inproj_kernel,
        grid=(t // tm,),
        in_specs=[row(D_MODEL), _full((1, D_MODEL)), _full((D_MODEL, IN_W))],
        out_specs=[row(MLA_IN_W), row(QKV_W), row(QKV_W), row(LANE)],
        out_shape=[jax.ShapeDtypeStruct((t, MLA_IN_W), F32), jax.ShapeDtypeStruct((t, QKV_W), BF16),
                   jax.ShapeDtypeStruct((t, QKV_W), BF16), jax.ShapeDtypeStruct((t, LANE), F32)],
        compiler_params=_cparams("parallel"),
        name="inproj",
    )(x2, g, w_cat)


def _mla_prep_kernel(in_ref, cos_ref, sin_ref, gq_ref, wqa_ref, wqb_ref, gkv_ref, wk_ref, wv_ref,
                     q_ref, k_ref, v_ref):
    cos = cos_ref[...]
    sin = sin_ref[...]
    qn = _rms(in_ref[:, :Q_LORA], gq_ref[...]).astype(BF16)
    kn = _rms(in_ref[:, Q_LORA:Q_LORA + KV_LORA], gkv_ref[...]).astype(BF16)
    o = Q_LORA + KV_LORA
    k_rot = in_ref[:, o:o + LANE] * cos + in_ref[:, o + LANE:o + 2 * LANE] * sin
    scale = MLA_QK ** -0.5 * LOG2E
    for h in range(MLA_HEADS):
        sl = slice(h * LANE, (h + 1) * LANE)
        qa = _dot(qn, wqa_ref[:, sl])
        qb = _dot(qn, wqb_ref[:, sl])
        q_ref[:, sl] = ((qa * cos + qb * sin) * scale).astype(BF16)
        k_ref[:, sl] = (_dot(kn, wk_ref[:, sl]) + k_rot).astype(BF16)
    v_ref[...] = _dot(kn, wv_ref[...]).astype(BF16)


def _mla_prep(mla_in, cos, sin, gq, wqa, wqb, gkv, wk, wv, tm):
    t = mla_in.shape[0]
    row = lambda w: pl.BlockSpec((tm, w), lambda i: (i, 0))
    hw = MLA_HEADS * LANE
    return pl.pallas_call(
        _mla_prep_kernel,
        grid=(t // tm,),
        in_specs=[row(MLA_IN_W), row(LANE), row(LANE), _full((1, Q_LORA)), _full((Q_LORA, hw)),
                  _full((Q_LORA, hw)), _full((1, KV_LORA)), _full((KV_LORA, hw)), _full((KV_LORA, MLA_OUT))],
        out_specs=[row(hw), row(hw), row(MLA_OUT)],
        out_shape=[jax.ShapeDtypeStruct((t, hw), BF16), jax.ShapeDtypeStruct((t, hw), BF16),
                   jax.ShapeDtypeStruct((t, MLA_OUT), BF16)],
        compiler_params=_cparams("parallel"),
        name="mla_prep",
    )(mla_in, cos, sin, gq, wqa, wqb, gkv, wk, wv)


def _fox_prep_kernel(f_ref, bf_ref, cum_ref, cumt_ref, *, blk):
    s = f_ref.shape[1]
    row = lax.broadcasted_iota(jnp.int32, (blk, blk), 0)
    col = lax.broadcasted_iota(jnp.int32, (blk, blk), 1)
    tri = jnp.where(col <= row, 1.0, 0.0).astype(BF16)
    carry = jnp.zeros((1, LANE), F32)
    for b in range(s // blk):
        f = f_ref[0, b * blk:(b + 1) * blk, :] + bf_ref[...]
        lf = (jnp.minimum(f, 0.0) - jnp.log1p(jnp.exp(-jnp.abs(f)))) * LOG2E
        hi = lf.astype(BF16)
        r1 = lf - hi.astype(F32)
        mid = r1.astype(BF16)
        lo = (r1 - mid.astype(F32)).astype(BF16)
        c = _dot(tri, hi) + _dot(tri, mid) + _dot(tri, lo) + carry
        cum_ref[0, b * blk:(b + 1) * blk, :] = c
        cumt_ref[0, :, b * blk:(b + 1) * blk] = c.T[:8, :]
        carry = c[blk - 1:blk, :]


def _fox_prep(f3, bf_row, blk):
    b, s, _ = f3.shape
    return pl.pallas_call(
        functools.partial(_fox_prep_kernel, blk=blk),
        grid=(b,),
        in_specs=[pl.BlockSpec((1, s, LANE), lambda i: (i, 0, 0)), _full((1, LANE))],
        out_specs=[pl.BlockSpec((1, s, LANE), lambda i: (i, 0, 0)), pl.BlockSpec((1, 8, s), lambda i: (i, 0, 0))],
        out_shape=[jax.ShapeDtypeStruct((b, s, LANE), F32), jax.ShapeDtypeStruct((b, 8, s), F32)],
        compiler_params=_cparams("parallel"),
        name="fox_prep",
    )(f3, bf_row)


def _softmax_update(h, s, v, m_ref, l_ref, acc_ref):
    m_old = m_ref[h]
    m_new = jnp.maximum(m_old, jnp.max(s, axis=-1, keepdims=True))
    alpha = jnp.exp2(m_old - m_new)
    p = [jnp.exp2(s[:, n * LANE:(n + 1) * LANE] - m_new) for n in range(s.shape[1] // LANE)]
    part = p[0]
    for p_n in p[1:]:
        part = part + p_n
    l_ref[h] = alpha * l_ref[h] + part
    m_ref[h] = m_new
    acc_ref[h] = alpha * acc_ref[h] + _dot(jnp.concatenate(p, axis=1).astype(BF16), v)


def _softmax_reset(m_ref, l_ref, acc_ref):
    m_ref[...] = jnp.full(m_ref.shape, NEG_INF, F32)
    l_ref[...] = jnp.zeros(l_ref.shape, F32)
    acc_ref[...] = jnp.zeros(acc_ref.shape, F32)


def _half_mask(tq, half):
    lane = lax.broadcasted_iota(jnp.int32, (tq, LANE), 1)
    return (lane < HEAD_DIM) if half == 0 else (lane >= HEAD_DIM)


def _store_head_pairs(o_ref, heads, value_of):
    tq = o_ref.shape[1]
    lo_half = _half_mask(tq, 0)
    for hb in range(o_ref.shape[2] // LANE):
        lo = value_of(2 * hb)
        hi = value_of(2 * hb + 1) if 2 * hb + 1 < heads else jnp.zeros_like(lo)
        o_ref[0, :, hb * LANE:(hb + 1) * LANE] = jnp.where(lo_half, lo, hi).astype(o_ref.dtype)


def _masked_queries(q_ref, qm_ref, heads):
    tq = q_ref.shape[1]
    for h in range(heads):
        qf = q_ref[0, :, (h // 2) * LANE:(h // 2 + 1) * LANE].astype(F32)
        qm_ref[h] = jnp.where(_half_mask(tq, h % 2), qf, 0.0).astype(BF16)


def _block_iotas(tq):
    return lax.broadcasted_iota(jnp.int32, (tq, tq), 0), lax.broadcasted_iota(jnp.int32, (tq, tq), 1)


def _mla_attn_kernel(q_ref, k_ref, v_ref, o_ref, m_ref, l_ref, acc_ref, *, tq):
    i = pl.program_id(1)
    _softmax_reset(m_ref, l_ref, acc_ref)

    def block(j, diagonal):
        rows = pl.ds(pl.multiple_of(j * tq, tq), tq)
        def scores(h):
            sl = slice(h * LANE, (h + 1) * LANE)
            return _dot_nt(q_ref[0, :, sl], k_ref[0, rows, sl])

        s_next = scores(0)
        for h in range(MLA_HEADS):
            s = s_next
            if h + 1 < MLA_HEADS:
                s_next = scores(h + 1)
            if diagonal:
                r, c = _block_iotas(tq)
                s = jnp.where((c // CHUNK) <= (r // CHUNK), s, NEG_INF)
            _softmax_update(h, s, v_ref[0, rows, (h // 2) * LANE:(h // 2 + 1) * LANE], m_ref, l_ref, acc_ref)

    def body(j, carry):
        block(j, False)
        return carry

    lax.fori_loop(0, i, body, 0)
    block(i, True)
    _store_head_pairs(o_ref, MLA_HEADS, lambda h: acc_ref[h] / jnp.sum(l_ref[h], axis=-1, keepdims=True))


def _fox_attn_kernel(q_ref, k_ref, v_ref, cum_ref, cumt_ref, o_ref, qm_ref, cq_ref, m_ref, l_ref, acc_ref, *, tq):
    i = pl.program_id(1)
    _softmax_reset(m_ref, l_ref, acc_ref)
    _masked_queries(q_ref, qm_ref, FOX_HEADS)
    for h in range(FOX_HEADS):
        cq_ref[h] = jnp.broadcast_to(cum_ref[0, :, h:h + 1], (tq, LANE))

    def block(j, diagonal):
        start = pl.multiple_of(j * tq, tq)
        rows = pl.ds(start, tq)
        def scores(h):
            return _dot_nt(qm_ref[h], k_ref[0, rows, (h // 2) * LANE:(h // 2 + 1) * LANE])

        s_next = scores(0)
        for h in range(FOX_HEADS):
            sl = slice((h // 2) * LANE, (h // 2 + 1) * LANE)
            s = s_next
            if h + 1 < FOX_HEADS:
                s_next = scores(h + 1)
            ck = cumt_ref[0, h:h + 1, rows]
            s = jnp.concatenate([s[:, n * LANE:(n + 1) * LANE] + (cq_ref[h] - ck[:, n * LANE:(n + 1) * LANE])
                                 for n in range(tq // LANE)], axis=1)
            if diagonal:
                r, c = _block_iotas(tq)
                s = jnp.where(c <= r, s, NEG_INF)
            _softmax_update(h, s, v_ref[0, rows, sl], m_ref, l_ref, acc_ref)

    def body(j, carry):
        block(j, False)
        return carry

    lax.fori_loop(0, i, body, 0)
    block(i, True)
    _store_head_pairs(o_ref, FOX_HEADS, lambda h: acc_ref[h] / jnp.sum(l_ref[h], axis=-1, keepdims=True))


def _sb_attn_kernel(q_ref, k_ref, v_ref, o_ref, qm_ref, suffix_ref, rest_ref, acc_ref, *, tq):
    i = pl.program_id(1)
    sub = suffix_ref.shape[0]
    r, c = _block_iotas(sub)
    suffix_ref[...] = jnp.where(r > c, 1.0, 0.0).astype(BF16)
    rest_ref[...] = jnp.zeros(rest_ref.shape, F32)
    acc_ref[...] = jnp.zeros(acc_ref.shape, F32)
    _masked_queries(q_ref, qm_ref, SB_HEADS)

    def block(j, diagonal):
        rows = pl.ds(pl.multiple_of(j * tq, tq), tq)
        def scores(h):
            return _dot_nt(qm_ref[h], k_ref[0, rows, (h // 2) * LANE:(h // 2 + 1) * LANE])

        z_next = scores(0)
        for h in range(SB_HEADS):
            sl = slice((h // 2) * LANE, (h // 2 + 1) * LANE)
            z = z_next
            if h + 1 < SB_HEADS:
                z_next = scores(h + 1)
            log_b = jnp.minimum(z, 0.0) - jnp.log(1.0 + jnp.exp2(-jnp.abs(z))) * LOG2E
            log_1m = log_b - z
            if diagonal:
                strict = _block_iotas(tq)[1] < _block_iotas(tq)[0]
                log_1m = jnp.where(strict, log_1m, 0.0)
            later = rest_ref[h]
            a = [None] * (tq // sub)
            for k in reversed(range(tq // sub)):
                cols = slice(k * sub, (k + 1) * sub)
                within = _dot(log_1m[:, cols].astype(BF16), suffix_ref[...])
                a[k] = jnp.concatenate([jnp.exp2(log_b[:, cols][:, n * LANE:(n + 1) * LANE]
                                                 + within[:, n * LANE:(n + 1) * LANE] + later)
                                        for n in range(sub // LANE)], axis=1)
                later = later + jnp.sum(log_1m[:, cols], axis=-1, keepdims=True)
            a = jnp.concatenate(a, axis=1)
            if diagonal:
                a = jnp.where(strict, a, 0.0)
            acc_ref[h] += _dot(a.astype(BF16), v_ref[0, rows, sl])
            rest_ref[h] = later

    block(i, True)

    def body(n, carry):
        block(i - 1 - n, False)
        return carry

    lax.fori_loop(0, i, body, 0)
    _store_head_pairs(o_ref, SB_HEADS, lambda h: acc_ref[h])


def _attention(kernel, q, k, v, cols, extra, extra_specs, scratch, qw, vw, tq, name):
    b, s, _ = q.shape
    blk = lambda w, n=0: pl.BlockSpec((1, tq, w), lambda bi, i: (bi, i, n))
    seq = lambda w, n: pl.BlockSpec((1, s, w), lambda bi, i: (bi, 0, n))
    return pl.pallas_call(
        functools.partial(kernel, tq=tq),
        grid=(b, s // tq),
        in_specs=[blk(qw, cols[0]), seq(qw, cols[1]), seq(vw, cols[2])] + extra_specs,
        out_specs=blk(vw),
        out_shape=jax.ShapeDtypeStruct((b, s, vw), BF16),
        scratch_shapes=scratch,
        compiler_params=_cparams("parallel", "parallel"),
        name=name,
    )(q, k, v, *extra)


def _head_state(heads, tq):
    return pltpu.VMEM((heads, tq, LANE), F32)


def _merge_kernel(x_ref, g_ref, wg_ref, bg_ref, oa_ref, ob_ref, oc_ref, wa_ref, wb_ref, wc_ref, wo_ref, out_ref):
    x = x_ref[...]
    hn = _rms(x, g_ref[...]).astype(BF16)
    y = None
    for n, (o_ref, w_ref) in enumerate(((oa_ref, wa_ref), (ob_ref, wb_ref), (oc_ref, wc_ref))):
        sl = slice(n * D_MODEL, (n + 1) * D_MODEL)
        gate = jax.nn.sigmoid(_dot(hn, wg_ref[:, sl]) + bg_ref[:, sl])
        term = gate * _dot(o_ref[...], w_ref[...])
        y = term if y is None else y + term
    out_ref[...] = x + _dot(y.astype(BF16), wo_ref[...])


def _merge(x2, g, wg, bg, oa, ob, oc, wa, wb, wc, wo, tm):
    t = x2.shape[0]
    row = lambda w: pl.BlockSpec((tm, w), lambda i: (i, 0))
    return pl.pallas_call(
        _merge_kernel,
        grid=(t // tm,),
        in_specs=[row(D_MODEL), _full((1, D_MODEL)), _full((D_MODEL, N_BRANCH * D_MODEL)),
                  _full((1, N_BRANCH * D_MODEL)), row(HEAD_PAD_W), row(HEAD_PAD_W), row(HEAD_PAD_W),
                  _full((HEAD_PAD_W, D_MODEL)), _full((HEAD_PAD_W, D_MODEL)), _full((HEAD_PAD_W, D_MODEL)),
                  _full((D_MODEL, D_MODEL))],
        out_specs=row(D_MODEL),
        out_shape=jax.ShapeDtypeStruct((t, D_MODEL), F32),
        compiler_params=_cparams("parallel"),
        name="merge",
    )(x2, g, wg, bg, oa, ob, oc, wa, wb, wc, wo)


def _kth_largest_rows(x, k):
    for _ in range(k - 1):
        x = jnp.where(x >= jnp.max(x, axis=0, keepdims=True), -jnp.inf, x)
    return jnp.max(x, axis=0, keepdims=True)


def _peer_route_kernel(x_ref, g_ref, wq_ref, keys_ref, h_ref, rank2_ref, e2_ref, cnt_ref, e1_ref,
                       s1_ref, s2_ref, q_ref, top_ref, cand_ref):
    hn = _rms(x_ref[...], g_ref[...]).astype(BF16)
    h_ref[...] = hn
    wide = 2 * D_KEY
    for n in range(PEER_HEADS * D_KEY // wide):
        q_ref[:, n * wide:(n + 1) * wide] = _dot(hn, wq_ref[:, n * wide:(n + 1) * wide]).astype(BF16)
    for h in range(PEER_HEADS):
        for side, s_ref in ((0, s1_ref), (1, s2_ref)):
            o = h * D_KEY + side * HALF_KEY
            s_ref[h] = _dot_nt(keys_ref[side], q_ref[:, o:o + HALF_KEY])

    def lane_tile(lt, carry):
        lanes = pl.ds(pl.multiple_of(lt * LANE, LANE), LANE)
        for h in range(PEER_HEADS):
            rank2 = jnp.full((N_KEYS, LANE), float(PEER_TOPK), F32)
            for side, s_ref in ((0, s1_ref), (1, s2_ref)):
                s = s_ref[h, :, lanes]
                for r in range(PEER_TOPK):
                    m = jnp.max(s, axis=0, keepdims=True)
                    top_ref[side, r:r + 1, :] = m
                    hit = s >= m
                    if side == 1:
                        rank2 = jnp.where(hit, float(r), rank2)
                    s = jnp.where(hit, -jnp.inf, s)
            v2_top8 = top_ref[1, 0:8, :]
            sub = lax.broadcasted_iota(jnp.int32, (8, LANE), 0)
            v2_top4_twice = jnp.where(sub < 4, v2_top8, pltpu.roll(v2_top8, 4, axis=0))
            cand_ref[0:16, :] = top_ref[0, 0:1, :] + top_ref[1]
            for a in (1, 2, 3):
                cand_ref[8 + 8 * a:16 + 8 * a, :] = top_ref[0, a:a + 1, :] + v2_top8
            for n, a in enumerate((4, 6)):
                v1_pair = jnp.where(sub < 4, top_ref[0, a:a + 1, :], top_ref[0, a + 1:a + 2, :])
                cand_ref[40 + 8 * n:48 + 8 * n, :] = v1_pair + v2_top4_twice
            cand_ref[56:64, :] = top_ref[0, 8:16, :] + top_ref[1, 0:1, :]
            cand = cand_ref[...]
            tau = _kth_largest_rows(cand, PEER_TOPK)
            m1 = top_ref[0, 0:1, :]
            m2 = top_ref[1, 0:1, :]
            zsum = jnp.sum(jnp.where(cand >= tau, jnp.exp(cand - (m1 + m2)), 0.0), axis=0, keepdims=True)
            s1 = s1_ref[h, :, lanes]
            thr = tau - s1
            cnt = jnp.zeros((N_KEYS, LANE), F32)
            for b in range(PEER_TOPK):
                cnt = jnp.where(top_ref[1, b:b + 1, :] >= thr, float(b + 1), cnt)
            rank2_ref[h, :, lanes] = rank2.astype(BF16)
            e2_ref[h, :, lanes] = jnp.exp(s2_ref[h, :, lanes] - m2).astype(BF16)
            cnt_ref[h, :, lanes] = cnt
            e1_ref[h, :, lanes] = 0.5 * jnp.exp(s1 - m1) / zsum
        return carry

    lax.fori_loop(0, x_ref.shape[0] // LANE, lane_tile, 0)


def _peer_route(x2, g, wq, keys, tm):
    t = x2.shape[0]
    row = pl.BlockSpec((tm, D_MODEL), lambda i: (i, 0))
    sc = pl.BlockSpec((PEER_HEADS, N_KEYS, tm), lambda i: (0, 0, i))
    return pl.pallas_call(
        _peer_route_kernel,
        grid=(t // tm,),
        in_specs=[row, _full((1, D_MODEL)), _full((D_MODEL, PEER_HEADS * D_KEY)), _full((2, N_KEYS, HALF_KEY))],
        out_specs=[row, sc, sc, sc, sc],
        out_shape=[jax.ShapeDtypeStruct((t, D_MODEL), BF16),
                   jax.ShapeDtypeStruct((PEER_HEADS, N_KEYS, t), BF16),
                   jax.ShapeDtypeStruct((PEER_HEADS, N_KEYS, t), BF16),
                   jax.ShapeDtypeStruct((PEER_HEADS, N_KEYS, t), F32),
                   jax.ShapeDtypeStruct((PEER_HEADS, N_KEYS, t), F32)],
        scratch_shapes=[pltpu.VMEM((PEER_HEADS, N_KEYS, tm), F32), pltpu.VMEM((PEER_HEADS, N_KEYS, tm), F32),
                        pltpu.VMEM((tm, PEER_HEADS * D_KEY), BF16), pltpu.VMEM((2, PEER_TOPK, LANE), F32),
                        pltpu.VMEM((4 * PEER_TOPK, LANE), F32)],
        compiler_params=_cparams("parallel"),
        name="peer_route",
    )(x2, g, wq, keys)


def _peer_dense_step(c, parity, h_ref, rank2_ref, e2_ref, cnt_ref, e1_ref, u_ref, vt_ref, pre_ref, p_ref, acc_ref,
                     *, te, n_chunks):
    tm = h_ref.shape[0]
    rows_per_step = te // N_KEYS
    assert 2 * rows_per_step == 8
    part = 64
    gate_w = 2 * LANE
    score_buf, gate_buf, fold_buf = parity, 1 - parity, parity
    key_tile = pl.ds(pl.multiple_of((jnp.clip(c - 1, 0, n_chunks - 1) // 2) * 8, 8), 8)
    row0 = rows_per_step * (1 - parity)

    def score(piece):
        rows = slice((piece // 2) * te // 2, (piece // 2 + 1) * te // 2)
        toks = slice((piece % 2) * gate_w, (piece % 2 + 1) * gate_w)
        u_rows = slice(parity * te + rows.start, parity * te + rows.stop)
        pre_ref[score_buf, rows, toks] = _dot_nt(u_ref[u_rows, :], h_ref[toks, :])

    def fold(piece):
        rows = slice((piece // 2) * D_MODEL // 4, (piece // 2 + 1) * D_MODEL // 4)
        toks = slice((piece % 2) * gate_w, (piece % 2 + 1) * gate_w)
        acc_ref[rows, toks] += _dot(vt_ref[parity, rows, :], p_ref[fold_buf, :, toks])

    def gate_heads(lt, rp, w, heads):
        lanes = slice(lt * gate_w, (lt + 1) * gate_w)
        for h in heads:
            rank2 = rank2_ref[h, rp * part:(rp + 1) * part, lanes]
            e2 = e2_ref[h, rp * part:(rp + 1) * part, lanes]
            cnt = cnt_ref[h, key_tile, lanes]
            e1 = e1_ref[h, key_tile, lanes]
            for j in range(rows_per_step):
                cnt_j = jnp.broadcast_to(cnt[row0 + j:row0 + j + 1, :], (part, gate_w)).astype(BF16)
                e1_j = jnp.broadcast_to(e1[row0 + j:row0 + j + 1, :], (part, gate_w)).astype(BF16)
                term = jnp.where(rank2 < cnt_j, e2, jnp.zeros_like(e2)) * e1_j
                w[j] = term if w[j] is None else w[j] + term

    def gate_store(lt, rp, w):
        lanes = slice(lt * gate_w, (lt + 1) * gate_w)
        for j in range(rows_per_step):
            rows = slice(j * N_KEYS + rp * part, j * N_KEYS + (rp + 1) * part)
            pre = pre_ref[gate_buf, rows, lanes]
            act = pre * (1.0 + lax.erf(pre * (0.5 ** 0.5)))
            p_ref[gate_buf, rows, lanes] = w[j] * act.astype(BF16)

    assert tm // gate_w == 2 and N_KEYS // part == 2
    mxu = [lambda n=n: score(n) for n in range(4)] + [lambda n=n: fold(n) for n in range(8)]
    order = [0, 4, 5, 1, 6, 7, 2, 8, 9, 3, 10, 11]
    pieces = iter([mxu[n] for n in order])
    for lt in range(2):
        for rp in range(2):
            w = [None] * rows_per_step
            for h in range(PEER_HEADS):
                gate_heads(lt, rp, w, (h,))
                if h % 3 == 1:
                    next(pieces)()
            gate_store(lt, rp, w)


def _peer_dense_kernel(x_ref, h_ref, rank2_ref, e2_ref, cnt_ref, e1_ref, u_ref, vt_ref, gf_ref, out_ref,
                       pre_ref, p_ref, acc_ref, *, te, n_chunks, final_norm):
    g = pl.program_id(1)

    @pl.when(g == 0)
    def _():
        acc_ref[...] = jnp.zeros_like(acc_ref)
        pre_ref[...] = jnp.zeros_like(pre_ref)
        p_ref[...] = jnp.zeros_like(p_ref)

    for parity in range(2):
        _peer_dense_step(2 * g + parity, parity, h_ref, rank2_ref, e2_ref, cnt_ref, e1_ref, u_ref, vt_ref, pre_ref,
                         p_ref, acc_ref, te=te, n_chunks=n_chunks)

    @pl.when(g == n_chunks // 2)
    def _():
        y = x_ref[...] + acc_ref[...].T
        if final_norm:
            y = _rms(y, gf_ref[...])
        out_ref[...] = y


def _peer_dense(x2, hn, rank2, e2, cnt, e1, u, vt, gf, tm, te, final_norm):
    t = x2.shape[0]
    n_chunks = N_EXPERTS // te
    row = pl.BlockSpec((tm, D_MODEL), lambda i, c: (i, 0))
    sc = pl.BlockSpec((PEER_HEADS, N_KEYS, tm), lambda i, c: (0, 0, i))
    return pl.pallas_call(
        functools.partial(_peer_dense_kernel, te=te, n_chunks=n_chunks, final_norm=final_norm),
        grid=(t // tm, n_chunks // 2 + 1),
        in_specs=[row, row, sc, sc, sc, sc,
                  pl.BlockSpec((2 * te, D_MODEL), lambda i, g: (jnp.minimum(g, n_chunks // 2 - 1), 0)),
                  pl.BlockSpec((2, D_MODEL, te), lambda i, g: (jnp.clip(g - 1, 0, n_chunks // 2 - 1), 0, 0)),
                  pl.BlockSpec((1, D_MODEL), lambda i, c: (0, 0))],
        out_specs=row,
        out_shape=jax.ShapeDtypeStruct((t, D_MODEL), F32),
        scratch_shapes=[pltpu.VMEM((2, te, tm), F32), pltpu.VMEM((2, te, tm), BF16),
                        pltpu.VMEM((D_MODEL, tm), F32)],
        compiler_params=_cparams("parallel", "arbitrary"),
        name="peer_dense",
    )(x2, hn, rank2, e2, cnt, e1, u, vt, gf)


def _pad_cols(w, width):
    return jnp.pad(w, ((0, 0), (0, width - w.shape[1])))


def _rope_partner(w):
    half = MLA_ROPE // 2
    return jnp.concatenate([-w[:, half:], w[:, :half]], axis=1)


def _rope_lanes(w):
    return jnp.pad(w, ((0, 0), (MLA_NOPE, LANE - MLA_NOPE - MLA_ROPE)))


def _layer_weights(w_in, w_uq, w_ukv, w_o_sb, w_o_fox):
    o = IN_OFFSETS
    seg = lambda n: w_in[:, o[n]:o[n + 1]]
    scale = HEAD_DIM ** -0.5 * LOG2E
    w_kr = seg(2)
    w_cat = jnp.concatenate(
        [seg(0), seg(1), _rope_lanes(w_kr), _rope_lanes(_rope_partner(w_kr)),
         _pad_cols(seg(3) * scale, HEAD_PAD_W), _pad_cols(seg(4), HEAD_PAD_W), _pad_cols(seg(5), HEAD_PAD_W),
         _pad_cols(seg(6) * scale, HEAD_PAD_W), _pad_cols(seg(7), HEAD_PAD_W), _pad_cols(seg(8), HEAD_PAD_W),
         _pad_cols(seg(9), LANE)], axis=1).astype(BF16)
    w_gate = seg(10).astype(BF16)
    uq = w_uq.reshape(Q_LORA, MLA_HEADS, MLA_QK)
    nope, rope = uq[..., :MLA_NOPE], uq[..., MLA_NOPE:]
    zpad = jnp.zeros((Q_LORA, MLA_HEADS, LANE - MLA_QK), F32)
    wqa = jnp.concatenate([nope, rope, zpad], axis=-1).reshape(Q_LORA, MLA_HEADS * LANE).astype(BF16)
    partner = jnp.concatenate([-rope[..., MLA_ROPE // 2:], rope[..., :MLA_ROPE // 2]], axis=-1)
    wqb = jnp.concatenate([jnp.zeros_like(nope), partner, zpad], axis=-1).reshape(Q_LORA, MLA_HEADS * LANE).astype(BF16)
    ukv = w_ukv.reshape(KV_LORA, MLA_HEADS, MLA_NOPE + MLA_V)
    wk = jnp.pad(ukv[..., :MLA_NOPE], ((0, 0), (0, 0), (0, LANE - MLA_NOPE))).reshape(KV_LORA, MLA_HEADS * LANE).astype(BF16)
    wv = ukv[..., MLA_NOPE:].reshape(KV_LORA, MLA_OUT).astype(BF16)
    pad_rows = lambda w: jnp.pad(w, ((0, HEAD_PAD_W - w.shape[0]), (0, 0))).astype(BF16)
    return w_cat, w_gate, wqa, wqb, wk, wv, pad_rows(w_o_sb), pad_rows(w_o_fox)


def _tiles(b, s):
    t = b * s
    tm = min(512, t)
    tq = min(512, s)
    return t, tm, tq


def kernel(x, positions, norm1_g, w_in, mla_q_norm_g, w_uq, mla_kv_norm_g, w_ukv, fox_b_f, w_o_mla, w_o_sb,
           w_o_fox, b_gate, w_out, norm2_g, peer_w_q, peer_sub_keys, peer_u, peer_v, final_norm_g):
    b, s, d = x.shape
    assert d == D_MODEL and MLA_OUT == HEAD_PAD_W
    depth = w_in.shape[0]
    t, tm, tq = _tiles(b, s)
    assert t % tm == 0 and s % tq == 0 and tq % CHUNK == 0
    te = 4 * N_KEYS
    x2 = x.reshape(t, d)
    cos, sin = _rope_tables(positions, tm)
    gf = final_norm_g.reshape(1, d)
    for l in range(depth):
        w_cat, w_gate, wqa, wqb, wk, wv, wo_sb, wo_fox = _layer_weights(w_in[l], w_uq[l], w_ukv[l], w_o_sb[l], w_o_fox[l])
        g1 = norm1_g[l].reshape(1, d)
        mla_in, sb, fx, f = _inproj(x2, g1, w_cat, tm)
        q_a, k_a, v_a = _mla_prep(mla_in, cos, sin, mla_q_norm_g[l].reshape(1, Q_LORA), wqa, wqb,
                                  mla_kv_norm_g[l].reshape(1, KV_LORA), wk, wv, tm)
        bf_row = jnp.zeros((1, LANE), F32).at[0, :FOX_HEADS].set(fox_b_f[l])
        cum, cumt = _fox_prep(f.reshape(b, s, LANE), bf_row, min(256, s))
        hw = MLA_HEADS * LANE
        acc_a = _head_state(MLA_HEADS, tq)
        o_a = _attention(_mla_attn_kernel, q_a.reshape(b, s, hw), k_a.reshape(b, s, hw), v_a.reshape(b, s, MLA_OUT),
                         (0, 0, 0), [], [], [acc_a, acc_a, acc_a], hw, MLA_OUT, tq, "mla_attn")
        sb3 = sb.reshape(b, s, QKV_W)
        fx3 = fx.reshape(b, s, QKV_W)
        acc_b = _head_state(SB_HEADS, tq)
        qm = pltpu.VMEM((SB_HEADS, tq, LANE), BF16)
        o_b = _attention(_sb_attn_kernel, sb3, sb3, sb3, (0, 1, 2), [], [],
                         [qm, pltpu.VMEM((min(256, tq),) * 2, BF16), acc_b, acc_b], HEAD_PAD_W, HEAD_PAD_W, tq,
                         "sb_attn")
        fox_specs = [pl.BlockSpec((1, tq, LANE), lambda bi, i: (bi, i, 0)),
                     pl.BlockSpec((1, 8, s), lambda bi, i: (bi, 0, 0))]
        o_c = _attention(_fox_attn_kernel, fx3, fx3, fx3, (0, 1, 2), [cum, cumt], fox_specs,
                         [qm, acc_b, acc_b, acc_b, acc_b], HEAD_PAD_W, HEAD_PAD_W, tq, "fox_attn")
        x2 = _merge(x2, g1, w_gate, b_gate[l].reshape(1, -1), o_a.reshape(t, MLA_OUT), o_b.reshape(t, HEAD_PAD_W),
                    o_c.reshape(t, HEAD_PAD_W), w_o_mla[l].astype(BF16), wo_sb, wo_fox, w_out[l].astype(BF16), tm)
        hn, rank2, e2, cnt, e1 = _peer_route(x2, norm2_g[l].reshape(1, d), peer_w_q[l].astype(BF16),
                                             peer_sub_keys[l].astype(BF16), tm)
        vt = peer_v[l].astype(BF16).reshape(N_EXPERTS // te, te, d).transpose(0, 2, 1)
        x2 = _peer_dense(x2, hn, rank2, e2, cnt, e1, peer_u[l].astype(BF16), vt, gf, tm, te,
                         final_norm=(l == depth - 1))
    return x2.reshape(b, s, d)
```

```python
import functools

import numpy as np
import jax
import jax.numpy as jnp
from jax import lax
from jax.experimental import pallas as pl
from jax.experimental.pallas import tpu as pltpu

F32 = jnp.float32
BF16 = jnp.bfloat16

D_MODEL = 1024
CHUNK = 64
HEAD_DIM = 64
NORM_EPS = 1e-6
NEG_INF = -1e30
MLA_HEADS = 6
MLA_NOPE = 64
MLA_ROPE = 32
MLA_V = 64
Q_LORA = 256
KV_LORA = 128
ROPE_THETA = 10000.0
SB_HEADS = 5
FOX_HEADS = 5
N_BRANCH = 3
PEER_HEADS = 8
N_KEYS = 128
N_EXPERTS = N_KEYS * N_KEYS
D_KEY = 256
HALF_KEY = D_KEY // 2
PEER_TOPK = 16
MLA_QK = MLA_NOPE + MLA_ROPE
SB_W = SB_HEADS * HEAD_DIM
FOX_W = FOX_HEADS * HEAD_DIM
MLA_OUT = MLA_HEADS * MLA_V
IN_SPLITS = (Q_LORA, KV_LORA, MLA_ROPE, SB_W, SB_W, SB_W, FOX_W, FOX_W, FOX_W, FOX_HEADS, N_BRANCH * D_MODEL)
IN_OFFSETS = tuple(int(o) for o in np.cumsum((0,) + IN_SPLITS))

LANE = 128
HEAD_PAD_W = 384
MLA_IN_W = Q_LORA + KV_LORA + 2 * LANE
QKV_W = 3 * HEAD_PAD_W
IN_W = MLA_IN_W + 2 * QKV_W + LANE
VMEM_LIMIT = 48 * 1024 * 1024

LOG2E = 1.4426950408889634
_NT = (((1,), (1,)), ((), ()))


def _cparams(*sem):
    return pltpu.CompilerParams(dimension_semantics=sem, vmem_limit_bytes=VMEM_LIMIT)


def _rms(x, g):
    return x * lax.rsqrt(jnp.mean(x * x, axis=-1, keepdims=True) + NORM_EPS) * g


def _dot(a, b):
    return jnp.dot(a, b, preferred_element_type=F32)


def _dot_nt(a, b):
    return lax.dot_general(a, b, _NT, preferred_element_type=F32)


def _full(shape):
    return pl.BlockSpec(shape, lambda *_: (0,) * len(shape))


def _rope_kernel(pos_ref, inv_ref, cos_ref, sin_ref):
    ang = pos_ref[...].astype(F32) * inv_ref[...]
    cos_ref[...] = jnp.cos(ang)
    sin_ref[...] = jnp.sin(ang)


def _rope_tables(positions, tm):
    t = positions.size
    inv = ROPE_THETA ** (-jnp.arange(0, MLA_ROPE, 2, dtype=F32) / MLA_ROPE)
    inv_row = jnp.zeros((1, LANE), F32).at[0, MLA_NOPE:MLA_NOPE + MLA_ROPE].set(jnp.concatenate([inv, inv]))
    return pl.pallas_call(
        _rope_kernel,
        grid=(t // tm,),
        in_specs=[pl.BlockSpec((tm, 1), lambda i: (i, 0)), _full((1, LANE))],
        out_specs=[pl.BlockSpec((tm, LANE), lambda i: (i, 0))] * 2,
        out_shape=[jax.ShapeDtypeStruct((t, LANE), F32)] * 2,
        compiler_params=_cparams("parallel"),
        name="rope_tables",
    )(positions.reshape(t, 1), inv_row)


def _inproj_kernel(x_ref, g_ref, w_ref, cos_ref, sin_ref, gq_ref, wqa_ref, wqb_ref, gkv_ref, wk_ref, wv_ref,
                   q_ref, k_ref, v_ref, sb_ref, fx_ref, f_ref, mla_ref):
    hn = _rms(x_ref[...], g_ref[...]).astype(BF16)
    o = 0
    for ref, width in ((mla_ref, MLA_IN_W), (sb_ref, QKV_W), (fx_ref, QKV_W), (f_ref, LANE)):
        ref[...] = _dot(hn, w_ref[:, o:o + width]).astype(ref.dtype)
        o += width
    _mla_prep(mla_ref, cos_ref, sin_ref, gq_ref, wqa_ref, wqb_ref, gkv_ref, wk_ref, wv_ref, q_ref, k_ref, v_ref)


def _inproj(x2, g, w_cat, cos, sin, gq, wqa, wqb, gkv, wk, wv, tm):
    t = x2.shape[0]
    row = lambda w: pl.BlockSpec((tm, w), lambda i: (i, 0))
    hw = MLA_HEADS * LANE
    return pl.pallas_call(
        _inproj_kernel,
        grid=(t // tm,),
        in_specs=[row(D_MODEL), _full((1, D_MODEL)), _full((D_MODEL, IN_W)), row(LANE), row(LANE),
                  _full((1, Q_LORA)), _full((Q_LORA, hw)), _full((Q_LORA, hw)), _full((1, KV_LORA)),
                  _full((KV_LORA, hw)), _full((KV_LORA, MLA_OUT))],
        out_specs=[row(hw), row(hw), row(MLA_OUT), row(QKV_W), row(QKV_W), row(LANE)],
        out_shape=[jax.ShapeDtypeStruct((t, hw), BF16), jax.ShapeDtypeStruct((t, hw), BF16),
                   jax.ShapeDtypeStruct((t, MLA_OUT), BF16), jax.ShapeDtypeStruct((t, QKV_W), BF16),
                   jax.ShapeDtypeStruct((t, QKV_W), BF16), jax.ShapeDtypeStruct((t, LANE), F32)],
        scratch_shapes=[pltpu.VMEM((tm, MLA_IN_W), F32)],
        compiler_params=_cparams("parallel"),
        name="inproj",
    )(x2, g, w_cat, cos, sin, gq, wqa, wqb, gkv, wk, wv)


def _mla_prep(in_ref, cos_ref, sin_ref, gq_ref, wqa_ref, wqb_ref, gkv_ref, wk_ref, wv_ref, q_ref, k_ref, v_ref):
    cos = cos_ref[...]
    sin = sin_ref[...]
    qn = _rms(in_ref[:, :Q_LORA], gq_ref[...]).astype(BF16)
    kn = _rms(in_ref[:, Q_LORA:Q_LORA + KV_LORA], gkv_ref[...]).astype(BF16)
    o = Q_LORA + KV_LORA
    k_rot = in_ref[:, o:o + LANE] * cos + in_ref[:, o + LANE:o + 2 * LANE] * sin
    scale = MLA_QK ** -0.5 * LOG2E
    for h in range(MLA_HEADS):
        sl = slice(h * LANE, (h + 1) * LANE)
        qa = _dot(qn, wqa_ref[:, sl])
        qb = _dot(qn, wqb_ref[:, sl])
        q_ref[:, sl] = ((qa * cos + qb * sin) * scale).astype(BF16)
        k_ref[:, sl] = (_dot(kn, wk_ref[:, sl]) + k_rot).astype(BF16)
    v_ref[...] = _dot(kn, wv_ref[...]).astype(BF16)


def _fox_prep_kernel(f_ref, bf_ref, cum_ref, cumt_ref, *, blk):
    s = f_ref.shape[1]
    row = lax.broadcasted_iota(jnp.int32, (blk, blk), 0)
    col = lax.broadcasted_iota(jnp.int32, (blk, blk), 1)
    tri = jnp.where(col <= row, 1.0, 0.0).astype(BF16)
    carry = jnp.zeros((1, LANE), F32)
    for b in range(s // blk):
        f = f_ref[0, b * blk:(b + 1) * blk, :] + bf_ref[...]
        lf = (jnp.minimum(f, 0.0) - jnp.log1p(jnp.exp(-jnp.abs(f)))) * LOG2E
        hi = lf.astype(BF16)
        r1 = lf - hi.astype(F32)
        mid = r1.astype(BF16)
        lo = (r1 - mid.astype(F32)).astype(BF16)
        c = _dot(tri, hi) + _dot(tri, mid) + _dot(tri, lo) + carry
        cum_ref[0, b * blk:(b + 1) * blk, :] = c
        cumt_ref[0, :, b * blk:(b + 1) * blk] = c.T[:8, :]
        carry = c[blk - 1:blk, :]


def _fox_prep(f3, bf_row, blk):
    b, s, _ = f3.shape
    return pl.pallas_call(
        functools.partial(_fox_prep_kernel, blk=blk),
        grid=(b,),
        in_specs=[pl.BlockSpec((1, s, LANE), lambda i: (i, 0, 0)), _full((1, LANE))],
        out_specs=[pl.BlockSpec((1, s, LANE), lambda i: (i, 0, 0)), pl.BlockSpec((1, 8, s), lambda i: (i, 0, 0))],
        out_shape=[jax.ShapeDtypeStruct((b, s, LANE), F32), jax.ShapeDtypeStruct((b, 8, s), F32)],
        compiler_params=_cparams("parallel"),
        name="fox_prep",
    )(f3, bf_row)


def _softmax_update(h, s, v, m_ref, l_ref, acc_ref):
    m_old = m_ref[h]
    m_new = jnp.maximum(m_old, jnp.max(s, axis=-1, keepdims=True))
    alpha = jnp.exp2(m_old - m_new)
    p = [jnp.exp2(s[:, n * LANE:(n + 1) * LANE] - m_new) for n in range(s.shape[1] // LANE)]
    part = p[0]
    for p_n in p[1:]:
        part = part + p_n
    l_ref[h] = alpha * l_ref[h] + part
    m_ref[h] = m_new
    acc_ref[h] = alpha * acc_ref[h] + _dot(jnp.concatenate(p, axis=1).astype(BF16), v)


def _softmax_reset(m_ref, l_ref, acc_ref):
    m_ref[...] = jnp.full(m_ref.shape, NEG_INF, F32)
    l_ref[...] = jnp.zeros(l_ref.shape, F32)
    acc_ref[...] = jnp.zeros(acc_ref.shape, F32)


def _half_mask(tq, half):
    lane = lax.broadcasted_iota(jnp.int32, (tq, LANE), 1)
    return (lane < HEAD_DIM) if half == 0 else (lane >= HEAD_DIM)


def _store_head_pairs(o_ref, heads, value_of):
    tq = o_ref.shape[1]
    lo_half = _half_mask(tq, 0)
    for hb in range(o_ref.shape[2] // LANE):
        lo = value_of(2 * hb)
        hi = value_of(2 * hb + 1) if 2 * hb + 1 < heads else jnp.zeros_like(lo)
        o_ref[0, :, hb * LANE:(hb + 1) * LANE] = jnp.where(lo_half, lo, hi).astype(o_ref.dtype)


def _masked_queries(q_ref, qm_ref, heads):
    tq = q_ref.shape[1]
    for h in range(heads):
        qf = q_ref[0, :, (h // 2) * LANE:(h // 2 + 1) * LANE].astype(F32)
        qm_ref[h] = jnp.where(_half_mask(tq, h % 2), qf, 0.0).astype(BF16)


def _block_iotas(tq):
    return lax.broadcasted_iota(jnp.int32, (tq, tq), 0), lax.broadcasted_iota(jnp.int32, (tq, tq), 1)


def _mla_attn_kernel(q_ref, k_ref, v_ref, o_ref, m_ref, l_ref, acc_ref, *, tq):
    i = pl.program_id(1)
    _softmax_reset(m_ref, l_ref, acc_ref)

    def block(j, diagonal):
        rows = pl.ds(pl.multiple_of(j * tq, tq), tq)
        def scores(h):
            sl = slice(h * LANE, (h + 1) * LANE)
            return _dot_nt(q_ref[0, :, sl], k_ref[0, rows, sl])

        s_next = scores(0)
        for h in range(MLA_HEADS):
            s = s_next
            if h + 1 < MLA_HEADS:
                s_next = scores(h + 1)
            if diagonal:
                r, c = _block_iotas(tq)
                s = jnp.where((c // CHUNK) <= (r // CHUNK), s, NEG_INF)
            _softmax_update(h, s, v_ref[0, rows, (h // 2) * LANE:(h // 2 + 1) * LANE], m_ref, l_ref, acc_ref)

    def body(j, carry):
        block(j, False)
        return carry

    lax.fori_loop(0, i, body, 0)
    block(i, True)
    _store_head_pairs(o_ref, MLA_HEADS, lambda h: acc_ref[h] / jnp.sum(l_ref[h], axis=-1, keepdims=True))


def _fox_attn_kernel(q_ref, k_ref, v_ref, cum_ref, cumt_ref, o_ref, qm_ref, cq_ref, m_ref, l_ref, acc_ref, *, tq):
    i = pl.program_id(1)
    _softmax_reset(m_ref, l_ref, acc_ref)
    _masked_queries(q_ref, qm_ref, FOX_HEADS)
    for h in range(FOX_HEADS):
        cq_ref[h] = jnp.broadcast_to(cum_ref[0, :, h:h + 1], (tq, LANE))

    def block(j, diagonal):
        start = pl.multiple_of(j * tq, tq)
        rows = pl.ds(start, tq)
        def scores(h):
            return _dot_nt(qm_ref[h], k_ref[0, rows, (h // 2) * LANE:(h // 2 + 1) * LANE])

        s_next = scores(0)
        for h in range(FOX_HEADS):
            sl = slice((h // 2) * LANE, (h // 2 + 1) * LANE)
            s = s_next
            if h + 1 < FOX_HEADS:
                s_next = scores(h + 1)
            ck = cumt_ref[0, h:h + 1, rows]
            s = jnp.concatenate([s[:, n * LANE:(n + 1) * LANE] + (cq_ref[h] - ck[:, n * LANE:(n + 1) * LANE])
                                 for n in range(tq // LANE)], axis=1)
            if diagonal:
                r, c = _block_iotas(tq)
                s = jnp.where(c <= r, s, NEG_INF)
            _softmax_update(h, s, v_ref[0, rows, sl], m_ref, l_ref, acc_ref)

    def body(j, carry):
        block(j, False)
        return carry

    lax.fori_loop(0, i, body, 0)
    block(i, True)
    _store_head_pairs(o_ref, FOX_HEADS, lambda h: acc_ref[h] / jnp.sum(l_ref[h], axis=-1, keepdims=True))


def _sb_attn_kernel(q_ref, k_ref, v_ref, o_ref, qm_ref, suffix_ref, rest_ref, acc_ref, *, tq):
    i = pl.program_id(1)
    sub = suffix_ref.shape[0]
    r, c = _block_iotas(sub)
    suffix_ref[...] = jnp.where(r > c, 1.0, 0.0).astype(BF16)
    for h in range(SB_HEADS):
        rest_ref[h] = jnp.zeros(rest_ref.shape[1:], F32)
        acc_ref[h] = jnp.zeros(acc_ref.shape[1:], F32)
    _masked_queries(q_ref, qm_ref, SB_HEADS)

    def block(j, diagonal):
        rows = pl.ds(pl.multiple_of(j * tq, tq), tq)
        def scores(h):
            return _dot_nt(qm_ref[h], k_ref[0, rows, (h // 2) * LANE:(h // 2 + 1) * LANE])

        z_next = scores(0)
        for h in range(SB_HEADS):
            sl = slice((h // 2) * LANE, (h // 2 + 1) * LANE)
            z = z_next
            if h + 1 < SB_HEADS:
                z_next = scores(h + 1)
            log_b = jnp.minimum(z, 0.0) - jnp.log(1.0 + jnp.exp2(-jnp.abs(z))) * LOG2E
            log_1m = log_b - z
            if diagonal:
                strict = _block_iotas(tq)[1] < _block_iotas(tq)[0]
                log_1m = jnp.where(strict, log_1m, 0.0)
            later = rest_ref[h]
            a = [None] * (tq // sub)
            for k in reversed(range(tq // sub)):
                cols = slice(k * sub, (k + 1) * sub)
                within = _dot(log_1m[:, cols].astype(BF16), suffix_ref[...])
                a[k] = jnp.concatenate([jnp.exp2(log_b[:, cols][:, n * LANE:(n + 1) * LANE]
                                                 + within[:, n * LANE:(n + 1) * LANE] + later)
                                        for n in range(sub // LANE)], axis=1)
                later = later + jnp.sum(log_1m[:, cols], axis=-1, keepdims=True)
            a = jnp.concatenate(a, axis=1)
            if diagonal:
                a = jnp.where(strict, a, 0.0)
            acc_ref[h] += _dot(a.astype(BF16), v_ref[0, rows, sl])
            rest_ref[h] = later

    block(i, True)

    def body(n, carry):
        block(i - 1 - n, False)
        return carry

    lax.fori_loop(0, i, body, 0)
    _store_head_pairs(o_ref, SB_HEADS, lambda h: acc_ref[h])


def _attention(kernel, q, k, v, cols, extra, extra_specs, scratch, qw, vw, tq, name):
    b, s, _ = q.shape
    blk = lambda w, n=0: pl.BlockSpec((1, tq, w), lambda bi, i: (bi, i, n))
    seq = lambda w, n: pl.BlockSpec((1, s, w), lambda bi, i: (bi, 0, n))
    return pl.pallas_call(
        functools.partial(kernel, tq=tq),
        grid=(b, s // tq),
        in_specs=[blk(qw, cols[0]), seq(qw, cols[1]), seq(vw, cols[2])] + extra_specs,
        out_specs=blk(vw),
        out_shape=jax.ShapeDtypeStruct((b, s, vw), BF16),
        scratch_shapes=scratch,
        compiler_params=_cparams("parallel", "parallel"),
        name=name,
    )(q, k, v, *extra)


def _head_state(heads, tq):
    return pltpu.VMEM((heads, tq, LANE), F32)


def _merge_kernel(x_ref, g_ref, wg_ref, bg_ref, oa_ref, ob_ref, oc_ref, wa_ref, wb_ref, wc_ref, wo_ref, out_ref):
    x = x_ref[...]
    hn = _rms(x, g_ref[...]).astype(BF16)
    y = None
    for n, (o_ref, w_ref) in enumerate(((oa_ref, wa_ref), (ob_ref, wb_ref), (oc_ref, wc_ref))):
        sl = slice(n * D_MODEL, (n + 1) * D_MODEL)
        gate = jax.nn.sigmoid(_dot(hn, wg_ref[:, sl]) + bg_ref[:, sl])
        term = gate * _dot(o_ref[...], w_ref[...])
        y = term if y is None else y + term
    out_ref[...] = x + _dot(y.astype(BF16), wo_ref[...])


def _merge(x2, g, wg, bg, oa, ob, oc, wa, wb, wc, wo, tm):
    t = x2.shape[0]
    row = lambda w: pl.BlockSpec((tm, w), lambda i: (i, 0))
    return pl.pallas_call(
        _merge_kernel,
        grid=(t // tm,),
        in_specs=[row(D_MODEL), _full((1, D_MODEL)), _full((D_MODEL, N_BRANCH * D_MODEL)),
                  _full((1, N_BRANCH * D_MODEL)), row(HEAD_PAD_W), row(HEAD_PAD_W), row(HEAD_PAD_W),
                  _full((HEAD_PAD_W, D_MODEL)), _full((HEAD_PAD_W, D_MODEL)), _full((HEAD_PAD_W, D_MODEL)),
                  _full((D_MODEL, D_MODEL))],
        out_specs=row(D_MODEL),
        out_shape=jax.ShapeDtypeStruct((t, D_MODEL), F32),
        compiler_params=_cparams("parallel"),
        name="merge",
    )(x2, g, wg, bg, oa, ob, oc, wa, wb, wc, wo)


def _kth_largest_rows(x, k):
    for _ in range(k - 1):
        x = jnp.where(x >= jnp.max(x, axis=0, keepdims=True), -jnp.inf, x)
    return jnp.max(x, axis=0, keepdims=True)


def _peer_route_kernel(x_ref, g_ref, wq_ref, keys_ref, h_ref, rank2_ref, e2_ref, cnt_ref, e1_ref,
                       s1_ref, s2_ref, q_ref, top_ref, cand_ref):
    hn = _rms(x_ref[...], g_ref[...]).astype(BF16)
    h_ref[...] = hn
    wide = 2 * D_KEY
    for n in range(PEER_HEADS * D_KEY // wide):
        q_ref[:, n * wide:(n + 1) * wide] = _dot(hn, wq_ref[:, n * wide:(n + 1) * wide]).astype(BF16)
    for h in range(PEER_HEADS):
        for side, s_ref in ((0, s1_ref), (1, s2_ref)):
            o = h * D_KEY + side * HALF_KEY
            s_ref[h] = _dot_nt(keys_ref[side], q_ref[:, o:o + HALF_KEY])

    def lane_tile(lt, carry):
        lanes = pl.ds(pl.multiple_of(lt * LANE, LANE), LANE)
        for h in range(PEER_HEADS):
            rank2 = jnp.full((N_KEYS, LANE), float(PEER_TOPK), F32)
            for side, s_ref in ((0, s1_ref), (1, s2_ref)):
                s = s_ref[h, :, lanes]
                for r in range(PEER_TOPK):
                    m = jnp.max(s, axis=0, keepdims=True)
                    top_ref[side, r:r + 1, :] = m
                    hit = s >= m
                    if side == 1:
                        rank2 = jnp.where(hit, float(r), rank2)
                    s = jnp.where(hit, -jnp.inf, s)
            v2_top8 = top_ref[1, 0:8, :]
            sub = lax.broadcasted_iota(jnp.int32, (8, LANE), 0)
            v2_top4_twice = jnp.where(sub < 4, v2_top8, pltpu.roll(v2_top8, 4, axis=0))
            cand_ref[0:16, :] = top_ref[0, 0:1, :] + top_ref[1]
            for a in (1, 2, 3):
                cand_ref[8 + 8 * a:16 + 8 * a, :] = top_ref[0, a:a + 1, :] + v2_top8
            for n, a in enumerate((4, 6)):
                v1_pair = jnp.where(sub < 4, top_ref[0, a:a + 1, :], top_ref[0, a + 1:a + 2, :])
                cand_ref[40 + 8 * n:48 + 8 * n, :] = v1_pair + v2_top4_twice
            cand_ref[56:64, :] = top_ref[0, 8:16, :] + top_ref[1, 0:1, :]
            cand = cand_ref[...]
            tau = _kth_largest_rows(cand, PEER_TOPK)
            m1 = top_ref[0, 0:1, :]
            m2 = top_ref[1, 0:1, :]
            zsum = jnp.sum(jnp.where(cand >= tau, jnp.exp(cand - (m1 + m2)), 0.0), axis=0, keepdims=True)
            s1 = s1_ref[h, :, lanes]
            thr = tau - s1
            cnt = jnp.zeros((N_KEYS, LANE), F32)
            for b in range(PEER_TOPK):
                cnt = jnp.where(top_ref[1, b:b + 1, :] >= thr, float(b + 1), cnt)
            rank2_ref[h, :, lanes] = rank2.astype(BF16)
            e2_ref[h, :, lanes] = jnp.exp(s2_ref[h, :, lanes] - m2).astype(BF16)
            cnt_ref[h, :, lanes] = cnt
            e1_ref[h, :, lanes] = 0.5 * jnp.exp(s1 - m1) / zsum
        return carry

    lax.fori_loop(0, x_ref.shape[0] // LANE, lane_tile, 0)


def _peer_route(x2, g, wq, keys, tm):
    t = x2.shape[0]
    row = pl.BlockSpec((tm, D_MODEL), lambda i: (i, 0))
    sc = pl.BlockSpec((PEER_HEADS, N_KEYS, tm), lambda i: (0, 0, i))
    return pl.pallas_call(
        _peer_route_kernel,
        grid=(t // tm,),
        in_specs=[row, _full((1, D_MODEL)), _full((D_MODEL, PEER_HEADS * D_KEY)), _full((2, N_KEYS, HALF_KEY))],
        out_specs=[row, sc, sc, sc, sc],
        out_shape=[jax.ShapeDtypeStruct((t, D_MODEL), BF16),
                   jax.ShapeDtypeStruct((PEER_HEADS, N_KEYS, t), BF16),
                   jax.ShapeDtypeStruct((PEER_HEADS, N_KEYS, t), BF16),
                   jax.ShapeDtypeStruct((PEER_HEADS, N_KEYS, t), F32),
                   jax.ShapeDtypeStruct((PEER_HEADS, N_KEYS, t), F32)],
        scratch_shapes=[pltpu.VMEM((PEER_HEADS, N_KEYS, tm), F32), pltpu.VMEM((PEER_HEADS, N_KEYS, tm), F32),
                        pltpu.VMEM((tm, PEER_HEADS * D_KEY), BF16), pltpu.VMEM((2, PEER_TOPK, LANE), F32),
                        pltpu.VMEM((4 * PEER_TOPK, LANE), F32)],
        compiler_params=_cparams("parallel"),
        name="peer_route",
    )(x2, g, wq, keys)


def _peer_dense_step(c, parity, h_ref, rank2_ref, e2_ref, cnt_ref, e1_ref, u_ref, vt_ref, pre_ref, p_ref, acc_ref,
                     *, te, n_chunks):
    tm = h_ref.shape[0]
    rows_per_step = te // N_KEYS
    assert 2 * rows_per_step == 8
    part = 64
    gate_w = 2 * LANE
    score_buf, gate_buf, fold_buf = parity, 1 - parity, parity
    key_tile = pl.ds(pl.multiple_of((jnp.clip(c - 1, 0, n_chunks - 1) // 2) * 8, 8), 8)
    row0 = rows_per_step * (1 - parity)

    def score(piece):
        rows = slice((piece // 2) * te // 2, (piece // 2 + 1) * te // 2)
        toks = slice((piece % 2) * gate_w, (piece % 2 + 1) * gate_w)
        u_rows = slice(parity * te + rows.start, parity * te + rows.stop)
        pre_ref[score_buf, rows, toks] = _dot_nt(u_ref[u_rows, :], h_ref[toks, :])

    def fold(piece):
        rows = slice((piece // 2) * D_MODEL // 4, (piece // 2 + 1) * D_MODEL // 4)
        toks = slice((piece % 2) * gate_w, (piece % 2 + 1) * gate_w)
        acc_ref[rows, toks] += _dot(vt_ref[parity, rows, :], p_ref[fold_buf, :, toks])

    def gate_heads(lt, rp, w, heads):
        lanes = slice(lt * gate_w, (lt + 1) * gate_w)
        for h in heads:
            rank2 = rank2_ref[h, rp * part:(rp + 1) * part, lanes]
            e2 = e2_ref[h, rp * part:(rp + 1) * part, lanes]
            cnt = cnt_ref[h, key_tile, lanes]
            e1 = e1_ref[h, key_tile, lanes]
            for j in range(rows_per_step):
                cnt_j = jnp.broadcast_to(cnt[row0 + j:row0 + j + 1, :], (part, gate_w)).astype(BF16)
                e1_j = jnp.broadcast_to(e1[row0 + j:row0 + j + 1, :], (part, gate_w)).astype(BF16)
                term = jnp.where(rank2 < cnt_j, e2, jnp.zeros_like(e2)) * e1_j
                w[j] = term if w[j] is None else w[j] + term

    def gate_store(lt, rp, w):
        lanes = slice(lt * gate_w, (lt + 1) * gate_w)
        for j in range(rows_per_step):
            rows = slice(j * N_KEYS + rp * part, j * N_KEYS + (rp + 1) * part)
            pre = pre_ref[gate_buf, rows, lanes]
            pre = pre.astype(BF16)
            act = pre * (1.0 + lax.erf(pre * (0.5 ** 0.5)))
            p_ref[gate_buf, rows, lanes] = w[j] * act

    assert tm // gate_w == 2 and N_KEYS // part == 2
    mxu = [lambda n=n: score(n) for n in range(4)] + [lambda n=n: fold(n) for n in range(8)]
    order = [0, 4, 5, 1, 6, 7, 2, 8, 9, 3, 10, 11]
    pieces = iter([mxu[n] for n in order])
    for lt in range(2):
        for rp in range(2):
            w = [None] * rows_per_step
            for h in range(PEER_HEADS):
                gate_heads(lt, rp, w, (h,))
                if h % 3 == 1:
                    next(pieces)()
            gate_store(lt, rp, w)


def _peer_dense_kernel(x_ref, h_ref, rank2_ref, e2_ref, cnt_ref, e1_ref, u_ref, vt_ref, gf_ref, out_ref,
                       pre_ref, p_ref, acc_ref, *, te, n_chunks, final_norm):
    g = pl.program_id(1)

    @pl.when(g == 0)
    def _():
        acc_ref[...] = jnp.zeros_like(acc_ref)
        pre_ref[...] = jnp.zeros_like(pre_ref)
        p_ref[...] = jnp.zeros_like(p_ref)

    for parity in range(2):
        @pl.when((g >= 0) if parity == 0 else (g <= n_chunks))
        def _(parity=parity):
            _peer_dense_step(2 * g + parity, parity, h_ref, rank2_ref, e2_ref, cnt_ref, e1_ref, u_ref, vt_ref,
                             pre_ref, p_ref, acc_ref, te=te, n_chunks=n_chunks)

    @pl.when(g == n_chunks // 2)
    def _():
        y = x_ref[...] + acc_ref[...].T
        if final_norm:
            y = _rms(y, gf_ref[...])
        out_ref[...] = y


def _peer_dense(x2, hn, rank2, e2, cnt, e1, u, vt, gf, tm, te, final_norm):
    t = x2.shape[0]
    n_chunks = N_EXPERTS // te
    row = pl.BlockSpec((tm, D_MODEL), lambda i, c: (i, 0))
    sc = pl.BlockSpec((PEER_HEADS, N_KEYS, tm), lambda i, c: (0, 0, i))
    return pl.pallas_call(
        functools.partial(_peer_dense_kernel, te=te, n_chunks=n_chunks, final_norm=final_norm),
        grid=(t // tm, n_chunks // 2 + 1),
        in_specs=[row, row, sc, sc, sc, sc,
                  pl.BlockSpec((2 * te, D_MODEL), lambda i, g: (jnp.minimum(g, n_chunks // 2 - 1), 0)),
                  pl.BlockSpec((2, D_MODEL, te), lambda i, g: (jnp.clip(g - 1, 0, n_chunks // 2 - 1), 0, 0)),
                  pl.BlockSpec((1, D_MODEL), lambda i, c: (0, 0))],
        out_specs=row,
        out_shape=jax.ShapeDtypeStruct((t, D_MODEL), F32),
        scratch_shapes=[pltpu.VMEM((2, te, tm), F32), pltpu.VMEM((2, te, tm), BF16),
                        pltpu.VMEM((D_MODEL, tm), F32)],
        compiler_params=_cparams("parallel", "arbitrary"),
        name="peer_dense",
    )(x2, hn, rank2, e2, cnt, e1, u, vt, gf)


def _pad_cols(w, width):
    return jnp.pad(w, ((0, 0), (0, width - w.shape[1])))


def _rope_partner(w):
    half = MLA_ROPE // 2
    return jnp.concatenate([-w[:, half:], w[:, :half]], axis=1)


def _rope_lanes(w):
    return jnp.pad(w, ((0, 0), (MLA_NOPE, LANE - MLA_NOPE - MLA_ROPE)))


def _layer_weights(w_in, w_uq, w_ukv, w_o_sb, w_o_fox):
    o = IN_OFFSETS
    seg = lambda n: w_in[:, o[n]:o[n + 1]]
    scale = HEAD_DIM ** -0.5 * LOG2E
    w_kr = seg(2)
    w_cat = jnp.concatenate(
        [seg(0), seg(1), _rope_lanes(w_kr), _rope_lanes(_rope_partner(w_kr)),
         _pad_cols(seg(3) * scale, HEAD_PAD_W), _pad_cols(seg(4), HEAD_PAD_W), _pad_cols(seg(5), HEAD_PAD_W),
         _pad_cols(seg(6) * scale, HEAD_PAD_W), _pad_cols(seg(7), HEAD_PAD_W), _pad_cols(seg(8), HEAD_PAD_W),
         _pad_cols(seg(9), LANE)], axis=1).astype(BF16)
    w_gate = seg(10).astype(BF16)
    uq = w_uq.reshape(Q_LORA, MLA_HEADS, MLA_QK)
    nope, rope = uq[..., :MLA_NOPE], uq[..., MLA_NOPE:]
    zpad = jnp.zeros((Q_LORA, MLA_HEADS, LANE - MLA_QK), F32)
    wqa = jnp.concatenate([nope, rope, zpad], axis=-1).reshape(Q_LORA, MLA_HEADS * LANE).astype(BF16)
    partner = jnp.concatenate([-rope[..., MLA_ROPE // 2:], rope[..., :MLA_ROPE // 2]], axis=-1)
    wqb = jnp.concatenate([jnp.zeros_like(nope), partner, zpad], axis=-1).reshape(Q_LORA, MLA_HEADS * LANE).astype(BF16)
    ukv = w_ukv.reshape(KV_LORA, MLA_HEADS, MLA_NOPE + MLA_V)
    wk = jnp.pad(ukv[..., :MLA_NOPE], ((0, 0), (0, 0), (0, LANE - MLA_NOPE))).reshape(KV_LORA, MLA_HEADS * LANE).astype(BF16)
    wv = ukv[..., MLA_NOPE:].reshape(KV_LORA, MLA_OUT).astype(BF16)
    pad_rows = lambda w: jnp.pad(w, ((0, HEAD_PAD_W - w.shape[0]), (0, 0))).astype(BF16)
    return w_cat, w_gate, wqa, wqb, wk, wv, pad_rows(w_o_sb), pad_rows(w_o_fox)


def _tiles(b, s):
    t = b * s
    tm = min(512, t)
    tq = min(512, s)
    return t, tm, tq


def kernel(x, positions, norm1_g, w_in, mla_q_norm_g, w_uq, mla_kv_norm_g, w_ukv, fox_b_f, w_o_mla, w_o_sb,
           w_o_fox, b_gate, w_out, norm2_g, peer_w_q, peer_sub_keys, peer_u, peer_v, final_norm_g):
    b, s, d = x.shape
    assert d == D_MODEL and MLA_OUT == HEAD_PAD_W
    depth = w_in.shape[0]
    t, tm, tq = _tiles(b, s)
    assert t % tm == 0 and s % tq == 0 and tq % CHUNK == 0
    te = 4 * N_KEYS
    x2 = x.reshape(t, d)
    cos, sin = _rope_tables(positions, tm)
    gf = final_norm_g.reshape(1, d)
    for l in range(depth):
        w_cat, w_gate, wqa, wqb, wk, wv, wo_sb, wo_fox = _layer_weights(w_in[l], w_uq[l], w_ukv[l], w_o_sb[l], w_o_fox[l])
        g1 = norm1_g[l].reshape(1, d)
        q_a, k_a, v_a, sb, fx, f = _inproj(x2, g1, w_cat, cos, sin, mla_q_norm_g[l].reshape(1, Q_LORA), wqa, wqb,
                                           mla_kv_norm_g[l].reshape(1, KV_LORA), wk, wv, tm)
        bf_row = jnp.zeros((1, LANE), F32).at[0, :FOX_HEADS].set(fox_b_f[l])
        cum, cumt = _fox_prep(f.reshape(b, s, LANE), bf_row, min(256, s))
        hw = MLA_HEADS * LANE
        acc_a = _head_state(MLA_HEADS, tq)
        o_a = _attention(_mla_attn_kernel, q_a.reshape(b, s, hw), k_a.reshape(b, s, hw), v_a.reshape(b, s, MLA_OUT),
                         (0, 0, 0), [], [], [acc_a, acc_a, acc_a], hw, MLA_OUT, tq, "mla_attn")
        sb3 = sb.reshape(b, s, QKV_W)
        fx3 = fx.reshape(b, s, QKV_W)
        acc_b = _head_state(SB_HEADS, tq)
        qm = pltpu.VMEM((SB_HEADS, tq, LANE), BF16)
        o_b = _attention(_sb_attn_kernel, sb3, sb3, sb3, (0, 1, 2), [], [],
                         [qm, pltpu.VMEM((min(256, tq),) * 2, BF16), acc_b, acc_b], HEAD_PAD_W, HEAD_PAD_W, tq,
                         "sb_attn")
        fox_specs = [pl.BlockSpec((1, tq, LANE), lambda bi, i: (bi, i, 0)),
                     pl.BlockSpec((1, 8, s), lambda bi, i: (bi, 0, 0))]
        o_c = _attention(_fox_attn_kernel, fx3, fx3, fx3, (0, 1, 2), [cum, cumt], fox_specs,
                         [qm, acc_b, acc_b, acc_b, acc_b], HEAD_PAD_W, HEAD_PAD_W, tq, "fox_attn")
        x2 = _merge(x2, g1, w_gate, b_gate[l].reshape(1, -1), o_a.reshape(t, MLA_OUT), o_b.reshape(t, HEAD_PAD_W),
                    o_c.reshape(t, HEAD_PAD_W), w_o_mla[l].astype(BF16), wo_sb, wo_fox, w_out[l].astype(BF16), tm)
        hn, rank2, e2, cnt, e1 = _peer_route(x2, norm2_g[l].reshape(1, d), peer_w_q[l].astype(BF16),
                                             peer_sub_keys[l].astype(BF16), tm)
        vt = peer_v[l].astype(BF16).reshape(N_EXPERTS // te, te, d).transpose(0, 2, 1)
        x2 = _peer_dense(x2, hn, rank2, e2, cnt, e1, peer_u[l].astype(BF16), vt, gf, tm, te,
                         final_norm=(l == depth - 1))
    return x2.reshape(b, s, d)
```

```python
import functools

import numpy as np
import jax
import jax.numpy as jnp
from jax import lax
from jax.experimental import pallas as pl
from jax.experimental.pallas import tpu as pltpu

F32 = jnp.float32
BF16 = jnp.bfloat16

D_MODEL = 1024
CHUNK = 64
HEAD_DIM = 64
NORM_EPS = 1e-6
NEG_INF = -1e30
MLA_HEADS = 6
MLA_NOPE = 64
MLA_ROPE = 32
MLA_V = 64
Q_LORA = 256
KV_LORA = 128
ROPE_THETA = 10000.0
SB_HEADS = 5
FOX_HEADS = 5
N_BRANCH = 3
PEER_HEADS = 8
N_KEYS = 128
N_EXPERTS = N_KEYS * N_KEYS
D_KEY = 256
HALF_KEY = D_KEY // 2
PEER_TOPK = 16
MLA_QK = MLA_NOPE + MLA_ROPE
SB_W = SB_HEADS * HEAD_DIM
FOX_W = FOX_HEADS * HEAD_DIM
MLA_OUT = MLA_HEADS * MLA_V
IN_SPLITS = (Q_LORA, KV_LORA, MLA_ROPE, SB_W, SB_W, SB_W, FOX_W, FOX_W, FOX_W, FOX_HEADS, N_BRANCH * D_MODEL)
IN_OFFSETS = tuple(int(o) for o in np.cumsum((0,) + IN_SPLITS))

LANE = 128
HEAD_PAD_W = 384
MLA_IN_W = Q_LORA + KV_LORA + 2 * LANE
QKV_W = 3 * HEAD_PAD_W
IN_W = MLA_IN_W + 2 * QKV_W + LANE
VMEM_LIMIT = 48 * 1024 * 1024

LOG2E = 1.4426950408889634
_NT = (((1,), (1,)), ((), ()))


def _cparams(*sem):
    return pltpu.CompilerParams(dimension_semantics=sem, vmem_limit_bytes=VMEM_LIMIT)


def _rms(x, g):
    return x * lax.rsqrt(jnp.mean(x * x, axis=-1, keepdims=True) + NORM_EPS) * g


def _dot(a, b):
    return jnp.dot(a, b, preferred_element_type=F32)


def _dot_nt(a, b):
    return lax.dot_general(a, b, _NT, preferred_element_type=F32)


def _full(shape):
    return pl.BlockSpec(shape, lambda *_: (0,) * len(shape))


def _rope_kernel(pos_ref, inv_ref, cos_ref, sin_ref):
    ang = pos_ref[...].astype(F32) * inv_ref[...]
    cos_ref[...] = jnp.cos(ang)
    sin_ref[...] = jnp.sin(ang)


def _rope_tables(positions, tm):
    t = positions.size
    inv = ROPE_THETA ** (-jnp.arange(0, MLA_ROPE, 2, dtype=F32) / MLA_ROPE)
    inv_row = jnp.zeros((1, LANE), F32).at[0, MLA_NOPE:MLA_NOPE + MLA_ROPE].set(jnp.concatenate([inv, inv]))
    return pl.pallas_call(
        _rope_kernel,
        grid=(t // tm,),
        in_specs=[pl.BlockSpec((tm, 1), lambda i: (i, 0)), _full((1, LANE))],
        out_specs=[pl.BlockSpec((tm, LANE), lambda i: (i, 0))] * 2,
        out_shape=[jax.ShapeDtypeStruct((t, LANE), F32)] * 2,
        compiler_params=_cparams("parallel"),
        name="rope_tables",
    )(positions.reshape(t, 1), inv_row)


def _inproj_kernel(x_ref, g_ref, w_ref, cos_ref, sin_ref, gq_ref, wqa_ref, wqb_ref, gkv_ref, wk_ref, wv_ref,
                   q_ref, k_ref, v_ref, sb_ref, fx_ref, f_ref, mla_ref):
    hn = _rms(x_ref[...], g_ref[...]).astype(BF16)
    o = 0
    for ref, width in ((mla_ref, MLA_IN_W), (sb_ref, QKV_W), (fx_ref, QKV_W), (f_ref, LANE)):
        ref[...] = _dot(hn, w_ref[:, o:o + width]).astype(ref.dtype)
        o += width
    _mla_prep(mla_ref, cos_ref, sin_ref, gq_ref, wqa_ref, wqb_ref, gkv_ref, wk_ref, wv_ref, q_ref, k_ref, v_ref)


def _inproj(x2, g, w_cat, cos, sin, gq, wqa, wqb, gkv, wk, wv, tm):
    t = x2.shape[0]
    row = lambda w: pl.BlockSpec((tm, w), lambda i: (i, 0))
    hw = MLA_HEADS * LANE
    return pl.pallas_call(
        _inproj_kernel,
        grid=(t // tm,),
        in_specs=[row(D_MODEL), _full((1, D_MODEL)), _full((D_MODEL, IN_W)), row(LANE), row(LANE),
                  _full((1, Q_LORA)), _full((Q_LORA, hw)), _full((Q_LORA, hw)), _full((1, KV_LORA)),
                  _full((KV_LORA, hw)), _full((KV_LORA, MLA_OUT))],
        out_specs=[row(hw), row(hw), row(MLA_OUT), row(QKV_W), row(QKV_W), row(LANE)],
        out_shape=[jax.ShapeDtypeStruct((t, hw), BF16), jax.ShapeDtypeStruct((t, hw), BF16),
                   jax.ShapeDtypeStruct((t, MLA_OUT), BF16), jax.ShapeDtypeStruct((t, QKV_W), BF16),
                   jax.ShapeDtypeStruct((t, QKV_W), BF16), jax.ShapeDtypeStruct((t, LANE), F32)],
        scratch_shapes=[pltpu.VMEM((tm, MLA_IN_W), F32)],
        compiler_params=_cparams("parallel"),
        name="inproj",
    )(x2, g, w_cat, cos, sin, gq, wqa, wqb, gkv, wk, wv)


def _mla_prep(in_ref, cos_ref, sin_ref, gq_ref, wqa_ref, wqb_ref, gkv_ref, wk_ref, wv_ref, q_ref, k_ref, v_ref):
    cos = cos_ref[...]
    sin = sin_ref[...]
    qn = _rms(in_ref[:, :Q_LORA], gq_ref[...]).astype(BF16)
    kn = _rms(in_ref[:, Q_LORA:Q_LORA + KV_LORA], gkv_ref[...]).astype(BF16)
    o = Q_LORA + KV_LORA
    k_rot = in_ref[:, o:o + LANE] * cos + in_ref[:, o + LANE:o + 2 * LANE] * sin
    scale = MLA_QK ** -0.5 * LOG2E
    for h in range(MLA_HEADS):
        sl = slice(h * LANE, (h + 1) * LANE)
        qa = _dot(qn, wqa_ref[:, sl])
        qb = _dot(qn, wqb_ref[:, sl])
        q_ref[:, sl] = ((qa * cos + qb * sin) * scale).astype(BF16)
        k_ref[:, sl] = (_dot(kn, wk_ref[:, sl]) + k_rot).astype(BF16)
    v_ref[...] = _dot(kn, wv_ref[...]).astype(BF16)


def _fox_prep_kernel(f_ref, bf_ref, cum_ref, cumt_ref, *, blk):
    s = f_ref.shape[1]
    row = lax.broadcasted_iota(jnp.int32, (blk, blk), 0)
    col = lax.broadcasted_iota(jnp.int32, (blk, blk), 1)
    tri = jnp.where(col <= row, 1.0, 0.0).astype(BF16)
    carry = jnp.zeros((1, LANE), F32)
    for b in range(s // blk):
        f = f_ref[0, b * blk:(b + 1) * blk, :] + bf_ref[...]
        lf = (jnp.minimum(f, 0.0) - jnp.log1p(jnp.exp(-jnp.abs(f)))) * LOG2E
        hi = lf.astype(BF16)
        r1 = lf - hi.astype(F32)
        mid = r1.astype(BF16)
        lo = (r1 - mid.astype(F32)).astype(BF16)
        c = _dot(tri, hi) + _dot(tri, mid) + _dot(tri, lo) + carry
        cum_ref[0, b * blk:(b + 1) * blk, :] = c
        cumt_ref[0, :, b * blk:(b + 1) * blk] = c.T[:8, :]
        carry = c[blk - 1:blk, :]


def _fox_prep(f3, bf_row, blk):
    b, s, _ = f3.shape
    return pl.pallas_call(
        functools.partial(_fox_prep_kernel, blk=blk),
        grid=(b,),
        in_specs=[pl.BlockSpec((1, s, LANE), lambda i: (i, 0, 0)), _full((1, LANE))],
        out_specs=[pl.BlockSpec((1, s, LANE), lambda i: (i, 0, 0)), pl.BlockSpec((1, 8, s), lambda i: (i, 0, 0))],
        out_shape=[jax.ShapeDtypeStruct((b, s, LANE), F32), jax.ShapeDtypeStruct((b, 8, s), F32)],
        compiler_params=_cparams("parallel"),
        name="fox_prep",
    )(f3, bf_row)


def _softmax_update(h, s, v, m_ref, l_ref, acc_ref):
    m_old = m_ref[h]
    m_new = jnp.maximum(m_old, jnp.max(s, axis=-1, keepdims=True))
    alpha = jnp.exp2(m_old - m_new)
    p = [jnp.exp2(s[:, n * LANE:(n + 1) * LANE] - m_new) for n in range(s.shape[1] // LANE)]
    part = p[0]
    for p_n in p[1:]:
        part = part + p_n
    l_ref[h] = alpha * l_ref[h] + part
    m_ref[h] = m_new
    acc_ref[h] = alpha * acc_ref[h] + _dot(jnp.concatenate(p, axis=1).astype(BF16), v)


def _softmax_reset(m_ref, l_ref, acc_ref):
    m_ref[...] = jnp.full(m_ref.shape, NEG_INF, F32)
    l_ref[...] = jnp.zeros(l_ref.shape, F32)
    acc_ref[...] = jnp.zeros(acc_ref.shape, F32)


def _half_mask(tq, half):
    lane = lax.broadcasted_iota(jnp.int32, (tq, LANE), 1)
    return (lane < HEAD_DIM) if half == 0 else (lane >= HEAD_DIM)


def _store_head_pairs(o_ref, heads, value_of):
    tq = o_ref.shape[1]
    lo_half = _half_mask(tq, 0)
    for hb in range(o_ref.shape[2] // LANE):
        lo = value_of(2 * hb)
        hi = value_of(2 * hb + 1) if 2 * hb + 1 < heads else jnp.zeros_like(lo)
        o_ref[0, :, hb * LANE:(hb + 1) * LANE] = jnp.where(lo_half, lo, hi).astype(o_ref.dtype)


def _masked_queries(q_ref, qm_ref, heads):
    tq = q_ref.shape[1]
    for h in range(heads):
        qf = q_ref[0, :, (h // 2) * LANE:(h // 2 + 1) * LANE].astype(F32)
        qm_ref[h] = jnp.where(_half_mask(tq, h % 2), qf, 0.0).astype(BF16)


def _block_iotas(tq):
    return lax.broadcasted_iota(jnp.int32, (tq, tq), 0), lax.broadcasted_iota(jnp.int32, (tq, tq), 1)


def _mla_attn_kernel(q_ref, k_ref, v_ref, o_ref, m_ref, l_ref, acc_ref, *, tq):
    i = pl.program_id(1)
    _softmax_reset(m_ref, l_ref, acc_ref)

    def block(j, diagonal):
        rows = pl.ds(pl.multiple_of(j * tq, tq), tq)
        def scores(h):
            sl = slice(h * LANE, (h + 1) * LANE)
            return _dot_nt(q_ref[0, :, sl], k_ref[0, rows, sl])

        s_next = scores(0)
        for h in range(MLA_HEADS):
            s = s_next
            if h + 1 < MLA_HEADS:
                s_next = scores(h + 1)
            if diagonal:
                r, c = _block_iotas(tq)
                s = jnp.where((c // CHUNK) <= (r // CHUNK), s, NEG_INF)
            _softmax_update(h, s, v_ref[0, rows, (h // 2) * LANE:(h // 2 + 1) * LANE], m_ref, l_ref, acc_ref)

    def body(j, carry):
        block(j, False)
        return carry

    lax.fori_loop(0, i, body, 0)
    block(i, True)
    _store_head_pairs(o_ref, MLA_HEADS, lambda h: acc_ref[h] / jnp.sum(l_ref[h], axis=-1, keepdims=True))


def _fox_attn_kernel(q_ref, k_ref, v_ref, cum_ref, cumt_ref, o_ref, qm_ref, cq_ref, m_ref, l_ref, acc_ref, *, tq):
    i = pl.program_id(1)
    _softmax_reset(m_ref, l_ref, acc_ref)
    _masked_queries(q_ref, qm_ref, FOX_HEADS)
    for h in range(FOX_HEADS):
        cq_ref[h] = jnp.broadcast_to(cum_ref[0, :, h:h + 1], (tq, LANE))

    def block(j, diagonal):
        start = pl.multiple_of(j * tq, tq)
        rows = pl.ds(start, tq)
        def scores(h):
            return _dot_nt(qm_ref[h], k_ref[0, rows, (h // 2) * LANE:(h // 2 + 1) * LANE])

        s_next = scores(0)
        for h in range(FOX_HEADS):
            sl = slice((h // 2) * LANE, (h // 2 + 1) * LANE)
            s = s_next
            if h + 1 < FOX_HEADS:
                s_next = scores(h + 1)
            ck = cumt_ref[0, h:h + 1, rows]
            s = jnp.concatenate([s[:, n * LANE:(n + 1) * LANE] + (cq_ref[h] - ck[:, n * LANE:(n + 1) * LANE])
                                 for n in range(tq // LANE)], axis=1)
            if diagonal:
                r, c = _block_iotas(tq)
                s = jnp.where(c <= r, s, NEG_INF)
            _softmax_update(h, s, v_ref[0, rows, sl], m_ref, l_ref, acc_ref)

    def body(j, carry):
        block(j, False)
        return carry

    lax.fori_loop(0, i, body, 0)
    block(i, True)
    _store_head_pairs(o_ref, FOX_HEADS, lambda h: acc_ref[h] / jnp.sum(l_ref[h], axis=-1, keepdims=True))


def _sb_attn_kernel(q_ref, k_ref, v_ref, o_ref, qm_ref, suffix_ref, rest_ref, acc_ref, *, tq):
    i = pl.program_id(1)
    sub = suffix_ref.shape[0]
    r, c = _block_iotas(sub)
    suffix_ref[...] = jnp.where(r > c, 1.0, 0.0).astype(BF16)
    for h in range(SB_HEADS):
        rest_ref[h] = jnp.zeros(rest_ref.shape[1:], F32)
        acc_ref[h] = jnp.zeros(acc_ref.shape[1:], F32)
    _masked_queries(q_ref, qm_ref, SB_HEADS)

    def block(j, diagonal):
        rows = pl.ds(pl.multiple_of(j * tq, tq), tq)
        def scores(h):
            return _dot_nt(qm_ref[h], k_ref[0, rows, (h // 2) * LANE:(h // 2 + 1) * LANE])

        z_next = scores(0)
        for h in range(SB_HEADS):
            sl = slice((h // 2) * LANE, (h // 2 + 1) * LANE)
            z = z_next
            if h + 1 < SB_HEADS:
                z_next = scores(h + 1)
            log_b = jnp.minimum(z, 0.0) - jnp.log(1.0 + jnp.exp2(-jnp.abs(z))) * LOG2E
            log_1m = log_b - z
            if diagonal:
                strict = _block_iotas(tq)[1] < _block_iotas(tq)[0]
                log_1m = jnp.where(strict, log_1m, 0.0)
            later = rest_ref[h]
            a = [None] * (tq // sub)
            for k in reversed(range(tq // sub)):
                cols = slice(k * sub, (k + 1) * sub)
                within = _dot(log_1m[:, cols].astype(BF16), suffix_ref[...])
                a[k] = jnp.concatenate([jnp.exp2(log_b[:, cols][:, n * LANE:(n + 1) * LANE]
                                                 + within[:, n * LANE:(n + 1) * LANE] + later)
                                        for n in range(sub // LANE)], axis=1)
                later = later + jnp.sum(log_1m[:, cols], axis=-1, keepdims=True)
            a = jnp.concatenate(a, axis=1)
            if diagonal:
                a = jnp.where(strict, a, 0.0)
            acc_ref[h] += _dot(a.astype(BF16), v_ref[0, rows, sl])
            rest_ref[h] = later

    block(i, True)

    def body(n, carry):
        block(i - 1 - n, False)
        return carry

    lax.fori_loop(0, i, body, 0)
    _store_head_pairs(o_ref, SB_HEADS, lambda h: acc_ref[h])


def _attention(kernel, q, k, v, cols, extra, extra_specs, scratch, qw, vw, tq, name):
    b, s, _ = q.shape
    blk = lambda w, n=0: pl.BlockSpec((1, tq, w), lambda bi, i: (bi, i, n))
    seq = lambda w, n: pl.BlockSpec((1, s, w), lambda bi, i: (bi, 0, n))
    return pl.pallas_call(
        functools.partial(kernel, tq=tq),
        grid=(b, s // tq),
        in_specs=[blk(qw, cols[0]), seq(qw, cols[1]), seq(vw, cols[2])] + extra_specs,
        out_specs=blk(vw),
        out_shape=jax.ShapeDtypeStruct((b, s, vw), BF16),
        scratch_shapes=scratch,
        compiler_params=_cparams("parallel", "parallel"),
        name=name,
    )(q, k, v, *extra)


def _head_state(heads, tq):
    return pltpu.VMEM((heads, tq, LANE), F32)


def _merge_kernel(x_ref, g_ref, wg_ref, bg_ref, oa_ref, ob_ref, oc_ref, wa_ref, wb_ref, wc_ref, wo_ref, out_ref):
    x = x_ref[...]
    hn = _rms(x, g_ref[...]).astype(BF16)
    y = None
    for n, (o_ref, w_ref) in enumerate(((oa_ref, wa_ref), (ob_ref, wb_ref), (oc_ref, wc_ref))):
        sl = slice(n * D_MODEL, (n + 1) * D_MODEL)
        gate = jax.nn.sigmoid(_dot(hn, wg_ref[:, sl]) + bg_ref[:, sl])
        term = gate * _dot(o_ref[...], w_ref[...])
        y = term if y is None else y + term
    out_ref[...] = x + _dot(y.astype(BF16), wo_ref[...])


def _merge(x2, g, wg, bg, oa, ob, oc, wa, wb, wc, wo, tm):
    t = x2.shape[0]
    row = lambda w: pl.BlockSpec((tm, w), lambda i: (i, 0))
    return pl.pallas_call(
        _merge_kernel,
        grid=(t // tm,),
        in_specs=[row(D_MODEL), _full((1, D_MODEL)), _full((D_MODEL, N_BRANCH * D_MODEL)),
                  _full((1, N_BRANCH * D_MODEL)), row(HEAD_PAD_W), row(HEAD_PAD_W), row(HEAD_PAD_W),
                  _full((HEAD_PAD_W, D_MODEL)), _full((HEAD_PAD_W, D_MODEL)), _full((HEAD_PAD_W, D_MODEL)),
                  _full((D_MODEL, D_MODEL))],
        out_specs=row(D_MODEL),
        out_shape=jax.ShapeDtypeStruct((t, D_MODEL), F32),
        compiler_params=_cparams("parallel"),
        name="merge",
    )(x2, g, wg, bg, oa, ob, oc, wa, wb, wc, wo)


def _kth_largest_rows(x, k):
    for _ in range(k - 1):
        x = jnp.where(x >= jnp.max(x, axis=0, keepdims=True), -jnp.inf, x)
    return jnp.max(x, axis=0, keepdims=True)


def _peer_route_kernel(x_ref, g_ref, wq_ref, keys_ref, h_ref, rank2_ref, e2_ref, cnt_ref, e1_ref,
                       s1_ref, s2_ref, q_ref, top_ref, cand_ref):
    hn = _rms(x_ref[...], g_ref[...]).astype(BF16)
    h_ref[...] = hn
    wide = 2 * D_KEY
    for n in range(PEER_HEADS * D_KEY // wide):
        q_ref[:, n * wide:(n + 1) * wide] = _dot(hn, wq_ref[:, n * wide:(n + 1) * wide]).astype(BF16)
    for h in range(PEER_HEADS):
        for side, s_ref in ((0, s1_ref), (1, s2_ref)):
            o = h * D_KEY + side * HALF_KEY
            s_ref[h] = _dot_nt(keys_ref[side], q_ref[:, o:o + HALF_KEY])

    def lane_tile(lt, carry):
        lanes = pl.ds(pl.multiple_of(lt * LANE, LANE), LANE)
        for h in range(PEER_HEADS):
            rank2 = jnp.full((N_KEYS, LANE), float(PEER_TOPK), F32)
            for side, s_ref in ((0, s1_ref), (1, s2_ref)):
                s = s_ref[h, :, lanes]
                for r in range(PEER_TOPK):
                    m = jnp.max(s, axis=0, keepdims=True)
                    top_ref[side, r:r + 1, :] = m
                    hit = s >= m
                    if side == 1:
                        rank2 = jnp.where(hit, float(r), rank2)
                    s = jnp.where(hit, -jnp.inf, s)
            v2_top8 = top_ref[1, 0:8, :]
            sub = lax.broadcasted_iota(jnp.int32, (8, LANE), 0)
            v2_top4_twice = jnp.where(sub < 4, v2_top8, pltpu.roll(v2_top8, 4, axis=0))
            cand_ref[0:16, :] = top_ref[0, 0:1, :] + top_ref[1]
            for a in (1, 2, 3):
                cand_ref[8 + 8 * a:16 + 8 * a, :] = top_ref[0, a:a + 1, :] + v2_top8
            for n, a in enumerate((4, 6)):
                v1_pair = jnp.where(sub < 4, top_ref[0, a:a + 1, :], top_ref[0, a + 1:a + 2, :])
                cand_ref[40 + 8 * n:48 + 8 * n, :] = v1_pair + v2_top4_twice
            cand_ref[56:64, :] = top_ref[0, 8:16, :] + top_ref[1, 0:1, :]
            cand = cand_ref[...]
            tau = _kth_largest_rows(cand, PEER_TOPK)
            m1 = top_ref[0, 0:1, :]
            m2 = top_ref[1, 0:1, :]
            zsum = jnp.sum(jnp.where(cand >= tau, jnp.exp(cand - (m1 + m2)), 0.0), axis=0, keepdims=True)
            s1 = s1_ref[h, :, lanes]
            thr = tau - s1
            cnt = jnp.zeros((N_KEYS, LANE), F32)
            for b in range(PEER_TOPK):
                cnt = jnp.where(top_ref[1, b:b + 1, :] >= thr, float(b + 1), cnt)
            rank2_ref[h, :, lanes] = rank2.astype(BF16)
            e2_ref[h, :, lanes] = jnp.exp(s2_ref[h, :, lanes] - m2).astype(BF16)
            cnt_ref[h, :, lanes] = cnt
            e1_ref[h, :, lanes] = 0.5 * jnp.exp(s1 - m1) / zsum
        return carry

    lax.fori_loop(0, x_ref.shape[0] // LANE, lane_tile, 0)


def _peer_route(x2, g, wq, keys, tm):
    t = x2.shape[0]
    row = pl.BlockSpec((tm, D_MODEL), lambda i: (i, 0))
    sc = pl.BlockSpec((PEER_HEADS, N_KEYS, tm), lambda i: (0, 0, i))
    return pl.pallas_call(
        _peer_route_kernel,
        grid=(t // tm,),
        in_specs=[row, _full((1, D_MODEL)), _full((D_MODEL, PEER_HEADS * D_KEY)), _full((2, N_KEYS, HALF_KEY))],
        out_specs=[row, sc, sc, sc, sc],
        out_shape=[jax.ShapeDtypeStruct((t, D_MODEL), BF16),
                   jax.ShapeDtypeStruct((PEER_HEADS, N_KEYS, t), BF16),
                   jax.ShapeDtypeStruct((PEER_HEADS, N_KEYS, t), BF16),
                   jax.ShapeDtypeStruct((PEER_HEADS, N_KEYS, t), F32),
                   jax.ShapeDtypeStruct((PEER_HEADS, N_KEYS, t), F32)],
        scratch_shapes=[pltpu.VMEM((PEER_HEADS, N_KEYS, tm), F32), pltpu.VMEM((PEER_HEADS, N_KEYS, tm), F32),
                        pltpu.VMEM((tm, PEER_HEADS * D_KEY), BF16), pltpu.VMEM((2, PEER_TOPK, LANE), F32),
                        pltpu.VMEM((4 * PEER_TOPK, LANE), F32)],
        compiler_params=_cparams("parallel"),
        name="peer_route",
    )(x2, g, wq, keys)


def _peer_dense_step(c, parity, h_ref, rank2_ref, e2_ref, cnt_ref, e1_ref, u_ref, vt_ref, pre_ref, p_ref, acc_ref,
                     *, te, n_chunks):
    tm = h_ref.shape[0]
    rows_per_step = te // N_KEYS
    assert 2 * rows_per_step == 8
    part = 64
    gate_w = 2 * LANE
    score_buf, gate_buf, fold_buf = parity, 1 - parity, parity
    key_tile = pl.ds(pl.multiple_of((jnp.clip(c - 1, 0, n_chunks - 1) // 2) * 8, 8), 8)
    row0 = rows_per_step * (1 - parity)

    def score(piece):
        rows = slice((piece // 2) * te // 2, (piece // 2 + 1) * te // 2)
        toks = slice((piece % 2) * gate_w, (piece % 2 + 1) * gate_w)
        u_rows = slice(parity * te + rows.start, parity * te + rows.stop)
        pre_ref[score_buf, rows, toks] = _dot_nt(u_ref[u_rows, :], h_ref[toks, :])

    def fold(piece):
        rows = slice((piece // 2) * D_MODEL // 4, (piece // 2 + 1) * D_MODEL // 4)
        toks = slice((piece % 2) * gate_w, (piece % 2 + 1) * gate_w)
        acc_ref[rows, toks] += _dot(vt_ref[parity, rows, :], p_ref[fold_buf, :, toks])

    def gate_heads(lt, rp, w, heads):
        lanes = slice(lt * gate_w, (lt + 1) * gate_w)
        for h in heads:
            rank2 = rank2_ref[h, rp * part:(rp + 1) * part, lanes]
            e2 = e2_ref[h, rp * part:(rp + 1) * part, lanes]
            cnt = cnt_ref[h, key_tile, lanes]
            e1 = e1_ref[h, key_tile, lanes]
            for j in range(rows_per_step):
                cnt_j = jnp.broadcast_to(cnt[row0 + j:row0 + j + 1, :], (part, gate_w)).astype(BF16)
                e1_j = jnp.broadcast_to(e1[row0 + j:row0 + j + 1, :], (part, gate_w)).astype(BF16)
                term = jnp.where(rank2 < cnt_j, e2, jnp.zeros_like(e2)) * e1_j
                w[j] = term if w[j] is None else w[j] + term

    def gate_store(lt, rp, w):
        lanes = slice(lt * gate_w, (lt + 1) * gate_w)
        for j in range(rows_per_step):
            rows = slice(j * N_KEYS + rp * part, j * N_KEYS + (rp + 1) * part)
            pre = pre_ref[gate_buf, rows, lanes]
            act = pre * (1.0 + lax.erf(pre * (0.5 ** 0.5)))
            p_ref[gate_buf, rows, lanes] = w[j] * act.astype(BF16)

    assert tm // gate_w == 2 and N_KEYS // part == 2
    mxu = [lambda n=n: score(n) for n in range(4)] + [lambda n=n: fold(n) for n in range(8)]
    order = [0, 4, 5, 1, 6, 7, 2, 8, 9, 3, 10, 11]
    pieces = iter([mxu[n] for n in order])
    for lt in range(2):
        for rp in range(2):
            w = [None] * rows_per_step
            for h in range(PEER_HEADS):
                gate_heads(lt, rp, w, (h,))
                if h % 3 == 1:
                    next(pieces)()
            gate_store(lt, rp, w)


def _peer_dense_kernel(x_ref, h_ref, rank2_ref, e2_ref, cnt_ref, e1_ref, u_ref, vt_ref, gf_ref, out_ref,
                       pre_ref, p_ref, acc_ref, *, te, n_chunks, final_norm):
    g = pl.program_id(1)

    @pl.when(g == 0)
    def _():
        acc_ref[...] = jnp.zeros_like(acc_ref)
        pre_ref[...] = jnp.zeros_like(pre_ref)
        p_ref[...] = jnp.zeros_like(p_ref)

    for parity in range(2):
        @pl.when((g >= 0) if parity == 0 else (g <= n_chunks))
        def _(parity=parity):
            _peer_dense_step(2 * g + parity, parity, h_ref, rank2_ref, e2_ref, cnt_ref, e1_ref, u_ref, vt_ref,
                             pre_ref, p_ref, acc_ref, te=te, n_chunks=n_chunks)

    @pl.when(g == n_chunks // 2)
    def _():
        y = x_ref[...] + acc_ref[...].T
        if final_norm:
            y = _rms(y, gf_ref[...])
        out_ref[...] = y


def _peer_dense(x2, hn, rank2, e2, cnt, e1, u, vt, gf, tm, te, final_norm):
    t = x2.shape[0]
    n_chunks = N_EXPERTS // te
    row = pl.BlockSpec((tm, D_MODEL), lambda i, c: (i, 0))
    sc = pl.BlockSpec((PEER_HEADS, N_KEYS, tm), lambda i, c: (0, 0, i))
    return pl.pallas_call(
        functools.partial(_peer_dense_kernel, te=te, n_chunks=n_chunks, final_norm=final_norm),
        grid=(t // tm, n_chunks // 2 + 1),
        in_specs=[row, row, sc, sc, sc, sc,
                  pl.BlockSpec((2 * te, D_MODEL), lambda i, g: (jnp.minimum(g, n_chunks // 2 - 1), 0)),
                  pl.BlockSpec((2, D_MODEL, te), lambda i, g: (jnp.clip(g - 1, 0, n_chunks // 2 - 1), 0, 0)),
                  pl.BlockSpec((1, D_MODEL), lambda i, c: (0, 0))],
        out_specs=row,
        out_shape=jax.ShapeDtypeStruct((t, D_MODEL), F32),
        scratch_shapes=[pltpu.VMEM((2, te, tm), F32), pltpu.VMEM((2, te, tm), BF16),
                        pltpu.VMEM((D_MODEL, tm), F32)],
        compiler_params=_cparams("parallel", "arbitrary"),
        name="peer_dense",
    )(x2, hn, rank2, e2, cnt, e1, u, vt, gf)


def _pad_cols(w, width):
    return jnp.pad(w, ((0, 0), (0, width - w.shape[1])))


def _rope_partner(w):
    half = MLA_ROPE // 2
    return jnp.concatenate([-w[:, half:], w[:, :half]], axis=1)


def _rope_lanes(w):
    return jnp.pad(w, ((0, 0), (MLA_NOPE, LANE - MLA_NOPE - MLA_ROPE)))


def _layer_weights(w_in, w_uq, w_ukv, w_o_sb, w_o_fox):
    o = IN_OFFSETS
    seg = lambda n: w_in[:, o[n]:o[n + 1]]
    scale = HEAD_DIM ** -0.5 * LOG2E
    w_kr = seg(2)
    w_cat = jnp.concatenate(
        [seg(0), seg(1), _rope_lanes(w_kr), _rope_lanes(_rope_partner(w_kr)),
         _pad_cols(seg(3) * scale, HEAD_PAD_W), _pad_cols(seg(4), HEAD_PAD_W), _pad_cols(seg(5), HEAD_PAD_W),
         _pad_cols(seg(6) * scale, HEAD_PAD_W), _pad_cols(seg(7), HEAD_PAD_W), _pad_cols(seg(8), HEAD_PAD_W),
         _pad_cols(seg(9), LANE)], axis=1).astype(BF16)
    w_gate = seg(10).astype(BF16)
    uq = w_uq.reshape(Q_LORA, MLA_HEADS, MLA_QK)
    nope, rope = uq[..., :MLA_NOPE], uq[..., MLA_NOPE:]
    zpad = jnp.zeros((Q_LORA, MLA_HEADS, LANE - MLA_QK), F32)
    wqa = jnp.concatenate([nope, rope, zpad], axis=-1).reshape(Q_LORA, MLA_HEADS * LANE).astype(BF16)
    partner = jnp.concatenate([-rope[..., MLA_ROPE // 2:], rope[..., :MLA_ROPE // 2]], axis=-1)
    wqb = jnp.concatenate([jnp.zeros_like(nope), partner, zpad], axis=-1).reshape(Q_LORA, MLA_HEADS * LANE).astype(BF16)
    ukv = w_ukv.reshape(KV_LORA, MLA_HEADS, MLA_NOPE + MLA_V)
    wk = jnp.pad(ukv[..., :MLA_NOPE], ((0, 0), (0, 0), (0, LANE - MLA_NOPE))).reshape(KV_LORA, MLA_HEADS * LANE).astype(BF16)
    wv = ukv[..., MLA_NOPE:].reshape(KV_LORA, MLA_OUT).astype(BF16)
    pad_rows = lambda w: jnp.pad(w, ((0, HEAD_PAD_W - w.shape[0]), (0, 0))).astype(BF16)
    return w_cat, w_gate, wqa, wqb, wk, wv, pad_rows(w_o_sb), pad_rows(w_o_fox)


def _tiles(b, s):
    t = b * s
    tm = min(512, t)
    tq = min(512, s)
    return t, tm, tq


def kernel(x, positions, norm1_g, w_in, mla_q_norm_g, w_uq, mla_kv_norm_g, w_ukv, fox_b_f, w_o_mla, w_o_sb,
           w_o_fox, b_gate, w_out, norm2_g, peer_w_q, peer_sub_keys, peer_u, peer_v, final_norm_g):
    b, s, d = x.shape
    assert d == D_MODEL and MLA_OUT == HEAD_PAD_W
    depth = w_in.shape[0]
    t, tm, tq = _tiles(b, s)
    assert t % tm == 0 and s % tq == 0 and tq % CHUNK == 0
    te = 4 * N_KEYS
    x2 = x.reshape(t, d)
    cos, sin = _rope_tables(positions, tm)
    gf = final_norm_g.reshape(1, d)
    for l in range(depth):
        w_cat, w_gate, wqa, wqb, wk, wv, wo_sb, wo_fox = _layer_weights(w_in[l], w_uq[l], w_ukv[l], w_o_sb[l], w_o_fox[l])
        g1 = norm1_g[l].reshape(1, d)
        q_a, k_a, v_a, sb, fx, f = _inproj(x2, g1, w_cat, cos, sin, mla_q_norm_g[l].reshape(1, Q_LORA), wqa, wqb,
                                           mla_kv_norm_g[l].reshape(1, KV_LORA), wk, wv, tm)
        bf_row = jnp.zeros((1, LANE), F32).at[0, :FOX_HEADS].set(fox_b_f[l])
        cum, cumt = _fox_prep(f.reshape(b, s, LANE), bf_row, min(256, s))
        hw = MLA_HEADS * LANE
        acc_a = _head_state(MLA_HEADS, tq)
        o_a = _attention(_mla_attn_kernel, q_a.reshape(b, s, hw), k_a.reshape(b, s, hw), v_a.reshape(b, s, MLA_OUT),
                         (0, 0, 0), [], [], [acc_a, acc_a, acc_a], hw, MLA_OUT, tq, "mla_attn")
        sb3 = sb.reshape(b, s, QKV_W)
        fx3 = fx.reshape(b, s, QKV_W)
        acc_b = _head_state(SB_HEADS, tq)
        qm = pltpu.VMEM((SB_HEADS, tq, LANE), BF16)
        o_b = _attention(_sb_attn_kernel, sb3, sb3, sb3, (0, 1, 2), [], [],
                         [qm, pltpu.VMEM((min(256, tq),) * 2, BF16), acc_b, acc_b], HEAD_PAD_W, HEAD_PAD_W, tq,
                         "sb_attn")
        fox_specs = [pl.BlockSpec((1, tq, LANE), lambda bi, i: (bi, i, 0)),
                     pl.BlockSpec((1, 8, s), lambda bi, i: (bi, 0, 0))]
        o_c = _attention(_fox_attn_kernel, fx3, fx3, fx3, (0, 1, 2), [cum, cumt], fox_specs,
                         [qm, acc_b, acc_b, acc_b, acc_b], HEAD_PAD_W, HEAD_PAD_W, tq, "fox_attn")
        x2 = _merge(x2, g1, w_gate, b_gate[l].reshape(1, -1), o_a.reshape(t, MLA_OUT), o_b.reshape(t, HEAD_PAD_W),
                    o_c.reshape(t, HEAD_PAD_W), w_o_mla[l].astype(BF16), wo_sb, wo_fox, w_out[l].astype(BF16), tm)
        hn, rank2, e2, cnt, e1 = _peer_route(x2, norm2_g[l].reshape(1, d), peer_w_q[l].astype(BF16),
                                             peer_sub_keys[l].astype(BF16), tm)
        vt = peer_v[l].astype(BF16).reshape(N_EXPERTS // te, te, d).transpose(0, 2, 1)
        x2 = _peer_dense(x2, hn, rank2, e2, cnt, e1, peer_u[l].astype(BF16), vt, gf, tm, te,
                         final_norm=(l == depth - 1))
    return x2.reshape(b, s, d)
```

```python
import functools

import numpy as np
import jax
import jax.numpy as jnp
from jax import lax
from jax.experimental import pallas as pl
from jax.experimental.pallas import tpu as pltpu

F32 = jnp.float32
BF16 = jnp.bfloat16

D_MODEL = 1024
CHUNK = 64
HEAD_DIM = 64
NORM_EPS = 1e-6
NEG_INF = -1e30
MLA_HEADS = 6
MLA_NOPE = 64
MLA_ROPE = 32
MLA_V = 64
Q_LORA = 256
KV_LORA = 128
ROPE_THETA = 10000.0
SB_HEADS = 5
FOX_HEADS = 5
N_BRANCH = 3
PEER_HEADS = 8
N_KEYS = 128
N_EXPERTS = N_KEYS * N_KEYS
D_KEY = 256
HALF_KEY = D_KEY // 2
PEER_TOPK = 16
MLA_QK = MLA_NOPE + MLA_ROPE
SB_W = SB_HEADS * HEAD_DIM
FOX_W = FOX_HEADS * HEAD_DIM
MLA_OUT = MLA_HEADS * MLA_V
IN_SPLITS = (Q_LORA, KV_LORA, MLA_ROPE, SB_W, SB_W, SB_W, FOX_W, FOX_W, FOX_W, FOX_HEADS, N_BRANCH * D_MODEL)
IN_OFFSETS = tuple(int(o) for o in np.cumsum((0,) + IN_SPLITS))

LANE = 128
HEAD_PAD_W = 384
MLA_IN_W = Q_LORA + KV_LORA + 2 * LANE
QKV_W = 3 * HEAD_PAD_W
IN_W = MLA_IN_W + 2 * QKV_W + LANE
VMEM_LIMIT = 48 * 1024 * 1024

LOG2E = 1.4426950408889634
_NT = (((1,), (1,)), ((), ()))


def _cparams(*sem):
    return pltpu.CompilerParams(dimension_semantics=sem, vmem_limit_bytes=VMEM_LIMIT)


def _rms(x, g):
    return x * lax.rsqrt(jnp.mean(x * x, axis=-1, keepdims=True) + NORM_EPS) * g


def _dot(a, b):
    return jnp.dot(a, b, preferred_element_type=F32)


def _dot_nt(a, b):
    return lax.dot_general(a, b, _NT, preferred_element_type=F32)


def _full(shape):
    return pl.BlockSpec(shape, lambda *_: (0,) * len(shape))


def _rope_kernel(pos_ref, inv_ref, cos_ref, sin_ref):
    ang = pos_ref[...].astype(F32) * inv_ref[...]
    cos_ref[...] = jnp.cos(ang)
    sin_ref[...] = jnp.sin(ang)


def _rope_tables(positions, tm):
    t = positions.size
    inv = ROPE_THETA ** (-jnp.arange(0, MLA_ROPE, 2, dtype=F32) / MLA_ROPE)
    inv_row = jnp.zeros((1, LANE), F32).at[0, MLA_NOPE:MLA_NOPE + MLA_ROPE].set(jnp.concatenate([inv, inv]))
    return pl.pallas_call(
        _rope_kernel,
        grid=(t // tm,),
        in_specs=[pl.BlockSpec((tm, 1), lambda i: (i, 0)), _full((1, LANE))],
        out_specs=[pl.BlockSpec((tm, LANE), lambda i: (i, 0))] * 2,
        out_shape=[jax.ShapeDtypeStruct((t, LANE), F32)] * 2,
        compiler_params=_cparams("parallel"),
        name="rope_tables",
    )(positions.reshape(t, 1), inv_row)


def _inproj_kernel(x_ref, g_ref, w_ref, cos_ref, sin_ref, gq_ref, wqa_ref, wqb_ref, gkv_ref, wk_ref, wv_ref,
                   q_ref, k_ref, v_ref, sb_ref, fx_ref, f_ref, mla_ref):
    hn = _rms(x_ref[...], g_ref[...]).astype(BF16)
    o = 0
    for ref, width in ((mla_ref, MLA_IN_W), (sb_ref, QKV_W), (fx_ref, QKV_W), (f_ref, LANE)):
        ref[...] = _dot(hn, w_ref[:, o:o + width]).astype(ref.dtype)
        o += width
    _mla_prep(mla_ref, cos_ref, sin_ref, gq_ref, wqa_ref, wqb_ref, gkv_ref, wk_ref, wv_ref, q_ref, k_ref, v_ref)


def _inproj(x2, g, w_cat, cos, sin, gq, wqa, wqb, gkv, wk, wv, tm):
    t = x2.shape[0]
    row = lambda w: pl.BlockSpec((tm, w), lambda i: (i, 0))
    hw = MLA_HEADS * LANE
    return pl.pallas_call(
        _inproj_kernel,
        grid=(t // tm,),
        in_specs=[row(D_MODEL), _full((1, D_MODEL)), _full((D_MODEL, IN_W)), row(LANE), row(LANE),
                  _full((1, Q_LORA)), _full((Q_LORA, hw)), _full((Q_LORA, hw)), _full((1, KV_LORA)),
                  _full((KV_LORA, hw)), _full((KV_LORA, MLA_OUT))],
        out_specs=[row(hw), row(hw), row(MLA_OUT), row(QKV_W), row(QKV_W), row(LANE)],
        out_shape=[jax.ShapeDtypeStruct((t, hw), BF16), jax.ShapeDtypeStruct((t, hw), BF16),
                   jax.ShapeDtypeStruct((t, MLA_OUT), BF16), jax.ShapeDtypeStruct((t, QKV_W), BF16),
                   jax.ShapeDtypeStruct((t, QKV_W), BF16), jax.ShapeDtypeStruct((t, LANE), F32)],
        scratch_shapes=[pltpu.VMEM((tm, MLA_IN_W), F32)],
        compiler_params=_cparams("parallel"),
        name="inproj",
    )(x2, g, w_cat, cos, sin, gq, wqa, wqb, gkv, wk, wv)


def _mla_prep(in_ref, cos_ref, sin_ref, gq_ref, wqa_ref, wqb_ref, gkv_ref, wk_ref, wv_ref, q_ref, k_ref, v_ref):
    cos = cos_ref[...]
    sin = sin_ref[...]
    qn = _rms(in_ref[:, :Q_LORA], gq_ref[...]).astype(BF16)
    kn = _rms(in_ref[:, Q_LORA:Q_LORA + KV_LORA], gkv_ref[...]).astype(BF16)
    o = Q_LORA + KV_LORA
    k_rot = in_ref[:, o:o + LANE] * cos + in_ref[:, o + LANE:o + 2 * LANE] * sin
    scale = MLA_QK ** -0.5 * LOG2E
    for h in range(MLA_HEADS):
        sl = slice(h * LANE, (h + 1) * LANE)
        qa = _dot(qn, wqa_ref[:, sl])
        qb = _dot(qn, wqb_ref[:, sl])
        q_ref[:, sl] = ((qa * cos + qb * sin) * scale).astype(BF16)
        k_ref[:, sl] = (_dot(kn, wk_ref[:, sl]) + k_rot).astype(BF16)
    v_ref[...] = _dot(kn, wv_ref[...]).astype(BF16)


def _fox_prep_kernel(f_ref, bf_ref, cum_ref, cumt_ref, *, blk):
    s = f_ref.shape[1]
    row = lax.broadcasted_iota(jnp.int32, (blk, blk), 0)
    col = lax.broadcasted_iota(jnp.int32, (blk, blk), 1)
    tri = jnp.where(col <= row, 1.0, 0.0).astype(BF16)
    carry = jnp.zeros((1, LANE), F32)
    for b in range(s // blk):
        f = f_ref[0, b * blk:(b + 1) * blk, :] + bf_ref[...]
        lf = (jnp.minimum(f, 0.0) - jnp.log1p(jnp.exp(-jnp.abs(f)))) * LOG2E
        hi = lf.astype(BF16)
        r1 = lf - hi.astype(F32)
        mid = r1.astype(BF16)
        lo = (r1 - mid.astype(F32)).astype(BF16)
        c = _dot(tri, hi) + _dot(tri, mid) + _dot(tri, lo) + carry
        cum_ref[0, b * blk:(b + 1) * blk, :] = c
        cumt_ref[0, :, b * blk:(b + 1) * blk] = c.T[:8, :]
        carry = c[blk - 1:blk, :]


def _fox_prep(f3, bf_row, blk):
    b, s, _ = f3.shape
    return pl.pallas_call(
        functools.partial(_fox_prep_kernel, blk=blk),
        grid=(b,),
        in_specs=[pl.BlockSpec((1, s, LANE), lambda i: (i, 0, 0)), _full((1, LANE))],
        out_specs=[pl.BlockSpec((1, s, LANE), lambda i: (i, 0, 0)), pl.BlockSpec((1, 8, s), lambda i: (i, 0, 0))],
        out_shape=[jax.ShapeDtypeStruct((b, s, LANE), F32), jax.ShapeDtypeStruct((b, 8, s), F32)],
        compiler_params=_cparams("parallel"),
        name="fox_prep",
    )(f3, bf_row)


def _softmax_update(h, s, v, m_ref, l_ref, acc_ref):
    m_old = m_ref[h]
    m_new = jnp.maximum(m_old, jnp.max(s, axis=-1, keepdims=True))
    alpha = jnp.exp2(m_old - m_new)
    p = [jnp.exp2(s[:, n * LANE:(n + 1) * LANE] - m_new) for n in range(s.shape[1] // LANE)]
    part = p[0]
    for p_n in p[1:]:
        part = part + p_n
    l_ref[h] = alpha * l_ref[h] + part
    m_ref[h] = m_new
    acc_ref[h] = alpha * acc_ref[h] + _dot(jnp.concatenate(p, axis=1).astype(BF16), v)


def _softmax_reset(m_ref, l_ref, acc_ref):
    m_ref[...] = jnp.full(m_ref.shape, NEG_INF, F32)
    l_ref[...] = jnp.zeros(l_ref.shape, F32)
    acc_ref[...] = jnp.zeros(acc_ref.shape, F32)


def _half_mask(tq, half):
    lane = lax.broadcasted_iota(jnp.int32, (tq, LANE), 1)
    return (lane < HEAD_DIM) if half == 0 else (lane >= HEAD_DIM)


def _store_head_pairs(o_ref, heads, value_of):
    tq = o_ref.shape[1]
    lo_half = _half_mask(tq, 0)
    for hb in range(o_ref.shape[2] // LANE):
        lo = value_of(2 * hb)
        hi = value_of(2 * hb + 1) if 2 * hb + 1 < heads else jnp.zeros_like(lo)
        o_ref[0, :, hb * LANE:(hb + 1) * LANE] = jnp.where(lo_half, lo, hi).astype(o_ref.dtype)


def _masked_queries(q_ref, qm_ref, heads):
    tq = q_ref.shape[1]
    for h in range(heads):
        qf = q_ref[0, :, (h // 2) * LANE:(h // 2 + 1) * LANE].astype(F32)
        qm_ref[h] = jnp.where(_half_mask(tq, h % 2), qf, 0.0).astype(BF16)


def _block_iotas(tq):
    return lax.broadcasted_iota(jnp.int32, (tq, tq), 0), lax.broadcasted_iota(jnp.int32, (tq, tq), 1)


def _mla_attn_kernel(q_ref, k_ref, v_ref, o_ref, m_ref, l_ref, acc_ref, *, tq):
    i = pl.program_id(1)
    _softmax_reset(m_ref, l_ref, acc_ref)

    def block(j, diagonal):
        rows = pl.ds(pl.multiple_of(j * tq, tq), tq)
        def scores(h):
            sl = slice(h * LANE, (h + 1) * LANE)
            return _dot_nt(q_ref[0, :, sl], k_ref[0, rows, sl])

        s_next = scores(0)
        for h in range(MLA_HEADS):
            s = s_next
            if h + 1 < MLA_HEADS:
                s_next = scores(h + 1)
            if diagonal:
                r, c = _block_iotas(tq)
                s = jnp.where((c // CHUNK) <= (r // CHUNK), s, NEG_INF)
            _softmax_update(h, s, v_ref[0, rows, (h // 2) * LANE:(h // 2 + 1) * LANE], m_ref, l_ref, acc_ref)

    def body(j, carry):
        block(j, False)
        return carry

    lax.fori_loop(0, i, body, 0)
    block(i, True)
    _store_head_pairs(o_ref, MLA_HEADS, lambda h: acc_ref[h] / jnp.sum(l_ref[h], axis=-1, keepdims=True))


def _fox_attn_kernel(q_ref, k_ref, v_ref, cum_ref, cumt_ref, o_ref, qm_ref, cq_ref, m_ref, l_ref, acc_ref, *, tq):
    i = pl.program_id(1)
    _softmax_reset(m_ref, l_ref, acc_ref)
    _masked_queries(q_ref, qm_ref, FOX_HEADS)
    for h in range(FOX_HEADS):
        cq_ref[h] = jnp.broadcast_to(cum_ref[0, :, h:h + 1], (tq, LANE))

    def block(j, diagonal):
        start = pl.multiple_of(j * tq, tq)
        rows = pl.ds(start, tq)
        def scores(h):
            return _dot_nt(qm_ref[h], k_ref[0, rows, (h // 2) * LANE:(h // 2 + 1) * LANE])

        s_next = scores(0)
        for h in range(FOX_HEADS):
            sl = slice((h // 2) * LANE, (h // 2 + 1) * LANE)
            s = s_next
            if h + 1 < FOX_HEADS:
                s_next = scores(h + 1)
            ck = cumt_ref[0, h:h + 1, rows]
            s = jnp.concatenate([s[:, n * LANE:(n + 1) * LANE] + (cq_ref[h] - ck[:, n * LANE:(n + 1) * LANE])
                                 for n in range(tq // LANE)], axis=1)
            if diagonal:
                r, c = _block_iotas(tq)
                s = jnp.where(c <= r, s, NEG_INF)
            _softmax_update(h, s, v_ref[0, rows, sl], m_ref, l_ref, acc_ref)

    def body(j, carry):
        block(j, False)
        return carry

    lax.fori_loop(0, i, body, 0)
    block(i, True)
    _store_head_pairs(o_ref, FOX_HEADS, lambda h: acc_ref[h] / jnp.sum(l_ref[h], axis=-1, keepdims=True))


def _sb_attn_kernel(q_ref, k_ref, v_ref, o_ref, qm_ref, suffix_ref, rest_ref, acc_ref, *, tq):
    i = pl.program_id(1)
    sub = suffix_ref.shape[0]
    r, c = _block_iotas(sub)
    suffix_ref[...] = jnp.where(r > c, 1.0, 0.0).astype(BF16)
    for h in range(SB_HEADS):
        rest_ref[h] = jnp.zeros(rest_ref.shape[1:], F32)
        acc_ref[h] = jnp.zeros(acc_ref.shape[1:], F32)
    _masked_queries(q_ref, qm_ref, SB_HEADS)

    def block(j, diagonal):
        rows = pl.ds(pl.multiple_of(j * tq, tq), tq)
        def scores(h):
            return _dot_nt(qm_ref[h], k_ref[0, rows, (h // 2) * LANE:(h // 2 + 1) * LANE])

        z_next = scores(0)
        for h in range(SB_HEADS):
            sl = slice((h // 2) * LANE, (h // 2 + 1) * LANE)
            z = z_next
            if h + 1 < SB_HEADS:
                z_next = scores(h + 1)
            log_b = jnp.minimum(z, 0.0) - jnp.log(1.0 + jnp.exp2(-jnp.abs(z))) * LOG2E
            log_1m = log_b - z
            if diagonal:
                strict = _block_iotas(tq)[1] < _block_iotas(tq)[0]
                log_1m = jnp.where(strict, log_1m, 0.0)
            later = rest_ref[h]
            a = [None] * (tq // sub)
            for k in reversed(range(tq // sub)):
                cols = slice(k * sub, (k + 1) * sub)
                within = _dot(log_1m[:, cols].astype(BF16), suffix_ref[...])
                a[k] = jnp.concatenate([jnp.exp2(log_b[:, cols][:, n * LANE:(n + 1) * LANE]
                                                 + within[:, n * LANE:(n + 1) * LANE] + later)
                                        for n in range(sub // LANE)], axis=1)
                later = later + jnp.sum(log_1m[:, cols], axis=-1, keepdims=True)
            a = jnp.concatenate(a, axis=1)
            if diagonal:
                a = jnp.where(strict, a, 0.0)
            acc_ref[h] += _dot(a.astype(BF16), v_ref[0, rows, sl])
            rest_ref[h] = later

    block(i, True)

    def body(n, carry):
        block(i - 1 - n, False)
        return carry

    lax.fori_loop(0, i, body, 0)
    _store_head_pairs(o_ref, SB_HEADS, lambda h: acc_ref[h])


def _attention(kernel, q, k, v, cols, extra, extra_specs, scratch, qw, vw, tq, name):
    b, s, _ = q.shape
    blk = lambda w, n=0: pl.BlockSpec((1, tq, w), lambda bi, i: (bi, i, n))
    seq = lambda w, n: pl.BlockSpec((1, s, w), lambda bi, i: (bi, 0, n))
    return pl.pallas_call(
        functools.partial(kernel, tq=tq),
        grid=(b, s // tq),
        in_specs=[blk(qw, cols[0]), seq(qw, cols[1]), seq(vw, cols[2])] + extra_specs,
        out_specs=blk(vw),
        out_shape=jax.ShapeDtypeStruct((b, s, vw), BF16),
        scratch_shapes=scratch,
        compiler_params=_cparams("parallel", "parallel"),
        name=name,
    )(q, k, v, *extra)


def _head_state(heads, tq):
    return pltpu.VMEM((heads, tq, LANE), F32)


def _merge_kernel(x_ref, g_ref, wg_ref, bg_ref, oa_ref, ob_ref, oc_ref, wa_ref, wb_ref, wc_ref, wo_ref, out_ref):
    x = x_ref[...]
    hn = _rms(x, g_ref[...]).astype(BF16)
    y = None
    for n, (o_ref, w_ref) in enumerate(((oa_ref, wa_ref), (ob_ref, wb_ref), (oc_ref, wc_ref))):
        sl = slice(n * D_MODEL, (n + 1) * D_MODEL)
        gate = jax.nn.sigmoid(_dot(hn, wg_ref[:, sl]) + bg_ref[:, sl])
        term = gate * _dot(o_ref[...], w_ref[...])
        y = term if y is None else y + term
    out_ref[...] = x + _dot(y.astype(BF16), wo_ref[...])


def _merge(x2, g, wg, bg, oa, ob, oc, wa, wb, wc, wo, tm):
    t = x2.shape[0]
    row = lambda w: pl.BlockSpec((tm, w), lambda i: (i, 0))
    return pl.pallas_call(
        _merge_kernel,
        grid=(t // tm,),
        in_specs=[row(D_MODEL), _full((1, D_MODEL)), _full((D_MODEL, N_BRANCH * D_MODEL)),
                  _full((1, N_BRANCH * D_MODEL)), row(HEAD_PAD_W), row(HEAD_PAD_W), row(HEAD_PAD_W),
                  _full((HEAD_PAD_W, D_MODEL)), _full((HEAD_PAD_W, D_MODEL)), _full((HEAD_PAD_W, D_MODEL)),
                  _full((D_MODEL, D_MODEL))],
        out_specs=row(D_MODEL),
        out_shape=jax.ShapeDtypeStruct((t, D_MODEL), F32),
        compiler_params=_cparams("parallel"),
        name="merge",
    )(x2, g, wg, bg, oa, ob, oc, wa, wb, wc, wo)


def _kth_largest_rows(x, k):
    for _ in range(k - 1):
        x = jnp.where(x >= jnp.max(x, axis=0, keepdims=True), -jnp.inf, x)
    return jnp.max(x, axis=0, keepdims=True)


def _peer_route_kernel(x_ref, g_ref, wq_ref, keys_ref, h_ref, rank2_ref, e2_ref, cnt_ref, e1_ref,
                       s1_ref, s2_ref, q_ref, top_ref, cand_ref):
    hn = _rms(x_ref[...], g_ref[...]).astype(BF16)
    h_ref[...] = hn
    wide = 2 * D_KEY
    for n in range(PEER_HEADS * D_KEY // wide):
        q_ref[:, n * wide:(n + 1) * wide] = _dot(hn, wq_ref[:, n * wide:(n + 1) * wide]).astype(BF16)
    for h in range(PEER_HEADS):
        for side, s_ref in ((0, s1_ref), (1, s2_ref)):
            o = h * D_KEY + side * HALF_KEY
            s_ref[h] = _dot_nt(keys_ref[side], q_ref[:, o:o + HALF_KEY])

    def lane_tile(lt, carry):
        lanes = pl.ds(pl.multiple_of(lt * LANE, LANE), LANE)
        for h in range(PEER_HEADS):
            rank2 = jnp.full((N_KEYS, LANE), float(PEER_TOPK), F32)
            for side, s_ref in ((0, s1_ref), (1, s2_ref)):
                s = s_ref[h, :, lanes]
                for r in range(PEER_TOPK):
                    m = jnp.max(s, axis=0, keepdims=True)
                    top_ref[side, r:r + 1, :] = m
                    hit = s >= m
                    if side == 1:
                        rank2 = jnp.where(hit, float(r), rank2)
                    s = jnp.where(hit, -jnp.inf, s)
            v2_top8 = top_ref[1, 0:8, :]
            sub = lax.broadcasted_iota(jnp.int32, (8, LANE), 0)
            v2_top4_twice = jnp.where(sub < 4, v2_top8, pltpu.roll(v2_top8, 4, axis=0))
            cand_ref[0:16, :] = top_ref[0, 0:1, :] + top_ref[1]
            for a in (1, 2, 3):
                cand_ref[8 + 8 * a:16 + 8 * a, :] = top_ref[0, a:a + 1, :] + v2_top8
            for n, a in enumerate((4, 6)):
                v1_pair = jnp.where(sub < 4, top_ref[0, a:a + 1, :], top_ref[0, a + 1:a + 2, :])
                cand_ref[40 + 8 * n:48 + 8 * n, :] = v1_pair + v2_top4_twice
            cand_ref[56:64, :] = top_ref[0, 8:16, :] + top_ref[1, 0:1, :]
            cand = cand_ref[...]
            tau = _kth_largest_rows(cand, PEER_TOPK)
            m1 = top_ref[0, 0:1, :]
            m2 = top_ref[1, 0:1, :]
            zsum = jnp.sum(jnp.where(cand >= tau, jnp.exp(cand - (m1 + m2)), 0.0), axis=0, keepdims=True)
            s1 = s1_ref[h, :, lanes]
            thr = tau - s1
            cnt = jnp.zeros((N_KEYS, LANE), F32)
            for b in range(PEER_TOPK):
                cnt = jnp.where(top_ref[1, b:b + 1, :] >= thr, float(b + 1), cnt)
            rank2_ref[h, :, lanes] = rank2.astype(BF16)
            e2_ref[h, :, lanes] = jnp.exp(s2_ref[h, :, lanes] - m2).astype(BF16)
            cnt_ref[h, :, lanes] = cnt
            e1_ref[h, :, lanes] = 0.5 * jnp.exp(s1 - m1) / zsum
        return carry

    lax.fori_loop(0, x_ref.shape[0] // LANE, lane_tile, 0)


def _peer_route(x2, g, wq, keys, tm):
    t = x2.shape[0]
    row = pl.BlockSpec((tm, D_MODEL), lambda i: (i, 0))
    sc = pl.BlockSpec((None, PEER_HEADS, N_KEYS, tm), lambda i: (i, 0, 0, 0))
    return pl.pallas_call(
        _peer_route_kernel,
        grid=(t // tm,),
        in_specs=[row, _full((1, D_MODEL)), _full((D_MODEL, PEER_HEADS * D_KEY)), _full((2, N_KEYS, HALF_KEY))],
        out_specs=[row, sc, sc, sc, sc],
        out_shape=[jax.ShapeDtypeStruct((t, D_MODEL), BF16),
                   jax.ShapeDtypeStruct((t // tm, PEER_HEADS, N_KEYS, tm), BF16),
                   jax.ShapeDtypeStruct((t // tm, PEER_HEADS, N_KEYS, tm), BF16),
                   jax.ShapeDtypeStruct((t // tm, PEER_HEADS, N_KEYS, tm), F32),
                   jax.ShapeDtypeStruct((t // tm, PEER_HEADS, N_KEYS, tm), F32)],
        scratch_shapes=[pltpu.VMEM((PEER_HEADS, N_KEYS, tm), F32), pltpu.VMEM((PEER_HEADS, N_KEYS, tm), F32),
                        pltpu.VMEM((tm, PEER_HEADS * D_KEY), BF16), pltpu.VMEM((2, PEER_TOPK, LANE), F32),
                        pltpu.VMEM((4 * PEER_TOPK, LANE), F32)],
        compiler_params=_cparams("parallel"),
        name="peer_route",
    )(x2, g, wq, keys)


def _peer_dense_step(c, parity, h_ref, rank2_ref, e2_ref, cnt_ref, e1_ref, u_ref, vt_ref, pre_ref, p_ref, acc_ref,
                     *, te, n_chunks, stages):
    tm = h_ref.shape[0]
    rows_per_step = te // N_KEYS
    assert 2 * rows_per_step == 8
    part = 64
    gate_w = 2 * LANE
    score_buf, gate_buf, fold_buf = parity, 1 - parity, parity
    key_tile = pl.ds(pl.multiple_of((jnp.clip(c - 1, 0, n_chunks - 1) // 2) * 8, 8), 8)
    row0 = rows_per_step * (1 - parity)

    def score(piece):
        rows = slice((piece // 2) * te // 2, (piece // 2 + 1) * te // 2)
        toks = slice((piece % 2) * gate_w, (piece % 2 + 1) * gate_w)
        u_rows = slice(parity * te + rows.start, parity * te + rows.stop)
        pre_ref[score_buf, rows, toks] = _dot_nt(u_ref[u_rows, :], h_ref[toks, :])

    def fold(piece):
        rows = slice((piece // 2) * D_MODEL // 4, (piece // 2 + 1) * D_MODEL // 4)
        toks = slice((piece % 2) * gate_w, (piece % 2 + 1) * gate_w)
        acc_ref[rows, toks] += _dot(vt_ref[parity, rows, :], p_ref[fold_buf, :, toks])

    def gate_heads(lt, rp, w, heads):
        lanes = slice(lt * gate_w, (lt + 1) * gate_w)
        for h in heads:
            rank2 = rank2_ref[h, rp * part:(rp + 1) * part, lanes]
            e2 = e2_ref[h, rp * part:(rp + 1) * part, lanes]
            cnt = cnt_ref[h, key_tile, lanes]
            e1 = e1_ref[h, key_tile, lanes]
            for j in range(rows_per_step):
                cnt_j = jnp.broadcast_to(cnt[row0 + j:row0 + j + 1, :], (part, gate_w)).astype(BF16)
                e1_j = jnp.broadcast_to(e1[row0 + j:row0 + j + 1, :], (part, gate_w)).astype(BF16)
                term = jnp.where(rank2 < cnt_j, e2, jnp.zeros_like(e2)) * e1_j
                w[j] = term if w[j] is None else w[j] + term

    def gate_store(lt, rp, w):
        lanes = slice(lt * gate_w, (lt + 1) * gate_w)
        for j in range(rows_per_step):
            rows = slice(j * N_KEYS + rp * part, j * N_KEYS + (rp + 1) * part)
            pre = pre_ref[gate_buf, rows, lanes]
            act = pre * (1.0 + lax.erf(pre * (0.5 ** 0.5)))
            p_ref[gate_buf, rows, lanes] = w[j] * act.astype(BF16)

    assert tm // gate_w == 2 and N_KEYS // part == 2
    mxu = [lambda n=n: score(n) for n in range(4)] + [lambda n=n: fold(n) for n in range(8)]
    order = [0, 4, 5, 1, 6, 7, 2, 8, 9, 3, 10, 11]
    live = [n for n in order if ("score" if n < 4 else "fold") in stages]
    if "gate" not in stages:
        for n in live:
            mxu[n]()
        return
    slots = 4 * 3
    pieces = iter([mxu[n] for n in live] + [None] * (slots - len(live)))
    for lt in range(2):
        for rp in range(2):
            w = [None] * rows_per_step
            for h in range(PEER_HEADS):
                gate_heads(lt, rp, w, (h,))
                if h % 3 == 1:
                    piece = next(pieces)
                    if piece is not None:
                        piece()
            gate_store(lt, rp, w)


def _peer_dense_kernel(x_ref, h_ref, rank2_ref, e2_ref, cnt_ref, e1_ref, u_ref, vt_ref, gf_ref, out_ref,
                       pre_ref, p_ref, acc_ref, *, te, n_chunks, final_norm):
    g = pl.program_id(1)

    last = n_chunks // 2

    @pl.when(g == 0)
    def _():
        acc_ref[...] = jnp.zeros_like(acc_ref)

    full = ("score", "gate", "fold")
    plans = (((0, 0), (("score",), ("score", "gate"))),
             ((1, last - 1), (full, full)),
             ((last, last), (("gate", "fold"), ("fold",))))
    for (lo, hi), stage_sets in plans:
        for parity, stages in enumerate(stage_sets):
            guard = jnp.logical_and(g >= lo, g <= hi) if parity == 0 else jnp.logical_and(g > lo - 1, g < hi + 1)

            @pl.when(guard)
            def _(parity=parity, stages=stages):
                _peer_dense_step(2 * g + parity, parity, h_ref, rank2_ref, e2_ref, cnt_ref, e1_ref, u_ref, vt_ref,
                                 pre_ref, p_ref, acc_ref, te=te, n_chunks=n_chunks, stages=stages)

    @pl.when(g == last)
    def _():
        y = x_ref[...] + acc_ref[...].T
        if final_norm:
            y = _rms(y, gf_ref[...])
        out_ref[...] = y


def _peer_dense(x2, hn, rank2, e2, cnt, e1, u, vt, gf, tm, te, final_norm):
    t = x2.shape[0]
    n_chunks = N_EXPERTS // te
    row = pl.BlockSpec((tm, D_MODEL), lambda i, c: (i, 0))
    sc = pl.BlockSpec((None, PEER_HEADS, N_KEYS, tm), lambda i, c: (i, 0, 0, 0))
    return pl.pallas_call(
        functools.partial(_peer_dense_kernel, te=te, n_chunks=n_chunks, final_norm=final_norm),
        grid=(t // tm, n_chunks // 2 + 1),
        in_specs=[row, row, sc, sc, sc, sc,
                  pl.BlockSpec((2 * te, D_MODEL), lambda i, g: (jnp.minimum(g, n_chunks // 2 - 1), 0)),
                  pl.BlockSpec((2, D_MODEL, te), lambda i, g: (jnp.clip(g - 1, 0, n_chunks // 2 - 1), 0, 0)),
                  pl.BlockSpec((1, D_MODEL), lambda i, c: (0, 0))],
        out_specs=row,
        out_shape=jax.ShapeDtypeStruct((t, D_MODEL), F32),
        scratch_shapes=[pltpu.VMEM((2, te, tm), F32), pltpu.VMEM((2, te, tm), BF16),
                        pltpu.VMEM((D_MODEL, tm), F32)],
        compiler_params=_cparams("parallel", "arbitrary"),
        name="peer_dense",
    )(x2, hn, rank2, e2, cnt, e1, u, vt, gf)


def _pad_cols(w, width):
    return jnp.pad(w, ((0, 0), (0, width - w.shape[1])))


def _rope_partner(w):
    half = MLA_ROPE // 2
    return jnp.concatenate([-w[:, half:], w[:, :half]], axis=1)


def _rope_lanes(w):
    return jnp.pad(w, ((0, 0), (MLA_NOPE, LANE - MLA_NOPE - MLA_ROPE)))


def _layer_weights(w_in, w_uq, w_ukv, w_o_sb, w_o_fox):
    o = IN_OFFSETS
    seg = lambda n: w_in[:, o[n]:o[n + 1]]
    scale = HEAD_DIM ** -0.5 * LOG2E
    w_kr = seg(2)
    w_cat = jnp.concatenate(
        [seg(0), seg(1), _rope_lanes(w_kr), _rope_lanes(_rope_partner(w_kr)),
         _pad_cols(seg(3) * scale, HEAD_PAD_W), _pad_cols(seg(4), HEAD_PAD_W), _pad_cols(seg(5), HEAD_PAD_W),
         _pad_cols(seg(6) * scale, HEAD_PAD_W), _pad_cols(seg(7), HEAD_PAD_W), _pad_cols(seg(8), HEAD_PAD_W),
         _pad_cols(seg(9), LANE)], axis=1).astype(BF16)
    w_gate = seg(10).astype(BF16)
    uq = w_uq.reshape(Q_LORA, MLA_HEADS, MLA_QK)
    nope, rope = uq[..., :MLA_NOPE], uq[..., MLA_NOPE:]
    zpad = jnp.zeros((Q_LORA, MLA_HEADS, LANE - MLA_QK), F32)
    wqa = jnp.concatenate([nope, rope, zpad], axis=-1).reshape(Q_LORA, MLA_HEADS * LANE).astype(BF16)
    partner = jnp.concatenate([-rope[..., MLA_ROPE // 2:], rope[..., :MLA_ROPE // 2]], axis=-1)
    wqb = jnp.concatenate([jnp.zeros_like(nope), partner, zpad], axis=-1).reshape(Q_LORA, MLA_HEADS * LANE).astype(BF16)
    ukv = w_ukv.reshape(KV_LORA, MLA_HEADS, MLA_NOPE + MLA_V)
    wk = jnp.pad(ukv[..., :MLA_NOPE], ((0, 0), (0, 0), (0, LANE - MLA_NOPE))).reshape(KV_LORA, MLA_HEADS * LANE).astype(BF16)
    wv = ukv[..., MLA_NOPE:].reshape(KV_LORA, MLA_OUT).astype(BF16)
    pad_rows = lambda w: jnp.pad(w, ((0, HEAD_PAD_W - w.shape[0]), (0, 0))).astype(BF16)
    return w_cat, w_gate, wqa, wqb, wk, wv, pad_rows(w_o_sb), pad_rows(w_o_fox)


def _tiles(b, s):
    t = b * s
    tm = min(512, t)
    tq = min(512, s)
    return t, tm, tq


def kernel(x, positions, norm1_g, w_in, mla_q_norm_g, w_uq, mla_kv_norm_g, w_ukv, fox_b_f, w_o_mla, w_o_sb,
           w_o_fox, b_gate, w_out, norm2_g, peer_w_q, peer_sub_keys, peer_u, peer_v, final_norm_g):
    b, s, d = x.shape
    assert d == D_MODEL and MLA_OUT == HEAD_PAD_W
    depth = w_in.shape[0]
    t, tm, tq = _tiles(b, s)
    assert t % tm == 0 and s % tq == 0 and tq % CHUNK == 0
    te = 4 * N_KEYS
    x2 = x.reshape(t, d)
    cos, sin = _rope_tables(positions, tm)
    gf = final_norm_g.reshape(1, d)
    for l in range(depth):
        w_cat, w_gate, wqa, wqb, wk, wv, wo_sb, wo_fox = _layer_weights(w_in[l], w_uq[l], w_ukv[l], w_o_sb[l], w_o_fox[l])
        g1 = norm1_g[l].reshape(1, d)
        q_a, k_a, v_a, sb, fx, f = _inproj(x2, g1, w_cat, cos, sin, mla_q_norm_g[l].reshape(1, Q_LORA), wqa, wqb,
                                           mla_kv_norm_g[l].reshape(1, KV_LORA), wk, wv, tm)
        bf_row = jnp.zeros((1, LANE), F32).at[0, :FOX_HEADS].set(fox_b_f[l])
        cum, cumt = _fox_prep(f.reshape(b, s, LANE), bf_row, min(256, s))
        hw = MLA_HEADS * LANE
        acc_a = _head_state(MLA_HEADS, tq)
        o_a = _attention(_mla_attn_kernel, q_a.reshape(b, s, hw), k_a.reshape(b, s, hw), v_a.reshape(b, s, MLA_OUT),
                         (0, 0, 0), [], [], [acc_a, acc_a, acc_a], hw, MLA_OUT, tq, "mla_attn")
        sb3 = sb.reshape(b, s, QKV_W)
        fx3 = fx.reshape(b, s, QKV_W)
        acc_b = _head_state(SB_HEADS, tq)
        qm = pltpu.VMEM((SB_HEADS, tq, LANE), BF16)
        o_b = _attention(_sb_attn_kernel, sb3, sb3, sb3, (0, 1, 2), [], [],
                         [qm, pltpu.VMEM((min(256, tq),) * 2, BF16), acc_b, acc_b], HEAD_PAD_W, HEAD_PAD_W, tq,
                         "sb_attn")
        fox_specs = [pl.BlockSpec((1, tq, LANE), lambda bi, i: (bi, i, 0)),
                     pl.BlockSpec((1, 8, s), lambda bi, i: (bi, 0, 0))]
        o_c = _attention(_fox_attn_kernel, fx3, fx3, fx3, (0, 1, 2), [cum, cumt], fox_specs,
                         [qm, acc_b, acc_b, acc_b, acc_b], HEAD_PAD_W, HEAD_PAD_W, tq, "fox_attn")
        x2 = _merge(x2, g1, w_gate, b_gate[l].reshape(1, -1), o_a.reshape(t, MLA_OUT), o_b.reshape(t, HEAD_PAD_W),
                    o_c.reshape(t, HEAD_PAD_W), w_o_mla[l].astype(BF16), wo_sb, wo_fox, w_out[l].astype(BF16), tm)
        hn, rank2, e2, cnt, e1 = _peer_route(x2, norm2_g[l].reshape(1, d), peer_w_q[l].astype(BF16),
                                             peer_sub_keys[l].astype(BF16), tm)
        vt = peer_v[l].astype(BF16).reshape(N_EXPERTS // te, te, d).transpose(0, 2, 1)
        x2 = _peer_dense(x2, hn, rank2, e2, cnt, e1, peer_u[l].astype(BF16), vt, gf, tm, te,
                         final_norm=(l == depth - 1))
    return x2.reshape(b, s, d)
```

```python
import functools

import numpy as np
import jax
import jax.numpy as jnp
from jax import lax
from jax.experimental import pallas as pl
from jax.experimental.pallas import tpu as pltpu

F32 = jnp.float32
BF16 = jnp.bfloat16

D_MODEL = 1024
CHUNK = 64
HEAD_DIM = 64
NORM_EPS = 1e-6
NEG_INF = -1e30
MLA_HEADS = 6
MLA_NOPE = 64
MLA_ROPE = 32
MLA_V = 64
Q_LORA = 256
KV_LORA = 128
ROPE_THETA = 10000.0
SB_HEADS = 5
FOX_HEADS = 5
N_BRANCH = 3
PEER_HEADS = 8
N_KEYS = 128
N_EXPERTS = N_KEYS * N_KEYS
D_KEY = 256
HALF_KEY = D_KEY // 2
PEER_TOPK = 16
MLA_QK = MLA_NOPE + MLA_ROPE
SB_W = SB_HEADS * HEAD_DIM
FOX_W = FOX_HEADS * HEAD_DIM
MLA_OUT = MLA_HEADS * MLA_V
IN_SPLITS = (Q_LORA, KV_LORA, MLA_ROPE, SB_W, SB_W, SB_W, FOX_W, FOX_W, FOX_W, FOX_HEADS, N_BRANCH * D_MODEL)
IN_OFFSETS = tuple(int(o) for o in np.cumsum((0,) + IN_SPLITS))

LANE = 128
HEAD_PAD_W = 384
MLA_IN_W = Q_LORA + KV_LORA + 2 * LANE
QKV_W = 3 * HEAD_PAD_W
IN_W = MLA_IN_W + 2 * QKV_W + LANE
VMEM_LIMIT = 48 * 1024 * 1024
TOKEN_TILE = 512
ATTN_TILE = 512
SUFFIX_TILE = 256
EXPERT_CHUNK = 4 * N_KEYS

LOG2E = 1.4426950408889634
_NT = (((1,), (1,)), ((), ()))


def _cparams(*sem):
    return pltpu.CompilerParams(dimension_semantics=sem, vmem_limit_bytes=VMEM_LIMIT)


def _rms(x, g):
    return x * lax.rsqrt(jnp.mean(x * x, axis=-1, keepdims=True) + NORM_EPS) * g


def _dot(a, b):
    return jnp.dot(a, b, preferred_element_type=F32)


def _dot_nt(a, b):
    return lax.dot_general(a, b, _NT, preferred_element_type=F32)


def _full(shape):
    return pl.BlockSpec(shape, lambda *_: (0,) * len(shape))


def _rope_kernel(pos_ref, inv_ref, cos_ref, sin_ref):
    ang = pos_ref[...].astype(F32) * inv_ref[...]
    cos_ref[...] = jnp.cos(ang)
    sin_ref[...] = jnp.sin(ang)


def _rope_tables(positions, tm):
    t = positions.size
    inv = ROPE_THETA ** (-jnp.arange(0, MLA_ROPE, 2, dtype=F32) / MLA_ROPE)
    inv_row = jnp.zeros((1, LANE), F32).at[0, MLA_NOPE:MLA_NOPE + MLA_ROPE].set(jnp.concatenate([inv, inv]))
    return pl.pallas_call(
        _rope_kernel,
        grid=(t // tm,),
        in_specs=[pl.BlockSpec((tm, 1), lambda i: (i, 0)), _full((1, LANE))],
        out_specs=[pl.BlockSpec((tm, LANE), lambda i: (i, 0))] * 2,
        out_shape=[jax.ShapeDtypeStruct((t, LANE), F32)] * 2,
        compiler_params=_cparams("parallel"),
        name="rope_tables",
    )(positions.reshape(t, 1), inv_row)


def _inproj_kernel(x_ref, g_ref, w_ref, cos_ref, sin_ref, gq_ref, wqa_ref, wqb_ref, gkv_ref, wk_ref, wv_ref,
                   q_ref, k_ref, v_ref, sb_ref, fx_ref, f_ref, mla_ref):
    hn = _rms(x_ref[...], g_ref[...]).astype(BF16)
    o = 0
    for ref, width in ((mla_ref, MLA_IN_W), (sb_ref, QKV_W), (fx_ref, QKV_W), (f_ref, LANE)):
        ref[...] = _dot(hn, w_ref[:, o:o + width]).astype(ref.dtype)
        o += width
    _mla_prep(mla_ref, cos_ref, sin_ref, gq_ref, wqa_ref, wqb_ref, gkv_ref, wk_ref, wv_ref, q_ref, k_ref, v_ref)


def _inproj(x2, g, w_cat, cos, sin, gq, wqa, wqb, gkv, wk, wv, tm):
    t = x2.shape[0]
    row = lambda w: pl.BlockSpec((tm, w), lambda i: (i, 0))
    hw = MLA_HEADS * LANE
    return pl.pallas_call(
        _inproj_kernel,
        grid=(t // tm,),
        in_specs=[row(D_MODEL), _full((1, D_MODEL)), _full((D_MODEL, IN_W)), row(LANE), row(LANE),
                  _full((1, Q_LORA)), _full((Q_LORA, hw)), _full((Q_LORA, hw)), _full((1, KV_LORA)),
                  _full((KV_LORA, hw)), _full((KV_LORA, MLA_OUT))],
        out_specs=[row(hw), row(hw), row(MLA_OUT), row(QKV_W), row(QKV_W), row(LANE)],
        out_shape=[jax.ShapeDtypeStruct((t, hw), BF16), jax.ShapeDtypeStruct((t, hw), BF16),
                   jax.ShapeDtypeStruct((t, MLA_OUT), BF16), jax.ShapeDtypeStruct((t, QKV_W), BF16),
                   jax.ShapeDtypeStruct((t, QKV_W), BF16), jax.ShapeDtypeStruct((t, LANE), F32)],
        scratch_shapes=[pltpu.VMEM((tm, MLA_IN_W), F32)],
        compiler_params=_cparams("parallel"),
        name="inproj",
    )(x2, g, w_cat, cos, sin, gq, wqa, wqb, gkv, wk, wv)


def _mla_prep(in_ref, cos_ref, sin_ref, gq_ref, wqa_ref, wqb_ref, gkv_ref, wk_ref, wv_ref, q_ref, k_ref, v_ref):
    cos = cos_ref[...]
    sin = sin_ref[...]
    qn = _rms(in_ref[:, :Q_LORA], gq_ref[...]).astype(BF16)
    kn = _rms(in_ref[:, Q_LORA:Q_LORA + KV_LORA], gkv_ref[...]).astype(BF16)
    o = Q_LORA + KV_LORA
    k_rot = in_ref[:, o:o + LANE] * cos + in_ref[:, o + LANE:o + 2 * LANE] * sin
    scale = MLA_QK ** -0.5 * LOG2E
    for h in range(MLA_HEADS):
        sl = slice(h * LANE, (h + 1) * LANE)
        qa = _dot(qn, wqa_ref[:, sl])
        qb = _dot(qn, wqb_ref[:, sl])
        q_ref[:, sl] = ((qa * cos + qb * sin) * scale).astype(BF16)
        k_ref[:, sl] = (_dot(kn, wk_ref[:, sl]) + k_rot).astype(BF16)
    v_ref[...] = _dot(kn, wv_ref[...]).astype(BF16)


def _fox_prep_kernel(f_ref, bf_ref, cum_ref, cumt_ref, *, blk):
    s = f_ref.shape[1]
    row = lax.broadcasted_iota(jnp.int32, (blk, blk), 0)
    col = lax.broadcasted_iota(jnp.int32, (blk, blk), 1)
    tri = jnp.where(col <= row, 1.0, 0.0).astype(BF16)
    carry = jnp.zeros((1, LANE), F32)
    for b in range(s // blk):
        f = f_ref[0, b * blk:(b + 1) * blk, :] + bf_ref[...]
        lf = (jnp.minimum(f, 0.0) - jnp.log1p(jnp.exp(-jnp.abs(f)))) * LOG2E
        hi = lf.astype(BF16)
        r1 = lf - hi.astype(F32)
        mid = r1.astype(BF16)
        lo = (r1 - mid.astype(F32)).astype(BF16)
        c = _dot(tri, hi) + _dot(tri, mid) + _dot(tri, lo) + carry
        cum_ref[0, b * blk:(b + 1) * blk, :] = c
        cumt_ref[0, :, b * blk:(b + 1) * blk] = c.T[:8, :]
        carry = c[blk - 1:blk, :]


def _fox_prep(f3, bf_row, blk):
    b, s, _ = f3.shape
    return pl.pallas_call(
        functools.partial(_fox_prep_kernel, blk=blk),
        grid=(b,),
        in_specs=[pl.BlockSpec((1, s, LANE), lambda i: (i, 0, 0)), _full((1, LANE))],
        out_specs=[pl.BlockSpec((1, s, LANE), lambda i: (i, 0, 0)), pl.BlockSpec((1, 8, s), lambda i: (i, 0, 0))],
        out_shape=[jax.ShapeDtypeStruct((b, s, LANE), F32), jax.ShapeDtypeStruct((b, 8, s), F32)],
        compiler_params=_cparams("parallel"),
        name="fox_prep",
    )(f3, bf_row)


def _softmax_update(h, s, v, m_ref, l_ref, acc_ref):
    m_old = m_ref[h]
    m_new = jnp.maximum(m_old, jnp.max(s, axis=-1, keepdims=True))
    alpha = jnp.exp2(m_old - m_new)
    p = [jnp.exp2(s[:, n * LANE:(n + 1) * LANE] - m_new) for n in range(s.shape[1] // LANE)]
    part = p[0]
    for p_n in p[1:]:
        part = part + p_n
    l_ref[h] = alpha * l_ref[h] + part
    m_ref[h] = m_new
    acc_ref[h] = alpha * acc_ref[h] + _dot(jnp.concatenate(p, axis=1).astype(BF16), v)


def _softmax_reset(m_ref, l_ref, acc_ref):
    m_ref[...] = jnp.full(m_ref.shape, NEG_INF, F32)
    l_ref[...] = jnp.zeros(l_ref.shape, F32)
    acc_ref[...] = jnp.zeros(acc_ref.shape, F32)


def _half_mask(tq, half):
    lane = lax.broadcasted_iota(jnp.int32, (tq, LANE), 1)
    return (lane < HEAD_DIM) if half == 0 else (lane >= HEAD_DIM)


def _store_head_pairs(o_ref, heads, value_of):
    tq = o_ref.shape[1]
    lo_half = _half_mask(tq, 0)
    for hb in range(o_ref.shape[2] // LANE):
        lo = value_of(2 * hb)
        hi = value_of(2 * hb + 1) if 2 * hb + 1 < heads else jnp.zeros_like(lo)
        o_ref[0, :, hb * LANE:(hb + 1) * LANE] = jnp.where(lo_half, lo, hi).astype(o_ref.dtype)


def _masked_queries(q_ref, qm_ref, heads):
    tq = q_ref.shape[1]
    for h in range(heads):
        qf = q_ref[0, :, (h // 2) * LANE:(h // 2 + 1) * LANE].astype(F32)
        qm_ref[h] = jnp.where(_half_mask(tq, h % 2), qf, 0.0).astype(BF16)


def _block_iotas(tq):
    return lax.broadcasted_iota(jnp.int32, (tq, tq), 0), lax.broadcasted_iota(jnp.int32, (tq, tq), 1)


def _mla_attn_kernel(q_ref, k_ref, v_ref, o_ref, m_ref, l_ref, acc_ref, *, tq):
    i = pl.program_id(1)
    _softmax_reset(m_ref, l_ref, acc_ref)

    def block(j, diagonal):
        rows = pl.ds(pl.multiple_of(j * tq, tq), tq)
        def scores(h):
            sl = slice(h * LANE, (h + 1) * LANE)
            return _dot_nt(q_ref[0, :, sl], k_ref[0, rows, sl])

        s_next = scores(0)
        for h in range(MLA_HEADS):
            s = s_next
            if h + 1 < MLA_HEADS:
                s_next = scores(h + 1)
            if diagonal:
                r, c = _block_iotas(tq)
                s = jnp.where((c // CHUNK) <= (r // CHUNK), s, NEG_INF)
            _softmax_update(h, s, v_ref[0, rows, (h // 2) * LANE:(h // 2 + 1) * LANE], m_ref, l_ref, acc_ref)

    def body(j, carry):
        block(j, False)
        return carry

    lax.fori_loop(0, i, body, 0)
    block(i, True)
    _store_head_pairs(o_ref, MLA_HEADS, lambda h: acc_ref[h] / jnp.sum(l_ref[h], axis=-1, keepdims=True))


def _fox_attn_kernel(q_ref, k_ref, v_ref, cum_ref, cumt_ref, o_ref, qm_ref, cq_ref, m_ref, l_ref, acc_ref, *, tq):
    i = pl.program_id(1)
    _softmax_reset(m_ref, l_ref, acc_ref)
    _masked_queries(q_ref, qm_ref, FOX_HEADS)
    for h in range(FOX_HEADS):
        cq_ref[h] = jnp.broadcast_to(cum_ref[0, :, h:h + 1], (tq, LANE))

    def block(j, diagonal):
        start = pl.multiple_of(j * tq, tq)
        rows = pl.ds(start, tq)
        def scores(h):
            return _dot_nt(qm_ref[h], k_ref[0, rows, (h // 2) * LANE:(h // 2 + 1) * LANE])

        s_next = scores(0)
        for h in range(FOX_HEADS):
            sl = slice((h // 2) * LANE, (h // 2 + 1) * LANE)
            s = s_next
            if h + 1 < FOX_HEADS:
                s_next = scores(h + 1)
            ck = cumt_ref[0, h:h + 1, rows]
            s = jnp.concatenate([s[:, n * LANE:(n + 1) * LANE] + (cq_ref[h] - ck[:, n * LANE:(n + 1) * LANE])
                                 for n in range(tq // LANE)], axis=1)
            if diagonal:
                r, c = _block_iotas(tq)
                s = jnp.where(c <= r, s, NEG_INF)
            _softmax_update(h, s, v_ref[0, rows, sl], m_ref, l_ref, acc_ref)

    def body(j, carry):
        block(j, False)
        return carry

    lax.fori_loop(0, i, body, 0)
    block(i, True)
    _store_head_pairs(o_ref, FOX_HEADS, lambda h: acc_ref[h] / jnp.sum(l_ref[h], axis=-1, keepdims=True))


def _sb_attn_kernel(q_ref, k_ref, v_ref, o_ref, qm_ref, suffix_ref, rest_ref, acc_ref, *, tq):
    i = pl.program_id(1)
    sub = suffix_ref.shape[0]
    r, c = _block_iotas(sub)
    suffix_ref[...] = jnp.where(r > c, 1.0, 0.0).astype(BF16)
    for h in range(SB_HEADS):
        rest_ref[h] = jnp.zeros(rest_ref.shape[1:], F32)
        acc_ref[h] = jnp.zeros(acc_ref.shape[1:], F32)
    _masked_queries(q_ref, qm_ref, SB_HEADS)

    def block(j, diagonal):
        rows = pl.ds(pl.multiple_of(j * tq, tq), tq)
        def scores(h):
            return _dot_nt(qm_ref[h], k_ref[0, rows, (h // 2) * LANE:(h // 2 + 1) * LANE])

        z_next = scores(0)
        for h in range(SB_HEADS):
            sl = slice((h // 2) * LANE, (h // 2 + 1) * LANE)
            z = z_next
            if h + 1 < SB_HEADS:
                z_next = scores(h + 1)
            log_b = jnp.minimum(z, 0.0) - jnp.log(1.0 + jnp.exp2(-jnp.abs(z))) * LOG2E
            log_1m = log_b - z
            if diagonal:
                strict = _block_iotas(tq)[1] < _block_iotas(tq)[0]
                log_1m = jnp.where(strict, log_1m, 0.0)
            later = rest_ref[h]
            a = [None] * (tq // sub)
            for k in reversed(range(tq // sub)):
                cols = slice(k * sub, (k + 1) * sub)
                within = _dot(log_1m[:, cols].astype(BF16), suffix_ref[...])
                a[k] = jnp.concatenate([jnp.exp2(log_b[:, cols][:, n * LANE:(n + 1) * LANE]
                                                 + within[:, n * LANE:(n + 1) * LANE] + later)
                                        for n in range(sub // LANE)], axis=1)
                later = later + jnp.sum(log_1m[:, cols], axis=-1, keepdims=True)
            a = jnp.concatenate(a, axis=1)
            if diagonal:
                a = jnp.where(strict, a, 0.0)
            acc_ref[h] += _dot(a.astype(BF16), v_ref[0, rows, sl])
            rest_ref[h] = later

    block(i, True)

    def body(n, carry):
        block(i - 1 - n, False)
        return carry

    lax.fori_loop(0, i, body, 0)
    _store_head_pairs(o_ref, SB_HEADS, lambda h: acc_ref[h])


def _attention(kernel, q, k, v, cols, extra, extra_specs, scratch, qw, vw, tq, name):
    b, s, _ = q.shape
    blk = lambda w, n=0: pl.BlockSpec((1, tq, w), lambda bi, i: (bi, i, n))
    seq = lambda w, n: pl.BlockSpec((1, s, w), lambda bi, i: (bi, 0, n))
    return pl.pallas_call(
        functools.partial(kernel, tq=tq),
        grid=(b, s // tq),
        in_specs=[blk(qw, cols[0]), seq(qw, cols[1]), seq(vw, cols[2])] + extra_specs,
        out_specs=blk(vw),
        out_shape=jax.ShapeDtypeStruct((b, s, vw), BF16),
        scratch_shapes=scratch,
        compiler_params=_cparams("parallel", "parallel"),
        name=name,
    )(q, k, v, *extra)


def _head_state(heads, tq):
    return pltpu.VMEM((heads, tq, LANE), F32)


def _merge_kernel(x_ref, g_ref, wg_ref, bg_ref, oa_ref, ob_ref, oc_ref, wa_ref, wb_ref, wc_ref, wo_ref, out_ref):
    x = x_ref[...]
    hn = _rms(x, g_ref[...]).astype(BF16)
    y = None
    for n, (o_ref, w_ref) in enumerate(((oa_ref, wa_ref), (ob_ref, wb_ref), (oc_ref, wc_ref))):
        sl = slice(n * D_MODEL, (n + 1) * D_MODEL)
        gate = jax.nn.sigmoid(_dot(hn, wg_ref[:, sl]) + bg_ref[:, sl])
        term = gate * _dot(o_ref[...], w_ref[...])
        y = term if y is None else y + term
    out_ref[...] = x + _dot(y.astype(BF16), wo_ref[...])


def _merge(x2, g, wg, bg, oa, ob, oc, wa, wb, wc, wo, tm):
    t = x2.shape[0]
    row = lambda w: pl.BlockSpec((tm, w), lambda i: (i, 0))
    return pl.pallas_call(
        _merge_kernel,
        grid=(t // tm,),
        in_specs=[row(D_MODEL), _full((1, D_MODEL)), _full((D_MODEL, N_BRANCH * D_MODEL)),
                  _full((1, N_BRANCH * D_MODEL)), row(HEAD_PAD_W), row(HEAD_PAD_W), row(HEAD_PAD_W),
                  _full((HEAD_PAD_W, D_MODEL)), _full((HEAD_PAD_W, D_MODEL)), _full((HEAD_PAD_W, D_MODEL)),
                  _full((D_MODEL, D_MODEL))],
        out_specs=row(D_MODEL),
        out_shape=jax.ShapeDtypeStruct((t, D_MODEL), F32),
        compiler_params=_cparams("parallel"),
        name="merge",
    )(x2, g, wg, bg, oa, ob, oc, wa, wb, wc, wo)


def _kth_largest_rows(x, k):
    for _ in range(k - 1):
        x = jnp.where(x >= jnp.max(x, axis=0, keepdims=True), -jnp.inf, x)
    return jnp.max(x, axis=0, keepdims=True)


def _sorted_top16(groups):
    x = list(groups)
    assert len(x) == PEER_TOPK

    def exchange(i, l, descending):
        hi, lo = jnp.maximum(x[i], x[l]), jnp.minimum(x[i], x[l])
        x[i], x[l] = (hi, lo) if descending else (lo, hi)

    def merge(descending_of):
        j = PEER_TOPK // 2
        while j >= 1:
            for i in range(PEER_TOPK):
                if i ^ j > i:
                    exchange(i, i ^ j, descending_of(i))
            j //= 2

    k = 2
    while k <= PEER_TOPK:
        j = k // 2
        while j >= 1:
            for i in range(PEER_TOPK):
                if i ^ j > i:
                    exchange(i, i ^ j, (i & k) == 0)
            j //= 2
        k *= 2
    for shift in (4, 2, 1):
        other = [pltpu.roll(v, shift, axis=0) for v in x]
        x = [jnp.maximum(x[r], other[PEER_TOPK - 1 - r]) for r in range(PEER_TOPK)]
        merge(lambda i: True)
    return x


def _peer_route_kernel(x_ref, g_ref, wq_ref, keys_ref, h_ref, rank2_ref, e2_ref, cnt_ref, e1_ref,
                       s1_ref, s2_ref, q_ref, top_ref, cand_ref):
    hn = _rms(x_ref[...], g_ref[...]).astype(BF16)
    h_ref[...] = hn
    wide = 2 * D_KEY
    for n in range(PEER_HEADS * D_KEY // wide):
        q_ref[:, n * wide:(n + 1) * wide] = _dot(hn, wq_ref[:, n * wide:(n + 1) * wide]).astype(BF16)
    for h in range(PEER_HEADS):
        for side, s_ref in ((0, s1_ref), (1, s2_ref)):
            o = h * D_KEY + side * HALF_KEY
            s_ref[h] = _dot_nt(keys_ref[side], q_ref[:, o:o + HALF_KEY])

    def lane_tile(lt, carry):
        lanes = pl.ds(pl.multiple_of(lt * LANE, LANE), LANE)
        for h in range(PEER_HEADS):
            for side, s_ref in ((0, s1_ref), (1, s2_ref)):
                groups = [s_ref[h, 8 * n:8 * n + 8, lanes] for n in range(N_KEYS // 8)]
                top = _sorted_top16(groups)
                for r in range(PEER_TOPK):
                    top_ref[side, r:r + 1, :] = top[r][0:1, :]
                if side == 1:
                    ranks = [jnp.zeros((8, LANE), F32)] * len(groups)
                    for r in range(PEER_TOPK):
                        ranks = [jnp.where(top[r] > grp, float(r + 1), rk) for grp, rk in zip(groups, ranks)]
                    rank2 = jnp.concatenate(ranks, axis=0)
            v2_top8 = top_ref[1, 0:8, :]
            sub = lax.broadcasted_iota(jnp.int32, (8, LANE), 0)
            v2_top4_twice = jnp.where(sub < 4, v2_top8, pltpu.roll(v2_top8, 4, axis=0))
            cand_ref[0:16, :] = top_ref[0, 0:1, :] + top_ref[1]
            for a in (1, 2, 3):
                cand_ref[8 + 8 * a:16 + 8 * a, :] = top_ref[0, a:a + 1, :] + v2_top8
            for n, a in enumerate((4, 6)):
                v1_pair = jnp.where(sub < 4, top_ref[0, a:a + 1, :], top_ref[0, a + 1:a + 2, :])
                cand_ref[40 + 8 * n:48 + 8 * n, :] = v1_pair + v2_top4_twice
            cand_ref[56:64, :] = top_ref[0, 8:16, :] + top_ref[1, 0:1, :]
            cand = cand_ref[...]
            tau = _kth_largest_rows(cand, PEER_TOPK)
            m1 = top_ref[0, 0:1, :]
            m2 = top_ref[1, 0:1, :]
            zsum = jnp.sum(jnp.where(cand >= tau, jnp.exp(cand - (m1 + m2)), 0.0), axis=0, keepdims=True)
            s1 = s1_ref[h, :, lanes]
            thr = tau - s1
            cnt = jnp.zeros((N_KEYS, LANE), F32)
            for b in range(PEER_TOPK):
                cnt = jnp.where(top_ref[1, b:b + 1, :] >= thr, float(b + 1), cnt)
            rank2_ref[h, :, lanes] = rank2.astype(BF16)
            e2_ref[h, :, lanes] = jnp.exp(s2_ref[h, :, lanes] - m2).astype(BF16)
            cnt_ref[h, :, lanes] = cnt
            e1_ref[h, :, lanes] = 0.5 * jnp.exp(s1 - m1) / zsum
        return carry

    lax.fori_loop(0, x_ref.shape[0] // LANE, lane_tile, 0)


def _peer_route(x2, g, wq, keys, tm):
    t = x2.shape[0]
    row = pl.BlockSpec((tm, D_MODEL), lambda i: (i, 0))
    sc = pl.BlockSpec((None, PEER_HEADS, N_KEYS, tm), lambda i: (i, 0, 0, 0))
    return pl.pallas_call(
        _peer_route_kernel,
        grid=(t // tm,),
        in_specs=[row, _full((1, D_MODEL)), _full((D_MODEL, PEER_HEADS * D_KEY)), _full((2, N_KEYS, HALF_KEY))],
        out_specs=[row, sc, sc, sc, sc],
        out_shape=[jax.ShapeDtypeStruct((t, D_MODEL), BF16),
                   jax.ShapeDtypeStruct((t // tm, PEER_HEADS, N_KEYS, tm), BF16),
                   jax.ShapeDtypeStruct((t // tm, PEER_HEADS, N_KEYS, tm), BF16),
                   jax.ShapeDtypeStruct((t // tm, PEER_HEADS, N_KEYS, tm), F32),
                   jax.ShapeDtypeStruct((t // tm, PEER_HEADS, N_KEYS, tm), F32)],
        scratch_shapes=[pltpu.VMEM((PEER_HEADS, N_KEYS, tm), F32), pltpu.VMEM((PEER_HEADS, N_KEYS, tm), F32),
                        pltpu.VMEM((tm, PEER_HEADS * D_KEY), BF16), pltpu.VMEM((2, PEER_TOPK, LANE), F32),
                        pltpu.VMEM((4 * PEER_TOPK, LANE), F32)],
        compiler_params=_cparams("parallel"),
        name="peer_route",
    )(x2, g, wq, keys)


def _peer_dense_step(c, parity, h_ref, rank2_ref, e2_ref, cnt_ref, e1_ref, u_ref, vt_ref, pre_ref, p_ref, acc_ref,
                     *, te, n_chunks, stages):
    tm = h_ref.shape[0]
    rows_per_step = te // N_KEYS
    assert 2 * rows_per_step == 8
    part = 64
    gate_w = 2 * LANE
    score_buf, gate_buf, fold_buf = parity, 1 - parity, parity
    key_tile = pl.ds(pl.multiple_of((jnp.clip(c - 1, 0, n_chunks - 1) // 2) * 8, 8), 8)
    row0 = rows_per_step * (1 - parity)

    def score(piece):
        rows = slice((piece // 2) * te // 2, (piece // 2 + 1) * te // 2)
        toks = slice((piece % 2) * gate_w, (piece % 2 + 1) * gate_w)
        u_rows = slice(parity * te + rows.start, parity * te + rows.stop)
        pre_ref[score_buf, rows, toks] = _dot_nt(u_ref[u_rows, :], h_ref[toks, :])

    def fold(piece):
        rows = slice((piece // 2) * D_MODEL // 4, (piece // 2 + 1) * D_MODEL // 4)
        toks = slice((piece % 2) * gate_w, (piece % 2 + 1) * gate_w)
        acc_ref[rows, toks] += _dot(vt_ref[parity, rows, :], p_ref[fold_buf, :, toks])

    def gate_heads(lt, rp, w, heads):
        lanes = slice(lt * gate_w, (lt + 1) * gate_w)
        for h in heads:
            rank2 = rank2_ref[h, rp * part:(rp + 1) * part, lanes]
            e2 = e2_ref[h, rp * part:(rp + 1) * part, lanes]
            cnt = cnt_ref[h, key_tile, lanes]
            e1 = e1_ref[h, key_tile, lanes]
            for j in range(rows_per_step):
                cnt_j = jnp.broadcast_to(cnt[row0 + j:row0 + j + 1, :], (part, gate_w)).astype(BF16)
                e1_j = jnp.broadcast_to(e1[row0 + j:row0 + j + 1, :], (part, gate_w)).astype(BF16)
                term = jnp.where(rank2 < cnt_j, e2, jnp.zeros_like(e2)) * e1_j
                w[j] = term if w[j] is None else w[j] + term

    def gate_store(lt, rp, w):
        lanes = slice(lt * gate_w, (lt + 1) * gate_w)
        for j in range(rows_per_step):
            rows = slice(j * N_KEYS + rp * part, j * N_KEYS + (rp + 1) * part)
            pre = pre_ref[gate_buf, rows, lanes]
            act = pre * (1.0 + lax.erf(pre * (0.5 ** 0.5)))
            p_ref[gate_buf, rows, lanes] = w[j] * act.astype(BF16)

    assert tm // gate_w == 2 and N_KEYS // part == 2
    mxu = [lambda n=n: score(n) for n in range(4)] + [lambda n=n: fold(n) for n in range(8)]
    order = [0, 4, 5, 1, 6, 7, 2, 8, 9, 3, 10, 11]
    live = [n for n in order if ("score" if n < 4 else "fold") in stages]
    if "gate" not in stages:
        for n in live:
            mxu[n]()
        return
    slots = 4 * 3
    pieces = iter([mxu[n] for n in live] + [None] * (slots - len(live)))
    for lt in range(2):
        for rp in range(2):
            w = [None] * rows_per_step
            for h in range(PEER_HEADS):
                gate_heads(lt, rp, w, (h,))
                if h % 3 == 1:
                    piece = next(pieces)
                    if piece is not None:
                        piece()
            gate_store(lt, rp, w)


def _peer_dense_kernel(x_ref, h_ref, rank2_ref, e2_ref, cnt_ref, e1_ref, u_ref, vt_ref, gf_ref, out_ref,
                       pre_ref, p_ref, acc_ref, *, te, n_chunks, final_norm):
    g = pl.program_id(1)

    last = n_chunks // 2

    @pl.when(g == 0)
    def _():
        acc_ref[...] = jnp.zeros_like(acc_ref)

    full = ("score", "gate", "fold")
    plans = (((0, 0), (("score",), ("score", "gate"))),
             ((1, last - 1), (full, full)),
             ((last, last), (("gate", "fold"), ("fold",))))
    for (lo, hi), stage_sets in plans:
        for parity, stages in enumerate(stage_sets):
            guard = jnp.logical_and(g >= lo, g <= hi) if parity == 0 else jnp.logical_and(g > lo - 1, g < hi + 1)

            @pl.when(guard)
            def _(parity=parity, stages=stages):
                _peer_dense_step(2 * g + parity, parity, h_ref, rank2_ref, e2_ref, cnt_ref, e1_ref, u_ref, vt_ref,
                                 pre_ref, p_ref, acc_ref, te=te, n_chunks=n_chunks, stages=stages)

    @pl.when(g == last)
    def _():
        y = x_ref[...] + acc_ref[...].T
        if final_norm:
            y = _rms(y, gf_ref[...])
        out_ref[...] = y


def _peer_dense(x2, hn, rank2, e2, cnt, e1, u, vt, gf, tm, te, final_norm):
    t = x2.shape[0]
    n_chunks = N_EXPERTS // te
    row = pl.BlockSpec((tm, D_MODEL), lambda i, c: (i, 0))
    sc = pl.BlockSpec((None, PEER_HEADS, N_KEYS, tm), lambda i, c: (i, 0, 0, 0))
    return pl.pallas_call(
        functools.partial(_peer_dense_kernel, te=te, n_chunks=n_chunks, final_norm=final_norm),
        grid=(t // tm, n_chunks // 2 + 1),
        in_specs=[row, row, sc, sc, sc, sc,
                  pl.BlockSpec((2 * te, D_MODEL), lambda i, g: (jnp.minimum(g, n_chunks // 2 - 1), 0)),
                  pl.BlockSpec((2, D_MODEL, te), lambda i, g: (jnp.clip(g - 1, 0, n_chunks // 2 - 1), 0, 0)),
                  pl.BlockSpec((1, D_MODEL), lambda i, c: (0, 0))],
        out_specs=row,
        out_shape=jax.ShapeDtypeStruct((t, D_MODEL), F32),
        scratch_shapes=[pltpu.VMEM((2, te, tm), F32), pltpu.VMEM((2, te, tm), BF16),
                        pltpu.VMEM((D_MODEL, tm), F32)],
        compiler_params=_cparams("parallel", "arbitrary"),
        name="peer_dense",
    )(x2, hn, rank2, e2, cnt, e1, u, vt, gf)


def _pad_cols(w, width):
    return jnp.pad(w, ((0, 0), (0, width - w.shape[1])))


def _rope_partner(w):
    half = MLA_ROPE // 2
    return jnp.concatenate([-w[:, half:], w[:, :half]], axis=1)


def _rope_lanes(w):
    return jnp.pad(w, ((0, 0), (MLA_NOPE, LANE - MLA_NOPE - MLA_ROPE)))


def _layer_weights(w_in, w_uq, w_ukv, w_o_sb, w_o_fox):
    o = IN_OFFSETS
    seg = lambda n: w_in[:, o[n]:o[n + 1]]
    scale = HEAD_DIM ** -0.5 * LOG2E
    w_kr = seg(2)
    w_cat = jnp.concatenate(
        [seg(0), seg(1), _rope_lanes(w_kr), _rope_lanes(_rope_partner(w_kr)),
         _pad_cols(seg(3) * scale, HEAD_PAD_W), _pad_cols(seg(4), HEAD_PAD_W), _pad_cols(seg(5), HEAD_PAD_W),
         _pad_cols(seg(6) * scale, HEAD_PAD_W), _pad_cols(seg(7), HEAD_PAD_W), _pad_cols(seg(8), HEAD_PAD_W),
         _pad_cols(seg(9), LANE)], axis=1).astype(BF16)
    w_gate = seg(10).astype(BF16)
    uq = w_uq.reshape(Q_LORA, MLA_HEADS, MLA_QK)
    nope, rope = uq[..., :MLA_NOPE], uq[..., MLA_NOPE:]
    zpad = jnp.zeros((Q_LORA, MLA_HEADS, LANE - MLA_QK), F32)
    wqa = jnp.concatenate([nope, rope, zpad], axis=-1).reshape(Q_LORA, MLA_HEADS * LANE).astype(BF16)
    partner = jnp.concatenate([-rope[..., MLA_ROPE // 2:], rope[..., :MLA_ROPE // 2]], axis=-1)
    wqb = jnp.concatenate([jnp.zeros_like(nope), partner, zpad], axis=-1).reshape(Q_LORA, MLA_HEADS * LANE).astype(BF16)
    ukv = w_ukv.reshape(KV_LORA, MLA_HEADS, MLA_NOPE + MLA_V)
    wk = jnp.pad(ukv[..., :MLA_NOPE], ((0, 0), (0, 0), (0, LANE - MLA_NOPE))).reshape(KV_LORA, MLA_HEADS * LANE).astype(BF16)
    wv = ukv[..., MLA_NOPE:].reshape(KV_LORA, MLA_OUT).astype(BF16)
    pad_rows = lambda w: jnp.pad(w, ((0, HEAD_PAD_W - w.shape[0]), (0, 0))).astype(BF16)
    return w_cat, w_gate, wqa, wqb, wk, wv, pad_rows(w_o_sb), pad_rows(w_o_fox)


def _tiles(b, s):
    t = b * s
    tm = min(TOKEN_TILE, t)
    tq = min(ATTN_TILE, s)
    return t, tm, tq


def kernel(x, positions, norm1_g, w_in, mla_q_norm_g, w_uq, mla_kv_norm_g, w_ukv, fox_b_f, w_o_mla, w_o_sb,
           w_o_fox, b_gate, w_out, norm2_g, peer_w_q, peer_sub_keys, peer_u, peer_v, final_norm_g):
    b, s, d = x.shape
    assert d == D_MODEL and MLA_OUT == HEAD_PAD_W
    depth = w_in.shape[0]
    t, tm, tq = _tiles(b, s)
    assert t % tm == 0 and s % tq == 0 and tq % CHUNK == 0
    te = EXPERT_CHUNK
    x2 = x.reshape(t, d)
    cos, sin = _rope_tables(positions, tm)
    gf = final_norm_g.reshape(1, d)
    for l in range(depth):
        w_cat, w_gate, wqa, wqb, wk, wv, wo_sb, wo_fox = _layer_weights(w_in[l], w_uq[l], w_ukv[l], w_o_sb[l], w_o_fox[l])
        g1 = norm1_g[l].reshape(1, d)
        q_a, k_a, v_a, sb, fx, f = _inproj(x2, g1, w_cat, cos, sin, mla_q_norm_g[l].reshape(1, Q_LORA), wqa, wqb,
                                           mla_kv_norm_g[l].reshape(1, KV_LORA), wk, wv, tm)
        bf_row = jnp.zeros((1, LANE), F32).at[0, :FOX_HEADS].set(fox_b_f[l])
        cum, cumt = _fox_prep(f.reshape(b, s, LANE), bf_row, min(SUFFIX_TILE, s))
        hw = MLA_HEADS * LANE
        acc_a = _head_state(MLA_HEADS, tq)
        o_a = _attention(_mla_attn_kernel, q_a.reshape(b, s, hw), k_a.reshape(b, s, hw), v_a.reshape(b, s, MLA_OUT),
                         (0, 0, 0), [], [], [acc_a, acc_a, acc_a], hw, MLA_OUT, tq, "mla_attn")
        sb3 = sb.reshape(b, s, QKV_W)
        fx3 = fx.reshape(b, s, QKV_W)
        acc_b = _head_state(SB_HEADS, tq)
        qm = pltpu.VMEM((SB_HEADS, tq, LANE), BF16)
        o_b = _attention(_sb_attn_kernel, sb3, sb3, sb3, (0, 1, 2), [], [],
                         [qm, pltpu.VMEM((min(SUFFIX_TILE, tq),) * 2, BF16), acc_b, acc_b], HEAD_PAD_W, HEAD_PAD_W, tq,
                         "sb_attn")
        fox_specs = [pl.BlockSpec((1, tq, LANE), lambda bi, i: (bi, i, 0)),
                     pl.BlockSpec((1, 8, s), lambda bi, i: (bi, 0, 0))]
        o_c = _attention(_fox_attn_kernel, fx3, fx3, fx3, (0, 1, 2), [cum, cumt], fox_specs,
                         [qm, acc_b, acc_b, acc_b, acc_b], HEAD_PAD_W, HEAD_PAD_W, tq, "fox_attn")
        x2 = _merge(x2, g1, w_gate, b_gate[l].reshape(1, -1), o_a.reshape(t, MLA_OUT), o_b.reshape(t, HEAD_PAD_W),
                    o_c.reshape(t, HEAD_PAD_W), w_o_mla[l].astype(BF16), wo_sb, wo_fox, w_out[l].astype(BF16), tm)
        hn, rank2, e2, cnt, e1 = _peer_route(x2, norm2_g[l].reshape(1, d), peer_w_q[l].astype(BF16),
                                             peer_sub_keys[l].astype(BF16), tm)
        vt = peer_v[l].astype(BF16).reshape(N_EXPERTS // te, te, d).transpose(0, 2, 1)
        x2 = _peer_dense(x2, hn, rank2, e2, cnt, e1, peer_u[l].astype(BF16), vt, gf, tm, te,
                         final_norm=(l == depth - 1))
    return x2.reshape(b, s, d)
```

```python
import functools

import numpy as np
import jax
import jax.numpy as jnp
from jax import lax
from jax.experimental import pallas as pl
from jax.experimental.pallas import tpu as pltpu

F32 = jnp.float32
BF16 = jnp.bfloat16

D_MODEL = 1024
CHUNK = 64
HEAD_DIM = 64
NORM_EPS = 1e-6
NEG_INF = -1e30
MLA_HEADS = 6
MLA_NOPE = 64
MLA_ROPE = 32
MLA_V = 64
Q_LORA = 256
KV_LORA = 128
ROPE_THETA = 10000.0
SB_HEADS = 5
FOX_HEADS = 5
N_BRANCH = 3
PEER_HEADS = 8
N_KEYS = 128
N_EXPERTS = N_KEYS * N_KEYS
D_KEY = 256
HALF_KEY = D_KEY // 2
PEER_TOPK = 16
MLA_QK = MLA_NOPE + MLA_ROPE
SB_W = SB_HEADS * HEAD_DIM
FOX_W = FOX_HEADS * HEAD_DIM
MLA_OUT = MLA_HEADS * MLA_V
IN_SPLITS = (Q_LORA, KV_LORA, MLA_ROPE, SB_W, SB_W, SB_W, FOX_W, FOX_W, FOX_W, FOX_HEADS, N_BRANCH * D_MODEL)
IN_OFFSETS = tuple(int(o) for o in np.cumsum((0,) + IN_SPLITS))

LANE = 128
HEAD_PAD_W = 384
MLA_IN_W = Q_LORA + KV_LORA + 2 * LANE
QKV_W = 3 * HEAD_PAD_W
IN_W = MLA_IN_W + 2 * QKV_W + LANE
VMEM_LIMIT = 48 * 1024 * 1024
TOKEN_TILE = 512
ATTN_TILE = 512
SUFFIX_TILE = 256
EXPERT_CHUNK = 4 * N_KEYS

LOG2E = 1.4426950408889634
_NT = (((1,), (1,)), ((), ()))


def _cparams(*sem):
    return pltpu.CompilerParams(dimension_semantics=sem, vmem_limit_bytes=VMEM_LIMIT)


def _rms(x, g):
    return x * lax.rsqrt(jnp.mean(x * x, axis=-1, keepdims=True) + NORM_EPS) * g


def _dot(a, b):
    return jnp.dot(a, b, preferred_element_type=F32)


def _dot_nt(a, b):
    return lax.dot_general(a, b, _NT, preferred_element_type=F32)


def _full(shape):
    return pl.BlockSpec(shape, lambda *_: (0,) * len(shape))


def _rope_kernel(pos_ref, inv_ref, cos_ref, sin_ref):
    ang = pos_ref[...].astype(F32) * inv_ref[...]
    cos_ref[...] = jnp.cos(ang)
    sin_ref[...] = jnp.sin(ang)


def _rope_tables(positions, tm):
    t = positions.size
    inv = ROPE_THETA ** (-jnp.arange(0, MLA_ROPE, 2, dtype=F32) / MLA_ROPE)
    inv_row = jnp.zeros((1, LANE), F32).at[0, MLA_NOPE:MLA_NOPE + MLA_ROPE].set(jnp.concatenate([inv, inv]))
    return pl.pallas_call(
        _rope_kernel,
        grid=(t // tm,),
        in_specs=[pl.BlockSpec((tm, 1), lambda i: (i, 0)), _full((1, LANE))],
        out_specs=[pl.BlockSpec((tm, LANE), lambda i: (i, 0))] * 2,
        out_shape=[jax.ShapeDtypeStruct((t, LANE), F32)] * 2,
        compiler_params=_cparams("parallel"),
        name="rope_tables",
    )(positions.reshape(t, 1), inv_row)


def _inproj_kernel(x_ref, g_ref, w_ref, cos_ref, sin_ref, gq_ref, wqa_ref, wqb_ref, gkv_ref, wk_ref, wv_ref,
                   q_ref, k_ref, v_ref, sb_ref, fx_ref, f_ref, mla_ref):
    hn = _rms(x_ref[...], g_ref[...]).astype(BF16)
    o = 0
    for ref, width in ((mla_ref, MLA_IN_W), (sb_ref, QKV_W), (fx_ref, QKV_W), (f_ref, LANE)):
        ref[...] = _dot(hn, w_ref[:, o:o + width]).astype(ref.dtype)
        o += width
    _mla_prep(mla_ref, cos_ref, sin_ref, gq_ref, wqa_ref, wqb_ref, gkv_ref, wk_ref, wv_ref, q_ref, k_ref, v_ref)


def _inproj(x2, g, w_cat, cos, sin, gq, wqa, wqb, gkv, wk, wv, tm):
    t = x2.shape[0]
    row = lambda w: pl.BlockSpec((tm, w), lambda i: (i, 0))
    hw = MLA_HEADS * LANE
    return pl.pallas_call(
        _inproj_kernel,
        grid=(t // tm,),
        in_specs=[row(D_MODEL), _full((1, D_MODEL)), _full((D_MODEL, IN_W)), row(LANE), row(LANE),
                  _full((1, Q_LORA)), _full((Q_LORA, hw)), _full((Q_LORA, hw)), _full((1, KV_LORA)),
                  _full((KV_LORA, hw)), _full((KV_LORA, MLA_OUT))],
        out_specs=[row(hw), row(hw), row(MLA_OUT), row(QKV_W), row(QKV_W), row(LANE)],
        out_shape=[jax.ShapeDtypeStruct((t, hw), BF16), jax.ShapeDtypeStruct((t, hw), BF16),
                   jax.ShapeDtypeStruct((t, MLA_OUT), BF16), jax.ShapeDtypeStruct((t, QKV_W), BF16),
                   jax.ShapeDtypeStruct((t, QKV_W), BF16), jax.ShapeDtypeStruct((t, LANE), F32)],
        scratch_shapes=[pltpu.VMEM((tm, MLA_IN_W), F32)],
        compiler_params=_cparams("parallel"),
        name="inproj",
    )(x2, g, w_cat, cos, sin, gq, wqa, wqb, gkv, wk, wv)


def _mla_prep(in_ref, cos_ref, sin_ref, gq_ref, wqa_ref, wqb_ref, gkv_ref, wk_ref, wv_ref, q_ref, k_ref, v_ref):
    cos = cos_ref[...]
    sin = sin_ref[...]
    qn = _rms(in_ref[:, :Q_LORA], gq_ref[...]).astype(BF16)
    kn = _rms(in_ref[:, Q_LORA:Q_LORA + KV_LORA], gkv_ref[...]).astype(BF16)
    o = Q_LORA + KV_LORA
    k_rot = in_ref[:, o:o + LANE] * cos + in_ref[:, o + LANE:o + 2 * LANE] * sin
    scale = MLA_QK ** -0.5 * LOG2E
    for h in range(MLA_HEADS):
        sl = slice(h * LANE, (h + 1) * LANE)
        qa = _dot(qn, wqa_ref[:, sl])
        qb = _dot(qn, wqb_ref[:, sl])
        q_ref[:, sl] = ((qa * cos + qb * sin) * scale).astype(BF16)
        k_ref[:, sl] = (_dot(kn, wk_ref[:, sl]) + k_rot).astype(BF16)
    v_ref[...] = _dot(kn, wv_ref[...]).astype(BF16)


def _fox_prep_kernel(f_ref, bf_ref, cum_ref, cumt_ref, *, blk):
    s = f_ref.shape[1]
    row = lax.broadcasted_iota(jnp.int32, (blk, blk), 0)
    col = lax.broadcasted_iota(jnp.int32, (blk, blk), 1)
    tri = jnp.where(col <= row, 1.0, 0.0).astype(BF16)
    carry = jnp.zeros((1, LANE), F32)
    for b in range(s // blk):
        f = f_ref[0, b * blk:(b + 1) * blk, :] + bf_ref[...]
        lf = (jnp.minimum(f, 0.0) - jnp.log1p(jnp.exp(-jnp.abs(f)))) * LOG2E
        hi = lf.astype(BF16)
        r1 = lf - hi.astype(F32)
        mid = r1.astype(BF16)
        lo = (r1 - mid.astype(F32)).astype(BF16)
        c = _dot(tri, hi) + _dot(tri, mid) + _dot(tri, lo) + carry
        cum_ref[0, b * blk:(b + 1) * blk, :] = c
        cumt_ref[0, :, b * blk:(b + 1) * blk] = c.T[:8, :]
        carry = c[blk - 1:blk, :]


def _fox_prep(f3, bf_row, blk):
    b, s, _ = f3.shape
    return pl.pallas_call(
        functools.partial(_fox_prep_kernel, blk=blk),
        grid=(b,),
        in_specs=[pl.BlockSpec((1, s, LANE), lambda i: (i, 0, 0)), _full((1, LANE))],
        out_specs=[pl.BlockSpec((1, s, LANE), lambda i: (i, 0, 0)), pl.BlockSpec((1, 8, s), lambda i: (i, 0, 0))],
        out_shape=[jax.ShapeDtypeStruct((b, s, LANE), F32), jax.ShapeDtypeStruct((b, 8, s), F32)],
        compiler_params=_cparams("parallel"),
        name="fox_prep",
    )(f3, bf_row)


def _softmax_update(h, s, v, m_ref, l_ref, acc_ref):
    m_old = m_ref[h]
    m_new = jnp.maximum(m_old, jnp.max(s, axis=-1, keepdims=True))
    alpha = jnp.exp2(m_old - m_new)
    p = [jnp.exp2(s[:, n * LANE:(n + 1) * LANE] - m_new) for n in range(s.shape[1] // LANE)]
    part = p[0]
    for p_n in p[1:]:
        part = part + p_n
    l_ref[h] = alpha * l_ref[h] + part
    m_ref[h] = m_new
    acc_ref[h] = alpha * acc_ref[h] + _dot(jnp.concatenate(p, axis=1).astype(BF16), v)


def _softmax_reset(m_ref, l_ref, acc_ref):
    m_ref[...] = jnp.full(m_ref.shape, NEG_INF, F32)
    l_ref[...] = jnp.zeros(l_ref.shape, F32)
    acc_ref[...] = jnp.zeros(acc_ref.shape, F32)


def _half_mask(tq, half):
    lane = lax.broadcasted_iota(jnp.int32, (tq, LANE), 1)
    return (lane < HEAD_DIM) if half == 0 else (lane >= HEAD_DIM)


def _store_head_pairs(o_ref, heads, value_of):
    tq = o_ref.shape[1]
    lo_half = _half_mask(tq, 0)
    for hb in range(o_ref.shape[2] // LANE):
        lo = value_of(2 * hb)
        hi = value_of(2 * hb + 1) if 2 * hb + 1 < heads else jnp.zeros_like(lo)
        o_ref[0, :, hb * LANE:(hb + 1) * LANE] = jnp.where(lo_half, lo, hi).astype(o_ref.dtype)


def _masked_queries(q_ref, qm_ref, heads):
    tq = q_ref.shape[1]
    for h in range(heads):
        qf = q_ref[0, :, (h // 2) * LANE:(h // 2 + 1) * LANE].astype(F32)
        qm_ref[h] = jnp.where(_half_mask(tq, h % 2), qf, 0.0).astype(BF16)


def _block_iotas(tq):
    return lax.broadcasted_iota(jnp.int32, (tq, tq), 0), lax.broadcasted_iota(jnp.int32, (tq, tq), 1)


def _mla_attn_kernel(q_ref, k_ref, v_ref, o_ref, m_ref, l_ref, acc_ref, *, tq):
    i = pl.program_id(1)
    _softmax_reset(m_ref, l_ref, acc_ref)

    def block(j, diagonal):
        rows = pl.ds(pl.multiple_of(j * tq, tq), tq)
        def scores(h):
            sl = slice(h * LANE, (h + 1) * LANE)
            return _dot_nt(q_ref[0, :, sl], k_ref[0, rows, sl])

        s_next = scores(0)
        for h in range(MLA_HEADS):
            s = s_next
            if h + 1 < MLA_HEADS:
                s_next = scores(h + 1)
            if diagonal:
                r, c = _block_iotas(tq)
                s = jnp.where((c // CHUNK) <= (r // CHUNK), s, NEG_INF)
            _softmax_update(h, s, v_ref[0, rows, (h // 2) * LANE:(h // 2 + 1) * LANE], m_ref, l_ref, acc_ref)

    def body(j, carry):
        block(j, False)
        return carry

    lax.fori_loop(0, i, body, 0)
    block(i, True)
    _store_head_pairs(o_ref, MLA_HEADS, lambda h: acc_ref[h] / jnp.sum(l_ref[h], axis=-1, keepdims=True))


def _fox_attn_kernel(q_ref, k_ref, v_ref, cum_ref, cumt_ref, o_ref, qm_ref, cq_ref, m_ref, l_ref, acc_ref, *, tq):
    i = pl.program_id(1)
    _softmax_reset(m_ref, l_ref, acc_ref)
    _masked_queries(q_ref, qm_ref, FOX_HEADS)
    for h in range(FOX_HEADS):
        cq_ref[h] = jnp.broadcast_to(cum_ref[0, :, h:h + 1], (tq, LANE))

    def block(j, diagonal):
        start = pl.multiple_of(j * tq, tq)
        rows = pl.ds(start, tq)
        def scores(h):
            return _dot_nt(qm_ref[h], k_ref[0, rows, (h // 2) * LANE:(h // 2 + 1) * LANE])

        s_next = scores(0)
        for h in range(FOX_HEADS):
            sl = slice((h // 2) * LANE, (h // 2 + 1) * LANE)
            s = s_next
            if h + 1 < FOX_HEADS:
                s_next = scores(h + 1)
            ck = cumt_ref[0, h:h + 1, rows]
            s = jnp.concatenate([s[:, n * LANE:(n + 1) * LANE] + (cq_ref[h] - ck[:, n * LANE:(n + 1) * LANE])
                                 for n in range(tq // LANE)], axis=1)
            if diagonal:
                r, c = _block_iotas(tq)
                s = jnp.where(c <= r, s, NEG_INF)
            _softmax_update(h, s, v_ref[0, rows, sl], m_ref, l_ref, acc_ref)

    def body(j, carry):
        block(j, False)
        return carry

    lax.fori_loop(0, i, body, 0)
    block(i, True)
    _store_head_pairs(o_ref, FOX_HEADS, lambda h: acc_ref[h] / jnp.sum(l_ref[h], axis=-1, keepdims=True))


def _sb_attn_kernel(q_ref, k_ref, v_ref, o_ref, qm_ref, suffix_ref, rest_ref, acc_ref, *, tq):
    i = pl.program_id(1)
    sub = suffix_ref.shape[0]
    r, c = _block_iotas(sub)
    suffix_ref[...] = jnp.where(r > c, 1.0, 0.0).astype(BF16)
    for h in range(SB_HEADS):
        rest_ref[h] = jnp.zeros(rest_ref.shape[1:], F32)
        acc_ref[h] = jnp.zeros(acc_ref.shape[1:], F32)
    _masked_queries(q_ref, qm_ref, SB_HEADS)

    def block(j, diagonal):
        rows = pl.ds(pl.multiple_of(j * tq, tq), tq)
        def scores(h):
            return _dot_nt(qm_ref[h], k_ref[0, rows, (h // 2) * LANE:(h // 2 + 1) * LANE])

        z_next = scores(0)
        for h in range(SB_HEADS):
            sl = slice((h // 2) * LANE, (h // 2 + 1) * LANE)
            z = z_next
            if h + 1 < SB_HEADS:
                z_next = scores(h + 1)
            log_b = jnp.minimum(z, 0.0) - jnp.log(1.0 + jnp.exp2(-jnp.abs(z))) * LOG2E
            log_1m = log_b - z
            if diagonal:
                strict = _block_iotas(tq)[1] < _block_iotas(tq)[0]
                log_1m = jnp.where(strict, log_1m, 0.0)
            later = rest_ref[h]
            a = [None] * (tq // sub)
            for k in reversed(range(tq // sub)):
                cols = slice(k * sub, (k + 1) * sub)
                within = _dot(log_1m[:, cols].astype(BF16), suffix_ref[...])
                a[k] = jnp.concatenate([jnp.exp2(log_b[:, cols][:, n * LANE:(n + 1) * LANE]
                                                 + within[:, n * LANE:(n + 1) * LANE] + later)
                                        for n in range(sub // LANE)], axis=1)
                later = later + jnp.sum(log_1m[:, cols], axis=-1, keepdims=True)
            a = jnp.concatenate(a, axis=1)
            if diagonal:
                a = jnp.where(strict, a, 0.0)
            acc_ref[h] += _dot(a.astype(BF16), v_ref[0, rows, sl])
            rest_ref[h] = later

    block(i, True)

    def body(n, carry):
        block(i - 1 - n, False)
        return carry

    lax.fori_loop(0, i, body, 0)
    _store_head_pairs(o_ref, SB_HEADS, lambda h: acc_ref[h])


def _attention(kernel, q, k, v, cols, extra, extra_specs, scratch, qw, vw, tq, name):
    b, s, _ = q.shape
    blk = lambda w, n=0: pl.BlockSpec((1, tq, w), lambda bi, i: (bi, i, n))
    seq = lambda w, n: pl.BlockSpec((1, s, w), lambda bi, i: (bi, 0, n))
    return pl.pallas_call(
        functools.partial(kernel, tq=tq),
        grid=(b, s // tq),
        in_specs=[blk(qw, cols[0]), seq(qw, cols[1]), seq(vw, cols[2])] + extra_specs,
        out_specs=blk(vw),
        out_shape=jax.ShapeDtypeStruct((b, s, vw), BF16),
        scratch_shapes=scratch,
        compiler_params=_cparams("parallel", "parallel"),
        name=name,
    )(q, k, v, *extra)


def _head_state(heads, tq):
    return pltpu.VMEM((heads, tq, LANE), F32)


def _merge_kernel(x_ref, g_ref, wg_ref, bg_ref, oa_ref, ob_ref, oc_ref, wa_ref, wb_ref, wc_ref, wo_ref, out_ref):
    x = x_ref[...]
    hn = _rms(x, g_ref[...]).astype(BF16)
    y = None
    for n, (o_ref, w_ref) in enumerate(((oa_ref, wa_ref), (ob_ref, wb_ref), (oc_ref, wc_ref))):
        sl = slice(n * D_MODEL, (n + 1) * D_MODEL)
        gate = jax.nn.sigmoid(_dot(hn, wg_ref[:, sl]) + bg_ref[:, sl])
        term = gate * _dot(o_ref[...], w_ref[...])
        y = term if y is None else y + term
    out_ref[...] = x + _dot(y.astype(BF16), wo_ref[...])


def _merge(x2, g, wg, bg, oa, ob, oc, wa, wb, wc, wo, tm):
    t = x2.shape[0]
    row = lambda w: pl.BlockSpec((tm, w), lambda i: (i, 0))
    return pl.pallas_call(
        _merge_kernel,
        grid=(t // tm,),
        in_specs=[row(D_MODEL), _full((1, D_MODEL)), _full((D_MODEL, N_BRANCH * D_MODEL)),
                  _full((1, N_BRANCH * D_MODEL)), row(HEAD_PAD_W), row(HEAD_PAD_W), row(HEAD_PAD_W),
                  _full((HEAD_PAD_W, D_MODEL)), _full((HEAD_PAD_W, D_MODEL)), _full((HEAD_PAD_W, D_MODEL)),
                  _full((D_MODEL, D_MODEL))],
        out_specs=row(D_MODEL),
        out_shape=jax.ShapeDtypeStruct((t, D_MODEL), F32),
        compiler_params=_cparams("parallel"),
        name="merge",
    )(x2, g, wg, bg, oa, ob, oc, wa, wb, wc, wo)


def _sorted_top16(groups):
    x = list(groups)
    assert len(x) in (PEER_TOPK // 2, PEER_TOPK)

    def exchange(i, l, descending):
        hi, lo = jnp.maximum(x[i], x[l]), jnp.minimum(x[i], x[l])
        x[i], x[l] = (hi, lo) if descending else (lo, hi)

    k = 2
    while k <= len(x):
        j = k // 2
        while j >= 1:
            for i in range(len(x)):
                if i ^ j > i:
                    exchange(i, i ^ j, (i & k) == 0)
            j //= 2
        k *= 2
    for shift in (4, 2, 1):
        other = [pltpu.roll(v, shift, axis=0) for v in x]
        if len(x) < PEER_TOPK:
            x = x + other[::-1]
        else:
            x = [jnp.maximum(x[r], other[PEER_TOPK - 1 - r]) for r in range(PEER_TOPK)]
        j = PEER_TOPK // 2
        while j >= 1:
            for i in range(PEER_TOPK):
                if i ^ j > i:
                    exchange(i, i ^ j, True)
            j //= 2
    return x


def _peer_route_kernel(x_ref, g_ref, wq_ref, keys_ref, h_ref, rank2_ref, e2_ref, cnt_ref, e1_ref,
                       s1_ref, s2_ref, q_ref, top_ref, cand_ref):
    hn = _rms(x_ref[...], g_ref[...]).astype(BF16)
    h_ref[...] = hn
    wide = 2 * D_KEY
    for n in range(PEER_HEADS * D_KEY // wide):
        q_ref[:, n * wide:(n + 1) * wide] = _dot(hn, wq_ref[:, n * wide:(n + 1) * wide]).astype(BF16)
    for h in range(PEER_HEADS):
        for side, s_ref in ((0, s1_ref), (1, s2_ref)):
            o = h * D_KEY + side * HALF_KEY
            s_ref[h] = _dot_nt(keys_ref[side], q_ref[:, o:o + HALF_KEY])

    def lane_tile(lt, carry):
        lanes = pl.ds(pl.multiple_of(lt * LANE, LANE), LANE)
        for h in range(PEER_HEADS):
            scores, tops = [], []
            for side, s_ref in ((0, s1_ref), (1, s2_ref)):
                groups = [s_ref[h, 8 * n:8 * n + 8, lanes] for n in range(N_KEYS // 8)]
                top = _sorted_top16(groups)
                for r in range(PEER_TOPK):
                    top_ref[side, r:r + 1, :] = top[r][0:1, :]
                scores.append(groups)
                tops.append(top)
            (s1, s2), (top1, top2) = scores, tops
            v2_top8 = top_ref[1, 0:8, :]
            sub = lax.broadcasted_iota(jnp.int32, (8, LANE), 0)
            v2_top4_twice = jnp.where(sub < 4, v2_top8, pltpu.roll(v2_top8, 4, axis=0))
            cand_ref[0:16, :] = top_ref[0, 0:1, :] + top_ref[1]
            for a in (1, 2, 3):
                cand_ref[8 + 8 * a:16 + 8 * a, :] = top_ref[0, a:a + 1, :] + v2_top8
            for n, a in enumerate((4, 6)):
                v1_pair = jnp.where(sub < 4, top_ref[0, a:a + 1, :], top_ref[0, a + 1:a + 2, :])
                cand_ref[40 + 8 * n:48 + 8 * n, :] = v1_pair + v2_top4_twice
            cand_ref[56:64, :] = top_ref[0, 8:16, :] + top_ref[1, 0:1, :]
            best = _sorted_top16([cand_ref[8 * n:8 * n + 8, :] for n in range(cand_ref.shape[0] // 8)])
            tau = best[PEER_TOPK - 1]
            m1, m2 = top1[0], top2[0]
            zsum = best[0] * 0.0
            for v in best:
                zsum = zsum + jnp.exp(v - (m1 + m2))
            half_inv_z = 0.5 / zsum
            ranks, cnts = [], []
            for g1, g2 in zip(s1, s2):
                thr = tau - g1
                rank, cnt = jnp.zeros((8, LANE), F32), jnp.zeros((8, LANE), F32)
                for b in range(PEER_TOPK):
                    rank = jnp.where(top2[b] > g2, float(b + 1), rank)
                    cnt = jnp.where(top2[b] >= thr, float(b + 1), cnt)
                ranks.append(rank)
                cnts.append(cnt)
            rank2_ref[h, :, lanes] = jnp.concatenate(ranks, axis=0).astype(BF16)
            e2_ref[h, :, lanes] = jnp.concatenate([jnp.exp(g2 - m2) for g2 in s2], axis=0).astype(BF16)
            cnt_ref[h, :, lanes] = jnp.concatenate(cnts, axis=0)
            e1_ref[h, :, lanes] = jnp.concatenate([jnp.exp(g1 - m1) * half_inv_z for g1 in s1], axis=0)
        return carry

    lax.fori_loop(0, x_ref.shape[0] // LANE, lane_tile, 0)


def _peer_route(x2, g, wq, keys, tm):
    t = x2.shape[0]
    row = pl.BlockSpec((tm, D_MODEL), lambda i: (i, 0))
    sc = pl.BlockSpec((None, PEER_HEADS, N_KEYS, tm), lambda i: (i, 0, 0, 0))
    return pl.pallas_call(
        _peer_route_kernel,
        grid=(t // tm,),
        in_specs=[row, _full((1, D_MODEL)), _full((D_MODEL, PEER_HEADS * D_KEY)), _full((2, N_KEYS, HALF_KEY))],
        out_specs=[row, sc, sc, sc, sc],
        out_shape=[jax.ShapeDtypeStruct((t, D_MODEL), BF16),
                   jax.ShapeDtypeStruct((t // tm, PEER_HEADS, N_KEYS, tm), BF16),
                   jax.ShapeDtypeStruct((t // tm, PEER_HEADS, N_KEYS, tm), BF16),
                   jax.ShapeDtypeStruct((t // tm, PEER_HEADS, N_KEYS, tm), F32),
                   jax.ShapeDtypeStruct((t // tm, PEER_HEADS, N_KEYS, tm), F32)],
        scratch_shapes=[pltpu.VMEM((PEER_HEADS, N_KEYS, tm), F32), pltpu.VMEM((PEER_HEADS, N_KEYS, tm), F32),
                        pltpu.VMEM((tm, PEER_HEADS * D_KEY), BF16), pltpu.VMEM((2, PEER_TOPK, LANE), F32),
                        pltpu.VMEM((4 * PEER_TOPK, LANE), F32)],
        compiler_params=_cparams("parallel"),
        name="peer_route",
    )(x2, g, wq, keys)


def _peer_dense_step(c, parity, h_ref, rank2_ref, e2_ref, cnt_ref, e1_ref, u_ref, vt_ref, pre_ref, p_ref, acc_ref,
                     *, te, n_chunks, stages):
    tm = h_ref.shape[0]
    rows_per_step = te // N_KEYS
    assert 2 * rows_per_step == 8
    part = 64
    gate_w = 2 * LANE
    score_buf, gate_buf, fold_buf = parity, 1 - parity, parity
    key_tile = pl.ds(pl.multiple_of((jnp.clip(c - 1, 0, n_chunks - 1) // 2) * 8, 8), 8)
    row0 = rows_per_step * (1 - parity)

    def score(piece):
        rows = slice((piece // 2) * te // 2, (piece // 2 + 1) * te // 2)
        toks = slice((piece % 2) * gate_w, (piece % 2 + 1) * gate_w)
        u_rows = slice(parity * te + rows.start, parity * te + rows.stop)
        pre_ref[score_buf, rows, toks] = _dot_nt(u_ref[u_rows, :], h_ref[toks, :])

    def fold(piece):
        rows = slice((piece // 2) * D_MODEL // 4, (piece // 2 + 1) * D_MODEL // 4)
        toks = slice((piece % 2) * gate_w, (piece % 2 + 1) * gate_w)
        acc_ref[rows, toks] += _dot(vt_ref[parity, rows, :], p_ref[fold_buf, :, toks])

    def gate_heads(lt, rp, w, heads):
        lanes = slice(lt * gate_w, (lt + 1) * gate_w)
        for h in heads:
            rank2 = rank2_ref[h, rp * part:(rp + 1) * part, lanes]
            e2 = e2_ref[h, rp * part:(rp + 1) * part, lanes]
            cnt = cnt_ref[h, key_tile, lanes]
            e1 = e1_ref[h, key_tile, lanes]
            for j in range(rows_per_step):
                cnt_j = jnp.broadcast_to(cnt[row0 + j:row0 + j + 1, :], (part, gate_w)).astype(BF16)
                e1_j = jnp.broadcast_to(e1[row0 + j:row0 + j + 1, :], (part, gate_w)).astype(BF16)
                term = jnp.where(rank2 < cnt_j, e2, jnp.zeros_like(e2)) * e1_j
                w[j] = term if w[j] is None else w[j] + term

    def gate_store(lt, rp, w):
        lanes = slice(lt * gate_w, (lt + 1) * gate_w)
        for j in range(rows_per_step):
            rows = slice(j * N_KEYS + rp * part, j * N_KEYS + (rp + 1) * part)
            pre = pre_ref[gate_buf, rows, lanes]
            act = pre * (1.0 + lax.erf(pre * (0.5 ** 0.5)))
            p_ref[gate_buf, rows, lanes] = w[j] * act.astype(BF16)

    assert tm // gate_w == 2 and N_KEYS // part == 2
    mxu = [lambda n=n: score(n) for n in range(4)] + [lambda n=n: fold(n) for n in range(8)]
    order = [0, 4, 5, 1, 6, 7, 2, 8, 9, 3, 10, 11]
    live = [n for n in order if ("score" if n < 4 else "fold") in stages]
    if "gate" not in stages:
        for n in live:
            mxu[n]()
        return
    slots = 4 * 3
    pieces = iter([mxu[n] for n in live] + [None] * (slots - len(live)))
    for lt in range(2):
        for rp in range(2):
            w = [None] * rows_per_step
            for h in range(PEER_HEADS):
                gate_heads(lt, rp, w, (h,))
                if h % 3 == 1:
                    piece = next(pieces)
                    if piece is not None:
                        piece()
            gate_store(lt, rp, w)


def _peer_dense_kernel(x_ref, h_ref, rank2_ref, e2_ref, cnt_ref, e1_ref, u_ref, vt_ref, gf_ref, out_ref,
                       pre_ref, p_ref, acc_ref, *, te, n_chunks, final_norm):
    g = pl.program_id(1)

    last = n_chunks // 2

    @pl.when(g == 0)
    def _():
        acc_ref[...] = jnp.zeros_like(acc_ref)

    full = ("score", "gate", "fold")
    plans = (((0, 0), (("score",), ("score", "gate"))),
             ((1, last - 1), (full, full)),
             ((last, last), (("gate", "fold"), ("fold",))))
    for (lo, hi), stage_sets in plans:
        for parity, stages in enumerate(stage_sets):
            guard = jnp.logical_and(g >= lo, g <= hi) if parity == 0 else jnp.logical_and(g > lo - 1, g < hi + 1)

            @pl.when(guard)
            def _(parity=parity, stages=stages):
                _peer_dense_step(2 * g + parity, parity, h_ref, rank2_ref, e2_ref, cnt_ref, e1_ref, u_ref, vt_ref,
                                 pre_ref, p_ref, acc_ref, te=te, n_chunks=n_chunks, stages=stages)

    @pl.when(g == last)
    def _():
        y = x_ref[...] + acc_ref[...].T
        if final_norm:
            y = _rms(y, gf_ref[...])
        out_ref[...] = y


def _peer_dense(x2, hn, rank2, e2, cnt, e1, u, vt, gf, tm, te, final_norm):
    t = x2.shape[0]
    n_chunks = N_EXPERTS // te
    row = pl.BlockSpec((tm, D_MODEL), lambda i, c: (i, 0))
    sc = pl.BlockSpec((None, PEER_HEADS, N_KEYS, tm), lambda i, c: (i, 0, 0, 0))
    return pl.pallas_call(
        functools.partial(_peer_dense_kernel, te=te, n_chunks=n_chunks, final_norm=final_norm),
        grid=(t // tm, n_chunks // 2 + 1),
        in_specs=[row, row, sc, sc, sc, sc,
                  pl.BlockSpec((2 * te, D_MODEL), lambda i, g: (jnp.minimum(g, n_chunks // 2 - 1), 0)),
                  pl.BlockSpec((2, D_MODEL, te), lambda i, g: (jnp.clip(g - 1, 0, n_chunks // 2 - 1), 0, 0)),
                  pl.BlockSpec((1, D_MODEL), lambda i, c: (0, 0))],
        out_specs=row,
        out_shape=jax.ShapeDtypeStruct((t, D_MODEL), F32),
        scratch_shapes=[pltpu.VMEM((2, te, tm), F32), pltpu.VMEM((2, te, tm), BF16),
                        pltpu.VMEM((D_MODEL, tm), F32)],
        compiler_params=_cparams("parallel", "arbitrary"),
        name="peer_dense",
    )(x2, hn, rank2, e2, cnt, e1, u, vt, gf)


def _pad_cols(w, width):
    return jnp.pad(w, ((0, 0), (0, width - w.shape[1])))


def _rope_partner(w):
    half = MLA_ROPE // 2
    return jnp.concatenate([-w[:, half:], w[:, :half]], axis=1)


def _rope_lanes(w):
    return jnp.pad(w, ((0, 0), (MLA_NOPE, LANE - MLA_NOPE - MLA_ROPE)))


def _layer_weights(w_in, w_uq, w_ukv, w_o_sb, w_o_fox):
    o = IN_OFFSETS
    seg = lambda n: w_in[:, o[n]:o[n + 1]]
    scale = HEAD_DIM ** -0.5 * LOG2E
    w_kr = seg(2)
    w_cat = jnp.concatenate(
        [seg(0), seg(1), _rope_lanes(w_kr), _rope_lanes(_rope_partner(w_kr)),
         _pad_cols(seg(3) * scale, HEAD_PAD_W), _pad_cols(seg(4), HEAD_PAD_W), _pad_cols(seg(5), HEAD_PAD_W),
         _pad_cols(seg(6) * scale, HEAD_PAD_W), _pad_cols(seg(7), HEAD_PAD_W), _pad_cols(seg(8), HEAD_PAD_W),
         _pad_cols(seg(9), LANE)], axis=1).astype(BF16)
    w_gate = seg(10).astype(BF16)
    uq = w_uq.reshape(Q_LORA, MLA_HEADS, MLA_QK)
    nope, rope = uq[..., :MLA_NOPE], uq[..., MLA_NOPE:]
    zpad = jnp.zeros((Q_LORA, MLA_HEADS, LANE - MLA_QK), F32)
    wqa = jnp.concatenate([nope, rope, zpad], axis=-1).reshape(Q_LORA, MLA_HEADS * LANE).astype(BF16)
    partner = jnp.concatenate([-rope[..., MLA_ROPE // 2:], rope[..., :MLA_ROPE // 2]], axis=-1)
    wqb = jnp.concatenate([jnp.zeros_like(nope), partner, zpad], axis=-1).reshape(Q_LORA, MLA_HEADS * LANE).astype(BF16)
    ukv = w_ukv.reshape(KV_LORA, MLA_HEADS, MLA_NOPE + MLA_V)
    wk = jnp.pad(ukv[..., :MLA_NOPE], ((0, 0), (0, 0), (0, LANE - MLA_NOPE))).reshape(KV_LORA, MLA_HEADS * LANE).astype(BF16)
    wv = ukv[..., MLA_NOPE:].reshape(KV_LORA, MLA_OUT).astype(BF16)
    pad_rows = lambda w: jnp.pad(w, ((0, HEAD_PAD_W - w.shape[0]), (0, 0))).astype(BF16)
    return w_cat, w_gate, wqa, wqb, wk, wv, pad_rows(w_o_sb), pad_rows(w_o_fox)


def _tiles(b, s):
    t = b * s
    tm = min(TOKEN_TILE, t)
    tq = min(ATTN_TILE, s)
    return t, tm, tq


def kernel(x, positions, norm1_g, w_in, mla_q_norm_g, w_uq, mla_kv_norm_g, w_ukv, fox_b_f, w_o_mla, w_o_sb,
           w_o_fox, b_gate, w_out, norm2_g, peer_w_q, peer_sub_keys, peer_u, peer_v, final_norm_g):
    b, s, d = x.shape
    assert d == D_MODEL and MLA_OUT == HEAD_PAD_W
    depth = w_in.shape[0]
    t, tm, tq = _tiles(b, s)
    assert t % tm == 0 and s % tq == 0 and tq % CHUNK == 0
    te = EXPERT_CHUNK
    x2 = x.reshape(t, d)
    cos, sin = _rope_tables(positions, tm)
    gf = final_norm_g.reshape(1, d)
    for l in range(depth):
        w_cat, w_gate, wqa, wqb, wk, wv, wo_sb, wo_fox = _layer_weights(w_in[l], w_uq[l], w_ukv[l], w_o_sb[l], w_o_fox[l])
        g1 = norm1_g[l].reshape(1, d)
        q_a, k_a, v_a, sb, fx, f = _inproj(x2, g1, w_cat, cos, sin, mla_q_norm_g[l].reshape(1, Q_LORA), wqa, wqb,
                                           mla_kv_norm_g[l].reshape(1, KV_LORA), wk, wv, tm)
        bf_row = jnp.zeros((1, LANE), F32).at[0, :FOX_HEADS].set(fox_b_f[l])
        cum, cumt = _fox_prep(f.reshape(b, s, LANE), bf_row, min(SUFFIX_TILE, s))
        hw = MLA_HEADS * LANE
        acc_a = _head_state(MLA_HEADS, tq)
        o_a = _attention(_mla_attn_kernel, q_a.reshape(b, s, hw), k_a.reshape(b, s, hw), v_a.reshape(b, s, MLA_OUT),
                         (0, 0, 0), [], [], [acc_a, acc_a, acc_a], hw, MLA_OUT, tq, "mla_attn")
        sb3 = sb.reshape(b, s, QKV_W)
        fx3 = fx.reshape(b, s, QKV_W)
        acc_b = _head_state(SB_HEADS, tq)
        qm = pltpu.VMEM((SB_HEADS, tq, LANE), BF16)
        o_b = _attention(_sb_attn_kernel, sb3, sb3, sb3, (0, 1, 2), [], [],
                         [qm, pltpu.VMEM((min(SUFFIX_TILE, tq),) * 2, BF16), acc_b, acc_b], HEAD_PAD_W, HEAD_PAD_W, tq,
                         "sb_attn")
        fox_specs = [pl.BlockSpec((1, tq, LANE), lambda bi, i: (bi, i, 0)),
                     pl.BlockSpec((1, 8, s), lambda bi, i: (bi, 0, 0))]
        o_c = _attention(_fox_attn_kernel, fx3, fx3, fx3, (0, 1, 2), [cum, cumt], fox_specs,
                         [qm, acc_b, acc_b, acc_b, acc_b], HEAD_PAD_W, HEAD_PAD_W, tq, "fox_attn")
        x2 = _merge(x2, g1, w_gate, b_gate[l].reshape(1, -1), o_a.reshape(t, MLA_OUT), o_b.reshape(t, HEAD_PAD_W),
                    o_c.reshape(t, HEAD_PAD_W), w_o_mla[l].astype(BF16), wo_sb, wo_fox, w_out[l].astype(BF16), tm)
        hn, rank2, e2, cnt, e1 = _peer_route(x2, norm2_g[l].reshape(1, d), peer_w_q[l].astype(BF16),
                                             peer_sub_keys[l].astype(BF16), tm)
        vt = peer_v[l].astype(BF16).reshape(N_EXPERTS // te, te, d).transpose(0, 2, 1)
        x2 = _peer_dense(x2, hn, rank2, e2, cnt, e1, peer_u[l].astype(BF16), vt, gf, tm, te,
                         final_norm=(l == depth - 1))
    return x2.reshape(b, s, d)
```

```python
import functools

import numpy as np
import jax
import jax.numpy as jnp
from jax import lax
from jax.experimental import pallas as pl
from jax.experimental.pallas import tpu as pltpu

F32 = jnp.float32
BF16 = jnp.bfloat16

D_MODEL = 1024
CHUNK = 64
HEAD_DIM = 64
NORM_EPS = 1e-6
NEG_INF = -1e30
MLA_HEADS = 6
MLA_NOPE = 64
MLA_ROPE = 32
MLA_V = 64
Q_LORA = 256
KV_LORA = 128
ROPE_THETA = 10000.0
SB_HEADS = 5
FOX_HEADS = 5
N_BRANCH = 3
PEER_HEADS = 8
N_KEYS = 128
N_EXPERTS = N_KEYS * N_KEYS
D_KEY = 256
HALF_KEY = D_KEY // 2
PEER_TOPK = 16
MLA_QK = MLA_NOPE + MLA_ROPE
SB_W = SB_HEADS * HEAD_DIM
FOX_W = FOX_HEADS * HEAD_DIM
MLA_OUT = MLA_HEADS * MLA_V
IN_SPLITS = (Q_LORA, KV_LORA, MLA_ROPE, SB_W, SB_W, SB_W, FOX_W, FOX_W, FOX_W, FOX_HEADS, N_BRANCH * D_MODEL)
IN_OFFSETS = tuple(int(o) for o in np.cumsum((0,) + IN_SPLITS))

LANE = 128
HEAD_PAD_W = 384
MLA_IN_W = Q_LORA + KV_LORA + 2 * LANE
QKV_W = 3 * HEAD_PAD_W
IN_W = MLA_IN_W + 2 * QKV_W + LANE
VMEM_LIMIT = 48 * 1024 * 1024
TOKEN_TILE = 512
ATTN_TILE = 512
SUFFIX_TILE = 256
EXPERT_CHUNK = 4 * N_KEYS

LOG2E = 1.4426950408889634
_NT = (((1,), (1,)), ((), ()))


def _cparams(*sem):
    return pltpu.CompilerParams(dimension_semantics=sem, vmem_limit_bytes=VMEM_LIMIT)


def _rms(x, g):
    return x * lax.rsqrt(jnp.mean(x * x, axis=-1, keepdims=True) + NORM_EPS) * g


def _dot(a, b):
    return jnp.dot(a, b, preferred_element_type=F32)


def _dot_nt(a, b):
    return lax.dot_general(a, b, _NT, preferred_element_type=F32)


def _full(shape):
    return pl.BlockSpec(shape, lambda *_: (0,) * len(shape))


def _rope_kernel(pos_ref, inv_ref, cos_ref, sin_ref):
    ang = pos_ref[...].astype(F32) * inv_ref[...]
    cos_ref[...] = jnp.cos(ang)
    sin_ref[...] = jnp.sin(ang)


def _rope_tables(positions, tm):
    t = positions.size
    inv = ROPE_THETA ** (-jnp.arange(0, MLA_ROPE, 2, dtype=F32) / MLA_ROPE)
    inv_row = jnp.zeros((1, LANE), F32).at[0, MLA_NOPE:MLA_NOPE + MLA_ROPE].set(jnp.concatenate([inv, inv]))
    return pl.pallas_call(
        _rope_kernel,
        grid=(t // tm,),
        in_specs=[pl.BlockSpec((tm, 1), lambda i: (i, 0)), _full((1, LANE))],
        out_specs=[pl.BlockSpec((tm, LANE), lambda i: (i, 0))] * 2,
        out_shape=[jax.ShapeDtypeStruct((t, LANE), F32)] * 2,
        compiler_params=_cparams("parallel"),
        name="rope_tables",
    )(positions.reshape(t, 1), inv_row)


def _inproj_kernel(x_ref, g_ref, w_ref, cos_ref, sin_ref, gq_ref, wqa_ref, wqb_ref, gkv_ref, wk_ref, wv_ref,
                   q_ref, k_ref, v_ref, sb_ref, fx_ref, f_ref, mla_ref):
    hn = _rms(x_ref[...], g_ref[...]).astype(BF16)
    o = 0
    for ref, width in ((mla_ref, MLA_IN_W), (sb_ref, QKV_W), (fx_ref, QKV_W), (f_ref, LANE)):
        ref[...] = _dot(hn, w_ref[:, o:o + width]).astype(ref.dtype)
        o += width
    _mla_prep(mla_ref, cos_ref, sin_ref, gq_ref, wqa_ref, wqb_ref, gkv_ref, wk_ref, wv_ref, q_ref, k_ref, v_ref)


def _inproj(x2, g, w_cat, cos, sin, gq, wqa, wqb, gkv, wk, wv, tm):
    t = x2.shape[0]
    row = lambda w: pl.BlockSpec((tm, w), lambda i: (i, 0))
    hw = MLA_HEADS * LANE
    return pl.pallas_call(
        _inproj_kernel,
        grid=(t // tm,),
        in_specs=[row(D_MODEL), _full((1, D_MODEL)), _full((D_MODEL, IN_W)), row(LANE), row(LANE),
                  _full((1, Q_LORA)), _full((Q_LORA, hw)), _full((Q_LORA, hw)), _full((1, KV_LORA)),
                  _full((KV_LORA, hw)), _full((KV_LORA, MLA_OUT))],
        out_specs=[row(hw), row(hw), row(MLA_OUT), row(QKV_W), row(QKV_W), row(LANE)],
        out_shape=[jax.ShapeDtypeStruct((t, hw), BF16), jax.ShapeDtypeStruct((t, hw), BF16),
                   jax.ShapeDtypeStruct((t, MLA_OUT), BF16), jax.ShapeDtypeStruct((t, QKV_W), BF16),
                   jax.ShapeDtypeStruct((t, QKV_W), BF16), jax.ShapeDtypeStruct((t, LANE), F32)],
        scratch_shapes=[pltpu.VMEM((tm, MLA_IN_W), F32)],
        compiler_params=_cparams("parallel"),
        name="inproj",
    )(x2, g, w_cat, cos, sin, gq, wqa, wqb, gkv, wk, wv)


def _mla_prep(in_ref, cos_ref, sin_ref, gq_ref, wqa_ref, wqb_ref, gkv_ref, wk_ref, wv_ref, q_ref, k_ref, v_ref):
    cos = cos_ref[...]
    sin = sin_ref[...]
    qn = _rms(in_ref[:, :Q_LORA], gq_ref[...]).astype(BF16)
    kn = _rms(in_ref[:, Q_LORA:Q_LORA + KV_LORA], gkv_ref[...]).astype(BF16)
    o = Q_LORA + KV_LORA
    k_rot = in_ref[:, o:o + LANE] * cos + in_ref[:, o + LANE:o + 2 * LANE] * sin
    scale = MLA_QK ** -0.5 * LOG2E
    qa = _dot(qn, wqa_ref[...])
    qb = _dot(qn, wqb_ref[...])
    ka = _dot(kn, wk_ref[...])
    for h in range(MLA_HEADS):
        sl = slice(h * LANE, (h + 1) * LANE)
        q_ref[:, sl] = ((qa[:, sl] * cos + qb[:, sl] * sin) * scale).astype(BF16)
        k_ref[:, sl] = (ka[:, sl] + k_rot).astype(BF16)
    v_ref[...] = _dot(kn, wv_ref[...]).astype(BF16)


def _fox_prep_kernel(f_ref, bf_ref, cum_ref, cumt_ref, *, blk):
    s = f_ref.shape[1]
    row = lax.broadcasted_iota(jnp.int32, (blk, blk), 0)
    col = lax.broadcasted_iota(jnp.int32, (blk, blk), 1)
    tri = jnp.where(col <= row, 1.0, 0.0).astype(BF16)
    carry = jnp.zeros((1, LANE), F32)
    for b in range(s // blk):
        f = f_ref[0, b * blk:(b + 1) * blk, :] + bf_ref[...]
        lf = (jnp.minimum(f, 0.0) - jnp.log1p(jnp.exp(-jnp.abs(f)))) * LOG2E
        hi = lf.astype(BF16)
        r1 = lf - hi.astype(F32)
        mid = r1.astype(BF16)
        lo = (r1 - mid.astype(F32)).astype(BF16)
        c = _dot(tri, hi) + _dot(tri, mid) + _dot(tri, lo) + carry
        cum_ref[0, b * blk:(b + 1) * blk, :] = c
        cumt_ref[0, :, b * blk:(b + 1) * blk] = c.T[:8, :]
        carry = c[blk - 1:blk, :]


def _fox_prep(f3, bf_row, blk):
    b, s, _ = f3.shape
    return pl.pallas_call(
        functools.partial(_fox_prep_kernel, blk=blk),
        grid=(b,),
        in_specs=[pl.BlockSpec((1, s, LANE), lambda i: (i, 0, 0)), _full((1, LANE))],
        out_specs=[pl.BlockSpec((1, s, LANE), lambda i: (i, 0, 0)), pl.BlockSpec((1, 8, s), lambda i: (i, 0, 0))],
        out_shape=[jax.ShapeDtypeStruct((b, s, LANE), F32), jax.ShapeDtypeStruct((b, 8, s), F32)],
        compiler_params=_cparams("parallel"),
        name="fox_prep",
    )(f3, bf_row)


def _softmax_update(h, s, v, m_ref, l_ref, acc_ref):
    m_old = m_ref[h]
    m_new = jnp.maximum(m_old, jnp.max(s, axis=-1, keepdims=True))
    alpha = jnp.exp2(m_old - m_new)
    p = [jnp.exp2(s[:, n * LANE:(n + 1) * LANE] - m_new) for n in range(s.shape[1] // LANE)]
    part = p[0]
    for p_n in p[1:]:
        part = part + p_n
    l_ref[h] = alpha * l_ref[h] + part
    m_ref[h] = m_new
    acc_ref[h] = alpha * acc_ref[h] + _dot(jnp.concatenate(p, axis=1).astype(BF16), v)


def _softmax_reset(m_ref, l_ref, acc_ref):
    m_ref[...] = jnp.full(m_ref.shape, NEG_INF, F32)
    l_ref[...] = jnp.zeros(l_ref.shape, F32)
    acc_ref[...] = jnp.zeros(acc_ref.shape, F32)


def _half_mask(tq, half):
    lane = lax.broadcasted_iota(jnp.int32, (tq, LANE), 1)
    return (lane < HEAD_DIM) if half == 0 else (lane >= HEAD_DIM)


def _store_head_pairs(o_ref, heads, value_of):
    tq = o_ref.shape[1]
    lo_half = _half_mask(tq, 0)
    for hb in range(o_ref.shape[2] // LANE):
        lo = value_of(2 * hb)
        hi = value_of(2 * hb + 1) if 2 * hb + 1 < heads else jnp.zeros_like(lo)
        o_ref[0, :, hb * LANE:(hb + 1) * LANE] = jnp.where(lo_half, lo, hi).astype(o_ref.dtype)


def _masked_queries(q_ref, qm_ref, heads):
    tq = q_ref.shape[1]
    for h in range(heads):
        qf = q_ref[0, :, (h // 2) * LANE:(h // 2 + 1) * LANE].astype(F32)
        qm_ref[h] = jnp.where(_half_mask(tq, h % 2), qf, 0.0).astype(BF16)


def _block_iotas(tq):
    return lax.broadcasted_iota(jnp.int32, (tq, tq), 0), lax.broadcasted_iota(jnp.int32, (tq, tq), 1)


def _mla_attn_kernel(q_ref, k_ref, v_ref, o_ref, m_ref, l_ref, acc_ref, *, tq):
    i = pl.program_id(1)
    _softmax_reset(m_ref, l_ref, acc_ref)

    def block(j, diagonal):
        rows = pl.ds(pl.multiple_of(j * tq, tq), tq)
        def scores(h):
            sl = slice(h * LANE, (h + 1) * LANE)
            return _dot_nt(q_ref[0, :, sl], k_ref[0, rows, sl])

        s_next = scores(0)
        for h in range(MLA_HEADS):
            s = s_next
            if h + 1 < MLA_HEADS:
                s_next = scores(h + 1)
            if diagonal:
                r, c = _block_iotas(tq)
                s = jnp.where((c // CHUNK) <= (r // CHUNK), s, NEG_INF)
            _softmax_update(h, s, v_ref[0, rows, (h // 2) * LANE:(h // 2 + 1) * LANE], m_ref, l_ref, acc_ref)

    def body(j, carry):
        block(j, False)
        return carry

    lax.fori_loop(0, i, body, 0)
    block(i, True)
    _store_head_pairs(o_ref, MLA_HEADS, lambda h: acc_ref[h] / jnp.sum(l_ref[h], axis=-1, keepdims=True))


def _fox_attn_kernel(q_ref, k_ref, v_ref, cum_ref, cumt_ref, o_ref, qm_ref, cq_ref, m_ref, l_ref, acc_ref, *, tq):
    i = pl.program_id(1)
    _softmax_reset(m_ref, l_ref, acc_ref)
    _masked_queries(q_ref, qm_ref, FOX_HEADS)
    for h in range(FOX_HEADS):
        cq_ref[h] = jnp.broadcast_to(cum_ref[0, :, h:h + 1], (tq, LANE))

    def block(j, diagonal):
        start = pl.multiple_of(j * tq, tq)
        rows = pl.ds(start, tq)
        def scores(h):
            return _dot_nt(qm_ref[h], k_ref[0, rows, (h // 2) * LANE:(h // 2 + 1) * LANE])

        s_next = scores(0)
        for h in range(FOX_HEADS):
            sl = slice((h // 2) * LANE, (h // 2 + 1) * LANE)
            s = s_next
            if h + 1 < FOX_HEADS:
                s_next = scores(h + 1)
            ck = cumt_ref[0, h:h + 1, rows]
            s = jnp.concatenate([s[:, n * LANE:(n + 1) * LANE] + (cq_ref[h] - ck[:, n * LANE:(n + 1) * LANE])
                                 for n in range(tq // LANE)], axis=1)
            if diagonal:
                r, c = _block_iotas(tq)
                s = jnp.where(c <= r, s, NEG_INF)
            _softmax_update(h, s, v_ref[0, rows, sl], m_ref, l_ref, acc_ref)

    def body(j, carry):
        block(j, False)
        return carry

    lax.fori_loop(0, i, body, 0)
    block(i, True)
    _store_head_pairs(o_ref, FOX_HEADS, lambda h: acc_ref[h] / jnp.sum(l_ref[h], axis=-1, keepdims=True))


def _sb_attn_kernel(q_ref, k_ref, v_ref, o_ref, qm_ref, suffix_ref, rest_ref, acc_ref, *, tq):
    i = pl.program_id(1)
    sub = suffix_ref.shape[0]
    r, c = _block_iotas(sub)
    suffix_ref[...] = jnp.where(r > c, 1.0, 0.0).astype(BF16)
    for h in range(SB_HEADS):
        rest_ref[h] = jnp.zeros(rest_ref.shape[1:], F32)
        acc_ref[h] = jnp.zeros(acc_ref.shape[1:], F32)
    _masked_queries(q_ref, qm_ref, SB_HEADS)

    def block(j, diagonal):
        rows = pl.ds(pl.multiple_of(j * tq, tq), tq)
        def scores(h):
            return _dot_nt(qm_ref[h], k_ref[0, rows, (h // 2) * LANE:(h // 2 + 1) * LANE])

        z_next = scores(0)
        for h in range(SB_HEADS):
            sl = slice((h // 2) * LANE, (h // 2 + 1) * LANE)
            z = z_next
            if h + 1 < SB_HEADS:
                z_next = scores(h + 1)
            log_b = jnp.minimum(z, 0.0) - jnp.log(1.0 + jnp.exp2(-jnp.abs(z))) * LOG2E
            log_1m = log_b - z
            if diagonal:
                strict = _block_iotas(tq)[1] < _block_iotas(tq)[0]
                log_1m = jnp.where(strict, log_1m, 0.0)
            later = rest_ref[h]
            a = [None] * (tq // sub)
            for k in reversed(range(tq // sub)):
                cols = slice(k * sub, (k + 1) * sub)
                within = _dot(log_1m[:, cols].astype(BF16), suffix_ref[...])
                a[k] = jnp.concatenate([jnp.exp2(log_b[:, cols][:, n * LANE:(n + 1) * LANE]
                                                 + within[:, n * LANE:(n + 1) * LANE] + later)
                                        for n in range(sub // LANE)], axis=1)
                later = later + jnp.sum(log_1m[:, cols], axis=-1, keepdims=True)
            a = jnp.concatenate(a, axis=1)
            if diagonal:
                a = jnp.where(strict, a, 0.0)
            acc_ref[h] += _dot(a.astype(BF16), v_ref[0, rows, sl])
            rest_ref[h] = later

    block(i, True)

    def body(n, carry):
        block(i - 1 - n, False)
        return carry

    lax.fori_loop(0, i, body, 0)
    _store_head_pairs(o_ref, SB_HEADS, lambda h: acc_ref[h])


def _attention(kernel, q, k, v, cols, extra, extra_specs, scratch, qw, vw, tq, name):
    b, s, _ = q.shape
    blk = lambda w, n=0: pl.BlockSpec((1, tq, w), lambda bi, i: (bi, i, n))
    seq = lambda w, n: pl.BlockSpec((1, s, w), lambda bi, i: (bi, 0, n))
    return pl.pallas_call(
        functools.partial(kernel, tq=tq),
        grid=(b, s // tq),
        in_specs=[blk(qw, cols[0]), seq(qw, cols[1]), seq(vw, cols[2])] + extra_specs,
        out_specs=blk(vw),
        out_shape=jax.ShapeDtypeStruct((b, s, vw), BF16),
        scratch_shapes=scratch,
        compiler_params=_cparams("parallel", "parallel"),
        name=name,
    )(q, k, v, *extra)


def _head_state(heads, tq):
    return pltpu.VMEM((heads, tq, LANE), F32)


def _merge_kernel(x_ref, g_ref, wg_ref, bg_ref, oa_ref, ob_ref, oc_ref, wa_ref, wb_ref, wc_ref, wo_ref, out_ref):
    x = x_ref[...]
    hn = _rms(x, g_ref[...]).astype(BF16)
    y = None
    for n, (o_ref, w_ref) in enumerate(((oa_ref, wa_ref), (ob_ref, wb_ref), (oc_ref, wc_ref))):
        sl = slice(n * D_MODEL, (n + 1) * D_MODEL)
        gate = jax.nn.sigmoid(_dot(hn, wg_ref[:, sl]) + bg_ref[:, sl])
        term = gate * _dot(o_ref[...], w_ref[...])
        y = term if y is None else y + term
    out_ref[...] = x + _dot(y.astype(BF16), wo_ref[...])


def _merge(x2, g, wg, bg, oa, ob, oc, wa, wb, wc, wo, tm):
    t = x2.shape[0]
    row = lambda w: pl.BlockSpec((tm, w), lambda i: (i, 0))
    return pl.pallas_call(
        _merge_kernel,
        grid=(t // tm,),
        in_specs=[row(D_MODEL), _full((1, D_MODEL)), _full((D_MODEL, N_BRANCH * D_MODEL)),
                  _full((1, N_BRANCH * D_MODEL)), row(HEAD_PAD_W), row(HEAD_PAD_W), row(HEAD_PAD_W),
                  _full((HEAD_PAD_W, D_MODEL)), _full((HEAD_PAD_W, D_MODEL)), _full((HEAD_PAD_W, D_MODEL)),
                  _full((D_MODEL, D_MODEL))],
        out_specs=row(D_MODEL),
        out_shape=jax.ShapeDtypeStruct((t, D_MODEL), F32),
        compiler_params=_cparams("parallel"),
        name="merge",
    )(x2, g, wg, bg, oa, ob, oc, wa, wb, wc, wo)


def _sorted_top16(groups):
    x = list(groups)
    assert len(x) in (PEER_TOPK // 2, PEER_TOPK)

    def exchange(i, l, descending):
        hi, lo = jnp.maximum(x[i], x[l]), jnp.minimum(x[i], x[l])
        x[i], x[l] = (hi, lo) if descending else (lo, hi)

    k = 2
    while k <= len(x):
        j = k // 2
        while j >= 1:
            for i in range(len(x)):
                if i ^ j > i:
                    exchange(i, i ^ j, (i & k) == 0)
            j //= 2
        k *= 2
    for shift in (4, 2, 1):
        other = [pltpu.roll(v, shift, axis=0) for v in x]
        if len(x) < PEER_TOPK:
            x = x + other[::-1]
        else:
            x = [jnp.maximum(x[r], other[PEER_TOPK - 1 - r]) for r in range(PEER_TOPK)]
        j = PEER_TOPK // 2
        while j >= 1:
            for i in range(PEER_TOPK):
                if i ^ j > i:
                    exchange(i, i ^ j, True)
            j //= 2
    return x


def _peer_route_kernel(x_ref, g_ref, wq_ref, keys_ref, h_ref, rank2_ref, e2_ref, cnt_ref, e1_ref,
                       s1_ref, s2_ref, q_ref, top_ref, cand_ref):
    hn = _rms(x_ref[...], g_ref[...]).astype(BF16)
    h_ref[...] = hn
    wide = 2 * D_KEY
    for n in range(PEER_HEADS * D_KEY // wide):
        q_ref[:, n * wide:(n + 1) * wide] = _dot(hn, wq_ref[:, n * wide:(n + 1) * wide]).astype(BF16)
    for h in range(PEER_HEADS):
        for side, s_ref in ((0, s1_ref), (1, s2_ref)):
            o = h * D_KEY + side * HALF_KEY
            s_ref[h] = _dot_nt(keys_ref[side], q_ref[:, o:o + HALF_KEY])

    def lane_tile(lt, carry):
        lanes = pl.ds(pl.multiple_of(lt * LANE, LANE), LANE)
        for h in range(PEER_HEADS):
            scores, tops = [], []
            for side, s_ref in ((0, s1_ref), (1, s2_ref)):
                groups = [s_ref[h, 8 * n:8 * n + 8, lanes] for n in range(N_KEYS // 8)]
                top = _sorted_top16(groups)
                for r in range(PEER_TOPK):
                    top_ref[side, r:r + 1, :] = top[r][0:1, :]
                scores.append(groups)
                tops.append(top)
            (s1, s2), (top1, top2) = scores, tops
            v2_top8 = top_ref[1, 0:8, :]
            sub = lax.broadcasted_iota(jnp.int32, (8, LANE), 0)
            v2_top4_twice = jnp.where(sub < 4, v2_top8, pltpu.roll(v2_top8, 4, axis=0))
            cand_ref[0:16, :] = top_ref[0, 0:1, :] + top_ref[1]
            for a in (1, 2, 3):
                cand_ref[8 + 8 * a:16 + 8 * a, :] = top_ref[0, a:a + 1, :] + v2_top8
            for n, a in enumerate((4, 6)):
                v1_pair = jnp.where(sub < 4, top_ref[0, a:a + 1, :], top_ref[0, a + 1:a + 2, :])
                cand_ref[40 + 8 * n:48 + 8 * n, :] = v1_pair + v2_top4_twice
            cand_ref[56:64, :] = top_ref[0, 8:16, :] + top_ref[1, 0:1, :]
            best = _sorted_top16([cand_ref[8 * n:8 * n + 8, :] for n in range(cand_ref.shape[0] // 8)])
            tau = best[PEER_TOPK - 1]
            m1, m2 = top1[0], top2[0]
            zsum = best[0] * 0.0
            for v in best:
                zsum = zsum + jnp.exp(v - (m1 + m2))
            half_inv_z = 0.5 / zsum
            ranks, cnts = [], []
            for g1, g2 in zip(s1, s2):
                thr = tau - g1
                rank, cnt = jnp.zeros((8, LANE), F32), jnp.zeros((8, LANE), F32)
                for b in range(PEER_TOPK):
                    rank = jnp.where(top2[b] > g2, float(b + 1), rank)
                    cnt = jnp.where(top2[b] >= thr, float(b + 1), cnt)
                ranks.append(rank)
                cnts.append(cnt)
            rank2_ref[h, :, lanes] = jnp.concatenate(ranks, axis=0).astype(BF16)
            e2_ref[h, :, lanes] = jnp.concatenate([jnp.exp(g2 - m2) for g2 in s2], axis=0).astype(BF16)
            cnt_ref[h, :, lanes] = jnp.concatenate(cnts, axis=0)
            e1_ref[h, :, lanes] = jnp.concatenate([jnp.exp(g1 - m1) * half_inv_z for g1 in s1], axis=0)
        return carry

    lax.fori_loop(0, x_ref.shape[0] // LANE, lane_tile, 0)


def _peer_route(x2, g, wq, keys, tm):
    t = x2.shape[0]
    row = pl.BlockSpec((tm, D_MODEL), lambda i: (i, 0))
    sc = pl.BlockSpec((None, PEER_HEADS, N_KEYS, tm), lambda i: (i, 0, 0, 0))
    return pl.pallas_call(
        _peer_route_kernel,
        grid=(t // tm,),
        in_specs=[row, _full((1, D_MODEL)), _full((D_MODEL, PEER_HEADS * D_KEY)), _full((2, N_KEYS, HALF_KEY))],
        out_specs=[row, sc, sc, sc, sc],
        out_shape=[jax.ShapeDtypeStruct((t, D_MODEL), BF16),
                   jax.ShapeDtypeStruct((t // tm, PEER_HEADS, N_KEYS, tm), BF16),
                   jax.ShapeDtypeStruct((t // tm, PEER_HEADS, N_KEYS, tm), BF16),
                   jax.ShapeDtypeStruct((t // tm, PEER_HEADS, N_KEYS, tm), F32),
                   jax.ShapeDtypeStruct((t // tm, PEER_HEADS, N_KEYS, tm), F32)],
        scratch_shapes=[pltpu.VMEM((PEER_HEADS, N_KEYS, tm), F32), pltpu.VMEM((PEER_HEADS, N_KEYS, tm), F32),
                        pltpu.VMEM((tm, PEER_HEADS * D_KEY), BF16), pltpu.VMEM((2, PEER_TOPK, LANE), F32),
                        pltpu.VMEM((4 * PEER_TOPK, LANE), F32)],
        compiler_params=_cparams("parallel"),
        name="peer_route",
    )(x2, g, wq, keys)


def _peer_dense_step(c, parity, h_ref, rank2_ref, e2_ref, cnt_ref, e1_ref, u_ref, vt_ref, pre_ref, p_ref, acc_ref,
                     *, te, n_chunks, stages):
    tm = h_ref.shape[0]
    rows_per_step = te // N_KEYS
    assert 2 * rows_per_step == 8
    part = 64
    gate_w = 2 * LANE
    score_buf, gate_buf, fold_buf = parity, 1 - parity, parity
    key_tile = pl.ds(pl.multiple_of((jnp.clip(c - 1, 0, n_chunks - 1) // 2) * 8, 8), 8)
    row0 = rows_per_step * (1 - parity)

    def score(piece):
        rows = slice((piece // 2) * te // 2, (piece // 2 + 1) * te // 2)
        toks = slice((piece % 2) * gate_w, (piece % 2 + 1) * gate_w)
        u_rows = slice(parity * te + rows.start, parity * te + rows.stop)
        pre_ref[score_buf, rows, toks] = _dot_nt(u_ref[u_rows, :], h_ref[toks, :])

    def fold(piece):
        rows = slice((piece // 2) * D_MODEL // 4, (piece // 2 + 1) * D_MODEL // 4)
        toks = slice((piece % 2) * gate_w, (piece % 2 + 1) * gate_w)
        acc_ref[rows, toks] += _dot(vt_ref[parity, rows, :], p_ref[fold_buf, :, toks])

    def gate_heads(lt, rp, w, heads):
        lanes = slice(lt * gate_w, (lt + 1) * gate_w)
        for h in heads:
            rank2 = rank2_ref[h, rp * part:(rp + 1) * part, lanes]
            e2 = e2_ref[h, rp * part:(rp + 1) * part, lanes]
            cnt = cnt_ref[h, key_tile, lanes]
            e1 = e1_ref[h, key_tile, lanes]
            for j in range(rows_per_step):
                cnt_j = jnp.broadcast_to(cnt[row0 + j:row0 + j + 1, :], (part, gate_w)).astype(BF16)
                e1_j = jnp.broadcast_to(e1[row0 + j:row0 + j + 1, :], (part, gate_w)).astype(BF16)
                term = jnp.where(rank2 < cnt_j, e2, jnp.zeros_like(e2)) * e1_j
                w[j] = term if w[j] is None else w[j] + term

    def gate_store(lt, rp, w):
        lanes = slice(lt * gate_w, (lt + 1) * gate_w)
        for j in range(rows_per_step):
            rows = slice(j * N_KEYS + rp * part, j * N_KEYS + (rp + 1) * part)
            pre = pre_ref[gate_buf, rows, lanes]
            act = pre * (1.0 + lax.erf(pre * (0.5 ** 0.5)))
            p_ref[gate_buf, rows, lanes] = w[j] * act.astype(BF16)

    assert tm // gate_w == 2 and N_KEYS // part == 2
    mxu = [lambda n=n: score(n) for n in range(4)] + [lambda n=n: fold(n) for n in range(8)]
    order = [0, 4, 5, 1, 6, 7, 2, 8, 9, 3, 10, 11]
    live = [n for n in order if ("score" if n < 4 else "fold") in stages]
    if "gate" not in stages:
        for n in live:
            mxu[n]()
        return
    slots = 4 * 3
    pieces = iter([mxu[n] for n in live] + [None] * (slots - len(live)))
    for lt in range(2):
        for rp in range(2):
            w = [None] * rows_per_step
            for h in range(PEER_HEADS):
                gate_heads(lt, rp, w, (h,))
                if h % 3 == 1:
                    piece = next(pieces)
                    if piece is not None:
                        piece()
            gate_store(lt, rp, w)


def _peer_dense_kernel(x_ref, h_ref, rank2_ref, e2_ref, cnt_ref, e1_ref, u_ref, vt_ref, gf_ref, out_ref,
                       pre_ref, p_ref, acc_ref, *, te, n_chunks, final_norm):
    g = pl.program_id(1)

    last = n_chunks // 2

    @pl.when(g == 0)
    def _():
        acc_ref[...] = jnp.zeros_like(acc_ref)

    full = ("score", "gate", "fold")
    plans = (((0, 0), (("score",), ("score", "gate"))),
             ((1, last - 1), (full, full)),
             ((last, last), (("gate", "fold"), ("fold",))))
    for (lo, hi), stage_sets in plans:
        for parity, stages in enumerate(stage_sets):
            guard = jnp.logical_and(g >= lo, g <= hi) if parity == 0 else jnp.logical_and(g > lo - 1, g < hi + 1)

            @pl.when(guard)
            def _(parity=parity, stages=stages):
                _peer_dense_step(2 * g + parity, parity, h_ref, rank2_ref, e2_ref, cnt_ref, e1_ref, u_ref, vt_ref,
                                 pre_ref, p_ref, acc_ref, te=te, n_chunks=n_chunks, stages=stages)

    @pl.when(g == last)
    def _():
        y = x_ref[...] + acc_ref[...].T
        if final_norm:
            y = _rms(y, gf_ref[...])
        out_ref[...] = y


def _peer_dense(x2, hn, rank2, e2, cnt, e1, u, vt, gf, tm, te, final_norm):
    t = x2.shape[0]
    n_chunks = N_EXPERTS // te
    row = pl.BlockSpec((tm, D_MODEL), lambda i, c: (i, 0))
    sc = pl.BlockSpec((None, PEER_HEADS, N_KEYS, tm), lambda i, c: (i, 0, 0, 0))
    return pl.pallas_call(
        functools.partial(_peer_dense_kernel, te=te, n_chunks=n_chunks, final_norm=final_norm),
        grid=(t // tm, n_chunks // 2 + 1),
        in_specs=[row, row, sc, sc, sc, sc,
                  pl.BlockSpec((2 * te, D_MODEL), lambda i, g: (jnp.minimum(g, n_chunks // 2 - 1), 0)),
                  pl.BlockSpec((2, D_MODEL, te), lambda i, g: (jnp.clip(g - 1, 0, n_chunks // 2 - 1), 0, 0)),
                  pl.BlockSpec((1, D_MODEL), lambda i, c: (0, 0))],
        out_specs=row,
        out_shape=jax.ShapeDtypeStruct((t, D_MODEL), F32),
        scratch_shapes=[pltpu.VMEM((2, te, tm), F32), pltpu.VMEM((2, te, tm), BF16),
                        pltpu.VMEM((D_MODEL, tm), F32)],
        compiler_params=_cparams("parallel", "arbitrary"),
        name="peer_dense",
    )(x2, hn, rank2, e2, cnt, e1, u, vt, gf)


def _pad_cols(w, width):
    return jnp.pad(w, ((0, 0), (0, width - w.shape[1])))


def _rope_partner(w):
    half = MLA_ROPE // 2
    return jnp.concatenate([-w[:, half:], w[:, :half]], axis=1)


def _rope_lanes(w):
    return jnp.pad(w, ((0, 0), (MLA_NOPE, LANE - MLA_NOPE - MLA_ROPE)))


def _layer_weights(w_in, w_uq, w_ukv, w_o_sb, w_o_fox):
    o = IN_OFFSETS
    seg = lambda n: w_in[:, o[n]:o[n + 1]]
    scale = HEAD_DIM ** -0.5 * LOG2E
    w_kr = seg(2)
    w_cat = jnp.concatenate(
        [seg(0), seg(1), _rope_lanes(w_kr), _rope_lanes(_rope_partner(w_kr)),
         _pad_cols(seg(3) * scale, HEAD_PAD_W), _pad_cols(seg(4), HEAD_PAD_W), _pad_cols(seg(5), HEAD_PAD_W),
         _pad_cols(seg(6) * scale, HEAD_PAD_W), _pad_cols(seg(7), HEAD_PAD_W), _pad_cols(seg(8), HEAD_PAD_W),
         _pad_cols(seg(9), LANE)], axis=1).astype(BF16)
    w_gate = seg(10).astype(BF16)
    uq = w_uq.reshape(Q_LORA, MLA_HEADS, MLA_QK)
    nope, rope = uq[..., :MLA_NOPE], uq[..., MLA_NOPE:]
    zpad = jnp.zeros((Q_LORA, MLA_HEADS, LANE - MLA_QK), F32)
    wqa = jnp.concatenate([nope, rope, zpad], axis=-1).reshape(Q_LORA, MLA_HEADS * LANE).astype(BF16)
    partner = jnp.concatenate([-rope[..., MLA_ROPE // 2:], rope[..., :MLA_ROPE // 2]], axis=-1)
    wqb = jnp.concatenate([jnp.zeros_like(nope), partner, zpad], axis=-1).reshape(Q_LORA, MLA_HEADS * LANE).astype(BF16)
    ukv = w_ukv.reshape(KV_LORA, MLA_HEADS, MLA_NOPE + MLA_V)
    wk = jnp.pad(ukv[..., :MLA_NOPE], ((0, 0), (0, 0), (0, LANE - MLA_NOPE))).reshape(KV_LORA, MLA_HEADS * LANE).astype(BF16)
    wv = ukv[..., MLA_NOPE:].reshape(KV_LORA, MLA_OUT).astype(BF16)
    pad_rows = lambda w: jnp.pad(w, ((0, HEAD_PAD_W - w.shape[0]), (0, 0))).astype(BF16)
    return w_cat, w_gate, wqa, wqb, wk, wv, pad_rows(w_o_sb), pad_rows(w_o_fox)


def _tiles(b, s):
    t = b * s
    tm = min(TOKEN_TILE, t)
    tq = min(ATTN_TILE, s)
    return t, tm, tq


def kernel(x, positions, norm1_g, w_in, mla_q_norm_g, w_uq, mla_kv_norm_g, w_ukv, fox_b_f, w_o_mla, w_o_sb,
           w_o_fox, b_gate, w_out, norm2_g, peer_w_q, peer_sub_keys, peer_u, peer_v, final_norm_g):
    b, s, d = x.shape
    assert d == D_MODEL and MLA_OUT == HEAD_PAD_W
    depth = w_in.shape[0]
    t, tm, tq = _tiles(b, s)
    assert t % tm == 0 and s % tq == 0 and tq % CHUNK == 0
    te = EXPERT_CHUNK
    x2 = x.reshape(t, d)
    cos, sin = _rope_tables(positions, tm)
    gf = final_norm_g.reshape(1, d)
    for l in range(depth):
        w_cat, w_gate, wqa, wqb, wk, wv, wo_sb, wo_fox = _layer_weights(w_in[l], w_uq[l], w_ukv[l], w_o_sb[l], w_o_fox[l])
        g1 = norm1_g[l].reshape(1, d)
        q_a, k_a, v_a, sb, fx, f = _inproj(x2, g1, w_cat, cos, sin, mla_q_norm_g[l].reshape(1, Q_LORA), wqa, wqb,
                                           mla_kv_norm_g[l].reshape(1, KV_LORA), wk, wv, tm)
        bf_row = jnp.zeros((1, LANE), F32).at[0, :FOX_HEADS].set(fox_b_f[l])
        cum, cumt = _fox_prep(f.reshape(b, s, LANE), bf_row, min(SUFFIX_TILE, s))
        hw = MLA_HEADS * LANE
        acc_a = _head_state(MLA_HEADS, tq)
        o_a = _attention(_mla_attn_kernel, q_a.reshape(b, s, hw), k_a.reshape(b, s, hw), v_a.reshape(b, s, MLA_OUT),
                         (0, 0, 0), [], [], [acc_a, acc_a, acc_a], hw, MLA_OUT, tq, "mla_attn")
        sb3 = sb.reshape(b, s, QKV_W)
        fx3 = fx.reshape(b, s, QKV_W)
        acc_b = _head_state(SB_HEADS, tq)
        qm = pltpu.VMEM((SB_HEADS, tq, LANE), BF16)
        o_b = _attention(_sb_attn_kernel, sb3, sb3, sb3, (0, 1, 2), [], [],
                         [qm, pltpu.VMEM((min(SUFFIX_TILE, tq),) * 2, BF16), acc_b, acc_b], HEAD_PAD_W, HEAD_PAD_W, tq,
                         "sb_attn")
        fox_specs = [pl.BlockSpec((1, tq, LANE), lambda bi, i: (bi, i, 0)),
                     pl.BlockSpec((1, 8, s), lambda bi, i: (bi, 0, 0))]
        o_c = _attention(_fox_attn_kernel, fx3, fx3, fx3, (0, 1, 2), [cum, cumt], fox_specs,
                         [qm, acc_b, acc_b, acc_b, acc_b], HEAD_PAD_W, HEAD_PAD_W, tq, "fox_attn")
        x2 = _merge(x2, g1, w_gate, b_gate[l].reshape(1, -1), o_a.reshape(t, MLA_OUT), o_b.reshape(t, HEAD_PAD_W),
                    o_c.reshape(t, HEAD_PAD_W), w_o_mla[l].astype(BF16), wo_sb, wo_fox, w_out[l].astype(BF16), tm)
        hn, rank2, e2, cnt, e1 = _peer_route(x2, norm2_g[l].reshape(1, d), peer_w_q[l].astype(BF16),
                                             peer_sub_keys[l].astype(BF16), tm)
        vt = peer_v[l].astype(BF16).reshape(N_EXPERTS // te, te, d).transpose(0, 2, 1)
        x2 = _peer_dense(x2, hn, rank2, e2, cnt, e1, peer_u[l].astype(BF16), vt, gf, tm, te,
                         final_norm=(l == depth - 1))
    return x2.reshape(b, s, d)
```

```python
import functools

import numpy as np
import jax
import jax.numpy as jnp
from jax import lax
from jax.experimental import pallas as pl
from jax.experimental.pallas import tpu as pltpu

F32 = jnp.float32
BF16 = jnp.bfloat16

D_MODEL = 1024
CHUNK = 64
HEAD_DIM = 64
NORM_EPS = 1e-6
NEG_INF = -1e30
MLA_HEADS = 6
MLA_NOPE = 64
MLA_ROPE = 32
MLA_V = 64
Q_LORA = 256
KV_LORA = 128
ROPE_THETA = 10000.0
SB_HEADS = 5
FOX_HEADS = 5
N_BRANCH = 3
PEER_HEADS = 8
N_KEYS = 128
N_EXPERTS = N_KEYS * N_KEYS
D_KEY = 256
HALF_KEY = D_KEY // 2
PEER_TOPK = 16
MLA_QK = MLA_NOPE + MLA_ROPE
SB_W = SB_HEADS * HEAD_DIM
FOX_W = FOX_HEADS * HEAD_DIM
MLA_OUT = MLA_HEADS * MLA_V
IN_SPLITS = (Q_LORA, KV_LORA, MLA_ROPE, SB_W, SB_W, SB_W, FOX_W, FOX_W, FOX_W, FOX_HEADS, N_BRANCH * D_MODEL)
IN_OFFSETS = tuple(int(o) for o in np.cumsum((0,) + IN_SPLITS))

LANE = 128
HEAD_PAD_W = 384
MLA_IN_W = Q_LORA + KV_LORA + 2 * LANE
QKV_W = 3 * HEAD_PAD_W
IN_W = MLA_IN_W + 2 * QKV_W + LANE
VMEM_LIMIT = 48 * 1024 * 1024
TOKEN_TILE = 512
ATTN_TILE = 512
SUFFIX_TILE = 256
EXPERT_CHUNK = 4 * N_KEYS

LOG2E = 1.4426950408889634
_NT = (((1,), (1,)), ((), ()))


def _cparams(*sem):
    return pltpu.CompilerParams(dimension_semantics=sem, vmem_limit_bytes=VMEM_LIMIT)


def _rms(x, g):
    return x * lax.rsqrt(jnp.mean(x * x, axis=-1, keepdims=True) + NORM_EPS) * g


def _dot(a, b):
    return jnp.dot(a, b, preferred_element_type=F32)


def _dot_nt(a, b):
    return lax.dot_general(a, b, _NT, preferred_element_type=F32)


def _full(shape):
    return pl.BlockSpec(shape, lambda *_: (0,) * len(shape))


def _rope_kernel(pos_ref, inv_ref, cos_ref, sin_ref):
    ang = pos_ref[...].astype(F32) * inv_ref[...]
    cos_ref[...] = jnp.cos(ang)
    sin_ref[...] = jnp.sin(ang)


def _rope_tables(positions, tm):
    t = positions.size
    inv = ROPE_THETA ** (-jnp.arange(0, MLA_ROPE, 2, dtype=F32) / MLA_ROPE)
    inv_row = jnp.zeros((1, LANE), F32).at[0, MLA_NOPE:MLA_NOPE + MLA_ROPE].set(jnp.concatenate([inv, inv]))
    return pl.pallas_call(
        _rope_kernel,
        grid=(t // tm,),
        in_specs=[pl.BlockSpec((tm, 1), lambda i: (i, 0)), _full((1, LANE))],
        out_specs=[pl.BlockSpec((tm, LANE), lambda i: (i, 0))] * 2,
        out_shape=[jax.ShapeDtypeStruct((t, LANE), F32)] * 2,
        compiler_params=_cparams("parallel"),
        name="rope_tables",
    )(positions.reshape(t, 1), inv_row)


def _inproj_kernel(x_ref, g_ref, w_ref, cos_ref, sin_ref, gq_ref, wqa_ref, wqb_ref, gkv_ref, wk_ref, wv_ref,
                   q_ref, k_ref, v_ref, sb_ref, fx_ref, f_ref, mla_ref):
    hn = _rms(x_ref[...], g_ref[...]).astype(BF16)
    o = 0
    for ref, width in ((mla_ref, MLA_IN_W), (sb_ref, QKV_W), (fx_ref, QKV_W), (f_ref, LANE)):
        ref[...] = _dot(hn, w_ref[:, o:o + width]).astype(ref.dtype)
        o += width
    _mla_prep(mla_ref, cos_ref, sin_ref, gq_ref, wqa_ref, wqb_ref, gkv_ref, wk_ref, wv_ref, q_ref, k_ref, v_ref)


def _inproj(x2, g, w_cat, cos, sin, gq, wqa, wqb, gkv, wk, wv, tm):
    t = x2.shape[0]
    row = lambda w: pl.BlockSpec((tm, w), lambda i: (i, 0))
    hw = MLA_HEADS * LANE
    return pl.pallas_call(
        _inproj_kernel,
        grid=(t // tm,),
        in_specs=[row(D_MODEL), _full((1, D_MODEL)), _full((D_MODEL, IN_W)), row(LANE), row(LANE),
                  _full((1, Q_LORA)), _full((Q_LORA, hw)), _full((Q_LORA, hw)), _full((1, KV_LORA)),
                  _full((KV_LORA, hw)), _full((KV_LORA, MLA_OUT))],
        out_specs=[row(hw), row(hw), row(MLA_OUT), row(QKV_W), row(QKV_W), row(LANE)],
        out_shape=[jax.ShapeDtypeStruct((t, hw), BF16), jax.ShapeDtypeStruct((t, hw), BF16),
                   jax.ShapeDtypeStruct((t, MLA_OUT), BF16), jax.ShapeDtypeStruct((t, QKV_W), BF16),
                   jax.ShapeDtypeStruct((t, QKV_W), BF16), jax.ShapeDtypeStruct((t, LANE), F32)],
        scratch_shapes=[pltpu.VMEM((tm, MLA_IN_W), F32)],
        compiler_params=_cparams("parallel"),
        name="inproj",
    )(x2, g, w_cat, cos, sin, gq, wqa, wqb, gkv, wk, wv)


def _mla_prep(in_ref, cos_ref, sin_ref, gq_ref, wqa_ref, wqb_ref, gkv_ref, wk_ref, wv_ref, q_ref, k_ref, v_ref):
    cos = cos_ref[...]
    sin = sin_ref[...]
    qn = _rms(in_ref[:, :Q_LORA], gq_ref[...]).astype(BF16)
    kn = _rms(in_ref[:, Q_LORA:Q_LORA + KV_LORA], gkv_ref[...]).astype(BF16)
    o = Q_LORA + KV_LORA
    k_rot = in_ref[:, o:o + LANE] * cos + in_ref[:, o + LANE:o + 2 * LANE] * sin
    scale = MLA_QK ** -0.5 * LOG2E
    qa = _dot(qn, wqa_ref[...])
    qb = _dot(qn, wqb_ref[...])
    ka = _dot(kn, wk_ref[...])
    for h in range(MLA_HEADS):
        sl = slice(h * LANE, (h + 1) * LANE)
        q_ref[:, sl] = ((qa[:, sl] * cos + qb[:, sl] * sin) * scale).astype(BF16)
        k_ref[:, sl] = (ka[:, sl] + k_rot).astype(BF16)
    v_ref[...] = _dot(kn, wv_ref[...]).astype(BF16)


def _fox_prep_kernel(f_ref, bf_ref, cum_ref, cumt_ref, *, blk):
    s = f_ref.shape[1]
    row = lax.broadcasted_iota(jnp.int32, (blk, blk), 0)
    col = lax.broadcasted_iota(jnp.int32, (blk, blk), 1)
    tri = jnp.where(col <= row, 1.0, 0.0).astype(BF16)
    carry = jnp.zeros((1, LANE), F32)
    for b in range(s // blk):
        f = f_ref[0, b * blk:(b + 1) * blk, :] + bf_ref[...]
        lf = (jnp.minimum(f, 0.0) - jnp.log1p(jnp.exp(-jnp.abs(f)))) * LOG2E
        hi = lf.astype(BF16)
        r1 = lf - hi.astype(F32)
        mid = r1.astype(BF16)
        lo = (r1 - mid.astype(F32)).astype(BF16)
        c = _dot(tri, hi) + _dot(tri, mid) + _dot(tri, lo) + carry
        cum_ref[0, b * blk:(b + 1) * blk, :] = c
        cumt_ref[0, :, b * blk:(b + 1) * blk] = c.T[:8, :]
        carry = c[blk - 1:blk, :]


def _fox_prep(f3, bf_row, blk):
    b, s, _ = f3.shape
    return pl.pallas_call(
        functools.partial(_fox_prep_kernel, blk=blk),
        grid=(b,),
        in_specs=[pl.BlockSpec((1, s, LANE), lambda i: (i, 0, 0)), _full((1, LANE))],
        out_specs=[pl.BlockSpec((1, s, LANE), lambda i: (i, 0, 0)), pl.BlockSpec((1, 8, s), lambda i: (i, 0, 0))],
        out_shape=[jax.ShapeDtypeStruct((b, s, LANE), F32), jax.ShapeDtypeStruct((b, 8, s), F32)],
        compiler_params=_cparams("parallel"),
        name="fox_prep",
    )(f3, bf_row)


def _softmax_update(h, s, v, m_ref, l_ref, acc_ref):
    m_old = m_ref[h]
    m_new = jnp.maximum(m_old, jnp.max(s, axis=-1, keepdims=True))
    alpha = jnp.exp2(m_old - m_new)
    p = [jnp.exp2(s[:, n * LANE:(n + 1) * LANE] - m_new) for n in range(s.shape[1] // LANE)]
    part = p[0]
    for p_n in p[1:]:
        part = part + p_n
    l_ref[h] = alpha * l_ref[h] + part
    m_ref[h] = m_new
    acc_ref[h] = alpha * acc_ref[h] + _dot(jnp.concatenate(p, axis=1).astype(BF16), v)


def _softmax_reset(m_ref, l_ref, acc_ref):
    m_ref[...] = jnp.full(m_ref.shape, NEG_INF, F32)
    l_ref[...] = jnp.zeros(l_ref.shape, F32)
    acc_ref[...] = jnp.zeros(acc_ref.shape, F32)


def _half_mask(tq, half):
    lane = lax.broadcasted_iota(jnp.int32, (tq, LANE), 1)
    return (lane < HEAD_DIM) if half == 0 else (lane >= HEAD_DIM)


def _store_head_pairs(o_ref, heads, value_of):
    tq = o_ref.shape[1]
    lo_half = _half_mask(tq, 0)
    for hb in range(o_ref.shape[2] // LANE):
        lo = value_of(2 * hb)
        hi = value_of(2 * hb + 1) if 2 * hb + 1 < heads else jnp.zeros_like(lo)
        o_ref[0, :, hb * LANE:(hb + 1) * LANE] = jnp.where(lo_half, lo, hi).astype(o_ref.dtype)


def _masked_queries(q_ref, qm_ref, heads):
    tq = q_ref.shape[1]
    for h in range(heads):
        qf = q_ref[0, :, (h // 2) * LANE:(h // 2 + 1) * LANE].astype(F32)
        qm_ref[h] = jnp.where(_half_mask(tq, h % 2), qf, 0.0).astype(BF16)


def _block_iotas(tq):
    return lax.broadcasted_iota(jnp.int32, (tq, tq), 0), lax.broadcasted_iota(jnp.int32, (tq, tq), 1)


def _mla_attn_kernel(q_ref, k_ref, v_ref, o_ref, m_ref, l_ref, acc_ref, *, tq):
    i = pl.program_id(1)
    _softmax_reset(m_ref, l_ref, acc_ref)

    def block(j, diagonal):
        rows = pl.ds(pl.multiple_of(j * tq, tq), tq)
        def scores(h):
            sl = slice(h * LANE, (h + 1) * LANE)
            return _dot_nt(q_ref[0, :, sl], k_ref[0, rows, sl])

        s_next = scores(0)
        for h in range(MLA_HEADS):
            s = s_next
            if h + 1 < MLA_HEADS:
                s_next = scores(h + 1)
            if diagonal:
                r, c = _block_iotas(tq)
                s = jnp.where((c // CHUNK) <= (r // CHUNK), s, NEG_INF)
            _softmax_update(h, s, v_ref[0, rows, (h // 2) * LANE:(h // 2 + 1) * LANE], m_ref, l_ref, acc_ref)

    def body(j, carry):
        block(j, False)
        return carry

    lax.fori_loop(0, i, body, 0)
    block(i, True)
    _store_head_pairs(o_ref, MLA_HEADS, lambda h: acc_ref[h] / jnp.sum(l_ref[h], axis=-1, keepdims=True))


def _fox_attn_kernel(q_ref, k_ref, v_ref, cum_ref, cumt_ref, o_ref, qm_ref, cq_ref, m_ref, l_ref, acc_ref, *, tq):
    i = pl.program_id(1)
    _softmax_reset(m_ref, l_ref, acc_ref)
    _masked_queries(q_ref, qm_ref, FOX_HEADS)
    for h in range(FOX_HEADS):
        cq_ref[h] = jnp.broadcast_to(cum_ref[0, :, h:h + 1], (tq, LANE))

    def block(j, diagonal):
        start = pl.multiple_of(j * tq, tq)
        rows = pl.ds(start, tq)
        def scores(h):
            return _dot_nt(qm_ref[h], k_ref[0, rows, (h // 2) * LANE:(h // 2 + 1) * LANE])

        s_next = scores(0)
        for h in range(FOX_HEADS):
            sl = slice((h // 2) * LANE, (h // 2 + 1) * LANE)
            s = s_next
            if h + 1 < FOX_HEADS:
                s_next = scores(h + 1)
            ck = cumt_ref[0, h:h + 1, rows]
            s = jnp.concatenate([s[:, n * LANE:(n + 1) * LANE] + (cq_ref[h] - ck[:, n * LANE:(n + 1) * LANE])
                                 for n in range(tq // LANE)], axis=1)
            if diagonal:
                r, c = _block_iotas(tq)
                s = jnp.where(c <= r, s, NEG_INF)
            _softmax_update(h, s, v_ref[0, rows, sl], m_ref, l_ref, acc_ref)

    def body(j, carry):
        block(j, False)
        return carry

    lax.fori_loop(0, i, body, 0)
    block(i, True)
    _store_head_pairs(o_ref, FOX_HEADS, lambda h: acc_ref[h] / jnp.sum(l_ref[h], axis=-1, keepdims=True))


def _sb_attn_kernel(q_ref, k_ref, v_ref, o_ref, qm_ref, suffix_ref, rest_ref, acc_ref, *, tq):
    i = pl.program_id(1)
    sub = suffix_ref.shape[0]
    r, c = _block_iotas(sub)
    suffix_ref[...] = jnp.where(r > c, 1.0, 0.0).astype(BF16)
    for h in range(SB_HEADS):
        rest_ref[h] = jnp.zeros(rest_ref.shape[1:], F32)
        acc_ref[h] = jnp.zeros(acc_ref.shape[1:], F32)
    _masked_queries(q_ref, qm_ref, SB_HEADS)

    def block(j, diagonal):
        rows = pl.ds(pl.multiple_of(j * tq, tq), tq)
        def scores(h):
            return _dot_nt(qm_ref[h], k_ref[0, rows, (h // 2) * LANE:(h // 2 + 1) * LANE])

        z_next = scores(0)
        for h in range(SB_HEADS):
            sl = slice((h // 2) * LANE, (h // 2 + 1) * LANE)
            z = z_next
            if h + 1 < SB_HEADS:
                z_next = scores(h + 1)
            log_b = jnp.minimum(z, 0.0) - jnp.log(1.0 + jnp.exp2(-jnp.abs(z))) * LOG2E
            log_1m = log_b - z
            if diagonal:
                strict = _block_iotas(tq)[1] < _block_iotas(tq)[0]
                log_1m = jnp.where(strict, log_1m, 0.0)
            later = rest_ref[h]
            a = [None] * (tq // sub)
            for k in reversed(range(tq // sub)):
                cols = slice(k * sub, (k + 1) * sub)
                within = _dot(log_1m[:, cols].astype(BF16), suffix_ref[...])
                a[k] = jnp.concatenate([jnp.exp2(log_b[:, cols][:, n * LANE:(n + 1) * LANE]
                                                 + within[:, n * LANE:(n + 1) * LANE] + later)
                                        for n in range(sub // LANE)], axis=1)
                later = later + jnp.sum(log_1m[:, cols], axis=-1, keepdims=True)
            a = jnp.concatenate(a, axis=1)
            if diagonal:
                a = jnp.where(strict, a, 0.0)
            acc_ref[h] += _dot(a.astype(BF16), v_ref[0, rows, sl])
            rest_ref[h] = later

    block(i, True)

    def body(n, carry):
        block(i - 1 - n, False)
        return carry

    lax.fori_loop(0, i, body, 0)
    _store_head_pairs(o_ref, SB_HEADS, lambda h: acc_ref[h])


def _attention(kernel, q, k, v, cols, extra, extra_specs, scratch, qw, vw, tq, name):
    b, s, _ = q.shape
    blk = lambda w, n=0: pl.BlockSpec((1, tq, w), lambda bi, i: (bi, i, n))
    seq = lambda w, n: pl.BlockSpec((1, s, w), lambda bi, i: (bi, 0, n))
    return pl.pallas_call(
        functools.partial(kernel, tq=tq),
        grid=(b, s // tq),
        in_specs=[blk(qw, cols[0]), seq(qw, cols[1]), seq(vw, cols[2])] + extra_specs,
        out_specs=blk(vw),
        out_shape=jax.ShapeDtypeStruct((b, s, vw), BF16),
        scratch_shapes=scratch,
        compiler_params=_cparams("parallel", "parallel"),
        name=name,
    )(q, k, v, *extra)


def _head_state(heads, tq):
    return pltpu.VMEM((heads, tq, LANE), F32)


def _merge_kernel(x_ref, g_ref, wg_ref, bg_ref, oa_ref, ob_ref, oc_ref, wa_ref, wb_ref, wc_ref, wo_ref, out_ref):
    x = x_ref[...]
    hn = _rms(x, g_ref[...]).astype(BF16)
    y = None
    for n, (o_ref, w_ref) in enumerate(((oa_ref, wa_ref), (ob_ref, wb_ref), (oc_ref, wc_ref))):
        sl = slice(n * D_MODEL, (n + 1) * D_MODEL)
        gate = jax.nn.sigmoid(_dot(hn, wg_ref[:, sl]) + bg_ref[:, sl])
        term = gate * _dot(o_ref[...], w_ref[...])
        y = term if y is None else y + term
    out_ref[...] = x + _dot(y.astype(BF16), wo_ref[...])


def _merge(x2, g, wg, bg, oa, ob, oc, wa, wb, wc, wo, tm):
    t = x2.shape[0]
    row = lambda w: pl.BlockSpec((tm, w), lambda i: (i, 0))
    return pl.pallas_call(
        _merge_kernel,
        grid=(t // tm,),
        in_specs=[row(D_MODEL), _full((1, D_MODEL)), _full((D_MODEL, N_BRANCH * D_MODEL)),
                  _full((1, N_BRANCH * D_MODEL)), row(HEAD_PAD_W), row(HEAD_PAD_W), row(HEAD_PAD_W),
                  _full((HEAD_PAD_W, D_MODEL)), _full((HEAD_PAD_W, D_MODEL)), _full((HEAD_PAD_W, D_MODEL)),
                  _full((D_MODEL, D_MODEL))],
        out_specs=row(D_MODEL),
        out_shape=jax.ShapeDtypeStruct((t, D_MODEL), F32),
        compiler_params=_cparams("parallel"),
        name="merge",
    )(x2, g, wg, bg, oa, ob, oc, wa, wb, wc, wo)


def _sorted_top16(groups):
    x = list(groups)
    assert len(x) in (PEER_TOPK // 2, PEER_TOPK)

    def exchange(i, l, descending):
        hi, lo = jnp.maximum(x[i], x[l]), jnp.minimum(x[i], x[l])
        x[i], x[l] = (hi, lo) if descending else (lo, hi)

    k = 2
    while k <= len(x):
        j = k // 2
        while j >= 1:
            for i in range(len(x)):
                if i ^ j > i:
                    exchange(i, i ^ j, (i & k) == 0)
            j //= 2
        k *= 2
    for shift in (4, 2, 1):
        other = [pltpu.roll(v, shift, axis=0) for v in x]
        if len(x) < PEER_TOPK:
            x = x + other[::-1]
        else:
            x = [jnp.maximum(x[r], other[PEER_TOPK - 1 - r]) for r in range(PEER_TOPK)]
        j = PEER_TOPK // 2
        while j >= 1:
            for i in range(PEER_TOPK):
                if i ^ j > i:
                    exchange(i, i ^ j, True)
            j //= 2
    return x


def _peer_route_kernel(x_ref, g_ref, wq_ref, keys_ref, h_ref, rank2_ref, e2_ref, cnt_ref, e1_ref,
                       s1_ref, s2_ref, q_ref, top_ref, cand_ref):
    hn = _rms(x_ref[...], g_ref[...]).astype(BF16)
    h_ref[...] = hn
    wide = 2 * D_KEY
    for n in range(PEER_HEADS * D_KEY // wide):
        q_ref[:, n * wide:(n + 1) * wide] = _dot(hn, wq_ref[:, n * wide:(n + 1) * wide]).astype(BF16)
    for h in range(PEER_HEADS):
        for side, s_ref in ((0, s1_ref), (1, s2_ref)):
            o = h * D_KEY + side * HALF_KEY
            s_ref[h] = _dot_nt(keys_ref[side], q_ref[:, o:o + HALF_KEY])

    def lane_tile(lt, carry):
        lanes = pl.ds(pl.multiple_of(lt * LANE, LANE), LANE)
        for h in range(PEER_HEADS):
            scores, tops = [], []
            for side, s_ref in ((0, s1_ref), (1, s2_ref)):
                groups = [s_ref[h, 8 * n:8 * n + 8, lanes] for n in range(N_KEYS // 8)]
                top = _sorted_top16(groups)
                for r in range(PEER_TOPK):
                    top_ref[side, r:r + 1, :] = top[r][0:1, :]
                scores.append(groups)
                tops.append(top)
            (s1, s2), (top1, top2) = scores, tops
            v2_top8 = top_ref[1, 0:8, :]
            sub = lax.broadcasted_iota(jnp.int32, (8, LANE), 0)
            v2_top4_twice = jnp.where(sub < 4, v2_top8, pltpu.roll(v2_top8, 4, axis=0))
            cand_ref[0:16, :] = top_ref[0, 0:1, :] + top_ref[1]
            for a in (1, 2, 3):
                cand_ref[8 + 8 * a:16 + 8 * a, :] = top_ref[0, a:a + 1, :] + v2_top8
            for n, a in enumerate((4, 6)):
                v1_pair = jnp.where(sub < 4, top_ref[0, a:a + 1, :], top_ref[0, a + 1:a + 2, :])
                cand_ref[40 + 8 * n:48 + 8 * n, :] = v1_pair + v2_top4_twice
            cand_ref[56:64, :] = top_ref[0, 8:16, :] + top_ref[1, 0:1, :]
            best = _sorted_top16([cand_ref[8 * n:8 * n + 8, :] for n in range(cand_ref.shape[0] // 8)])
            tau = best[PEER_TOPK - 1]
            m1, m2 = top1[0], top2[0]
            zsum = best[0] * 0.0
            for v in best:
                zsum = zsum + jnp.exp(v - (m1 + m2))
            half_inv_z = 0.5 / zsum
            ranks, cnts = [], []
            for g1, g2 in zip(s1, s2):
                thr = tau - g1
                rank, cnt = jnp.zeros((8, LANE), F32), jnp.zeros((8, LANE), F32)
                for b in range(PEER_TOPK):
                    rank = jnp.where(top2[b] > g2, float(b + 1), rank)
                    cnt = jnp.where(top2[b] >= thr, float(b + 1), cnt)
                ranks.append(rank)
                cnts.append(cnt)
            rank2_ref[h, :, lanes] = jnp.concatenate(ranks, axis=0).astype(BF16)
            e2_ref[h, :, lanes] = jnp.concatenate([jnp.exp(g2 - m2) for g2 in s2], axis=0).astype(BF16)
            cnt_ref[h, :, lanes] = jnp.concatenate(cnts, axis=0)
            e1_ref[h, :, lanes] = jnp.concatenate([jnp.exp(g1 - m1) * half_inv_z for g1 in s1], axis=0)
        return carry

    lax.fori_loop(0, x_ref.shape[0] // LANE, lane_tile, 0)


def _peer_route(x2, g, wq, keys, tm):
    t = x2.shape[0]
    row = pl.BlockSpec((tm, D_MODEL), lambda i: (i, 0))
    sc = pl.BlockSpec((None, PEER_HEADS, N_KEYS, tm), lambda i: (i, 0, 0, 0))
    return pl.pallas_call(
        _peer_route_kernel,
        grid=(t // tm,),
        in_specs=[row, _full((1, D_MODEL)), _full((D_MODEL, PEER_HEADS * D_KEY)), _full((2, N_KEYS, HALF_KEY))],
        out_specs=[row, sc, sc, sc, sc],
        out_shape=[jax.ShapeDtypeStruct((t, D_MODEL), BF16),
                   jax.ShapeDtypeStruct((t // tm, PEER_HEADS, N_KEYS, tm), BF16),
                   jax.ShapeDtypeStruct((t // tm, PEER_HEADS, N_KEYS, tm), BF16),
                   jax.ShapeDtypeStruct((t // tm, PEER_HEADS, N_KEYS, tm), F32),
                   jax.ShapeDtypeStruct((t // tm, PEER_HEADS, N_KEYS, tm), F32)],
        scratch_shapes=[pltpu.VMEM((PEER_HEADS, N_KEYS, tm), F32), pltpu.VMEM((PEER_HEADS, N_KEYS, tm), F32),
                        pltpu.VMEM((tm, PEER_HEADS * D_KEY), BF16), pltpu.VMEM((2, PEER_TOPK, LANE), F32),
                        pltpu.VMEM((4 * PEER_TOPK, LANE), F32)],
        compiler_params=_cparams("parallel"),
        name="peer_route",
    )(x2, g, wq, keys)


def _peer_tables_kernel(u_ref, v_ref, ub_ref, vt_ref):
    ub_ref[...] = u_ref[...].astype(BF16)
    vt_ref[0] = v_ref[...].T.astype(BF16)


def _peer_tables(u, v, te):
    n_chunks = N_EXPERTS // te
    blk = pl.BlockSpec((te, D_MODEL), lambda c: (c, 0))
    return pl.pallas_call(
        _peer_tables_kernel,
        grid=(n_chunks,),
        in_specs=[blk, blk],
        out_specs=[blk, pl.BlockSpec((1, D_MODEL, te), lambda c: (c, 0, 0))],
        out_shape=[jax.ShapeDtypeStruct((N_EXPERTS, D_MODEL), BF16),
                   jax.ShapeDtypeStruct((n_chunks, D_MODEL, te), BF16)],
        compiler_params=_cparams("parallel"),
        name="peer_tables",
    )(u, v)


def _peer_dense_step(c, parity, h_ref, rank2_ref, e2_ref, cnt_ref, e1_ref, u_ref, vt_ref, pre_ref, p_ref, acc_ref,
                     *, te, n_chunks, stages):
    tm = h_ref.shape[0]
    rows_per_step = te // N_KEYS
    assert 2 * rows_per_step == 8
    part = 64
    gate_w = 2 * LANE
    score_buf, gate_buf, fold_buf = parity, 1 - parity, parity
    key_tile = pl.ds(pl.multiple_of((jnp.clip(c - 1, 0, n_chunks - 1) // 2) * 8, 8), 8)
    row0 = rows_per_step * (1 - parity)

    def score(piece):
        rows = slice((piece // 2) * te // 2, (piece // 2 + 1) * te // 2)
        toks = slice((piece % 2) * gate_w, (piece % 2 + 1) * gate_w)
        u_rows = slice(parity * te + rows.start, parity * te + rows.stop)
        pre_ref[score_buf, rows, toks] = _dot_nt(u_ref[u_rows, :], h_ref[toks, :])

    def fold(piece):
        rows = slice((piece // 2) * D_MODEL // 4, (piece // 2 + 1) * D_MODEL // 4)
        toks = slice((piece % 2) * gate_w, (piece % 2 + 1) * gate_w)
        acc_ref[rows, toks] += _dot(vt_ref[parity, rows, :], p_ref[fold_buf, :, toks])

    def gate_heads(lt, rp, w, heads):
        lanes = slice(lt * gate_w, (lt + 1) * gate_w)
        for h in heads:
            rank2 = rank2_ref[h, rp * part:(rp + 1) * part, lanes]
            e2 = e2_ref[h, rp * part:(rp + 1) * part, lanes]
            cnt = cnt_ref[h, key_tile, lanes]
            e1 = e1_ref[h, key_tile, lanes]
            for j in range(rows_per_step):
                cnt_j = jnp.broadcast_to(cnt[row0 + j:row0 + j + 1, :], (part, gate_w)).astype(BF16)
                e1_j = jnp.broadcast_to(e1[row0 + j:row0 + j + 1, :], (part, gate_w)).astype(BF16)
                term = jnp.where(rank2 < cnt_j, e2, jnp.zeros_like(e2)) * e1_j
                w[j] = term if w[j] is None else w[j] + term

    def gate_store(lt, rp, w):
        lanes = slice(lt * gate_w, (lt + 1) * gate_w)
        for j in range(rows_per_step):
            rows = slice(j * N_KEYS + rp * part, j * N_KEYS + (rp + 1) * part)
            pre = pre_ref[gate_buf, rows, lanes]
            act = pre * (1.0 + lax.erf(pre * (0.5 ** 0.5)))
            p_ref[gate_buf, rows, lanes] = w[j] * act.astype(BF16)

    assert tm // gate_w == 2 and N_KEYS // part == 2
    mxu = [lambda n=n: score(n) for n in range(4)] + [lambda n=n: fold(n) for n in range(8)]
    order = [0, 4, 5, 1, 6, 7, 2, 8, 9, 3, 10, 11]
    live = [n for n in order if ("score" if n < 4 else "fold") in stages]
    if "gate" not in stages:
        for n in live:
            mxu[n]()
        return
    slots = 4 * 3
    pieces = iter([mxu[n] for n in live] + [None] * (slots - len(live)))
    for lt in range(2):
        for rp in range(2):
            w = [None] * rows_per_step
            for h in range(PEER_HEADS):
                gate_heads(lt, rp, w, (h,))
                if h % 3 == 1:
                    piece = next(pieces)
                    if piece is not None:
                        piece()
            gate_store(lt, rp, w)


def _peer_dense_kernel(x_ref, h_ref, rank2_ref, e2_ref, cnt_ref, e1_ref, u_ref, vt_ref, gf_ref, out_ref,
                       pre_ref, p_ref, acc_ref, *, te, n_chunks, final_norm):
    g = pl.program_id(1)

    last = n_chunks // 2

    @pl.when(g == 0)
    def _():
        acc_ref[...] = jnp.zeros_like(acc_ref)

    full = ("score", "gate", "fold")
    plans = (((0, 0), (("score",), ("score", "gate"))),
             ((1, last - 1), (full, full)),
             ((last, last), (("gate", "fold"), ("fold",))))
    for (lo, hi), stage_sets in plans:
        for parity, stages in enumerate(stage_sets):
            guard = jnp.logical_and(g >= lo, g <= hi) if parity == 0 else jnp.logical_and(g > lo - 1, g < hi + 1)

            @pl.when(guard)
            def _(parity=parity, stages=stages):
                _peer_dense_step(2 * g + parity, parity, h_ref, rank2_ref, e2_ref, cnt_ref, e1_ref, u_ref, vt_ref,
                                 pre_ref, p_ref, acc_ref, te=te, n_chunks=n_chunks, stages=stages)

    @pl.when(g == last)
    def _():
        y = x_ref[...] + acc_ref[...].T
        if final_norm:
            y = _rms(y, gf_ref[...])
        out_ref[...] = y


def _peer_dense(x2, hn, rank2, e2, cnt, e1, u, vt, gf, tm, te, final_norm):
    t = x2.shape[0]
    n_chunks = N_EXPERTS // te
    row = pl.BlockSpec((tm, D_MODEL), lambda i, c: (i, 0))
    sc = pl.BlockSpec((None, PEER_HEADS, N_KEYS, tm), lambda i, c: (i, 0, 0, 0))
    return pl.pallas_call(
        functools.partial(_peer_dense_kernel, te=te, n_chunks=n_chunks, final_norm=final_norm),
        grid=(t // tm, n_chunks // 2 + 1),
        in_specs=[row, row, sc, sc, sc, sc,
                  pl.BlockSpec((2 * te, D_MODEL), lambda i, g: (jnp.minimum(g, n_chunks // 2 - 1), 0)),
                  pl.BlockSpec((2, D_MODEL, te), lambda i, g: (jnp.clip(g - 1, 0, n_chunks // 2 - 1), 0, 0)),
                  pl.BlockSpec((1, D_MODEL), lambda i, c: (0, 0))],
        out_specs=row,
        out_shape=jax.ShapeDtypeStruct((t, D_MODEL), F32),
        scratch_shapes=[pltpu.VMEM((2, te, tm), F32), pltpu.VMEM((2, te, tm), BF16),
                        pltpu.VMEM((D_MODEL, tm), F32)],
        compiler_params=_cparams("parallel", "arbitrary"),
        name="peer_dense",
    )(x2, hn, rank2, e2, cnt, e1, u, vt, gf)


def _pad_cols(w, width):
    return jnp.pad(w, ((0, 0), (0, width - w.shape[1])))


def _rope_partner(w):
    half = MLA_ROPE // 2
    return jnp.concatenate([-w[:, half:], w[:, :half]], axis=1)


def _rope_lanes(w):
    return jnp.pad(w, ((0, 0), (MLA_NOPE, LANE - MLA_NOPE - MLA_ROPE)))


def _layer_weights(w_in, w_uq, w_ukv, w_o_sb, w_o_fox):
    o = IN_OFFSETS
    seg = lambda n: w_in[:, o[n]:o[n + 1]]
    scale = HEAD_DIM ** -0.5 * LOG2E
    w_kr = seg(2)
    w_cat = jnp.concatenate(
        [seg(0), seg(1), _rope_lanes(w_kr), _rope_lanes(_rope_partner(w_kr)),
         _pad_cols(seg(3) * scale, HEAD_PAD_W), _pad_cols(seg(4), HEAD_PAD_W), _pad_cols(seg(5), HEAD_PAD_W),
         _pad_cols(seg(6) * scale, HEAD_PAD_W), _pad_cols(seg(7), HEAD_PAD_W), _pad_cols(seg(8), HEAD_PAD_W),
         _pad_cols(seg(9), LANE)], axis=1).astype(BF16)
    w_gate = seg(10).astype(BF16)
    uq = w_uq.reshape(Q_LORA, MLA_HEADS, MLA_QK)
    nope, rope = uq[..., :MLA_NOPE], uq[..., MLA_NOPE:]
    zpad = jnp.zeros((Q_LORA, MLA_HEADS, LANE - MLA_QK), F32)
    wqa = jnp.concatenate([nope, rope, zpad], axis=-1).reshape(Q_LORA, MLA_HEADS * LANE).astype(BF16)
    partner = jnp.concatenate([-rope[..., MLA_ROPE // 2:], rope[..., :MLA_ROPE // 2]], axis=-1)
    wqb = jnp.concatenate([jnp.zeros_like(nope), partner, zpad], axis=-1).reshape(Q_LORA, MLA_HEADS * LANE).astype(BF16)
    ukv = w_ukv.reshape(KV_LORA, MLA_HEADS, MLA_NOPE + MLA_V)
    wk = jnp.pad(ukv[..., :MLA_NOPE], ((0, 0), (0, 0), (0, LANE - MLA_NOPE))).reshape(KV_LORA, MLA_HEADS * LANE).astype(BF16)
    wv = ukv[..., MLA_NOPE:].reshape(KV_LORA, MLA_OUT).astype(BF16)
    pad_rows = lambda w: jnp.pad(w, ((0, HEAD_PAD_W - w.shape[0]), (0, 0))).astype(BF16)
    return w_cat, w_gate, wqa, wqb, wk, wv, pad_rows(w_o_sb), pad_rows(w_o_fox)


def _tiles(b, s):
    t = b * s
    tm = min(TOKEN_TILE, t)
    tq = min(ATTN_TILE, s)
    return t, tm, tq


def kernel(x, positions, norm1_g, w_in, mla_q_norm_g, w_uq, mla_kv_norm_g, w_ukv, fox_b_f, w_o_mla, w_o_sb,
           w_o_fox, b_gate, w_out, norm2_g, peer_w_q, peer_sub_keys, peer_u, peer_v, final_norm_g):
    b, s, d = x.shape
    assert d == D_MODEL and MLA_OUT == HEAD_PAD_W
    depth = w_in.shape[0]
    t, tm, tq = _tiles(b, s)
    assert t % tm == 0 and s % tq == 0 and tq % CHUNK == 0
    te = EXPERT_CHUNK
    x2 = x.reshape(t, d)
    cos, sin = _rope_tables(positions, tm)
    gf = final_norm_g.reshape(1, d)
    for l in range(depth):
        w_cat, w_gate, wqa, wqb, wk, wv, wo_sb, wo_fox = _layer_weights(w_in[l], w_uq[l], w_ukv[l], w_o_sb[l], w_o_fox[l])
        g1 = norm1_g[l].reshape(1, d)
        q_a, k_a, v_a, sb, fx, f = _inproj(x2, g1, w_cat, cos, sin, mla_q_norm_g[l].reshape(1, Q_LORA), wqa, wqb,
                                           mla_kv_norm_g[l].reshape(1, KV_LORA), wk, wv, tm)
        bf_row = jnp.zeros((1, LANE), F32).at[0, :FOX_HEADS].set(fox_b_f[l])
        cum, cumt = _fox_prep(f.reshape(b, s, LANE), bf_row, min(SUFFIX_TILE, s))
        hw = MLA_HEADS * LANE
        acc_a = _head_state(MLA_HEADS, tq)
        o_a = _attention(_mla_attn_kernel, q_a.reshape(b, s, hw), k_a.reshape(b, s, hw), v_a.reshape(b, s, MLA_OUT),
                         (0, 0, 0), [], [], [acc_a, acc_a, acc_a], hw, MLA_OUT, tq, "mla_attn")
        sb3 = sb.reshape(b, s, QKV_W)
        fx3 = fx.reshape(b, s, QKV_W)
        acc_b = _head_state(SB_HEADS, tq)
        qm = pltpu.VMEM((SB_HEADS, tq, LANE), BF16)
        o_b = _attention(_sb_attn_kernel, sb3, sb3, sb3, (0, 1, 2), [], [],
                         [qm, pltpu.VMEM((min(SUFFIX_TILE, tq),) * 2, BF16), acc_b, acc_b], HEAD_PAD_W, HEAD_PAD_W, tq,
                         "sb_attn")
        fox_specs = [pl.BlockSpec((1, tq, LANE), lambda bi, i: (bi, i, 0)),
                     pl.BlockSpec((1, 8, s), lambda bi, i: (bi, 0, 0))]
        o_c = _attention(_fox_attn_kernel, fx3, fx3, fx3, (0, 1, 2), [cum, cumt], fox_specs,
                         [qm, acc_b, acc_b, acc_b, acc_b], HEAD_PAD_W, HEAD_PAD_W, tq, "fox_attn")
        x2 = _merge(x2, g1, w_gate, b_gate[l].reshape(1, -1), o_a.reshape(t, MLA_OUT), o_b.reshape(t, HEAD_PAD_W),
                    o_c.reshape(t, HEAD_PAD_W), w_o_mla[l].astype(BF16), wo_sb, wo_fox, w_out[l].astype(BF16), tm)
        hn, rank2, e2, cnt, e1 = _peer_route(x2, norm2_g[l].reshape(1, d), peer_w_q[l].astype(BF16),
                                             peer_sub_keys[l].astype(BF16), tm)
        u_bf, vt = _peer_tables(peer_u[l], peer_v[l], te)
        x2 = _peer_dense(x2, hn, rank2, e2, cnt, e1, u_bf, vt, gf, tm, te, final_norm=(l == depth - 1))
    return x2.reshape(b, s, d)
```

```python
import functools

import numpy as np
import jax
import jax.numpy as jnp
from jax import lax
from jax.experimental import pallas as pl
from jax.experimental.pallas import tpu as pltpu

F32 = jnp.float32
BF16 = jnp.bfloat16

D_MODEL = 1024
CHUNK = 64
HEAD_DIM = 64
NORM_EPS = 1e-6
NEG_INF = -1e30
MLA_HEADS = 6
MLA_NOPE = 64
MLA_ROPE = 32
MLA_V = 64
Q_LORA = 256
KV_LORA = 128
ROPE_THETA = 10000.0
SB_HEADS = 5
FOX_HEADS = 5
N_BRANCH = 3
PEER_HEADS = 8
N_KEYS = 128
N_EXPERTS = N_KEYS * N_KEYS
D_KEY = 256
HALF_KEY = D_KEY // 2
PEER_TOPK = 16
MLA_QK = MLA_NOPE + MLA_ROPE
SB_W = SB_HEADS * HEAD_DIM
FOX_W = FOX_HEADS * HEAD_DIM
MLA_OUT = MLA_HEADS * MLA_V
IN_SPLITS = (Q_LORA, KV_LORA, MLA_ROPE, SB_W, SB_W, SB_W, FOX_W, FOX_W, FOX_W, FOX_HEADS, N_BRANCH * D_MODEL)
IN_OFFSETS = tuple(int(o) for o in np.cumsum((0,) + IN_SPLITS))

LANE = 128
HEAD_PAD_W = 384
MLA_IN_W = Q_LORA + KV_LORA + 2 * LANE
QKV_W = 3 * HEAD_PAD_W
IN_W = MLA_IN_W + 2 * QKV_W + LANE
VMEM_LIMIT = 48 * 1024 * 1024
TOKEN_TILE = 512
ATTN_TILE = 512
SUFFIX_TILE = 256
EXPERT_CHUNK = 4 * N_KEYS

LOG2E = 1.4426950408889634
_NT = (((1,), (1,)), ((), ()))


def _cparams(*sem):
    return pltpu.CompilerParams(dimension_semantics=sem, vmem_limit_bytes=VMEM_LIMIT)


def _rms(x, g):
    return x * lax.rsqrt(jnp.mean(x * x, axis=-1, keepdims=True) + NORM_EPS) * g


def _dot(a, b):
    return jnp.dot(a, b, preferred_element_type=F32)


def _dot_nt(a, b):
    return lax.dot_general(a, b, _NT, preferred_element_type=F32)


def _full(shape):
    return pl.BlockSpec(shape, lambda *_: (0,) * len(shape))


def _rope_kernel(pos_ref, inv_ref, cos_ref, sin_ref):
    ang = pos_ref[...].astype(F32) * inv_ref[...]
    cos_ref[...] = jnp.cos(ang)
    sin_ref[...] = jnp.sin(ang)


def _rope_tables(positions, tm):
    t = positions.size
    inv = ROPE_THETA ** (-jnp.arange(0, MLA_ROPE, 2, dtype=F32) / MLA_ROPE)
    inv_row = jnp.zeros((1, LANE), F32).at[0, MLA_NOPE:MLA_NOPE + MLA_ROPE].set(jnp.concatenate([inv, inv]))
    return pl.pallas_call(
        _rope_kernel,
        grid=(t // tm,),
        in_specs=[pl.BlockSpec((tm, 1), lambda i: (i, 0)), _full((1, LANE))],
        out_specs=[pl.BlockSpec((tm, LANE), lambda i: (i, 0))] * 2,
        out_shape=[jax.ShapeDtypeStruct((t, LANE), F32)] * 2,
        compiler_params=_cparams("parallel"),
        name="rope_tables",
    )(positions.reshape(t, 1), inv_row)


def _inproj_kernel(x_ref, g_ref, w_ref, cos_ref, sin_ref, gq_ref, wqa_ref, wqb_ref, gkv_ref, wk_ref, wv_ref,
                   q_ref, k_ref, v_ref, sb_ref, fx_ref, f_ref, mla_ref):
    hn = _rms(x_ref[...], g_ref[...]).astype(BF16)
    o = 0
    for ref, width in ((mla_ref, MLA_IN_W), (sb_ref, QKV_W), (fx_ref, QKV_W), (f_ref, LANE)):
        ref[...] = _dot(hn, w_ref[:, o:o + width]).astype(ref.dtype)
        o += width
    _mla_prep(mla_ref, cos_ref, sin_ref, gq_ref, wqa_ref, wqb_ref, gkv_ref, wk_ref, wv_ref, q_ref, k_ref, v_ref)


def _inproj(x2, g, w_cat, cos, sin, gq, wqa, wqb, gkv, wk, wv, tm):
    t = x2.shape[0]
    row = lambda w: pl.BlockSpec((tm, w), lambda i: (i, 0))
    hw = MLA_HEADS * LANE
    return pl.pallas_call(
        _inproj_kernel,
        grid=(t // tm,),
        in_specs=[row(D_MODEL), _full((1, D_MODEL)), _full((D_MODEL, IN_W)), row(LANE), row(LANE),
                  _full((1, Q_LORA)), _full((Q_LORA, hw)), _full((Q_LORA, hw)), _full((1, KV_LORA)),
                  _full((KV_LORA, hw)), _full((KV_LORA, MLA_OUT))],
        out_specs=[row(hw), row(hw), row(MLA_OUT), row(QKV_W), row(QKV_W), row(LANE)],
        out_shape=[jax.ShapeDtypeStruct((t, hw), BF16), jax.ShapeDtypeStruct((t, hw), BF16),
                   jax.ShapeDtypeStruct((t, MLA_OUT), BF16), jax.ShapeDtypeStruct((t, QKV_W), BF16),
                   jax.ShapeDtypeStruct((t, QKV_W), BF16), jax.ShapeDtypeStruct((t, LANE), F32)],
        scratch_shapes=[pltpu.VMEM((tm, MLA_IN_W), F32)],
        compiler_params=_cparams("parallel"),
        name="inproj",
    )(x2, g, w_cat, cos, sin, gq, wqa, wqb, gkv, wk, wv)


def _mla_prep(in_ref, cos_ref, sin_ref, gq_ref, wqa_ref, wqb_ref, gkv_ref, wk_ref, wv_ref, q_ref, k_ref, v_ref):
    cos = cos_ref[...]
    sin = sin_ref[...]
    qn = _rms(in_ref[:, :Q_LORA], gq_ref[...]).astype(BF16)
    kn = _rms(in_ref[:, Q_LORA:Q_LORA + KV_LORA], gkv_ref[...]).astype(BF16)
    o = Q_LORA + KV_LORA
    k_rot = in_ref[:, o:o + LANE] * cos + in_ref[:, o + LANE:o + 2 * LANE] * sin
    scale = MLA_QK ** -0.5 * LOG2E
    qa = _dot(qn, wqa_ref[...])
    qb = _dot(qn, wqb_ref[...])
    ka = _dot(kn, wk_ref[...])
    for h in range(MLA_HEADS):
        sl = slice(h * LANE, (h + 1) * LANE)
        q_ref[:, sl] = ((qa[:, sl] * cos + qb[:, sl] * sin) * scale).astype(BF16)
        k_ref[:, sl] = (ka[:, sl] + k_rot).astype(BF16)
    v_ref[...] = _dot(kn, wv_ref[...]).astype(BF16)


def _fox_prep_kernel(f_ref, bf_ref, cum_ref, cumt_ref, *, blk):
    s = f_ref.shape[1]
    row = lax.broadcasted_iota(jnp.int32, (blk, blk), 0)
    col = lax.broadcasted_iota(jnp.int32, (blk, blk), 1)
    tri = jnp.where(col <= row, 1.0, 0.0).astype(BF16)
    carry = jnp.zeros((1, LANE), F32)
    for b in range(s // blk):
        f = f_ref[0, b * blk:(b + 1) * blk, :] + bf_ref[...]
        lf = (jnp.minimum(f, 0.0) - jnp.log1p(jnp.exp(-jnp.abs(f)))) * LOG2E
        hi = lf.astype(BF16)
        r1 = lf - hi.astype(F32)
        mid = r1.astype(BF16)
        lo = (r1 - mid.astype(F32)).astype(BF16)
        c = _dot(tri, hi) + _dot(tri, mid) + _dot(tri, lo) + carry
        cum_ref[0, b * blk:(b + 1) * blk, :] = c
        cumt_ref[0, :, b * blk:(b + 1) * blk] = c.T[:8, :]
        carry = c[blk - 1:blk, :]


def _fox_prep(f3, bf_row, blk):
    b, s, _ = f3.shape
    return pl.pallas_call(
        functools.partial(_fox_prep_kernel, blk=blk),
        grid=(b,),
        in_specs=[pl.BlockSpec((1, s, LANE), lambda i: (i, 0, 0)), _full((1, LANE))],
        out_specs=[pl.BlockSpec((1, s, LANE), lambda i: (i, 0, 0)), pl.BlockSpec((1, 8, s), lambda i: (i, 0, 0))],
        out_shape=[jax.ShapeDtypeStruct((b, s, LANE), F32), jax.ShapeDtypeStruct((b, 8, s), F32)],
        compiler_params=_cparams("parallel"),
        name="fox_prep",
    )(f3, bf_row)


def _softmax_update(h, s, v, m_ref, l_ref, acc_ref):
    m_old = m_ref[h]
    m_new = jnp.maximum(m_old, jnp.max(s, axis=-1, keepdims=True))
    alpha = jnp.exp2(m_old - m_new)
    p = [jnp.exp2(s[:, n * LANE:(n + 1) * LANE] - m_new) for n in range(s.shape[1] // LANE)]
    part = p[0]
    for p_n in p[1:]:
        part = part + p_n
    l_ref[h] = alpha * l_ref[h] + part
    m_ref[h] = m_new
    acc_ref[h] = alpha * acc_ref[h] + _dot(jnp.concatenate(p, axis=1).astype(BF16), v)


def _softmax_reset(m_ref, l_ref, acc_ref):
    m_ref[...] = jnp.full(m_ref.shape, NEG_INF, F32)
    l_ref[...] = jnp.zeros(l_ref.shape, F32)
    acc_ref[...] = jnp.zeros(acc_ref.shape, F32)


def _half_mask(tq, half):
    lane = lax.broadcasted_iota(jnp.int32, (tq, LANE), 1)
    return (lane < HEAD_DIM) if half == 0 else (lane >= HEAD_DIM)


def _store_head_pairs(o_ref, heads, value_of):
    tq = o_ref.shape[1]
    lo_half = _half_mask(tq, 0)
    for hb in range(o_ref.shape[2] // LANE):
        lo = value_of(2 * hb)
        hi = value_of(2 * hb + 1) if 2 * hb + 1 < heads else jnp.zeros_like(lo)
        o_ref[0, :, hb * LANE:(hb + 1) * LANE] = jnp.where(lo_half, lo, hi).astype(o_ref.dtype)


def _masked_queries(q_ref, qm_ref, heads):
    tq = q_ref.shape[1]
    for h in range(heads):
        qf = q_ref[0, :, (h // 2) * LANE:(h // 2 + 1) * LANE].astype(F32)
        qm_ref[h] = jnp.where(_half_mask(tq, h % 2), qf, 0.0).astype(BF16)


def _block_iotas(tq):
    return lax.broadcasted_iota(jnp.int32, (tq, tq), 0), lax.broadcasted_iota(jnp.int32, (tq, tq), 1)


def _mla_attn_kernel(q_ref, k_ref, v_ref, o_ref, m_ref, l_ref, acc_ref, *, tq):
    i = pl.program_id(1)
    _softmax_reset(m_ref, l_ref, acc_ref)

    def block(j, diagonal):
        rows = pl.ds(pl.multiple_of(j * tq, tq), tq)
        def scores(h):
            sl = slice(h * LANE, (h + 1) * LANE)
            return _dot_nt(q_ref[0, :, sl], k_ref[0, rows, sl])

        s_next = scores(0)
        for h in range(MLA_HEADS):
            s = s_next
            if h + 1 < MLA_HEADS:
                s_next = scores(h + 1)
            if diagonal:
                r, c = _block_iotas(tq)
                s = jnp.where((c // CHUNK) <= (r // CHUNK), s, NEG_INF)
            _softmax_update(h, s, v_ref[0, rows, (h // 2) * LANE:(h // 2 + 1) * LANE], m_ref, l_ref, acc_ref)

    def body(j, carry):
        block(j, False)
        return carry

    lax.fori_loop(0, i, body, 0)
    block(i, True)
    _store_head_pairs(o_ref, MLA_HEADS, lambda h: acc_ref[h] / jnp.sum(l_ref[h], axis=-1, keepdims=True))


def _fox_attn_kernel(q_ref, k_ref, v_ref, cum_ref, cumt_ref, o_ref, qm_ref, cq_ref, m_ref, l_ref, acc_ref, *, tq):
    i = pl.program_id(1)
    _softmax_reset(m_ref, l_ref, acc_ref)
    _masked_queries(q_ref, qm_ref, FOX_HEADS)
    for h in range(FOX_HEADS):
        cq_ref[h] = jnp.broadcast_to(cum_ref[0, :, h:h + 1], (tq, LANE))

    def block(j, diagonal):
        start = pl.multiple_of(j * tq, tq)
        rows = pl.ds(start, tq)
        def scores(h):
            return _dot_nt(qm_ref[h], k_ref[0, rows, (h // 2) * LANE:(h // 2 + 1) * LANE])

        s_next = scores(0)
        for h in range(FOX_HEADS):
            sl = slice((h // 2) * LANE, (h // 2 + 1) * LANE)
            s = s_next
            if h + 1 < FOX_HEADS:
                s_next = scores(h + 1)
            ck = cumt_ref[0, h:h + 1, rows]
            s = jnp.concatenate([s[:, n * LANE:(n + 1) * LANE] + (cq_ref[h] - ck[:, n * LANE:(n + 1) * LANE])
                                 for n in range(tq // LANE)], axis=1)
            if diagonal:
                r, c = _block_iotas(tq)
                s = jnp.where(c <= r, s, NEG_INF)
            _softmax_update(h, s, v_ref[0, rows, sl], m_ref, l_ref, acc_ref)

    def body(j, carry):
        block(j, False)
        return carry

    lax.fori_loop(0, i, body, 0)
    block(i, True)
    _store_head_pairs(o_ref, FOX_HEADS, lambda h: acc_ref[h] / jnp.sum(l_ref[h], axis=-1, keepdims=True))


def _sb_attn_kernel(q_ref, k_ref, v_ref, o_ref, qm_ref, suffix_ref, rest_ref, acc_ref, *, tq):
    i = pl.program_id(1)
    sub = suffix_ref.shape[0]
    r, c = _block_iotas(sub)
    suffix_ref[...] = jnp.where(r > c, 1.0, 0.0).astype(BF16)
    for h in range(SB_HEADS):
        rest_ref[h] = jnp.zeros(rest_ref.shape[1:], F32)
        acc_ref[h] = jnp.zeros(acc_ref.shape[1:], F32)
    _masked_queries(q_ref, qm_ref, SB_HEADS)

    def block(j, diagonal):
        rows = pl.ds(pl.multiple_of(j * tq, tq), tq)
        def scores(h):
            return _dot_nt(qm_ref[h], k_ref[0, rows, (h // 2) * LANE:(h // 2 + 1) * LANE])

        z_next = scores(0)
        for h in range(SB_HEADS):
            sl = slice((h // 2) * LANE, (h // 2 + 1) * LANE)
            z = z_next
            if h + 1 < SB_HEADS:
                z_next = scores(h + 1)
            log_b = jnp.minimum(z, 0.0) - jnp.log(1.0 + jnp.exp2(-jnp.abs(z))) * LOG2E
            log_1m = log_b - z
            if diagonal:
                strict = _block_iotas(tq)[1] < _block_iotas(tq)[0]
                log_1m = jnp.where(strict, log_1m, 0.0)
            later = rest_ref[h]
            a = [None] * (tq // sub)
            for k in reversed(range(tq // sub)):
                cols = slice(k * sub, (k + 1) * sub)
                within = _dot(log_1m[:, cols].astype(BF16), suffix_ref[...])
                a[k] = jnp.concatenate([jnp.exp2(log_b[:, cols][:, n * LANE:(n + 1) * LANE]
                                                 + within[:, n * LANE:(n + 1) * LANE] + later)
                                        for n in range(sub // LANE)], axis=1)
                later = later + jnp.sum(log_1m[:, cols], axis=-1, keepdims=True)
            a = jnp.concatenate(a, axis=1)
            if diagonal:
                a = jnp.where(strict, a, 0.0)
            acc_ref[h] += _dot(a.astype(BF16), v_ref[0, rows, sl])
            rest_ref[h] = later

    block(i, True)

    def body(n, carry):
        block(i - 1 - n, False)
        return carry

    lax.fori_loop(0, i, body, 0)
    _store_head_pairs(o_ref, SB_HEADS, lambda h: acc_ref[h])


def _attention(kernel, q, k, v, cols, extra, extra_specs, scratch, qw, vw, tq, name):
    b, s, _ = q.shape
    blk = lambda w, n=0: pl.BlockSpec((1, tq, w), lambda bi, i: (bi, i, n))
    seq = lambda w, n: pl.BlockSpec((1, s, w), lambda bi, i: (bi, 0, n))
    return pl.pallas_call(
        functools.partial(kernel, tq=tq),
        grid=(b, s // tq),
        in_specs=[blk(qw, cols[0]), seq(qw, cols[1]), seq(vw, cols[2])] + extra_specs,
        out_specs=blk(vw),
        out_shape=jax.ShapeDtypeStruct((b, s, vw), BF16),
        scratch_shapes=scratch,
        compiler_params=_cparams("parallel", "parallel"),
        name=name,
    )(q, k, v, *extra)


def _head_state(heads, tq):
    return pltpu.VMEM((heads, tq, LANE), F32)


def _merge_kernel(x_ref, g_ref, wg_ref, bg_ref, oa_ref, ob_ref, oc_ref, wa_ref, wb_ref, wc_ref, wo_ref, out_ref):
    x = x_ref[...]
    hn = _rms(x, g_ref[...]).astype(BF16)
    y = None
    for n, (o_ref, w_ref) in enumerate(((oa_ref, wa_ref), (ob_ref, wb_ref), (oc_ref, wc_ref))):
        sl = slice(n * D_MODEL, (n + 1) * D_MODEL)
        gate = jax.nn.sigmoid(_dot(hn, wg_ref[:, sl]) + bg_ref[:, sl])
        term = gate * _dot(o_ref[...], w_ref[...])
        y = term if y is None else y + term
    out_ref[...] = x + _dot(y.astype(BF16), wo_ref[...])


def _merge(x2, g, wg, bg, oa, ob, oc, wa, wb, wc, wo, tm):
    t = x2.shape[0]
    row = lambda w: pl.BlockSpec((tm, w), lambda i: (i, 0))
    return pl.pallas_call(
        _merge_kernel,
        grid=(t // tm,),
        in_specs=[row(D_MODEL), _full((1, D_MODEL)), _full((D_MODEL, N_BRANCH * D_MODEL)),
                  _full((1, N_BRANCH * D_MODEL)), row(HEAD_PAD_W), row(HEAD_PAD_W), row(HEAD_PAD_W),
                  _full((HEAD_PAD_W, D_MODEL)), _full((HEAD_PAD_W, D_MODEL)), _full((HEAD_PAD_W, D_MODEL)),
                  _full((D_MODEL, D_MODEL))],
        out_specs=row(D_MODEL),
        out_shape=jax.ShapeDtypeStruct((t, D_MODEL), F32),
        compiler_params=_cparams("parallel"),
        name="merge",
    )(x2, g, wg, bg, oa, ob, oc, wa, wb, wc, wo)


def _sorted_top16(groups):
    x = list(groups)
    assert len(x) in (PEER_TOPK // 2, PEER_TOPK)

    def exchange(i, l, descending):
        hi, lo = jnp.maximum(x[i], x[l]), jnp.minimum(x[i], x[l])
        x[i], x[l] = (hi, lo) if descending else (lo, hi)

    k = 2
    while k <= len(x):
        j = k // 2
        while j >= 1:
            for i in range(len(x)):
                if i ^ j > i:
                    exchange(i, i ^ j, (i & k) == 0)
            j //= 2
        k *= 2
    for shift in (4, 2, 1):
        other = [pltpu.roll(v, shift, axis=0) for v in x]
        if len(x) < PEER_TOPK:
            x = x + other[::-1]
        else:
            x = [jnp.maximum(x[r], other[PEER_TOPK - 1 - r]) for r in range(PEER_TOPK)]
        j = PEER_TOPK // 2
        while j >= 1:
            for i in range(PEER_TOPK):
                if i ^ j > i:
                    exchange(i, i ^ j, True)
            j //= 2
    return x


def _peer_route_kernel(x_ref, g_ref, wq_ref, keys_ref, h_ref, rank2_ref, e2_ref, cnt_ref, e1_ref,
                       s1_ref, s2_ref, q_ref, top_ref, cand_ref):
    hn = _rms(x_ref[...], g_ref[...]).astype(BF16)
    h_ref[...] = hn
    wide = 2 * D_KEY
    for n in range(PEER_HEADS * D_KEY // wide):
        q_ref[:, n * wide:(n + 1) * wide] = _dot(hn, wq_ref[:, n * wide:(n + 1) * wide]).astype(BF16)
    for h in range(PEER_HEADS):
        for side, s_ref in ((0, s1_ref), (1, s2_ref)):
            o = h * D_KEY + side * HALF_KEY
            s_ref[h] = _dot_nt(keys_ref[side], q_ref[:, o:o + HALF_KEY])

    def lane_tile(lt, carry):
        lanes = pl.ds(pl.multiple_of(lt * LANE, LANE), LANE)
        for h in range(PEER_HEADS):
            scores, tops = [], []
            for side, s_ref in ((0, s1_ref), (1, s2_ref)):
                groups = [s_ref[h, 8 * n:8 * n + 8, lanes] for n in range(N_KEYS // 8)]
                top = _sorted_top16(groups)
                for r in range(PEER_TOPK):
                    top_ref[side, r:r + 1, :] = top[r][0:1, :]
                scores.append(groups)
                tops.append(top)
            (s1, s2), (top1, top2) = scores, tops
            v2_top8 = top_ref[1, 0:8, :]
            sub = lax.broadcasted_iota(jnp.int32, (8, LANE), 0)
            v2_top4_twice = jnp.where(sub < 4, v2_top8, pltpu.roll(v2_top8, 4, axis=0))
            cand_ref[0:16, :] = top_ref[0, 0:1, :] + top_ref[1]
            for a in (1, 2, 3):
                cand_ref[8 + 8 * a:16 + 8 * a, :] = top_ref[0, a:a + 1, :] + v2_top8
            for n, a in enumerate((4, 6)):
                v1_pair = jnp.where(sub < 4, top_ref[0, a:a + 1, :], top_ref[0, a + 1:a + 2, :])
                cand_ref[40 + 8 * n:48 + 8 * n, :] = v1_pair + v2_top4_twice
            cand_ref[56:64, :] = top_ref[0, 8:16, :] + top_ref[1, 0:1, :]
            best = _sorted_top16([cand_ref[8 * n:8 * n + 8, :] for n in range(cand_ref.shape[0] // 8)])
            tau = best[PEER_TOPK - 1]
            m1, m2 = top1[0], top2[0]
            zsum = best[0] * 0.0
            for v in best:
                zsum = zsum + jnp.exp(v - (m1 + m2))
            half_inv_z = 0.5 / zsum
            ranks, cnts = [], []
            for g1, g2 in zip(s1, s2):
                thr = tau - g1
                rank, cnt = jnp.zeros((8, LANE), F32), jnp.zeros((8, LANE), F32)
                for b in range(PEER_TOPK):
                    rank = jnp.where(top2[b] > g2, float(b + 1), rank)
                    cnt = jnp.where(top2[b] >= thr, float(b + 1), cnt)
                ranks.append(rank)
                cnts.append(cnt)
            rank2_ref[h, :, lanes] = jnp.concatenate(ranks, axis=0).astype(BF16)
            e2_ref[h, :, lanes] = jnp.concatenate([jnp.exp(g2 - m2) for g2 in s2], axis=0).astype(BF16)
            cnt_ref[h, :, lanes] = jnp.concatenate(cnts, axis=0)
            e1_ref[h, :, lanes] = jnp.concatenate([jnp.exp(g1 - m1) * half_inv_z for g1 in s1], axis=0)
        return carry

    lax.fori_loop(0, x_ref.shape[0] // LANE, lane_tile, 0)


def _peer_route(x2, g, wq, keys, tm):
    t = x2.shape[0]
    row = pl.BlockSpec((tm, D_MODEL), lambda i: (i, 0))
    sc = pl.BlockSpec((None, PEER_HEADS, N_KEYS, tm), lambda i: (i, 0, 0, 0))
    return pl.pallas_call(
        _peer_route_kernel,
        grid=(t // tm,),
        in_specs=[row, _full((1, D_MODEL)), _full((D_MODEL, PEER_HEADS * D_KEY)), _full((2, N_KEYS, HALF_KEY))],
        out_specs=[row, sc, sc, sc, sc],
        out_shape=[jax.ShapeDtypeStruct((t, D_MODEL), BF16),
                   jax.ShapeDtypeStruct((t // tm, PEER_HEADS, N_KEYS, tm), BF16),
                   jax.ShapeDtypeStruct((t // tm, PEER_HEADS, N_KEYS, tm), BF16),
                   jax.ShapeDtypeStruct((t // tm, PEER_HEADS, N_KEYS, tm), F32),
                   jax.ShapeDtypeStruct((t // tm, PEER_HEADS, N_KEYS, tm), F32)],
        scratch_shapes=[pltpu.VMEM((PEER_HEADS, N_KEYS, tm), F32), pltpu.VMEM((PEER_HEADS, N_KEYS, tm), F32),
                        pltpu.VMEM((tm, PEER_HEADS * D_KEY), BF16), pltpu.VMEM((2, PEER_TOPK, LANE), F32),
                        pltpu.VMEM((4 * PEER_TOPK, LANE), F32)],
        compiler_params=_cparams("parallel"),
        name="peer_route",
    )(x2, g, wq, keys)


def _peer_tables_kernel(u_ref, v_ref, ub_ref, vt_ref):
    ub_ref[...] = u_ref[...].astype(BF16)
    vt_ref[0] = v_ref[...].T.astype(BF16)


def _peer_tables(u_all, v_all, layer, te):
    n_chunks = N_EXPERTS // te
    src = pl.BlockSpec((None, te, D_MODEL), lambda c: (layer, c, 0))
    return pl.pallas_call(
        _peer_tables_kernel,
        grid=(n_chunks,),
        in_specs=[src, src],
        out_specs=[pl.BlockSpec((te, D_MODEL), lambda c: (c, 0)), pl.BlockSpec((1, D_MODEL, te), lambda c: (c, 0, 0))],
        out_shape=[jax.ShapeDtypeStruct((N_EXPERTS, D_MODEL), BF16),
                   jax.ShapeDtypeStruct((n_chunks, D_MODEL, te), BF16)],
        compiler_params=_cparams("parallel"),
        name="peer_tables",
    )(u_all, v_all)


def _peer_dense_step(c, parity, h_ref, rank2_ref, e2_ref, cnt_ref, e1_ref, u_ref, vt_ref, pre_ref, p_ref, acc_ref,
                     *, te, n_chunks, stages):
    tm = h_ref.shape[0]
    rows_per_step = te // N_KEYS
    assert 2 * rows_per_step == 8
    part = 64
    gate_w = 2 * LANE
    score_buf, gate_buf, fold_buf = parity, 1 - parity, parity
    key_tile = pl.ds(pl.multiple_of((jnp.clip(c - 1, 0, n_chunks - 1) // 2) * 8, 8), 8)
    row0 = rows_per_step * (1 - parity)

    def score(piece):
        rows = slice((piece // 2) * te // 2, (piece // 2 + 1) * te // 2)
        toks = slice((piece % 2) * gate_w, (piece % 2 + 1) * gate_w)
        u_rows = slice(parity * te + rows.start, parity * te + rows.stop)
        pre_ref[score_buf, rows, toks] = _dot_nt(u_ref[u_rows, :], h_ref[toks, :])

    def fold(piece):
        rows = slice((piece // 2) * D_MODEL // 4, (piece // 2 + 1) * D_MODEL // 4)
        toks = slice((piece % 2) * gate_w, (piece % 2 + 1) * gate_w)
        acc_ref[rows, toks] += _dot(vt_ref[parity, rows, :], p_ref[fold_buf, :, toks])

    def gate_heads(lt, rp, w, heads):
        lanes = slice(lt * gate_w, (lt + 1) * gate_w)
        for h in heads:
            rank2 = rank2_ref[h, rp * part:(rp + 1) * part, lanes]
            e2 = e2_ref[h, rp * part:(rp + 1) * part, lanes]
            cnt = cnt_ref[h, key_tile, lanes]
            e1 = e1_ref[h, key_tile, lanes]
            for j in range(rows_per_step):
                cnt_j = jnp.broadcast_to(cnt[row0 + j:row0 + j + 1, :], (part, gate_w)).astype(BF16)
                e1_j = jnp.broadcast_to(e1[row0 + j:row0 + j + 1, :], (part, gate_w)).astype(BF16)
                term = jnp.where(rank2 < cnt_j, e2, jnp.zeros_like(e2)) * e1_j
                w[j] = term if w[j] is None else w[j] + term

    def gate_store(lt, rp, w):
        lanes = slice(lt * gate_w, (lt + 1) * gate_w)
        for j in range(rows_per_step):
            rows = slice(j * N_KEYS + rp * part, j * N_KEYS + (rp + 1) * part)
            pre = pre_ref[gate_buf, rows, lanes]
            act = pre * (1.0 + lax.erf(pre * (0.5 ** 0.5)))
            p_ref[gate_buf, rows, lanes] = w[j] * act.astype(BF16)

    assert tm // gate_w == 2 and N_KEYS // part == 2
    mxu = [lambda n=n: score(n) for n in range(4)] + [lambda n=n: fold(n) for n in range(8)]
    order = [0, 4, 5, 1, 6, 7, 2, 8, 9, 3, 10, 11]
    live = [n for n in order if ("score" if n < 4 else "fold") in stages]
    if "gate" not in stages:
        for n in live:
            mxu[n]()
        return
    slots = 4 * 3
    pieces = iter([mxu[n] for n in live] + [None] * (slots - len(live)))
    for lt in range(2):
        for rp in range(2):
            w = [None] * rows_per_step
            for h in range(PEER_HEADS):
                gate_heads(lt, rp, w, (h,))
                if h % 3 == 1:
                    piece = next(pieces)
                    if piece is not None:
                        piece()
            gate_store(lt, rp, w)


def _peer_dense_kernel(x_ref, h_ref, rank2_ref, e2_ref, cnt_ref, e1_ref, u_ref, vt_ref, gf_ref, out_ref,
                       pre_ref, p_ref, acc_ref, *, te, n_chunks, final_norm):
    g = pl.program_id(1)

    last = n_chunks // 2

    @pl.when(g == 0)
    def _():
        acc_ref[...] = jnp.zeros_like(acc_ref)

    full = ("score", "gate", "fold")
    plans = (((0, 0), (("score",), ("score", "gate"))),
             ((1, last - 1), (full, full)),
             ((last, last), (("gate", "fold"), ("fold",))))
    for (lo, hi), stage_sets in plans:
        for parity, stages in enumerate(stage_sets):
            guard = jnp.logical_and(g >= lo, g <= hi) if parity == 0 else jnp.logical_and(g > lo - 1, g < hi + 1)

            @pl.when(guard)
            def _(parity=parity, stages=stages):
                _peer_dense_step(2 * g + parity, parity, h_ref, rank2_ref, e2_ref, cnt_ref, e1_ref, u_ref, vt_ref,
                                 pre_ref, p_ref, acc_ref, te=te, n_chunks=n_chunks, stages=stages)

    @pl.when(g == last)
    def _():
        y = x_ref[...] + acc_ref[...].T
        if final_norm:
            y = _rms(y, gf_ref[...])
        out_ref[...] = y


def _peer_dense(x2, hn, rank2, e2, cnt, e1, u, vt, gf, tm, te, final_norm):
    t = x2.shape[0]
    n_chunks = N_EXPERTS // te
    row = pl.BlockSpec((tm, D_MODEL), lambda i, c: (i, 0))
    sc = pl.BlockSpec((None, PEER_HEADS, N_KEYS, tm), lambda i, c: (i, 0, 0, 0))
    return pl.pallas_call(
        functools.partial(_peer_dense_kernel, te=te, n_chunks=n_chunks, final_norm=final_norm),
        grid=(t // tm, n_chunks // 2 + 1),
        in_specs=[row, row, sc, sc, sc, sc,
                  pl.BlockSpec((2 * te, D_MODEL), lambda i, g: (jnp.minimum(g, n_chunks // 2 - 1), 0)),
                  pl.BlockSpec((2, D_MODEL, te), lambda i, g: (jnp.clip(g - 1, 0, n_chunks // 2 - 1), 0, 0)),
                  pl.BlockSpec((1, D_MODEL), lambda i, c: (0, 0))],
        out_specs=row,
        out_shape=jax.ShapeDtypeStruct((t, D_MODEL), F32),
        scratch_shapes=[pltpu.VMEM((2, te, tm), F32), pltpu.VMEM((2, te, tm), BF16),
                        pltpu.VMEM((D_MODEL, tm), F32)],
        compiler_params=_cparams("parallel", "arbitrary"),
        name="peer_dense",
    )(x2, hn, rank2, e2, cnt, e1, u, vt, gf)


def _pad_cols(w, width):
    return jnp.pad(w, ((0, 0), (0, width - w.shape[1])))


def _rope_partner(w):
    half = MLA_ROPE // 2
    return jnp.concatenate([-w[:, half:], w[:, :half]], axis=1)


def _rope_lanes(w):
    return jnp.pad(w, ((0, 0), (MLA_NOPE, LANE - MLA_NOPE - MLA_ROPE)))


def _layer_weights(w_in, w_uq, w_ukv, w_o_sb, w_o_fox):
    o = IN_OFFSETS
    seg = lambda n: w_in[:, o[n]:o[n + 1]]
    scale = HEAD_DIM ** -0.5 * LOG2E
    w_kr = seg(2)
    w_cat = jnp.concatenate(
        [seg(0), seg(1), _rope_lanes(w_kr), _rope_lanes(_rope_partner(w_kr)),
         _pad_cols(seg(3) * scale, HEAD_PAD_W), _pad_cols(seg(4), HEAD_PAD_W), _pad_cols(seg(5), HEAD_PAD_W),
         _pad_cols(seg(6) * scale, HEAD_PAD_W), _pad_cols(seg(7), HEAD_PAD_W), _pad_cols(seg(8), HEAD_PAD_W),
         _pad_cols(seg(9), LANE)], axis=1).astype(BF16)
    w_gate = seg(10).astype(BF16)
    uq = w_uq.reshape(Q_LORA, MLA_HEADS, MLA_QK)
    nope, rope = uq[..., :MLA_NOPE], uq[..., MLA_NOPE:]
    zpad = jnp.zeros((Q_LORA, MLA_HEADS, LANE - MLA_QK), F32)
    wqa = jnp.concatenate([nope, rope, zpad], axis=-1).reshape(Q_LORA, MLA_HEADS * LANE).astype(BF16)
    partner = jnp.concatenate([-rope[..., MLA_ROPE // 2:], rope[..., :MLA_ROPE // 2]], axis=-1)
    wqb = jnp.concatenate([jnp.zeros_like(nope), partner, zpad], axis=-1).reshape(Q_LORA, MLA_HEADS * LANE).astype(BF16)
    ukv = w_ukv.reshape(KV_LORA, MLA_HEADS, MLA_NOPE + MLA_V)
    wk = jnp.pad(ukv[..., :MLA_NOPE], ((0, 0), (0, 0), (0, LANE - MLA_NOPE))).reshape(KV_LORA, MLA_HEADS * LANE).astype(BF16)
    wv = ukv[..., MLA_NOPE:].reshape(KV_LORA, MLA_OUT).astype(BF16)
    pad_rows = lambda w: jnp.pad(w, ((0, HEAD_PAD_W - w.shape[0]), (0, 0))).astype(BF16)
    return w_cat, w_gate, wqa, wqb, wk, wv, pad_rows(w_o_sb), pad_rows(w_o_fox)


def _tiles(b, s):
    t = b * s
    tm = min(TOKEN_TILE, t)
    tq = min(ATTN_TILE, s)
    return t, tm, tq


def kernel(x, positions, norm1_g, w_in, mla_q_norm_g, w_uq, mla_kv_norm_g, w_ukv, fox_b_f, w_o_mla, w_o_sb,
           w_o_fox, b_gate, w_out, norm2_g, peer_w_q, peer_sub_keys, peer_u, peer_v, final_norm_g):
    b, s, d = x.shape
    assert d == D_MODEL and MLA_OUT == HEAD_PAD_W
    depth = w_in.shape[0]
    t, tm, tq = _tiles(b, s)
    assert t % tm == 0 and s % tq == 0 and tq % CHUNK == 0
    te = EXPERT_CHUNK
    x2 = x.reshape(t, d)
    cos, sin = _rope_tables(positions, tm)
    gf = final_norm_g.reshape(1, d)
    for l in range(depth):
        w_cat, w_gate, wqa, wqb, wk, wv, wo_sb, wo_fox = _layer_weights(w_in[l], w_uq[l], w_ukv[l], w_o_sb[l], w_o_fox[l])
        g1 = norm1_g[l].reshape(1, d)
        q_a, k_a, v_a, sb, fx, f = _inproj(x2, g1, w_cat, cos, sin, mla_q_norm_g[l].reshape(1, Q_LORA), wqa, wqb,
                                           mla_kv_norm_g[l].reshape(1, KV_LORA), wk, wv, tm)
        bf_row = jnp.zeros((1, LANE), F32).at[0, :FOX_HEADS].set(fox_b_f[l])
        cum, cumt = _fox_prep(f.reshape(b, s, LANE), bf_row, min(SUFFIX_TILE, s))
        hw = MLA_HEADS * LANE
        acc_a = _head_state(MLA_HEADS, tq)
        o_a = _attention(_mla_attn_kernel, q_a.reshape(b, s, hw), k_a.reshape(b, s, hw), v_a.reshape(b, s, MLA_OUT),
                         (0, 0, 0), [], [], [acc_a, acc_a, acc_a], hw, MLA_OUT, tq, "mla_attn")
        sb3 = sb.reshape(b, s, QKV_W)
        fx3 = fx.reshape(b, s, QKV_W)
        acc_b = _head_state(SB_HEADS, tq)
        qm = pltpu.VMEM((SB_HEADS, tq, LANE), BF16)
        o_b = _attention(_sb_attn_kernel, sb3, sb3, sb3, (0, 1, 2), [], [],
                         [qm, pltpu.VMEM((min(SUFFIX_TILE, tq),) * 2, BF16), acc_b, acc_b], HEAD_PAD_W, HEAD_PAD_W, tq,
                         "sb_attn")
        fox_specs = [pl.BlockSpec((1, tq, LANE), lambda bi, i: (bi, i, 0)),
                     pl.BlockSpec((1, 8, s), lambda bi, i: (bi, 0, 0))]
        o_c = _attention(_fox_attn_kernel, fx3, fx3, fx3, (0, 1, 2), [cum, cumt], fox_specs,
                         [qm, acc_b, acc_b, acc_b, acc_b], HEAD_PAD_W, HEAD_PAD_W, tq, "fox_attn")
        x2 = _merge(x2, g1, w_gate, b_gate[l].reshape(1, -1), o_a.reshape(t, MLA_OUT), o_b.reshape(t, HEAD_PAD_W),
                    o_c.reshape(t, HEAD_PAD_W), w_o_mla[l].astype(BF16), wo_sb, wo_fox, w_out[l].astype(BF16), tm)
        hn, rank2, e2, cnt, e1 = _peer_route(x2, norm2_g[l].reshape(1, d), peer_w_q[l].astype(BF16),
                                             peer_sub_keys[l].astype(BF16), tm)
        u_bf, vt = _peer_tables(peer_u, peer_v, l, te)
        x2 = _peer_dense(x2, hn, rank2, e2, cnt, e1, u_bf, vt, gf, tm, te, final_norm=(l == depth - 1))
    return x2.reshape(b, s, d)
```

```python
import functools

import numpy as np
import jax
import jax.numpy as jnp
from jax import lax
from jax.experimental import pallas as pl
from jax.experimental.pallas import tpu as pltpu

F32 = jnp.float32
BF16 = jnp.bfloat16

D_MODEL = 1024
CHUNK = 64
HEAD_DIM = 64
NORM_EPS = 1e-6
NEG_INF = -1e30
MLA_HEADS = 6
MLA_NOPE = 64
MLA_ROPE = 32
MLA_V = 64
Q_LORA = 256
KV_LORA = 128
ROPE_THETA = 10000.0
SB_HEADS = 5
FOX_HEADS = 5
N_BRANCH = 3
PEER_HEADS = 8
N_KEYS = 128
N_EXPERTS = N_KEYS * N_KEYS
D_KEY = 256
HALF_KEY = D_KEY // 2
PEER_TOPK = 16
MLA_QK = MLA_NOPE + MLA_ROPE
SB_W = SB_HEADS * HEAD_DIM
FOX_W = FOX_HEADS * HEAD_DIM
MLA_OUT = MLA_HEADS * MLA_V
IN_SPLITS = (Q_LORA, KV_LORA, MLA_ROPE, SB_W, SB_W, SB_W, FOX_W, FOX_W, FOX_W, FOX_HEADS, N_BRANCH * D_MODEL)
IN_OFFSETS = tuple(int(o) for o in np.cumsum((0,) + IN_SPLITS))

LANE = 128
HEAD_PAD_W = 384
MLA_IN_W = Q_LORA + KV_LORA + 2 * LANE
QKV_W = 3 * HEAD_PAD_W
IN_W = MLA_IN_W + 2 * QKV_W + LANE
VMEM_LIMIT = 48 * 1024 * 1024
TOKEN_TILE = 512
ATTN_TILE = 512
SUFFIX_TILE = 256
EXPERT_CHUNK = 4 * N_KEYS

LOG2E = 1.4426950408889634
_NT = (((1,), (1,)), ((), ()))


def _cparams(*sem):
    return pltpu.CompilerParams(dimension_semantics=sem, vmem_limit_bytes=VMEM_LIMIT)


def _rms(x, g):
    return x * lax.rsqrt(jnp.mean(x * x, axis=-1, keepdims=True) + NORM_EPS) * g


def _dot(a, b):
    return jnp.dot(a, b, preferred_element_type=F32)


def _dot_nt(a, b):
    return lax.dot_general(a, b, _NT, preferred_element_type=F32)


def _full(shape):
    return pl.BlockSpec(shape, lambda *_: (0,) * len(shape))


def _rope_kernel(pos_ref, inv_ref, cos_ref, sin_ref):
    ang = pos_ref[...].astype(F32) * inv_ref[...]
    cos_ref[...] = jnp.cos(ang)
    sin_ref[...] = jnp.sin(ang)


def _rope_tables(positions, tm):
    t = positions.size
    inv = ROPE_THETA ** (-jnp.arange(0, MLA_ROPE, 2, dtype=F32) / MLA_ROPE)
    inv_row = jnp.zeros((1, LANE), F32).at[0, MLA_NOPE:MLA_NOPE + MLA_ROPE].set(jnp.concatenate([inv, inv]))
    return pl.pallas_call(
        _rope_kernel,
        grid=(t // tm,),
        in_specs=[pl.BlockSpec((tm, 1), lambda i: (i, 0)), _full((1, LANE))],
        out_specs=[pl.BlockSpec((tm, LANE), lambda i: (i, 0))] * 2,
        out_shape=[jax.ShapeDtypeStruct((t, LANE), F32)] * 2,
        compiler_params=_cparams("parallel"),
        name="rope_tables",
    )(positions.reshape(t, 1), inv_row)


def _inproj_kernel(x_ref, g_ref, w_ref, cos_ref, sin_ref, gq_ref, wqa_ref, wqb_ref, gkv_ref, wk_ref, wv_ref,
                   q_ref, k_ref, v_ref, sb_ref, fx_ref, f_ref, mla_ref):
    hn = _rms(x_ref[...], g_ref[...]).astype(BF16)
    o = 0
    for ref, width in ((mla_ref, MLA_IN_W), (sb_ref, QKV_W), (fx_ref, QKV_W), (f_ref, LANE)):
        ref[...] = _dot(hn, w_ref[:, o:o + width]).astype(ref.dtype)
        o += width
    _mla_prep(mla_ref, cos_ref, sin_ref, gq_ref, wqa_ref, wqb_ref, gkv_ref, wk_ref, wv_ref, q_ref, k_ref, v_ref)


def _inproj(x2, g, w_cat, cos, sin, gq, wqa, wqb, gkv, wk, wv, tm):
    t = x2.shape[0]
    row = lambda w: pl.BlockSpec((tm, w), lambda i: (i, 0))
    hw = MLA_HEADS * LANE
    return pl.pallas_call(
        _inproj_kernel,
        grid=(t // tm,),
        in_specs=[row(D_MODEL), _full((1, D_MODEL)), _full((D_MODEL, IN_W)), row(LANE), row(LANE),
                  _full((1, Q_LORA)), _full((Q_LORA, hw)), _full((Q_LORA, hw)), _full((1, KV_LORA)),
                  _full((KV_LORA, hw)), _full((KV_LORA, MLA_OUT))],
        out_specs=[row(hw), row(hw), row(MLA_OUT), row(QKV_W), row(QKV_W), row(LANE)],
        out_shape=[jax.ShapeDtypeStruct((t, hw), BF16), jax.ShapeDtypeStruct((t, hw), BF16),
                   jax.ShapeDtypeStruct((t, MLA_OUT), BF16), jax.ShapeDtypeStruct((t, QKV_W), BF16),
                   jax.ShapeDtypeStruct((t, QKV_W), BF16), jax.ShapeDtypeStruct((t, LANE), F32)],
        scratch_shapes=[pltpu.VMEM((tm, MLA_IN_W), F32)],
        compiler_params=_cparams("parallel"),
        name="inproj",
    )(x2, g, w_cat, cos, sin, gq, wqa, wqb, gkv, wk, wv)


def _mla_prep(in_ref, cos_ref, sin_ref, gq_ref, wqa_ref, wqb_ref, gkv_ref, wk_ref, wv_ref, q_ref, k_ref, v_ref):
    cos = cos_ref[...]
    sin = sin_ref[...]
    qn = _rms(in_ref[:, :Q_LORA], gq_ref[...]).astype(BF16)
    kn = _rms(in_ref[:, Q_LORA:Q_LORA + KV_LORA], gkv_ref[...]).astype(BF16)
    o = Q_LORA + KV_LORA
    k_rot = in_ref[:, o:o + LANE] * cos + in_ref[:, o + LANE:o + 2 * LANE] * sin
    scale = MLA_QK ** -0.5 * LOG2E
    qa = _dot(qn, wqa_ref[...])
    qb = _dot(qn, wqb_ref[...])
    ka = _dot(kn, wk_ref[...])
    for h in range(MLA_HEADS):
        sl = slice(h * LANE, (h + 1) * LANE)
        q_ref[:, sl] = ((qa[:, sl] * cos + qb[:, sl] * sin) * scale).astype(BF16)
        k_ref[:, sl] = (ka[:, sl] + k_rot).astype(BF16)
    v_ref[...] = _dot(kn, wv_ref[...]).astype(BF16)


def _fox_prep_kernel(f_ref, bf_ref, cum_ref, cumt_ref, *, blk):
    s = f_ref.shape[1]
    row = lax.broadcasted_iota(jnp.int32, (blk, blk), 0)
    col = lax.broadcasted_iota(jnp.int32, (blk, blk), 1)
    tri = jnp.where(col <= row, 1.0, 0.0).astype(BF16)
    carry = jnp.zeros((1, LANE), F32)
    for b in range(s // blk):
        f = f_ref[0, b * blk:(b + 1) * blk, :] + bf_ref[...]
        lf = (jnp.minimum(f, 0.0) - jnp.log1p(jnp.exp(-jnp.abs(f)))) * LOG2E
        hi = lf.astype(BF16)
        r1 = lf - hi.astype(F32)
        mid = r1.astype(BF16)
        lo = (r1 - mid.astype(F32)).astype(BF16)
        c = _dot(tri, hi) + _dot(tri, mid) + _dot(tri, lo) + carry
        cum_ref[0, b * blk:(b + 1) * blk, :] = c
        cumt_ref[0, :, b * blk:(b + 1) * blk] = c.T[:8, :]
        carry = c[blk - 1:blk, :]


def _fox_prep(f3, bf_row, blk):
    b, s, _ = f3.shape
    return pl.pallas_call(
        functools.partial(_fox_prep_kernel, blk=blk),
        grid=(b,),
        in_specs=[pl.BlockSpec((1, s, LANE), lambda i: (i, 0, 0)), _full((1, LANE))],
        out_specs=[pl.BlockSpec((1, s, LANE), lambda i: (i, 0, 0)), pl.BlockSpec((1, 8, s), lambda i: (i, 0, 0))],
        out_shape=[jax.ShapeDtypeStruct((b, s, LANE), F32), jax.ShapeDtypeStruct((b, 8, s), F32)],
        compiler_params=_cparams("parallel"),
        name="fox_prep",
    )(f3, bf_row)


def _softmax_update(h, s, v, m_ref, l_ref, acc_ref):
    m_old = m_ref[h]
    m_new = jnp.maximum(m_old, jnp.max(s, axis=-1, keepdims=True))
    alpha = jnp.exp2(m_old - m_new)
    p = [jnp.exp2(s[:, n * LANE:(n + 1) * LANE] - m_new) for n in range(s.shape[1] // LANE)]
    part = p[0]
    for p_n in p[1:]:
        part = part + p_n
    l_ref[h] = alpha * l_ref[h] + part
    m_ref[h] = m_new
    acc_ref[h] = alpha * acc_ref[h] + _dot(jnp.concatenate(p, axis=1).astype(BF16), v)


def _softmax_reset(m_ref, l_ref, acc_ref):
    m_ref[...] = jnp.full(m_ref.shape, NEG_INF, F32)
    l_ref[...] = jnp.zeros(l_ref.shape, F32)
    acc_ref[...] = jnp.zeros(acc_ref.shape, F32)


def _half_mask(tq, half):
    lane = lax.broadcasted_iota(jnp.int32, (tq, LANE), 1)
    return (lane < HEAD_DIM) if half == 0 else (lane >= HEAD_DIM)


def _store_head_pairs(o_ref, heads, value_of):
    tq = o_ref.shape[1]
    lo_half = _half_mask(tq, 0)
    for hb in range(o_ref.shape[2] // LANE):
        lo = value_of(2 * hb)
        hi = value_of(2 * hb + 1) if 2 * hb + 1 < heads else jnp.zeros_like(lo)
        o_ref[0, :, hb * LANE:(hb + 1) * LANE] = jnp.where(lo_half, lo, hi).astype(o_ref.dtype)


def _masked_queries(q_ref, qm_ref, heads):
    tq = q_ref.shape[1]
    for h in range(heads):
        qf = q_ref[0, :, (h // 2) * LANE:(h // 2 + 1) * LANE].astype(F32)
        qm_ref[h] = jnp.where(_half_mask(tq, h % 2), qf, 0.0).astype(BF16)


def _block_iotas(tq):
    return lax.broadcasted_iota(jnp.int32, (tq, tq), 0), lax.broadcasted_iota(jnp.int32, (tq, tq), 1)


def _mla_attn_kernel(q_ref, k_ref, v_ref, o_ref, m_ref, l_ref, acc_ref, *, tq):
    i = pl.program_id(1)
    _softmax_reset(m_ref, l_ref, acc_ref)

    def block(j, diagonal):
        rows = pl.ds(pl.multiple_of(j * tq, tq), tq)
        def scores(h):
            sl = slice(h * LANE, (h + 1) * LANE)
            return _dot_nt(q_ref[0, :, sl], k_ref[0, rows, sl])

        s_next = scores(0)
        for h in range(MLA_HEADS):
            s = s_next
            if h + 1 < MLA_HEADS:
                s_next = scores(h + 1)
            if diagonal:
                r, c = _block_iotas(tq)
                s = jnp.where((c // CHUNK) <= (r // CHUNK), s, NEG_INF)
            _softmax_update(h, s, v_ref[0, rows, (h // 2) * LANE:(h // 2 + 1) * LANE], m_ref, l_ref, acc_ref)

    def body(j, carry):
        block(j, False)
        return carry

    lax.fori_loop(0, i, body, 0)
    block(i, True)
    _store_head_pairs(o_ref, MLA_HEADS, lambda h: acc_ref[h] / jnp.sum(l_ref[h], axis=-1, keepdims=True))


def _fox_attn_kernel(q_ref, k_ref, v_ref, cum_ref, cumt_ref, o_ref, qm_ref, cq_ref, m_ref, l_ref, acc_ref, *, tq):
    i = pl.program_id(1)
    _softmax_reset(m_ref, l_ref, acc_ref)
    _masked_queries(q_ref, qm_ref, FOX_HEADS)
    for h in range(FOX_HEADS):
        cq_ref[h] = jnp.broadcast_to(cum_ref[0, :, h:h + 1], (tq, LANE))

    def block(j, diagonal):
        start = pl.multiple_of(j * tq, tq)
        rows = pl.ds(start, tq)
        def scores(h):
            return _dot_nt(qm_ref[h], k_ref[0, rows, (h // 2) * LANE:(h // 2 + 1) * LANE])

        s_next = scores(0)
        for h in range(FOX_HEADS):
            sl = slice((h // 2) * LANE, (h // 2 + 1) * LANE)
            s = s_next
            if h + 1 < FOX_HEADS:
                s_next = scores(h + 1)
            ck = cumt_ref[0, h:h + 1, rows]
            s = jnp.concatenate([s[:, n * LANE:(n + 1) * LANE] + (cq_ref[h] - ck[:, n * LANE:(n + 1) * LANE])
                                 for n in range(tq // LANE)], axis=1)
            if diagonal:
                r, c = _block_iotas(tq)
                s = jnp.where(c <= r, s, NEG_INF)
            _softmax_update(h, s, v_ref[0, rows, sl], m_ref, l_ref, acc_ref)

    def body(j, carry):
        block(j, False)
        return carry

    lax.fori_loop(0, i, body, 0)
    block(i, True)
    _store_head_pairs(o_ref, FOX_HEADS, lambda h: acc_ref[h] / jnp.sum(l_ref[h], axis=-1, keepdims=True))


def _sb_attn_kernel(q_ref, k_ref, v_ref, o_ref, qm_ref, suffix_ref, rest_ref, acc_ref, *, tq):
    i = pl.program_id(1)
    sub = suffix_ref.shape[0]
    r, c = _block_iotas(sub)
    suffix_ref[...] = jnp.where(r > c, 1.0, 0.0).astype(BF16)
    for h in range(SB_HEADS):
        rest_ref[h] = jnp.zeros(rest_ref.shape[1:], F32)
        acc_ref[h] = jnp.zeros(acc_ref.shape[1:], F32)
    _masked_queries(q_ref, qm_ref, SB_HEADS)

    def block(j, diagonal):
        rows = pl.ds(pl.multiple_of(j * tq, tq), tq)
        def scores(h):
            return _dot_nt(qm_ref[h], k_ref[0, rows, (h // 2) * LANE:(h // 2 + 1) * LANE])

        z_next = scores(0)
        for h in range(SB_HEADS):
            sl = slice((h // 2) * LANE, (h // 2 + 1) * LANE)
            z = z_next
            if h + 1 < SB_HEADS:
                z_next = scores(h + 1)
            log_b = jnp.minimum(z, 0.0) - jnp.log(1.0 + jnp.exp2(-jnp.abs(z))) * LOG2E
            log_1m = log_b - z
            if diagonal:
                strict = _block_iotas(tq)[1] < _block_iotas(tq)[0]
                log_1m = jnp.where(strict, log_1m, 0.0)
            later = rest_ref[h]
            a = [None] * (tq // sub)
            for k in reversed(range(tq // sub)):
                cols = slice(k * sub, (k + 1) * sub)
                within = _dot(log_1m[:, cols].astype(BF16), suffix_ref[...])
                a[k] = jnp.concatenate([jnp.exp2(log_b[:, cols][:, n * LANE:(n + 1) * LANE]
                                                 + within[:, n * LANE:(n + 1) * LANE] + later)
                                        for n in range(sub // LANE)], axis=1)
                later = later + jnp.sum(log_1m[:, cols], axis=-1, keepdims=True)
            a = jnp.concatenate(a, axis=1)
            if diagonal:
                a = jnp.where(strict, a, 0.0)
            acc_ref[h] += _dot(a.astype(BF16), v_ref[0, rows, sl])
            rest_ref[h] = later

    block(i, True)

    def body(n, carry):
        block(i - 1 - n, False)
        return carry

    lax.fori_loop(0, i, body, 0)
    _store_head_pairs(o_ref, SB_HEADS, lambda h: acc_ref[h])


def _attention(kernel, q, k, v, cols, extra, extra_specs, scratch, qw, vw, tq, name):
    b, s, _ = q.shape
    blk = lambda w, n=0: pl.BlockSpec((1, tq, w), lambda bi, i: (bi, i, n))
    seq = lambda w, n: pl.BlockSpec((1, s, w), lambda bi, i: (bi, 0, n))
    return pl.pallas_call(
        functools.partial(kernel, tq=tq),
        grid=(b, s // tq),
        in_specs=[blk(qw, cols[0]), seq(qw, cols[1]), seq(vw, cols[2])] + extra_specs,
        out_specs=blk(vw),
        out_shape=jax.ShapeDtypeStruct((b, s, vw), BF16),
        scratch_shapes=scratch,
        compiler_params=_cparams("parallel", "parallel"),
        name=name,
    )(q, k, v, *extra)


def _head_state(heads, tq):
    return pltpu.VMEM((heads, tq, LANE), F32)


def _merge_kernel(x_ref, g_ref, wg_ref, bg_ref, oa_ref, ob_ref, oc_ref, wa_ref, wb_ref, wc_ref, wo_ref, out_ref):
    x = x_ref[...]
    hn = _rms(x, g_ref[...]).astype(BF16)
    y = None
    for n, (o_ref, w_ref) in enumerate(((oa_ref, wa_ref), (ob_ref, wb_ref), (oc_ref, wc_ref))):
        sl = slice(n * D_MODEL, (n + 1) * D_MODEL)
        gate = jax.nn.sigmoid(_dot(hn, wg_ref[:, sl]) + bg_ref[:, sl])
        term = gate * _dot(o_ref[...], w_ref[...])
        y = term if y is None else y + term
    out_ref[...] = x + _dot(y.astype(BF16), wo_ref[...])


def _merge(x2, g, wg, bg, oa, ob, oc, wa, wb, wc, wo, tm):
    t = x2.shape[0]
    row = lambda w: pl.BlockSpec((tm, w), lambda i: (i, 0))
    return pl.pallas_call(
        _merge_kernel,
        grid=(t // tm,),
        in_specs=[row(D_MODEL), _full((1, D_MODEL)), _full((D_MODEL, N_BRANCH * D_MODEL)),
                  _full((1, N_BRANCH * D_MODEL)), row(HEAD_PAD_W), row(HEAD_PAD_W), row(HEAD_PAD_W),
                  _full((HEAD_PAD_W, D_MODEL)), _full((HEAD_PAD_W, D_MODEL)), _full((HEAD_PAD_W, D_MODEL)),
                  _full((D_MODEL, D_MODEL))],
        out_specs=row(D_MODEL),
        out_shape=jax.ShapeDtypeStruct((t, D_MODEL), F32),
        compiler_params=_cparams("parallel"),
        name="merge",
    )(x2, g, wg, bg, oa, ob, oc, wa, wb, wc, wo)


def _sorted_top16(groups):
    x = list(groups)
    assert len(x) in (PEER_TOPK // 2, PEER_TOPK)

    def exchange(i, l, descending):
        hi, lo = jnp.maximum(x[i], x[l]), jnp.minimum(x[i], x[l])
        x[i], x[l] = (hi, lo) if descending else (lo, hi)

    k = 2
    while k <= len(x):
        j = k // 2
        while j >= 1:
            for i in range(len(x)):
                if i ^ j > i:
                    exchange(i, i ^ j, (i & k) == 0)
            j //= 2
        k *= 2
    for shift in (4, 2, 1):
        other = [pltpu.roll(v, shift, axis=0) for v in x]
        if len(x) < PEER_TOPK:
            x = x + other[::-1]
        else:
            x = [jnp.maximum(x[r], other[PEER_TOPK - 1 - r]) for r in range(PEER_TOPK)]
        j = PEER_TOPK // 2
        while j >= 1:
            for i in range(PEER_TOPK):
                if i ^ j > i:
                    exchange(i, i ^ j, True)
            j //= 2
    return x


def _peer_route_kernel(x_ref, g_ref, wq_ref, keys_ref, h_ref, rank2_ref, e2_ref, cnt_ref, e1_ref,
                       s1_ref, s2_ref, q_ref, top_ref, cand_ref):
    hf = _rms(x_ref[...], g_ref[...])
    hn = hf.astype(BF16)
    h_ref[...] = hf.T.astype(BF16)
    wide = 2 * D_KEY
    for n in range(PEER_HEADS * D_KEY // wide):
        q_ref[:, n * wide:(n + 1) * wide] = _dot(hn, wq_ref[:, n * wide:(n + 1) * wide]).astype(BF16)
    for h in range(PEER_HEADS):
        for side, s_ref in ((0, s1_ref), (1, s2_ref)):
            o = h * D_KEY + side * HALF_KEY
            s_ref[h] = _dot_nt(keys_ref[side], q_ref[:, o:o + HALF_KEY])

    def lane_tile(lt, carry):
        lanes = pl.ds(pl.multiple_of(lt * LANE, LANE), LANE)
        for h in range(PEER_HEADS):
            scores, tops = [], []
            for side, s_ref in ((0, s1_ref), (1, s2_ref)):
                groups = [s_ref[h, 8 * n:8 * n + 8, lanes] for n in range(N_KEYS // 8)]
                top = _sorted_top16(groups)
                for r in range(PEER_TOPK):
                    top_ref[side, r:r + 1, :] = top[r][0:1, :]
                scores.append(groups)
                tops.append(top)
            (s1, s2), (top1, top2) = scores, tops
            v2_top8 = top_ref[1, 0:8, :]
            sub = lax.broadcasted_iota(jnp.int32, (8, LANE), 0)
            v2_top4_twice = jnp.where(sub < 4, v2_top8, pltpu.roll(v2_top8, 4, axis=0))
            cand_ref[0:16, :] = top_ref[0, 0:1, :] + top_ref[1]
            for a in (1, 2, 3):
                cand_ref[8 + 8 * a:16 + 8 * a, :] = top_ref[0, a:a + 1, :] + v2_top8
            for n, a in enumerate((4, 6)):
                v1_pair = jnp.where(sub < 4, top_ref[0, a:a + 1, :], top_ref[0, a + 1:a + 2, :])
                cand_ref[40 + 8 * n:48 + 8 * n, :] = v1_pair + v2_top4_twice
            cand_ref[56:64, :] = top_ref[0, 8:16, :] + top_ref[1, 0:1, :]
            best = _sorted_top16([cand_ref[8 * n:8 * n + 8, :] for n in range(cand_ref.shape[0] // 8)])
            tau = best[PEER_TOPK - 1]
            m1, m2 = top1[0], top2[0]
            zsum = best[0] * 0.0
            for v in best:
                zsum = zsum + jnp.exp(v - (m1 + m2))
            half_inv_z = 0.5 / zsum
            ranks, cnts = [], []
            for g1, g2 in zip(s1, s2):
                thr = tau - g1
                rank, cnt = jnp.zeros((8, LANE), F32), jnp.zeros((8, LANE), F32)
                for b in range(PEER_TOPK):
                    rank = jnp.where(top2[b] > g2, float(b + 1), rank)
                    cnt = jnp.where(top2[b] >= thr, float(b + 1), cnt)
                ranks.append(rank)
                cnts.append(cnt)
            rank2_ref[h, :, lanes] = jnp.concatenate(ranks, axis=0).astype(BF16)
            e2_ref[h, :, lanes] = jnp.concatenate([jnp.exp(g2 - m2) for g2 in s2], axis=0).astype(BF16)
            cnt_ref[h, :, lanes] = jnp.concatenate(cnts, axis=0)
            e1_ref[h, :, lanes] = jnp.concatenate([jnp.exp(g1 - m1) * half_inv_z for g1 in s1], axis=0)
        return carry

    lax.fori_loop(0, x_ref.shape[0] // LANE, lane_tile, 0)


def _peer_route(x2, g, wq, keys, tm):
    t = x2.shape[0]
    row = pl.BlockSpec((tm, D_MODEL), lambda i: (i, 0))
    sc = pl.BlockSpec((None, PEER_HEADS, N_KEYS, tm), lambda i: (i, 0, 0, 0))
    return pl.pallas_call(
        _peer_route_kernel,
        grid=(t // tm,),
        in_specs=[row, _full((1, D_MODEL)), _full((D_MODEL, PEER_HEADS * D_KEY)), _full((2, N_KEYS, HALF_KEY))],
        out_specs=[pl.BlockSpec((D_MODEL, tm), lambda i: (0, i)), sc, sc, sc, sc],
        out_shape=[jax.ShapeDtypeStruct((D_MODEL, t), BF16),
                   jax.ShapeDtypeStruct((t // tm, PEER_HEADS, N_KEYS, tm), BF16),
                   jax.ShapeDtypeStruct((t // tm, PEER_HEADS, N_KEYS, tm), BF16),
                   jax.ShapeDtypeStruct((t // tm, PEER_HEADS, N_KEYS, tm), F32),
                   jax.ShapeDtypeStruct((t // tm, PEER_HEADS, N_KEYS, tm), F32)],
        scratch_shapes=[pltpu.VMEM((PEER_HEADS, N_KEYS, tm), F32), pltpu.VMEM((PEER_HEADS, N_KEYS, tm), F32),
                        pltpu.VMEM((tm, PEER_HEADS * D_KEY), BF16), pltpu.VMEM((2, PEER_TOPK, LANE), F32),
                        pltpu.VMEM((4 * PEER_TOPK, LANE), F32)],
        compiler_params=_cparams("parallel"),
        name="peer_route",
    )(x2, g, wq, keys)


def _peer_tables_kernel(u_ref, v_ref, ub_ref, vt_ref):
    ub_ref[...] = u_ref[...].astype(BF16)
    vt_ref[0] = v_ref[...].T.astype(BF16)


def _peer_tables(u_all, v_all, layer, te):
    n_chunks = N_EXPERTS // te
    src = pl.BlockSpec((None, te, D_MODEL), lambda c: (layer, c, 0))
    return pl.pallas_call(
        _peer_tables_kernel,
        grid=(n_chunks,),
        in_specs=[src, src],
        out_specs=[pl.BlockSpec((te, D_MODEL), lambda c: (c, 0)), pl.BlockSpec((1, D_MODEL, te), lambda c: (c, 0, 0))],
        out_shape=[jax.ShapeDtypeStruct((N_EXPERTS, D_MODEL), BF16),
                   jax.ShapeDtypeStruct((n_chunks, D_MODEL, te), BF16)],
        compiler_params=_cparams("parallel"),
        name="peer_tables",
    )(u_all, v_all)


def _peer_dense_step(c, parity, h_ref, rank2_ref, e2_ref, cnt_ref, e1_ref, u_ref, vt_ref, pre_ref, p_ref, acc_ref,
                     *, te, n_chunks, stages):
    tm = h_ref.shape[1]
    rows_per_step = te // N_KEYS
    assert 2 * rows_per_step == 8
    part = 64
    gate_w = 2 * LANE
    score_buf, gate_buf, fold_buf = parity, 1 - parity, parity
    key_tile = pl.ds(pl.multiple_of((jnp.clip(c - 1, 0, n_chunks - 1) // 2) * 8, 8), 8)
    row0 = rows_per_step * (1 - parity)

    def score(piece):
        rows = slice((piece // 2) * te // 2, (piece // 2 + 1) * te // 2)
        toks = slice((piece % 2) * gate_w, (piece % 2 + 1) * gate_w)
        u_rows = slice(parity * te + rows.start, parity * te + rows.stop)
        pre_ref[score_buf, rows, toks] = _dot(u_ref[u_rows, :], h_ref[:, toks])

    def fold(piece):
        rows = slice((piece // 2) * D_MODEL // 4, (piece // 2 + 1) * D_MODEL // 4)
        toks = slice((piece % 2) * gate_w, (piece % 2 + 1) * gate_w)
        acc_ref[rows, toks] += _dot(vt_ref[parity, rows, :], p_ref[fold_buf, :, toks])

    def gate_heads(lt, rp, w, heads):
        lanes = slice(lt * gate_w, (lt + 1) * gate_w)
        for h in heads:
            rank2 = rank2_ref[h, rp * part:(rp + 1) * part, lanes]
            e2 = e2_ref[h, rp * part:(rp + 1) * part, lanes]
            cnt = cnt_ref[h, key_tile, lanes]
            e1 = e1_ref[h, key_tile, lanes]
            for j in range(rows_per_step):
                cnt_j = jnp.broadcast_to(cnt[row0 + j:row0 + j + 1, :], (part, gate_w)).astype(BF16)
                e1_j = jnp.broadcast_to(e1[row0 + j:row0 + j + 1, :], (part, gate_w)).astype(BF16)
                term = jnp.where(rank2 < cnt_j, e2, jnp.zeros_like(e2)) * e1_j
                w[j] = term if w[j] is None else w[j] + term

    def gate_store(lt, rp, w):
        lanes = slice(lt * gate_w, (lt + 1) * gate_w)
        for j in range(rows_per_step):
            rows = slice(j * N_KEYS + rp * part, j * N_KEYS + (rp + 1) * part)
            pre = pre_ref[gate_buf, rows, lanes]
            act = pre * (1.0 + lax.erf(pre * (0.5 ** 0.5)))
            p_ref[gate_buf, rows, lanes] = w[j] * act.astype(BF16)

    assert tm // gate_w == 2 and N_KEYS // part == 2
    mxu = [lambda n=n: score(n) for n in range(4)] + [lambda n=n: fold(n) for n in range(8)]
    order = [0, 4, 5, 1, 6, 7, 2, 8, 9, 3, 10, 11]
    live = [n for n in order if ("score" if n < 4 else "fold") in stages]
    if "gate" not in stages:
        for n in live:
            mxu[n]()
        return
    slots = 4 * 3
    pieces = iter([mxu[n] for n in live] + [None] * (slots - len(live)))
    for lt in range(2):
        for rp in range(2):
            w = [None] * rows_per_step
            for h in range(PEER_HEADS):
                gate_heads(lt, rp, w, (h,))
                if h % 3 == 1:
                    piece = next(pieces)
                    if piece is not None:
                        piece()
            gate_store(lt, rp, w)


def _peer_dense_kernel(x_ref, h_ref, rank2_ref, e2_ref, cnt_ref, e1_ref, u_ref, vt_ref, gf_ref, out_ref,
                       pre_ref, p_ref, acc_ref, *, te, n_chunks, final_norm):
    g = pl.program_id(1)

    last = n_chunks // 2

    @pl.when(g == 0)
    def _():
        acc_ref[...] = jnp.zeros_like(acc_ref)

    full = ("score", "gate", "fold")
    plans = (((0, 0), (("score",), ("score", "gate"))),
             ((1, last - 1), (full, full)),
             ((last, last), (("gate", "fold"), ("fold",))))
    for (lo, hi), stage_sets in plans:
        for parity, stages in enumerate(stage_sets):
            guard = jnp.logical_and(g >= lo, g <= hi) if parity == 0 else jnp.logical_and(g > lo - 1, g < hi + 1)

            @pl.when(guard)
            def _(parity=parity, stages=stages):
                _peer_dense_step(2 * g + parity, parity, h_ref, rank2_ref, e2_ref, cnt_ref, e1_ref, u_ref, vt_ref,
                                 pre_ref, p_ref, acc_ref, te=te, n_chunks=n_chunks, stages=stages)

    @pl.when(g == last)
    def _():
        y = x_ref[...] + acc_ref[...].T
        if final_norm:
            y = _rms(y, gf_ref[...])
        out_ref[...] = y


def _peer_dense(x2, hn, rank2, e2, cnt, e1, u, vt, gf, tm, te, final_norm):
    t = x2.shape[0]
    n_chunks = N_EXPERTS // te
    row = pl.BlockSpec((tm, D_MODEL), lambda i, c: (i, 0))
    sc = pl.BlockSpec((None, PEER_HEADS, N_KEYS, tm), lambda i, c: (i, 0, 0, 0))
    return pl.pallas_call(
        functools.partial(_peer_dense_kernel, te=te, n_chunks=n_chunks, final_norm=final_norm),
        grid=(t // tm, n_chunks // 2 + 1),
        in_specs=[row, pl.BlockSpec((D_MODEL, tm), lambda i, g: (0, i)), sc, sc, sc, sc,
                  pl.BlockSpec((2 * te, D_MODEL), lambda i, g: (jnp.minimum(g, n_chunks // 2 - 1), 0)),
                  pl.BlockSpec((2, D_MODEL, te), lambda i, g: (jnp.clip(g - 1, 0, n_chunks // 2 - 1), 0, 0)),
                  pl.BlockSpec((1, D_MODEL), lambda i, c: (0, 0))],
        out_specs=row,
        out_shape=jax.ShapeDtypeStruct((t, D_MODEL), F32),
        scratch_shapes=[pltpu.VMEM((2, te, tm), F32), pltpu.VMEM((2, te, tm), BF16),
                        pltpu.VMEM((D_MODEL, tm), F32)],
        compiler_params=_cparams("parallel", "arbitrary"),
        name="peer_dense",
    )(x2, hn, rank2, e2, cnt, e1, u, vt, gf)


def _pad_cols(w, width):
    return jnp.pad(w, ((0, 0), (0, width - w.shape[1])))


def _rope_partner(w):
    half = MLA_ROPE // 2
    return jnp.concatenate([-w[:, half:], w[:, :half]], axis=1)


def _rope_lanes(w):
    return jnp.pad(w, ((0, 0), (MLA_NOPE, LANE - MLA_NOPE - MLA_ROPE)))


def _layer_weights(w_in, w_uq, w_ukv, w_o_sb, w_o_fox):
    o = IN_OFFSETS
    seg = lambda n: w_in[:, o[n]:o[n + 1]]
    scale = HEAD_DIM ** -0.5 * LOG2E
    w_kr = seg(2)
    w_cat = jnp.concatenate(
        [seg(0), seg(1), _rope_lanes(w_kr), _rope_lanes(_rope_partner(w_kr)),
         _pad_cols(seg(3) * scale, HEAD_PAD_W), _pad_cols(seg(4), HEAD_PAD_W), _pad_cols(seg(5), HEAD_PAD_W),
         _pad_cols(seg(6) * scale, HEAD_PAD_W), _pad_cols(seg(7), HEAD_PAD_W), _pad_cols(seg(8), HEAD_PAD_W),
         _pad_cols(seg(9), LANE)], axis=1).astype(BF16)
    w_gate = seg(10).astype(BF16)
    uq = w_uq.reshape(Q_LORA, MLA_HEADS, MLA_QK)
    nope, rope = uq[..., :MLA_NOPE], uq[..., MLA_NOPE:]
    zpad = jnp.zeros((Q_LORA, MLA_HEADS, LANE - MLA_QK), F32)
    wqa = jnp.concatenate([nope, rope, zpad], axis=-1).reshape(Q_LORA, MLA_HEADS * LANE).astype(BF16)
    partner = jnp.concatenate([-rope[..., MLA_ROPE // 2:], rope[..., :MLA_ROPE // 2]], axis=-1)
    wqb = jnp.concatenate([jnp.zeros_like(nope), partner, zpad], axis=-1).reshape(Q_LORA, MLA_HEADS * LANE).astype(BF16)
    ukv = w_ukv.reshape(KV_LORA, MLA_HEADS, MLA_NOPE + MLA_V)
    wk = jnp.pad(ukv[..., :MLA_NOPE], ((0, 0), (0, 0), (0, LANE - MLA_NOPE))).reshape(KV_LORA, MLA_HEADS * LANE).astype(BF16)
    wv = ukv[..., MLA_NOPE:].reshape(KV_LORA, MLA_OUT).astype(BF16)
    pad_rows = lambda w: jnp.pad(w, ((0, HEAD_PAD_W - w.shape[0]), (0, 0))).astype(BF16)
    return w_cat, w_gate, wqa, wqb, wk, wv, pad_rows(w_o_sb), pad_rows(w_o_fox)


def _tiles(b, s):
    t = b * s
    tm = min(TOKEN_TILE, t)
    tq = min(ATTN_TILE, s)
    return t, tm, tq


def kernel(x, positions, norm1_g, w_in, mla_q_norm_g, w_uq, mla_kv_norm_g, w_ukv, fox_b_f, w_o_mla, w_o_sb,
           w_o_fox, b_gate, w_out, norm2_g, peer_w_q, peer_sub_keys, peer_u, peer_v, final_norm_g):
    b, s, d = x.shape
    assert d == D_MODEL and MLA_OUT == HEAD_PAD_W
    depth = w_in.shape[0]
    t, tm, tq = _tiles(b, s)
    assert t % tm == 0 and s % tq == 0 and tq % CHUNK == 0
    te = EXPERT_CHUNK
    x2 = x.reshape(t, d)
    cos, sin = _rope_tables(positions, tm)
    gf = final_norm_g.reshape(1, d)
    for l in range(depth):
        w_cat, w_gate, wqa, wqb, wk, wv, wo_sb, wo_fox = _layer_weights(w_in[l], w_uq[l], w_ukv[l], w_o_sb[l], w_o_fox[l])
        g1 = norm1_g[l].reshape(1, d)
        q_a, k_a, v_a, sb, fx, f = _inproj(x2, g1, w_cat, cos, sin, mla_q_norm_g[l].reshape(1, Q_LORA), wqa, wqb,
                                           mla_kv_norm_g[l].reshape(1, KV_LORA), wk, wv, tm)
        bf_row = jnp.zeros((1, LANE), F32).at[0, :FOX_HEADS].set(fox_b_f[l])
        cum, cumt = _fox_prep(f.reshape(b, s, LANE), bf_row, min(SUFFIX_TILE, s))
        hw = MLA_HEADS * LANE
        acc_a = _head_state(MLA_HEADS, tq)
        o_a = _attention(_mla_attn_kernel, q_a.reshape(b, s, hw), k_a.reshape(b, s, hw), v_a.reshape(b, s, MLA_OUT),
                         (0, 0, 0), [], [], [acc_a, acc_a, acc_a], hw, MLA_OUT, tq, "mla_attn")
        sb3 = sb.reshape(b, s, QKV_W)
        fx3 = fx.reshape(b, s, QKV_W)
        acc_b = _head_state(SB_HEADS, tq)
        qm = pltpu.VMEM((SB_HEADS, tq, LANE), BF16)
        o_b = _attention(_sb_attn_kernel, sb3, sb3, sb3, (0, 1, 2), [], [],
                         [qm, pltpu.VMEM((min(SUFFIX_TILE, tq),) * 2, BF16), acc_b, acc_b], HEAD_PAD_W, HEAD_PAD_W, tq,
                         "sb_attn")
        fox_specs = [pl.BlockSpec((1, tq, LANE), lambda bi, i: (bi, i, 0)),
                     pl.BlockSpec((1, 8, s), lambda bi, i: (bi, 0, 0))]
        o_c = _attention(_fox_attn_kernel, fx3, fx3, fx3, (0, 1, 2), [cum, cumt], fox_specs,
                         [qm, acc_b, acc_b, acc_b, acc_b], HEAD_PAD_W, HEAD_PAD_W, tq, "fox_attn")
        x2 = _merge(x2, g1, w_gate, b_gate[l].reshape(1, -1), o_a.reshape(t, MLA_OUT), o_b.reshape(t, HEAD_PAD_W),
                    o_c.reshape(t, HEAD_PAD_W), w_o_mla[l].astype(BF16), wo_sb, wo_fox, w_out[l].astype(BF16), tm)
        hn, rank2, e2, cnt, e1 = _peer_route(x2, norm2_g[l].reshape(1, d), peer_w_q[l].astype(BF16),
                                             peer_sub_keys[l].astype(BF16), tm)
        u_bf, vt = _peer_tables(peer_u, peer_v, l, te)
        x2 = _peer_dense(x2, hn, rank2, e2, cnt, e1, u_bf, vt, gf, tm, te, final_norm=(l == depth - 1))
    return x2.reshape(b, s, d)
```

```python
import functools

import numpy as np
import jax
import jax.numpy as jnp
from jax import lax
from jax.experimental import pallas as pl
from jax.experimental.pallas import tpu as pltpu

F32 = jnp.float32
BF16 = jnp.bfloat16

D_MODEL = 1024
CHUNK = 64
HEAD_DIM = 64
NORM_EPS = 1e-6
NEG_INF = -1e30
MLA_HEADS = 6
MLA_NOPE = 64
MLA_ROPE = 32
MLA_V = 64
Q_LORA = 256
KV_LORA = 128
ROPE_THETA = 10000.0
SB_HEADS = 5
FOX_HEADS = 5
N_BRANCH = 3
PEER_HEADS = 8
N_KEYS = 128
N_EXPERTS = N_KEYS * N_KEYS
D_KEY = 256
HALF_KEY = D_KEY // 2
PEER_TOPK = 16
MLA_QK = MLA_NOPE + MLA_ROPE
SB_W = SB_HEADS * HEAD_DIM
FOX_W = FOX_HEADS * HEAD_DIM
MLA_OUT = MLA_HEADS * MLA_V
IN_SPLITS = (Q_LORA, KV_LORA, MLA_ROPE, SB_W, SB_W, SB_W, FOX_W, FOX_W, FOX_W, FOX_HEADS, N_BRANCH * D_MODEL)
IN_OFFSETS = tuple(int(o) for o in np.cumsum((0,) + IN_SPLITS))

LANE = 128
HEAD_PAD_W = 384
MLA_IN_W = Q_LORA + KV_LORA + 2 * LANE
QKV_W = 3 * HEAD_PAD_W
IN_W = MLA_IN_W + 2 * QKV_W + LANE
VMEM_LIMIT = 48 * 1024 * 1024
TOKEN_TILE = 512
ATTN_TILE = 512
SUFFIX_TILE = 256
EXPERT_CHUNK = 4 * N_KEYS

LOG2E = 1.4426950408889634
_NT = (((1,), (1,)), ((), ()))


def _cparams(*sem):
    return pltpu.CompilerParams(dimension_semantics=sem, vmem_limit_bytes=VMEM_LIMIT)


def _rms(x, g):
    return x * lax.rsqrt(jnp.mean(x * x, axis=-1, keepdims=True) + NORM_EPS) * g


def _dot(a, b):
    return jnp.dot(a, b, preferred_element_type=F32)


def _dot_nt(a, b):
    return lax.dot_general(a, b, _NT, preferred_element_type=F32)


def _full(shape):
    return pl.BlockSpec(shape, lambda *_: (0,) * len(shape))


def _rope_kernel(pos_ref, inv_ref, cos_ref, sin_ref):
    ang = pos_ref[...].astype(F32) * inv_ref[...]
    cos_ref[...] = jnp.cos(ang)
    sin_ref[...] = jnp.sin(ang)


def _rope_tables(positions, tm):
    t = positions.size
    inv = ROPE_THETA ** (-jnp.arange(0, MLA_ROPE, 2, dtype=F32) / MLA_ROPE)
    inv_row = jnp.zeros((1, LANE), F32).at[0, MLA_NOPE:MLA_NOPE + MLA_ROPE].set(jnp.concatenate([inv, inv]))
    return pl.pallas_call(
        _rope_kernel,
        grid=(t // tm,),
        in_specs=[pl.BlockSpec((tm, 1), lambda i: (i, 0)), _full((1, LANE))],
        out_specs=[pl.BlockSpec((tm, LANE), lambda i: (i, 0))] * 2,
        out_shape=[jax.ShapeDtypeStruct((t, LANE), F32)] * 2,
        compiler_params=_cparams("parallel"),
        name="rope_tables",
    )(positions.reshape(t, 1), inv_row)


def _inproj_kernel(x_ref, g_ref, w_ref, cos_ref, sin_ref, gq_ref, wqa_ref, wqb_ref, gkv_ref, wk_ref, wv_ref,
                   q_ref, k_ref, v_ref, sb_ref, fx_ref, f_ref, mla_ref):
    hn = _rms(x_ref[...], g_ref[...]).astype(BF16)
    o = 0
    for ref, width in ((mla_ref, MLA_IN_W), (sb_ref, QKV_W), (fx_ref, QKV_W), (f_ref, LANE)):
        ref[...] = _dot(hn, w_ref[:, o:o + width]).astype(ref.dtype)
        o += width
    _mla_prep(mla_ref, cos_ref, sin_ref, gq_ref, wqa_ref, wqb_ref, gkv_ref, wk_ref, wv_ref, q_ref, k_ref, v_ref)


def _inproj(x2, g, w_cat, cos, sin, gq, wqa, wqb, gkv, wk, wv, tm):
    t = x2.shape[0]
    row = lambda w: pl.BlockSpec((tm, w), lambda i: (i, 0))
    hw = MLA_HEADS * LANE
    return pl.pallas_call(
        _inproj_kernel,
        grid=(t // tm,),
        in_specs=[row(D_MODEL), _full((1, D_MODEL)), _full((D_MODEL, IN_W)), row(LANE), row(LANE),
                  _full((1, Q_LORA)), _full((Q_LORA, hw)), _full((Q_LORA, hw)), _full((1, KV_LORA)),
                  _full((KV_LORA, hw)), _full((KV_LORA, MLA_OUT))],
        out_specs=[row(hw), row(hw), row(MLA_OUT), row(QKV_W), row(QKV_W), row(LANE)],
        out_shape=[jax.ShapeDtypeStruct((t, hw), BF16), jax.ShapeDtypeStruct((t, hw), BF16),
                   jax.ShapeDtypeStruct((t, MLA_OUT), BF16), jax.ShapeDtypeStruct((t, QKV_W), BF16),
                   jax.ShapeDtypeStruct((t, QKV_W), BF16), jax.ShapeDtypeStruct((t, LANE), F32)],
        scratch_shapes=[pltpu.VMEM((tm, MLA_IN_W), F32)],
        compiler_params=_cparams("parallel"),
        name="inproj",
    )(x2, g, w_cat, cos, sin, gq, wqa, wqb, gkv, wk, wv)


def _mla_prep(in_ref, cos_ref, sin_ref, gq_ref, wqa_ref, wqb_ref, gkv_ref, wk_ref, wv_ref, q_ref, k_ref, v_ref):
    cos = cos_ref[...]
    sin = sin_ref[...]
    qn = _rms(in_ref[:, :Q_LORA], gq_ref[...]).astype(BF16)
    kn = _rms(in_ref[:, Q_LORA:Q_LORA + KV_LORA], gkv_ref[...]).astype(BF16)
    o = Q_LORA + KV_LORA
    k_rot = in_ref[:, o:o + LANE] * cos + in_ref[:, o + LANE:o + 2 * LANE] * sin
    scale = MLA_QK ** -0.5 * LOG2E
    qa = _dot(qn, wqa_ref[...])
    qb = _dot(qn, wqb_ref[...])
    ka = _dot(kn, wk_ref[...])
    for h in range(MLA_HEADS):
        sl = slice(h * LANE, (h + 1) * LANE)
        q_ref[:, sl] = ((qa[:, sl] * cos + qb[:, sl] * sin) * scale).astype(BF16)
        k_ref[:, sl] = (ka[:, sl] + k_rot).astype(BF16)
    v_ref[...] = _dot(kn, wv_ref[...]).astype(BF16)


def _fox_prep_kernel(f_ref, bf_ref, cum_ref, cumt_ref, *, blk):
    s = f_ref.shape[1]
    row = lax.broadcasted_iota(jnp.int32, (blk, blk), 0)
    col = lax.broadcasted_iota(jnp.int32, (blk, blk), 1)
    tri = jnp.where(col <= row, 1.0, 0.0).astype(BF16)
    carry = jnp.zeros((1, LANE), F32)
    for b in range(s // blk):
        f = f_ref[0, b * blk:(b + 1) * blk, :] + bf_ref[...]
        lf = (jnp.minimum(f, 0.0) - jnp.log1p(jnp.exp(-jnp.abs(f)))) * LOG2E
        hi = lf.astype(BF16)
        r1 = lf - hi.astype(F32)
        mid = r1.astype(BF16)
        lo = (r1 - mid.astype(F32)).astype(BF16)
        c = _dot(tri, hi) + _dot(tri, mid) + _dot(tri, lo) + carry
        cum_ref[0, b * blk:(b + 1) * blk, :] = c
        cumt_ref[0, :, b * blk:(b + 1) * blk] = c.T[:8, :]
        carry = c[blk - 1:blk, :]


def _fox_prep(f3, bf_row, blk):
    b, s, _ = f3.shape
    return pl.pallas_call(
        functools.partial(_fox_prep_kernel, blk=blk),
        grid=(b,),
        in_specs=[pl.BlockSpec((1, s, LANE), lambda i: (i, 0, 0)), _full((1, LANE))],
        out_specs=[pl.BlockSpec((1, s, LANE), lambda i: (i, 0, 0)), pl.BlockSpec((1, 8, s), lambda i: (i, 0, 0))],
        out_shape=[jax.ShapeDtypeStruct((b, s, LANE), F32), jax.ShapeDtypeStruct((b, 8, s), F32)],
        compiler_params=_cparams("parallel"),
        name="fox_prep",
    )(f3, bf_row)


def _softmax_update(h, s, v, m_ref, l_ref, acc_ref):
    m_old = m_ref[h]
    m_new = jnp.maximum(m_old, jnp.max(s, axis=-1, keepdims=True))
    alpha = jnp.exp2(m_old - m_new)
    p = [jnp.exp2(s[:, n * LANE:(n + 1) * LANE] - m_new) for n in range(s.shape[1] // LANE)]
    part = p[0]
    for p_n in p[1:]:
        part = part + p_n
    l_ref[h] = alpha * l_ref[h] + part
    m_ref[h] = m_new
    acc_ref[h] = alpha * acc_ref[h] + _dot(jnp.concatenate(p, axis=1).astype(BF16), v)


def _softmax_reset(m_ref, l_ref, acc_ref):
    m_ref[...] = jnp.full(m_ref.shape, NEG_INF, F32)
    l_ref[...] = jnp.zeros(l_ref.shape, F32)
    acc_ref[...] = jnp.zeros(acc_ref.shape, F32)


def _half_mask(tq, half):
    lane = lax.broadcasted_iota(jnp.int32, (tq, LANE), 1)
    return (lane < HEAD_DIM) if half == 0 else (lane >= HEAD_DIM)


def _store_head_pairs(o_ref, heads, value_of):
    tq = o_ref.shape[1]
    lo_half = _half_mask(tq, 0)
    for hb in range(o_ref.shape[2] // LANE):
        lo = value_of(2 * hb)
        hi = value_of(2 * hb + 1) if 2 * hb + 1 < heads else jnp.zeros_like(lo)
        o_ref[0, :, hb * LANE:(hb + 1) * LANE] = jnp.where(lo_half, lo, hi).astype(o_ref.dtype)


def _masked_queries(q_ref, qm_ref, heads):
    tq = q_ref.shape[1]
    for h in range(heads):
        qf = q_ref[0, :, (h // 2) * LANE:(h // 2 + 1) * LANE].astype(F32)
        qm_ref[h] = jnp.where(_half_mask(tq, h % 2), qf, 0.0).astype(BF16)


def _block_iotas(tq):
    return lax.broadcasted_iota(jnp.int32, (tq, tq), 0), lax.broadcasted_iota(jnp.int32, (tq, tq), 1)


def _mla_attn_kernel(q_ref, k_ref, v_ref, o_ref, m_ref, l_ref, acc_ref, *, tq):
    i = pl.program_id(1)
    _softmax_reset(m_ref, l_ref, acc_ref)

    def block(j, diagonal):
        rows = pl.ds(pl.multiple_of(j * tq, tq), tq)
        def scores(h):
            sl = slice(h * LANE, (h + 1) * LANE)
            return _dot_nt(q_ref[0, :, sl], k_ref[0, rows, sl])

        s_next = scores(0)
        for h in range(MLA_HEADS):
            s = s_next
            if h + 1 < MLA_HEADS:
                s_next = scores(h + 1)
            if diagonal:
                r, c = _block_iotas(tq)
                s = jnp.where((c // CHUNK) <= (r // CHUNK), s, NEG_INF)
            _softmax_update(h, s, v_ref[0, rows, (h // 2) * LANE:(h // 2 + 1) * LANE], m_ref, l_ref, acc_ref)

    def body(j, carry):
        block(j, False)
        return carry

    lax.fori_loop(0, i, body, 0)
    block(i, True)
    _store_head_pairs(o_ref, MLA_HEADS, lambda h: acc_ref[h] / jnp.sum(l_ref[h], axis=-1, keepdims=True))


def _fox_attn_kernel(q_ref, k_ref, v_ref, cum_ref, cumt_ref, o_ref, qm_ref, cq_ref, m_ref, l_ref, acc_ref, *, tq):
    i = pl.program_id(1)
    _softmax_reset(m_ref, l_ref, acc_ref)
    _masked_queries(q_ref, qm_ref, FOX_HEADS)
    for h in range(FOX_HEADS):
        cq_ref[h] = jnp.broadcast_to(cum_ref[0, :, h:h + 1], (tq, LANE))

    def block(j, diagonal):
        start = pl.multiple_of(j * tq, tq)
        rows = pl.ds(start, tq)
        def scores(h):
            return _dot_nt(qm_ref[h], k_ref[0, rows, (h // 2) * LANE:(h // 2 + 1) * LANE])

        s_next = scores(0)
        for h in range(FOX_HEADS):
            sl = slice((h // 2) * LANE, (h // 2 + 1) * LANE)
            s = s_next
            if h + 1 < FOX_HEADS:
                s_next = scores(h + 1)
            ck = cumt_ref[0, h:h + 1, rows]
            s = jnp.concatenate([s[:, n * LANE:(n + 1) * LANE] + (cq_ref[h] - ck[:, n * LANE:(n + 1) * LANE])
                                 for n in range(tq // LANE)], axis=1)
            if diagonal:
                r, c = _block_iotas(tq)
                s = jnp.where(c <= r, s, NEG_INF)
            _softmax_update(h, s, v_ref[0, rows, sl], m_ref, l_ref, acc_ref)

    def body(j, carry):
        block(j, False)
        return carry

    lax.fori_loop(0, i, body, 0)
    block(i, True)
    _store_head_pairs(o_ref, FOX_HEADS, lambda h: acc_ref[h] / jnp.sum(l_ref[h], axis=-1, keepdims=True))


def _sb_attn_kernel(q_ref, k_ref, v_ref, o_ref, qm_ref, suffix_ref, rest_ref, acc_ref, *, tq):
    i = pl.program_id(1)
    sub = suffix_ref.shape[0]
    r, c = _block_iotas(sub)
    suffix_ref[...] = jnp.where(r > c, 1.0, 0.0).astype(BF16)
    for h in range(SB_HEADS):
        rest_ref[h] = jnp.zeros(rest_ref.shape[1:], F32)
        acc_ref[h] = jnp.zeros(acc_ref.shape[1:], F32)
    _masked_queries(q_ref, qm_ref, SB_HEADS)

    def block(j, diagonal):
        rows = pl.ds(pl.multiple_of(j * tq, tq), tq)
        def scores(h):
            return _dot_nt(qm_ref[h], k_ref[0, rows, (h // 2) * LANE:(h // 2 + 1) * LANE])

        z_next = scores(0)
        for h in range(SB_HEADS):
            sl = slice((h // 2) * LANE, (h // 2 + 1) * LANE)
            z = z_next
            if h + 1 < SB_HEADS:
                z_next = scores(h + 1)
            neg_abs = pltpu.bitcast(pltpu.bitcast(z, jnp.uint32) | jnp.uint32(0x80000000), F32)
            log_b = jnp.minimum(z, 0.0) - jnp.log(1.0 + jnp.exp2(neg_abs)) * LOG2E
            log_1m = log_b - z
            if diagonal:
                strict = _block_iotas(tq)[1] < _block_iotas(tq)[0]
                log_1m = jnp.where(strict, log_1m, 0.0)
            later = rest_ref[h]
            a = [None] * (tq // sub)
            for k in reversed(range(tq // sub)):
                cols = slice(k * sub, (k + 1) * sub)
                within = _dot(log_1m[:, cols].astype(BF16), suffix_ref[...])
                a[k] = jnp.concatenate([jnp.exp2(log_b[:, cols][:, n * LANE:(n + 1) * LANE]
                                                 + within[:, n * LANE:(n + 1) * LANE] + later)
                                        for n in range(sub // LANE)], axis=1)
                later = later + jnp.sum(log_1m[:, cols], axis=-1, keepdims=True)
            a = jnp.concatenate(a, axis=1)
            if diagonal:
                a = jnp.where(strict, a, 0.0)
            acc_ref[h] += _dot(a.astype(BF16), v_ref[0, rows, sl])
            rest_ref[h] = later

    block(i, True)

    def body(n, carry):
        block(i - 1 - n, False)
        return carry

    lax.fori_loop(0, i, body, 0)
    _store_head_pairs(o_ref, SB_HEADS, lambda h: acc_ref[h])


def _attention(kernel, q, k, v, cols, extra, extra_specs, scratch, qw, vw, tq, name):
    b, s, _ = q.shape
    blk = lambda w, n=0: pl.BlockSpec((1, tq, w), lambda bi, i: (bi, i, n))
    seq = lambda w, n: pl.BlockSpec((1, s, w), lambda bi, i: (bi, 0, n))
    return pl.pallas_call(
        functools.partial(kernel, tq=tq),
        grid=(b, s // tq),
        in_specs=[blk(qw, cols[0]), seq(qw, cols[1]), seq(vw, cols[2])] + extra_specs,
        out_specs=blk(vw),
        out_shape=jax.ShapeDtypeStruct((b, s, vw), BF16),
        scratch_shapes=scratch,
        compiler_params=_cparams("parallel", "parallel"),
        name=name,
    )(q, k, v, *extra)


def _head_state(heads, tq):
    return pltpu.VMEM((heads, tq, LANE), F32)


def _merge_kernel(x_ref, g_ref, wg_ref, bg_ref, oa_ref, ob_ref, oc_ref, wa_ref, wb_ref, wc_ref, wo_ref, out_ref):
    x = x_ref[...]
    hn = _rms(x, g_ref[...]).astype(BF16)
    y = None
    for n, (o_ref, w_ref) in enumerate(((oa_ref, wa_ref), (ob_ref, wb_ref), (oc_ref, wc_ref))):
        sl = slice(n * D_MODEL, (n + 1) * D_MODEL)
        gate = jax.nn.sigmoid(_dot(hn, wg_ref[:, sl]) + bg_ref[:, sl])
        term = gate * _dot(o_ref[...], w_ref[...])
        y = term if y is None else y + term
    out_ref[...] = x + _dot(y.astype(BF16), wo_ref[...])


def _merge(x2, g, wg, bg, oa, ob, oc, wa, wb, wc, wo, tm):
    t = x2.shape[0]
    row = lambda w: pl.BlockSpec((tm, w), lambda i: (i, 0))
    return pl.pallas_call(
        _merge_kernel,
        grid=(t // tm,),
        in_specs=[row(D_MODEL), _full((1, D_MODEL)), _full((D_MODEL, N_BRANCH * D_MODEL)),
                  _full((1, N_BRANCH * D_MODEL)), row(HEAD_PAD_W), row(HEAD_PAD_W), row(HEAD_PAD_W),
                  _full((HEAD_PAD_W, D_MODEL)), _full((HEAD_PAD_W, D_MODEL)), _full((HEAD_PAD_W, D_MODEL)),
                  _full((D_MODEL, D_MODEL))],
        out_specs=row(D_MODEL),
        out_shape=jax.ShapeDtypeStruct((t, D_MODEL), F32),
        compiler_params=_cparams("parallel"),
        name="merge",
    )(x2, g, wg, bg, oa, ob, oc, wa, wb, wc, wo)


def _sorted_top16(groups):
    x = list(groups)
    assert len(x) in (PEER_TOPK // 2, PEER_TOPK)

    def exchange(i, l, descending):
        hi, lo = jnp.maximum(x[i], x[l]), jnp.minimum(x[i], x[l])
        x[i], x[l] = (hi, lo) if descending else (lo, hi)

    k = 2
    while k <= len(x):
        j = k // 2
        while j >= 1:
            for i in range(len(x)):
                if i ^ j > i:
                    exchange(i, i ^ j, (i & k) == 0)
            j //= 2
        k *= 2
    for shift in (4, 2, 1):
        other = [pltpu.roll(v, shift, axis=0) for v in x]
        if len(x) < PEER_TOPK:
            x = x + other[::-1]
        else:
            x = [jnp.maximum(x[r], other[PEER_TOPK - 1 - r]) for r in range(PEER_TOPK)]
        j = PEER_TOPK // 2
        while j >= 1:
            for i in range(PEER_TOPK):
                if i ^ j > i:
                    exchange(i, i ^ j, True)
            j //= 2
    return x


def _peer_route_kernel(x_ref, g_ref, wq_ref, keys_ref, h_ref, rank2_ref, e2_ref, cnt_ref, e1_ref,
                       s1_ref, s2_ref, q_ref, top_ref, cand_ref):
    hn = _rms(x_ref[...], g_ref[...]).astype(BF16)
    h_ref[...] = hn
    wide = 2 * D_KEY
    for n in range(PEER_HEADS * D_KEY // wide):
        q_ref[:, n * wide:(n + 1) * wide] = _dot(hn, wq_ref[:, n * wide:(n + 1) * wide]).astype(BF16)
    for h in range(PEER_HEADS):
        for side, s_ref in ((0, s1_ref), (1, s2_ref)):
            o = h * D_KEY + side * HALF_KEY
            s_ref[h] = _dot_nt(keys_ref[side], q_ref[:, o:o + HALF_KEY])

    def lane_tile(lt, carry):
        lanes = pl.ds(pl.multiple_of(lt * LANE, LANE), LANE)
        for h in range(PEER_HEADS):
            scores, tops = [], []
            for side, s_ref in ((0, s1_ref), (1, s2_ref)):
                groups = [s_ref[h, 8 * n:8 * n + 8, lanes] for n in range(N_KEYS // 8)]
                top = _sorted_top16(groups)
                for r in range(PEER_TOPK):
                    top_ref[side, r:r + 1, :] = top[r][0:1, :]
                scores.append(groups)
                tops.append(top)
            (s1, s2), (top1, top2) = scores, tops
            v2_top8 = top_ref[1, 0:8, :]
            sub = lax.broadcasted_iota(jnp.int32, (8, LANE), 0)
            v2_top4_twice = jnp.where(sub < 4, v2_top8, pltpu.roll(v2_top8, 4, axis=0))
            cand_ref[0:16, :] = top_ref[0, 0:1, :] + top_ref[1]
            for a in (1, 2, 3):
                cand_ref[8 + 8 * a:16 + 8 * a, :] = top_ref[0, a:a + 1, :] + v2_top8
            for n, a in enumerate((4, 6)):
                v1_pair = jnp.where(sub < 4, top_ref[0, a:a + 1, :], top_ref[0, a + 1:a + 2, :])
                cand_ref[40 + 8 * n:48 + 8 * n, :] = v1_pair + v2_top4_twice
            cand_ref[56:64, :] = top_ref[0, 8:16, :] + top_ref[1, 0:1, :]
            best = _sorted_top16([cand_ref[8 * n:8 * n + 8, :] for n in range(cand_ref.shape[0] // 8)])
            tau = best[PEER_TOPK - 1]
            m1, m2 = top1[0], top2[0]
            zsum = best[0] * 0.0
            for v in best:
                zsum = zsum + jnp.exp(v - (m1 + m2))
            half_inv_z = 0.5 / zsum
            ranks, cnts = [], []
            for g1, g2 in zip(s1, s2):
                thr = tau - g1
                rank, cnt = jnp.zeros((8, LANE), F32), jnp.zeros((8, LANE), F32)
                for b in range(PEER_TOPK):
                    rank = jnp.where(top2[b] > g2, float(b + 1), rank)
                    cnt = jnp.where(top2[b] >= thr, float(b + 1), cnt)
                ranks.append(rank)
                cnts.append(cnt)
            rank2_ref[h, :, lanes] = jnp.concatenate(ranks, axis=0).astype(BF16)
            e2_ref[h, :, lanes] = jnp.concatenate([jnp.exp(g2 - m2) for g2 in s2], axis=0).astype(BF16)
            cnt_ref[h, :, lanes] = jnp.concatenate(cnts, axis=0)
            e1_ref[h, :, lanes] = jnp.concatenate([jnp.exp(g1 - m1) * half_inv_z for g1 in s1], axis=0)
        return carry

    lax.fori_loop(0, x_ref.shape[0] // LANE, lane_tile, 0)


def _peer_route(x2, g, wq, keys, tm):
    t = x2.shape[0]
    row = pl.BlockSpec((tm, D_MODEL), lambda i: (i, 0))
    sc = pl.BlockSpec((None, PEER_HEADS, N_KEYS, tm), lambda i: (i, 0, 0, 0))
    return pl.pallas_call(
        _peer_route_kernel,
        grid=(t // tm,),
        in_specs=[row, _full((1, D_MODEL)), _full((D_MODEL, PEER_HEADS * D_KEY)), _full((2, N_KEYS, HALF_KEY))],
        out_specs=[row, sc, sc, sc, sc],
        out_shape=[jax.ShapeDtypeStruct((t, D_MODEL), BF16),
                   jax.ShapeDtypeStruct((t // tm, PEER_HEADS, N_KEYS, tm), BF16),
                   jax.ShapeDtypeStruct((t // tm, PEER_HEADS, N_KEYS, tm), BF16),
                   jax.ShapeDtypeStruct((t // tm, PEER_HEADS, N_KEYS, tm), F32),
                   jax.ShapeDtypeStruct((t // tm, PEER_HEADS, N_KEYS, tm), F32)],
        scratch_shapes=[pltpu.VMEM((PEER_HEADS, N_KEYS, tm), F32), pltpu.VMEM((PEER_HEADS, N_KEYS, tm), F32),
                        pltpu.VMEM((tm, PEER_HEADS * D_KEY), BF16), pltpu.VMEM((2, PEER_TOPK, LANE), F32),
                        pltpu.VMEM((4 * PEER_TOPK, LANE), F32)],
        compiler_params=_cparams("parallel"),
        name="peer_route",
    )(x2, g, wq, keys)


def _peer_tables_kernel(u_ref, v_ref, ub_ref, vt_ref):
    ub_ref[...] = u_ref[...].astype(BF16)
    vt_ref[0] = v_ref[...].T.astype(BF16)


def _peer_tables(u_all, v_all, layer, te):
    n_chunks = N_EXPERTS // te
    src = pl.BlockSpec((None, te, D_MODEL), lambda c: (layer, c, 0))
    return pl.pallas_call(
        _peer_tables_kernel,
        grid=(n_chunks,),
        in_specs=[src, src],
        out_specs=[pl.BlockSpec((te, D_MODEL), lambda c: (c, 0)), pl.BlockSpec((1, D_MODEL, te), lambda c: (c, 0, 0))],
        out_shape=[jax.ShapeDtypeStruct((N_EXPERTS, D_MODEL), BF16),
                   jax.ShapeDtypeStruct((n_chunks, D_MODEL, te), BF16)],
        compiler_params=_cparams("parallel"),
        name="peer_tables",
    )(u_all, v_all)


def _peer_dense_step(c, parity, h_ref, rank2_ref, e2_ref, cnt_ref, e1_ref, u_ref, vt_ref, pre_ref, p_ref, acc_ref,
                     *, te, n_chunks, stages):
    tm = h_ref.shape[0]
    rows_per_step = te // N_KEYS
    assert 2 * rows_per_step == 8
    part = 64
    gate_w = 2 * LANE
    score_buf, gate_buf, fold_buf = parity, 1 - parity, parity
    key_tile = pl.ds(pl.multiple_of((jnp.clip(c - 1, 0, n_chunks - 1) // 2) * 8, 8), 8)
    row0 = rows_per_step * (1 - parity)

    def score(piece):
        rows = slice((piece // 2) * te // 2, (piece // 2 + 1) * te // 2)
        toks = slice((piece % 2) * gate_w, (piece % 2 + 1) * gate_w)
        u_rows = slice(parity * te + rows.start, parity * te + rows.stop)
        pre_ref[score_buf, rows, toks] = _dot_nt(u_ref[u_rows, :], h_ref[toks, :])

    def fold(piece):
        rows = slice((piece // 2) * D_MODEL // 4, (piece // 2 + 1) * D_MODEL // 4)
        toks = slice((piece % 2) * gate_w, (piece % 2 + 1) * gate_w)
        acc_ref[rows, toks] += _dot(vt_ref[parity, rows, :], p_ref[fold_buf, :, toks])

    def gate_heads(lt, rp, w, heads):
        lanes = slice(lt * gate_w, (lt + 1) * gate_w)
        for h in heads:
            rank2 = rank2_ref[h, rp * part:(rp + 1) * part, lanes]
            e2 = e2_ref[h, rp * part:(rp + 1) * part, lanes]
            cnt = cnt_ref[h, key_tile, lanes]
            e1 = e1_ref[h, key_tile, lanes]
            for j in range(rows_per_step):
                cnt_j = jnp.broadcast_to(cnt[row0 + j:row0 + j + 1, :], (part, gate_w)).astype(BF16)
                e1_j = jnp.broadcast_to(e1[row0 + j:row0 + j + 1, :], (part, gate_w)).astype(BF16)
                term = jnp.where(rank2 < cnt_j, e2, jnp.zeros_like(e2)) * e1_j
                w[j] = term if w[j] is None else w[j] + term

    def gate_store(lt, rp, w):
        lanes = slice(lt * gate_w, (lt + 1) * gate_w)
        for j in range(rows_per_step):
            rows = slice(j * N_KEYS + rp * part, j * N_KEYS + (rp + 1) * part)
            pre = pre_ref[gate_buf, rows, lanes]
            act = pre * (1.0 + lax.erf(pre * (0.5 ** 0.5)))
            p_ref[gate_buf, rows, lanes] = w[j] * act.astype(BF16)

    assert tm // gate_w == 2 and N_KEYS // part == 2
    mxu = [lambda n=n: score(n) for n in range(4)] + [lambda n=n: fold(n) for n in range(8)]
    order = [0, 4, 5, 1, 6, 7, 2, 8, 9, 3, 10, 11]
    live = [n for n in order if ("score" if n < 4 else "fold") in stages]
    if "gate" not in stages:
        for n in live:
            mxu[n]()
        return
    slots = 4 * 3
    pieces = iter([mxu[n] for n in live] + [None] * (slots - len(live)))
    for lt in range(2):
        for rp in range(2):
            w = [None] * rows_per_step
            for h in range(PEER_HEADS):
                gate_heads(lt, rp, w, (h,))
                if h % 3 == 1:
                    piece = next(pieces)
                    if piece is not None:
                        piece()
            gate_store(lt, rp, w)


def _peer_dense_kernel(x_ref, h_ref, rank2_ref, e2_ref, cnt_ref, e1_ref, u_ref, vt_ref, gf_ref, out_ref,
                       pre_ref, p_ref, acc_ref, *, te, n_chunks, final_norm):
    g = pl.program_id(1)

    last = n_chunks // 2

    @pl.when(g == 0)
    def _():
        acc_ref[...] = jnp.zeros_like(acc_ref)

    full = ("score", "gate", "fold")
    plans = (((0, 0), (("score",), ("score", "gate"))),
             ((1, last - 1), (full, full)),
             ((last, last), (("gate", "fold"), ("fold",))))
    for (lo, hi), stage_sets in plans:
        for parity, stages in enumerate(stage_sets):
            guard = jnp.logical_and(g >= lo, g <= hi) if parity == 0 else jnp.logical_and(g > lo - 1, g < hi + 1)

            @pl.when(guard)
            def _(parity=parity, stages=stages):
                _peer_dense_step(2 * g + parity, parity, h_ref, rank2_ref, e2_ref, cnt_ref, e1_ref, u_ref, vt_ref,
                                 pre_ref, p_ref, acc_ref, te=te, n_chunks=n_chunks, stages=stages)

    @pl.when(g == last)
    def _():
        y = x_ref[...] + acc_ref[...].T
        if final_norm:
            y = _rms(y, gf_ref[...])
        out_ref[...] = y


def _peer_dense(x2, hn, rank2, e2, cnt, e1, u, vt, gf, tm, te, final_norm):
    t = x2.shape[0]
    n_chunks = N_EXPERTS // te
    row = pl.BlockSpec((tm, D_MODEL), lambda i, c: (i, 0))
    sc = pl.BlockSpec((None, PEER_HEADS, N_KEYS, tm), lambda i, c: (i, 0, 0, 0))
    return pl.pallas_call(
        functools.partial(_peer_dense_kernel, te=te, n_chunks=n_chunks, final_norm=final_norm),
        grid=(t // tm, n_chunks // 2 + 1),
        in_specs=[row, row, sc, sc, sc, sc,
                  pl.BlockSpec((2 * te, D_MODEL), lambda i, g: (jnp.minimum(g, n_chunks // 2 - 1), 0)),
                  pl.BlockSpec((2, D_MODEL, te), lambda i, g: (jnp.clip(g - 1, 0, n_chunks // 2 - 1), 0, 0)),
                  pl.BlockSpec((1, D_MODEL), lambda i, c: (0, 0))],
        out_specs=row,
        out_shape=jax.ShapeDtypeStruct((t, D_MODEL), F32),
        scratch_shapes=[pltpu.VMEM((2, te, tm), F32), pltpu.VMEM((2, te, tm), BF16),
                        pltpu.VMEM((D_MODEL, tm), F32)],
        compiler_params=_cparams("parallel", "arbitrary"),
        name="peer_dense",
    )(x2, hn, rank2, e2, cnt, e1, u, vt, gf)


def _pad_cols(w, width):
    return jnp.pad(w, ((0, 0), (0, width - w.shape[1])))


def _rope_partner(w):
    half = MLA_ROPE // 2
    return jnp.concatenate([-w[:, half:], w[:, :half]], axis=1)


def _rope_lanes(w):
    return jnp.pad(w, ((0, 0), (MLA_NOPE, LANE - MLA_NOPE - MLA_ROPE)))


def _layer_weights(w_in, w_uq, w_ukv, w_o_sb, w_o_fox):
    o = IN_OFFSETS
    seg = lambda n: w_in[:, o[n]:o[n + 1]]
    scale = HEAD_DIM ** -0.5 * LOG2E
    w_kr = seg(2)
    w_cat = jnp.concatenate(
        [seg(0), seg(1), _rope_lanes(w_kr), _rope_lanes(_rope_partner(w_kr)),
         _pad_cols(seg(3) * scale, HEAD_PAD_W), _pad_cols(seg(4), HEAD_PAD_W), _pad_cols(seg(5), HEAD_PAD_W),
         _pad_cols(seg(6) * scale, HEAD_PAD_W), _pad_cols(seg(7), HEAD_PAD_W), _pad_cols(seg(8), HEAD_PAD_W),
         _pad_cols(seg(9), LANE)], axis=1).astype(BF16)
    w_gate = seg(10).astype(BF16)
    uq = w_uq.reshape(Q_LORA, MLA_HEADS, MLA_QK)
    nope, rope = uq[..., :MLA_NOPE], uq[..., MLA_NOPE:]
    zpad = jnp.zeros((Q_LORA, MLA_HEADS, LANE - MLA_QK), F32)
    wqa = jnp.concatenate([nope, rope, zpad], axis=-1).reshape(Q_LORA, MLA_HEADS * LANE).astype(BF16)
    partner = jnp.concatenate([-rope[..., MLA_ROPE // 2:], rope[..., :MLA_ROPE // 2]], axis=-1)
    wqb = jnp.concatenate([jnp.zeros_like(nope), partner, zpad], axis=-1).reshape(Q_LORA, MLA_HEADS * LANE).astype(BF16)
    ukv = w_ukv.reshape(KV_LORA, MLA_HEADS, MLA_NOPE + MLA_V)
    wk = jnp.pad(ukv[..., :MLA_NOPE], ((0, 0), (0, 0), (0, LANE - MLA_NOPE))).reshape(KV_LORA, MLA_HEADS * LANE).astype(BF16)
    wv = ukv[..., MLA_NOPE:].reshape(KV_LORA, MLA_OUT).astype(BF16)
    pad_rows = lambda w: jnp.pad(w, ((0, HEAD_PAD_W - w.shape[0]), (0, 0))).astype(BF16)
    return w_cat, w_gate, wqa, wqb, wk, wv, pad_rows(w_o_sb), pad_rows(w_o_fox)


def _tiles(b, s):
    t = b * s
    tm = min(TOKEN_TILE, t)
    tq = min(ATTN_TILE, s)
    return t, tm, tq


def kernel(x, positions, norm1_g, w_in, mla_q_norm_g, w_uq, mla_kv_norm_g, w_ukv, fox_b_f, w_o_mla, w_o_sb,
           w_o_fox, b_gate, w_out, norm2_g, peer_w_q, peer_sub_keys, peer_u, peer_v, final_norm_g):
    b, s, d = x.shape
    assert d == D_MODEL and MLA_OUT == HEAD_PAD_W
    depth = w_in.shape[0]
    t, tm, tq = _tiles(b, s)
    assert t % tm == 0 and s % tq == 0 and tq % CHUNK == 0
    te = EXPERT_CHUNK
    x2 = x.reshape(t, d)
    cos, sin = _rope_tables(positions, tm)
    gf = final_norm_g.reshape(1, d)
    for l in range(depth):
        w_cat, w_gate, wqa, wqb, wk, wv, wo_sb, wo_fox = _layer_weights(w_in[l], w_uq[l], w_ukv[l], w_o_sb[l], w_o_fox[l])
        g1 = norm1_g[l].reshape(1, d)
        q_a, k_a, v_a, sb, fx, f = _inproj(x2, g1, w_cat, cos, sin, mla_q_norm_g[l].reshape(1, Q_LORA), wqa, wqb,
                                           mla_kv_norm_g[l].reshape(1, KV_LORA), wk, wv, tm)
        bf_row = jnp.zeros((1, LANE), F32).at[0, :FOX_HEADS].set(fox_b_f[l])
        cum, cumt = _fox_prep(f.reshape(b, s, LANE), bf_row, min(SUFFIX_TILE, s))
        hw = MLA_HEADS * LANE
        acc_a = _head_state(MLA_HEADS, tq)
        o_a = _attention(_mla_attn_kernel, q_a.reshape(b, s, hw), k_a.reshape(b, s, hw), v_a.reshape(b, s, MLA_OUT),
                         (0, 0, 0), [], [], [acc_a, acc_a, acc_a], hw, MLA_OUT, tq, "mla_attn")
        sb3 = sb.reshape(b, s, QKV_W)
        fx3 = fx.reshape(b, s, QKV_W)
        acc_b = _head_state(SB_HEADS, tq)
        qm = pltpu.VMEM((SB_HEADS, tq, LANE), BF16)
        o_b = _attention(_sb_attn_kernel, sb3, sb3, sb3, (0, 1, 2), [], [],
                         [qm, pltpu.VMEM((min(SUFFIX_TILE, tq),) * 2, BF16), acc_b, acc_b], HEAD_PAD_W, HEAD_PAD_W, tq,
                         "sb_attn")
        fox_specs = [pl.BlockSpec((1, tq, LANE), lambda bi, i: (bi, i, 0)),
                     pl.BlockSpec((1, 8, s), lambda bi, i: (bi, 0, 0))]
        o_c = _attention(_fox_attn_kernel, fx3, fx3, fx3, (0, 1, 2), [cum, cumt], fox_specs,
                         [qm, acc_b, acc_b, acc_b, acc_b], HEAD_PAD_W, HEAD_PAD_W, tq, "fox_attn")
        x2 = _merge(x2, g1, w_gate, b_gate[l].reshape(1, -1), o_a.reshape(t, MLA_OUT), o_b.reshape(t, HEAD_PAD_W),
                    o_c.reshape(t, HEAD_PAD_W), w_o_mla[l].astype(BF16), wo_sb, wo_fox, w_out[l].astype(BF16), tm)
        hn, rank2, e2, cnt, e1 = _peer_route(x2, norm2_g[l].reshape(1, d), peer_w_q[l].astype(BF16),
                                             peer_sub_keys[l].astype(BF16), tm)
        u_bf, vt = _peer_tables(peer_u, peer_v, l, te)
        x2 = _peer_dense(x2, hn, rank2, e2, cnt, e1, u_bf, vt, gf, tm, te, final_norm=(l == depth - 1))
    return x2.reshape(b, s, d)
```
